```python
import math
import jax, jax.numpy as jnp
from jax import lax
import numpy as np

D_MODEL = 1024
BATCH = 2
SEQ = 8192
DEPTH = 2
DEC_BATCH = 128
DEC_SEQ = 8
PAST_LEN = 8192
PAGE_SIZE = 128

HEAD_DIM = 64
ATTN_SCALE = HEAD_DIM ** -0.5
BLOCK = 128
A_Q_HEADS = 4
A_KV_HEADS = 2
A_GROUP = A_Q_HEADS // A_KV_HEADS
A_WINDOW = 128
B_HEADS = 6
B_BRANCHES = ((128, 1), (512, 4), (2048, 16))
B_SPAN = 2048
C_WIDTH = 6 * HEAD_DIM
CONV_WIDTH = 3
A_Q_W = A_Q_HEADS * HEAD_DIM
A_KV_W = A_KV_HEADS * HEAD_DIM
B_W = B_HEADS * HEAD_DIM
MIX_WIDTH = A_Q_W + B_W + C_WIDTH
IN_WIDTH = A_Q_W + 2 * A_KV_W + 3 * B_W + 3 * C_WIDTH
N_BUCKETS = 32
MAX_DISTANCE = 2048
N_ATT_HEADS = A_Q_HEADS + B_HEADS
N_GROUPS = 4
EXPERTS_PER_GROUP = 4
N_EXPERTS = N_GROUPS * EXPERTS_PER_GROUP
TOP_K = 2
EXPERT_FF = 512
RMS_EPS = 1e-6
NEG_INF = -1e30

kernel_name = 'hybrid_swa_dilated_shortconv_hmoe_step'


def rmsnorm(x, g):
    xf = x.astype(jnp.float32)
    y = xf * lax.rsqrt(jnp.mean(xf * xf, axis=-1, keepdims=True) + RMS_EPS)
    return (y * g.astype(jnp.float32)).astype(x.dtype)


def rel_bucket(dist):
    d = jnp.maximum(dist, 0)
    max_exact = N_BUCKETS // 2
    df = jnp.maximum(d, max_exact).astype(jnp.float32)
    large = max_exact + (jnp.log(df / max_exact) / math.log(MAX_DISTANCE / max_exact)
                         * (N_BUCKETS - max_exact)).astype(jnp.int32)
    large = jnp.minimum(large, N_BUCKETS - 1)
    return jnp.where(d < max_exact, d, large)


def rel_bias(table, dist, n_kv, n_group):
    b = table.astype(jnp.float32)[rel_bucket(dist)]
    return jnp.moveaxis(b, -1, 0).reshape(n_kv, n_group, *dist.shape)


def band_attention(q, k, v, max_dist, dist_scale, table):
    Bt, N, Hk, G, dh = q.shape
    nb = N // BLOCK
    qb = q.reshape(Bt, nb, BLOCK, Hk, G, dh)

    def band(x):
        xp = jnp.pad(x, ((0, 0), (BLOCK, 0), (0, 0), (0, 0))).reshape(Bt, nb + 1, BLOCK, Hk, dh)
        return jnp.concatenate([xp[:, :-1], xp[:, 1:]], axis=2)

    kb, vb = band(k), band(v)
    i = jnp.arange(BLOCK)[:, None]
    j = jnp.arange(2 * BLOCK)[None, :]
    dist = i + BLOCK - j
    key_idx = jnp.arange(nb)[:, None, None] * BLOCK + j[None] - BLOCK
    valid = (dist >= 0) & (dist <= max_dist) & (key_idx >= 0)
    bias = rel_bias(table, dist * dist_scale, Hk, G)
    logits = jnp.einsum('bnihgd,bnjhd->bnhgij', qb, kb).astype(jnp.float32) * ATTN_SCALE + bias
    logits = jnp.where(valid[None, :, None, None], logits, NEG_INF)
    lse = jax.nn.logsumexp(logits, axis=-1)
    p = jnp.exp(logits - lse[..., None])
    o = jnp.einsum('bnhgij,bnjhd->bnihgd', p.astype(v.dtype), vb).reshape(Bt, N, Hk, G, dh)
    lse = jnp.moveaxis(lse, -1, 2).reshape(Bt, N, Hk, G)
    return o, lse


def gather_attention(q, k_all, v_all, idx, dist, table):
    Hk, G = q.shape[2], q.shape[3]
    valid = idx >= 0
    safe = jnp.maximum(idx, 0)
    kg = k_all[:, safe]
    vg = v_all[:, safe]
    bias = rel_bias(table, dist, Hk, G)
    logits = jnp.einsum('bthgd,btmhd->bhgtm', q, kg).astype(jnp.float32) * ATTN_SCALE + bias
    logits = jnp.where(valid, logits, NEG_INF)
    lse = jax.nn.logsumexp(logits, axis=-1)
    p = jnp.exp(logits - lse[..., None])
    o = jnp.einsum('bhgtm,btmhd->bthgd', p.astype(vg.dtype), vg)
    return o, jnp.moveaxis(lse, -1, 1)


def apply_sinks(o, lse, sinks):
    s = sinks.astype(jnp.float32).reshape(A_KV_HEADS, A_GROUP)
    return o * jax.nn.sigmoid(lse - s)[..., None].astype(o.dtype)


def combine_branches(outs, lses):
    w = jax.nn.softmax(jnp.stack(lses), axis=0)
    o = jnp.sum(w[..., None] * jnp.stack(outs).astype(jnp.float32), axis=0)
    return o.astype(outs[0].dtype)


def to_strided(x, r, Sp):
    Bn, S = x.shape[:2]
    x = jnp.pad(x, ((0, 0), (0, Sp - S)) + ((0, 0),) * (x.ndim - 2))
    x = x.reshape(Bn, Sp // r, r, *x.shape[2:])
    return jnp.swapaxes(x, 1, 2).reshape(Bn * r, Sp // r, *x.shape[3:])


def from_strided(y, Bn, r, S):
    N, rest = y.shape[1], y.shape[2:]
    y = jnp.swapaxes(y.reshape(Bn, r, N, *rest), 1, 2)
    return y.reshape(Bn, N * r, *rest)[:, :S]


def dilated_prompt(q, k, v, table):
    Bn, S = q.shape[:2]
    outs, lses = [], []
    for w, r in B_BRANCHES:
        span = r * BLOCK
        Sp = -(-S // span) * span
        o, lse = band_attention(to_strided(q, r, Sp)[:, :, :, None], to_strided(k, r, Sp),
                                to_strided(v, r, Sp), w // r, r, table)
        outs.append(from_strided(o, Bn, r, S))
        lses.append(from_strided(lse, Bn, r, S))
    return combine_branches(outs, lses)


def dilated_sample(q, k_all, v_all, L, table):
    T = q.shape[1]
    t = jnp.arange(T)[:, None]
    outs, lses = [], []
    for w, r in B_BRANCHES:
        m = jnp.arange(w // r + 1)[None, :]
        idx = L + t - r * m
        dist = jnp.broadcast_to(r * m, idx.shape)
        o, lse = gather_attention(q[:, :, :, None], k_all, v_all, idx, dist, table)
        outs.append(o)
        lses.append(lse)
    return combine_branches(outs, lses)


def causal_conv(up, w):
    T = up.shape[1] - (CONV_WIDTH - 1)
    y = w[0] * up[:, 0:T]
    for j in range(1, CONV_WIDTH):
        y = y + w[j] * up[:, j:j + T]
    return y


def split_in(z):
    sizes = (A_Q_W, A_KV_W, A_KV_W, B_W, B_W, B_W, C_WIDTH, C_WIDTH, C_WIDTH)
    return jnp.split(z, np.cumsum(sizes)[:-1].tolist(), axis=-1)


def heads(x, n):
    return x.reshape(x.shape[0], x.shape[1], n, HEAD_DIM)


def mixer_prompt(h, w_in, conv_w, sinks, table):
    Bn, S = h.shape[:2]
    qa, ka, va, qb, kb, vb, xc, bg, cg = split_in(h @ w_in)
    qa = qa.reshape(Bn, S, A_KV_HEADS, A_GROUP, HEAD_DIM)
    ka, va = heads(ka, A_KV_HEADS), heads(va, A_KV_HEADS)
    oa, lse_a = band_attention(qa, ka, va, A_WINDOW, 1, table[:, :A_Q_HEADS])
    oa = apply_sinks(oa, lse_a, sinks)
    qb, kb, vb = heads(qb, B_HEADS), heads(kb, B_HEADS), heads(vb, B_HEADS)
    ob = dilated_prompt(qb, kb, vb, table[:, A_Q_HEADS:])
    up = jnp.pad(cg * xc, ((0, 0), (CONV_WIDTH - 1, 0), (0, 0)))
    oc = bg * causal_conv(up, conv_w)
    mix = jnp.concatenate([oa.reshape(Bn, S, A_Q_W), ob.reshape(Bn, S, B_W), oc], axis=-1)
    la, lb = min(A_WINDOW, S), min(B_SPAN, S)
    state = (ka[:, S - la:], va[:, S - la:], kb[:, S - lb:], vb[:, S - lb:], up[:, -(CONV_WIDTH - 1):])
    return mix, state


def mixer_sample(h, ca_k, ca_v, cb_k, cb_v, conv_state, w_in, conv_w, sinks, table):
    Bn, T = h.shape[:2]
    qa, ka, va, qb, kb, vb, xc, bg, cg = split_in(h @ w_in)
    qa = qa.reshape(Bn, T, A_KV_HEADS, A_GROUP, HEAD_DIM)
    ka, va = heads(ka, A_KV_HEADS), heads(va, A_KV_HEADS)
    La = ca_k.shape[1]
    t = jnp.arange(T)[:, None]
    m = jnp.arange(A_WINDOW + 1)[None, :]
    idx = La + t - m
    dist = jnp.broadcast_to(m, idx.shape)
    oa, lse_a = gather_attention(qa, jnp.concatenate([ca_k, ka], axis=1),
                                 jnp.concatenate([ca_v, va], axis=1), idx, dist, table[:, :A_Q_HEADS])
    oa = apply_sinks(oa, lse_a, sinks)
    qb, kb, vb = heads(qb, B_HEADS), heads(kb, B_HEADS), heads(vb, B_HEADS)
    ob = dilated_sample(qb, jnp.concatenate([cb_k, kb], axis=1), jnp.concatenate([cb_v, vb], axis=1),
                        cb_k.shape[1], table[:, A_Q_HEADS:])
    u = cg * xc
    up = jnp.concatenate([conv_state.astype(u.dtype), u], axis=1)
    oc = bg * causal_conv(up, conv_w)
    mix = jnp.concatenate([oa.reshape(Bn, T, A_Q_W), ob.reshape(Bn, T, B_W), oc], axis=-1)
    state = (ka, va, kb, vb, up[:, -(CONV_WIDTH - 1):])
    return mix, state


def hier_moe(h, w_group, b_group, w_router, b_router, w_gate, w_up, w_down):
    lead = h.shape[:-1]
    x = h.reshape(-1, D_MODEL)
    g_logits = (x @ w_group).astype(jnp.float32) + b_group.astype(jnp.float32)
    g_prob = jax.nn.softmax(g_logits, axis=-1)
    _, g_idx = lax.top_k(g_logits, 1)
    g_w = jnp.take_along_axis(g_prob, g_idx, axis=-1)
    e_logits = ((x @ w_router).astype(jnp.float32) + b_router.astype(jnp.float32)
                ).reshape(-1, N_GROUPS, EXPERTS_PER_GROUP)
    e_in = jnp.take_along_axis(e_logits, g_idx[:, :, None], axis=1)[:, 0]
    e_val, e_idx = lax.top_k(e_in, TOP_K)
    e_w = jax.nn.softmax(e_val, axis=-1) * g_w
    expert = g_idx * EXPERTS_PER_GROUP + e_idx
    combine = jnp.einsum('nk,nke->ne', e_w, jax.nn.one_hot(expert, N_EXPERTS, dtype=jnp.float32))
    hid = jax.nn.silu(jnp.einsum('nd,edf->nef', x, w_gate)) * jnp.einsum('nd,edf->nef', x, w_up)
    hid = hid * combine[..., None].astype(hid.dtype)
    y = jnp.einsum('nef,efd->nd', hid, w_down)
    return y.reshape(*lead, D_MODEL)


def setup_inputs(seed: int = 0) -> dict:
    key = jax.random.key(seed)
    ks = jax.random.split(key, 24)
    nrm = jax.random.normal
    La = min(A_WINDOW, PAST_LEN)
    Lb = min(B_SPAN, PAST_LEN)
    return {
        'x_prompt': nrm(ks[0], (BATCH, SEQ, D_MODEL), jnp.float32),
        'x_sample': nrm(ks[1], (DEC_BATCH, DEC_SEQ, D_MODEL), jnp.float32),
        'cache_a_k': nrm(ks[2], (DEPTH, DEC_BATCH, La, A_KV_HEADS, HEAD_DIM), jnp.float32),
        'cache_a_v': nrm(ks[3], (DEPTH, DEC_BATCH, La, A_KV_HEADS, HEAD_DIM), jnp.float32),
        'cache_b_k': nrm(ks[4], (DEPTH, DEC_BATCH, Lb, B_HEADS, HEAD_DIM), jnp.float32),
        'cache_b_v': nrm(ks[5], (DEPTH, DEC_BATCH, Lb, B_HEADS, HEAD_DIM), jnp.float32),
        'state_conv': nrm(ks[6], (DEPTH, DEC_BATCH, CONV_WIDTH - 1, C_WIDTH), jnp.float32),
        'rel_bias_table': 0.5 * nrm(ks[7], (N_BUCKETS, N_ATT_HEADS), jnp.float32),
        'w_in': nrm(ks[8], (DEPTH, D_MODEL, IN_WIDTH), jnp.float32) * D_MODEL ** -0.5,
        'w_out': nrm(ks[9], (DEPTH, MIX_WIDTH, D_MODEL), jnp.float32) * MIX_WIDTH ** -0.5,
        'conv_w': nrm(ks[10], (DEPTH, CONV_WIDTH, C_WIDTH), jnp.float32) * CONV_WIDTH ** -0.5,
        'attn_sinks': nrm(ks[11], (DEPTH, A_Q_HEADS), jnp.float32),
        'norm_mix': 1.0 + 0.05 * nrm(ks[12], (DEPTH, D_MODEL), jnp.float32),
        'norm_ffn': 1.0 + 0.05 * nrm(ks[13], (DEPTH, D_MODEL), jnp.float32),
        'w_group': nrm(ks[14], (DEPTH, D_MODEL, N_GROUPS), jnp.float32) * D_MODEL ** -0.5,
        'b_group': 0.01 * nrm(ks[15], (DEPTH, N_GROUPS), jnp.float32),
        'w_router': nrm(ks[16], (DEPTH, D_MODEL, N_EXPERTS), jnp.float32) * D_MODEL ** -0.5,
        'b_router': 0.01 * nrm(ks[17], (DEPTH, N_EXPERTS), jnp.float32),
        'w_gate': nrm(ks[18], (DEPTH, N_EXPERTS, D_MODEL, EXPERT_FF), jnp.float32) * D_MODEL ** -0.5,
        'w_up': nrm(ks[19], (DEPTH, N_EXPERTS, D_MODEL, EXPERT_FF), jnp.float32) * D_MODEL ** -0.5,
        'w_down': nrm(ks[20], (DEPTH, N_EXPERTS, EXPERT_FF, D_MODEL), jnp.float32) * EXPERT_FF ** -0.5,
        'norm_final': 1.0 + 0.05 * nrm(ks[21], (D_MODEL,), jnp.float32),
    }


def reference(x_prompt, x_sample, cache_a_k, cache_a_v, cache_b_k, cache_b_v, state_conv,
              rel_bias_table, w_in, w_out, conv_w, attn_sinks, norm_mix, norm_ffn,
              w_group, b_group, w_router, b_router, w_gate, w_up, w_down, norm_final):
    yp, ys = x_prompt, x_sample
    st_p = [[] for _ in range(5)]
    st_s = [[] for _ in range(5)]
    for l in range(DEPTH):
        mp, sp = mixer_prompt(rmsnorm(yp, norm_mix[l]), w_in[l], conv_w[l], attn_sinks[l], rel_bias_table)
        ms, ss = mixer_sample(rmsnorm(ys, norm_mix[l]), cache_a_k[l], cache_a_v[l], cache_b_k[l],
                              cache_b_v[l], state_conv[l], w_in[l], conv_w[l], attn_sinks[l], rel_bias_table)
        yp = yp + mp @ w_out[l]
        ys = ys + ms @ w_out[l]
        yp = yp + hier_moe(rmsnorm(yp, norm_ffn[l]), w_group[l], b_group[l], w_router[l], b_router[l],
                           w_gate[l], w_up[l], w_down[l])
        ys = ys + hier_moe(rmsnorm(ys, norm_ffn[l]), w_group[l], b_group[l], w_router[l], b_router[l],
                           w_gate[l], w_up[l], w_down[l])
        for i in range(5):
            st_p[i].append(sp[i])
            st_s[i].append(ss[i])
    y_prompt = rmsnorm(yp, norm_final)
    y_sample = rmsnorm(ys, norm_final)
    ak_p, av_p, bk_p, bv_p, cv_p = [jnp.stack(s) for s in st_p]
    ak_s, av_s, bk_s, bv_s, cv_s = [jnp.stack(s) for s in st_s]
    return (y_prompt, y_sample, ak_p, av_p, bk_p, bv_p, cv_p, ak_s, av_s, bk_s, bv_s, cv_s)
```

```python
import functools
import math

import numpy as np
import jax
import jax.numpy as jnp
from jax import lax
from jax.experimental import pallas as pl
from jax.experimental.pallas import tpu as pltpu

F32 = jnp.float32
BF16 = jnp.bfloat16

D_MODEL = 1024
HEAD_DIM = 64
ATTN_SCALE = HEAD_DIM ** -0.5
BLOCK = 128
LANES = 128
A_Q_HEADS = 4
A_KV_HEADS = 2
A_WINDOW = 128
B_HEADS = 6
B_BRANCHES = ((128, 1), (512, 4), (2048, 16))
DILATIONS = tuple(r for _, r in B_BRANCHES)
SPAN = BLOCK * max(DILATIONS)
C_WIDTH = 6 * HEAD_DIM
CONV_WIDTH = 3
A_Q_W = A_Q_HEADS * HEAD_DIM
A_KV_W = A_KV_HEADS * HEAD_DIM
B_W = B_HEADS * HEAD_DIM
IN_WIDTH = A_Q_W + 2 * A_KV_W + 3 * B_W + 3 * C_WIDTH
N_BUCKETS = 32
MAX_DISTANCE = 2048
N_GROUPS = 4
EXPERTS_PER_GROUP = 4
N_EXPERTS = N_GROUPS * EXPERTS_PER_GROUP
EXPERT_FF = 512
RMS_EPS = 1e-6
NEG_INF = -1e30

COL_CONV = 0
COL_KA = 3 * C_WIDTH
COL_QA = COL_KA + A_KV_W
COL_VA = COL_QA + A_Q_W
COL_QB = COL_VA + A_KV_W
COL_KB = COL_QB + B_W
COL_VB = COL_KB + B_W

ROW_TILE = 512
MOE_TILE = 256
PAIRS = ((0, 1), (0, 2), (0, 3), (1, 2), (1, 3), (2, 3))
N_ROUTE_BUCKETS = N_GROUPS * len(PAIRS)
ROUTE_ROWS = 32
VMEM_LIMIT = 56 * 1024 * 1024


def _params(*sem):
    return pltpu.CompilerParams(dimension_semantics=sem, vmem_limit_bytes=VMEM_LIMIT)


def _rel_bucket(dist):
    d = jnp.maximum(dist, 0)
    max_exact = N_BUCKETS // 2
    df = jnp.maximum(d, max_exact).astype(F32)
    large = max_exact + (jnp.log(df / max_exact) / math.log(MAX_DISTANCE / max_exact)
                         * (N_BUCKETS - max_exact)).astype(jnp.int32)
    large = jnp.minimum(large, N_BUCKETS - 1)
    return jnp.where(d < max_exact, d, large)


def _band_bias(table, scale):
    i = np.arange(BLOCK)[:, None]
    j = np.arange(BLOCK)[None, :]
    dist = np.stack([i + BLOCK - j, i - j])
    valid = (dist >= 0) & (dist <= BLOCK)
    by_dist = table.astype(F32)[_rel_bucket(jnp.arange(2 * BLOCK) * scale)]
    b = jnp.take(by_dist, jnp.asarray(np.clip(dist, 0, 2 * BLOCK - 1)), axis=0)
    b = jnp.where(jnp.asarray(valid)[..., None], b, NEG_INF)
    return jnp.moveaxis(b, -1, 0)


def _sample_bias(table, n_new, cache_len, window, dilation):
    t = np.arange(n_new)[:, None]
    c = np.arange(cache_len + LANES)[None, :]
    dist = cache_len + t - c
    valid = (dist >= 0) & (dist <= window) & (dist % dilation == 0) & (c < cache_len + n_new)
    b = table.astype(F32)[_rel_bucket(jnp.asarray(np.maximum(dist, 0)))]
    b = jnp.where(jnp.asarray(valid)[..., None], b, NEG_INF)
    return jnp.moveaxis(b, -1, 0)


def _inproj_kernel(x_ref, g_ref, w_ref, z_ref):
    x = x_ref[...]
    h = x * lax.rsqrt(jnp.mean(x * x, axis=-1, keepdims=True) + RMS_EPS) * g_ref[...]
    z_ref[...] = jnp.dot(h.astype(BF16), w_ref[...], preferred_element_type=F32)


def _inproj(x, gain, w):
    n = x.shape[0]
    return pl.pallas_call(
        _inproj_kernel,
        out_shape=jax.ShapeDtypeStruct((n, IN_WIDTH), F32),
        grid=(n // ROW_TILE,),
        in_specs=[pl.BlockSpec((ROW_TILE, D_MODEL), lambda i: (i, 0)),
                  pl.BlockSpec((1, D_MODEL), lambda i: (0, 0)),
                  pl.BlockSpec((D_MODEL, IN_WIDTH), lambda i: (0, 0))],
        out_specs=pl.BlockSpec((ROW_TILE, IN_WIDTH), lambda i: (i, 0)),
        compiler_params=_params("parallel"),
    )(x, gain.reshape(1, D_MODEL), w)


def _lane_half(shape):
    return lax.broadcasted_iota(jnp.int32, shape, len(shape) - 1) // HEAD_DIM


def _nt_dot(a, b):
    return lax.dot_general(a, b, (((1,), (1,)), ((), ())), preferred_element_type=F32)


def _band_tile(qt, kp, kc, vp, vc, bias_p, bias_c, k_half, penalty):
    keep = _lane_half(kp.shape) == k_half
    zero = jnp.zeros_like(kp)
    lp = _nt_dot(qt, jnp.where(keep, kp, zero)) + bias_p
    if penalty is not None:
        lp = lp + penalty
    lc = _nt_dot(qt, jnp.where(keep, kc, zero)) + bias_c
    m = jnp.maximum(jnp.max(lp, axis=-1, keepdims=True), jnp.max(lc, axis=-1, keepdims=True))
    pp = jnp.exp(lp - m)
    pc = jnp.exp(lc - m)
    s = jnp.sum(pp, axis=-1, keepdims=True) + jnp.sum(pc, axis=-1, keepdims=True)
    acc = (jnp.dot(pp.astype(BF16), vp, preferred_element_type=F32)
           + jnp.dot(pc.astype(BF16), vc, preferred_element_type=F32))
    return acc / s, m + jnp.log(s)


def _rows(start, dilation):
    if dilation == 1:
        return pl.ds(start, BLOCK)
    return pl.ds(start, BLOCK, stride=dilation)


def _dilated_kernel(q_ref, kc_ref, kp_ref, vc_ref, vp_ref, bias_ref, o_ref, o_scr, l_scr):
    penalty = jnp.where(pl.program_id(1) == 0, NEG_INF, 0.0).astype(F32)
    first_half = _lane_half((BLOCK, LANES)) == 0

    for bi, r in enumerate(DILATIONS):
        step = r * BLOCK

        def block(c, n, first, bi=bi, r=r, step=step):
            q_start = c + n * step
            if not isinstance(q_start, int):
                q_start = pl.multiple_of(q_start, BLOCK) if r == 1 else q_start
            cur = _rows(q_start, r)
            qt = (q_ref[cur, :] * ATTN_SCALE).astype(BF16)
            kc = kc_ref[cur, :].astype(BF16)
            vc = vc_ref[cur, :].astype(BF16)
            if first:
                prev = _rows(SPAN - step + c, r)
                kp = kp_ref[prev, :].astype(BF16)
                vp = vp_ref[prev, :].astype(BF16)
            else:
                p_start = q_start - step
                if r == 1:
                    p_start = pl.multiple_of(p_start, BLOCK)
                prev = _rows(p_start, r)
                kp = kc_ref[prev, :].astype(BF16)
                vp = vc_ref[prev, :].astype(BF16)
            outs = [_band_tile(qt, kp, kc, vp, vc, bias_ref[bi, h, 0], bias_ref[bi, h, 1], h,
                               penalty if first else None) for h in range(2)]
            o_scr[bi, cur, :] = jnp.where(first_half, outs[0][0], outs[1][0])
            l_scr[bi, cur, :] = jnp.where(first_half, outs[0][1], outs[1][1])

        n_blocks = SPAN // step
        if r == 1:
            block(0, 0, True)
            lax.fori_loop(1, n_blocks, lambda n, _: block(0, n, False), None)
        else:
            def per_class(c, _, block=block, n_blocks=n_blocks):
                block(c, 0, True)
                for n in range(1, n_blocks):
                    block(c, n, False)
            lax.fori_loop(0, r, per_class, None)

    chunk = 2 * BLOCK

    def combine(j, _):
        rows = pl.ds(pl.multiple_of(j * chunk, chunk), chunk)
        ls = [l_scr[bi, rows, :] for bi in range(len(DILATIONS))]
        m = functools.reduce(jnp.maximum, ls)
        ws = [jnp.exp(l - m) for l in ls]
        num = sum(w * o_scr[bi, rows, :] for bi, w in enumerate(ws))
        o_ref[rows, :] = (num / sum(ws)).astype(o_ref.dtype)

    lax.fori_loop(0, SPAN // chunk, combine, None)


def _dilated_attention(z, bias, n_seq, seq_len):
    spans = seq_len // SPAN
    qb, kb, vb = COL_QB // LANES, COL_KB // LANES, COL_VB // LANES
    blk = (SPAN, LANES)
    cur = lambda col: (lambda b, s, i: (b * spans + s, col + i))
    prev = lambda col: (lambda b, s, i: (b * spans + jnp.maximum(s - 1, 0), col + i))
    return pl.pallas_call(
        _dilated_kernel,
        out_shape=jax.ShapeDtypeStruct((n_seq * seq_len, B_W), BF16),
        grid=(n_seq, spans, B_W // LANES),
        in_specs=[pl.BlockSpec(blk, cur(qb)),
                  pl.BlockSpec(blk, cur(kb)), pl.BlockSpec(blk, prev(kb)),
                  pl.BlockSpec(blk, cur(vb)), pl.BlockSpec(blk, prev(vb)),
                  pl.BlockSpec((None, len(DILATIONS), 2, 2, BLOCK, BLOCK),
                               lambda b, s, i: (i, 0, 0, 0, 0, 0))],
        out_specs=pl.BlockSpec(blk, lambda b, s, i: (b * spans + s, i)),
        scratch_shapes=[pltpu.VMEM((len(DILATIONS), SPAN, LANES), F32),
                        pltpu.VMEM((len(DILATIONS), SPAN, LANES), F32)],
        compiler_params=_params("parallel", "parallel", "parallel"),
    )(z, z, z, z, z, bias)


def _window_kernel(sink_ref, q_ref, kc_ref, kp_ref, vc_ref, vp_ref, bias_ref, o_ref):
    penalty = jnp.where(pl.program_id(1) == 0, NEG_INF, 0.0).astype(F32)
    first_half = _lane_half((BLOCK, LANES)) == 0

    def block(n, first):
        start = n * BLOCK if isinstance(n, int) else pl.multiple_of(n * BLOCK, BLOCK)
        cur = pl.ds(start, BLOCK)
        if first:
            src_k, src_v, prev = kp_ref, vp_ref, pl.ds(SPAN - BLOCK, BLOCK)
        else:
            src_k, src_v, prev = kc_ref, vc_ref, pl.ds(pl.multiple_of(start - BLOCK, BLOCK), BLOCK)
        kc = kc_ref[cur, :].astype(BF16)
        kp = src_k[prev, :].astype(BF16)
        vcf = vc_ref[cur, :]
        vpf = src_v[prev, :]
        v_same = (vpf.astype(BF16), vcf.astype(BF16))
        v_swap = (pltpu.roll(vpf, HEAD_DIM, 1).astype(BF16), pltpu.roll(vcf, HEAD_DIM, 1).astype(BF16))
        for i in range(A_KV_HEADS):
            qf = q_ref[cur, i * LANES:(i + 1) * LANES] * ATTN_SCALE
            q_same = qf.astype(BF16)
            q_swap = pltpu.roll(qf, HEAD_DIM, 1).astype(BF16)
            outs = []
            for a in range(2):
                head = 2 * i + a
                qt = q_same if a == i else q_swap
                vp, vc = v_same if a == i else v_swap
                o, lse = _band_tile(qt, kp, kc, vp, vc, bias_ref[head, 0], bias_ref[head, 1], i,
                                    penalty if first else None)
                outs.append(o * jax.nn.sigmoid(lse - sink_ref[head]))
            o_ref[cur, i * LANES:(i + 1) * LANES] = jnp.where(first_half, outs[0], outs[1]).astype(o_ref.dtype)

    block(0, True)
    lax.fori_loop(1, SPAN // BLOCK, lambda n, _: block(n, False), None)


def _window_attention(z, bias, sinks, n_seq, seq_len):
    spans = seq_len // SPAN
    ka, va = COL_KA // LANES, COL_VA // LANES
    blk = (SPAN, LANES)
    cur = lambda col: (lambda b, s: (b * spans + s, col))
    prev = lambda col: (lambda b, s: (b * spans + jnp.maximum(s - 1, 0), col))
    return pl.pallas_call(
        _window_kernel,
        out_shape=jax.ShapeDtypeStruct((n_seq * seq_len, A_Q_W), BF16),
        grid=(n_seq, spans),
        in_specs=[pl.BlockSpec(memory_space=pltpu.SMEM),
                  pl.BlockSpec((SPAN, A_Q_W), lambda b, s: (b * spans + s, COL_QA // A_Q_W)),
                  pl.BlockSpec(blk, cur(ka)), pl.BlockSpec(blk, prev(ka)),
                  pl.BlockSpec(blk, cur(va)), pl.BlockSpec(blk, prev(va)),
                  pl.BlockSpec((A_Q_HEADS, 2, BLOCK, BLOCK), lambda b, s: (0, 0, 0, 0))],
        out_specs=pl.BlockSpec((SPAN, A_Q_W), lambda b, s: (b * spans + s, 0)),
        compiler_params=_params("parallel", "parallel"),
    )(sinks, z, z, z, z, z, bias)


def _shift_rows(u, filler, k):
    rolled = pltpu.roll(u, k, 0)
    row = lax.broadcasted_iota(jnp.int32, u.shape, 0)
    n_fill = filler.shape[0]
    for j in range(k):
        rolled = jnp.where(row == j, filler[n_fill - k + j:n_fill - k + j + 1, :], rolled)
    return rolled


def _gated_conv(xc, bg, cg, filler, cw):
    u = cg * xc
    conv = cw[0:1, :] * _shift_rows(u, filler, 2) + cw[1:2, :] * _shift_rows(u, filler, 1) + cw[2:3, :] * u
    return bg * conv, u


def _pad_rows(x, rows):
    return jnp.concatenate([x, jnp.zeros((rows - x.shape[0], x.shape[1]), x.dtype)], axis=0)


def _sample_kernel(z_ref, cak_ref, cav_ref, cbk_ref, cbv_ref, st_ref, cw_ref, sink_ref, bias_a_ref,
                   bias_b_ref, mix_ref, conv_ref, *, n_new):
    z = z_ref[...]
    la = cak_ref.shape[0]
    lb = cbk_ref.shape[0]

    ka_new = _pad_rows(z[:, COL_KA:COL_KA + A_KV_W], LANES).astype(BF16)
    va_new = _pad_rows(z[:, COL_VA:COL_VA + A_KV_W], LANES).astype(BF16)
    half = _lane_half((n_new, LANES))
    pieces = []
    for i in range(A_KV_HEADS):
        qf = z[:, COL_QA + i * LANES:COL_QA + (i + 1) * LANES] * ATTN_SCALE
        for a in range(2):
            pieces.append(jnp.where(half == i, qf if a == i else pltpu.roll(qf, HEAD_DIM, 1), 0.0))
    qa = jnp.concatenate(pieces, axis=0).astype(BF16)
    lc = _nt_dot(qa, cak_ref[...].astype(BF16)) + bias_a_ref[:, :la]
    ln = _nt_dot(qa, ka_new) + bias_a_ref[:, la:]
    m = jnp.maximum(jnp.max(lc, axis=-1, keepdims=True), jnp.max(ln, axis=-1, keepdims=True))
    pc = jnp.exp(lc - m)
    pn = jnp.exp(ln - m)
    s = jnp.sum(pc, axis=-1, keepdims=True) + jnp.sum(pn, axis=-1, keepdims=True)
    oa = (jnp.dot(pc.astype(BF16), cav_ref[...].astype(BF16), preferred_element_type=F32)
          + jnp.dot(pn.astype(BF16), va_new, preferred_element_type=F32))
    oa = oa / s * jax.nn.sigmoid(m + jnp.log(s) - sink_ref[...])
    oa_blocks = []
    for i in range(A_KV_HEADS):
        per_half = []
        for a in range(2):
            rows = oa[(2 * i + a) * n_new:(2 * i + a + 1) * n_new, :]
            per_half.append(rows if a == i else pltpu.roll(rows, HEAD_DIM, 1))
        oa_blocks.append(jnp.where(half == 0, per_half[0], per_half[1]))

    qf = z[:, COL_QB:COL_QB + B_W] * ATTN_SCALE
    head_of_lane = _lane_half((n_new, B_W))
    qb = jnp.concatenate([jnp.where(head_of_lane == h, qf, 0.0) for h in range(B_HEADS)], axis=0).astype(BF16)
    kb_new = _pad_rows(z[:, COL_KB:COL_KB + B_W], LANES).astype(BF16)
    vb_new = _pad_rows(z[:, COL_VB:COL_VB + B_W], LANES).astype(BF16)
    lg_c = _nt_dot(qb, cbk_ref[...].astype(BF16))
    lg_n = _nt_dot(qb, kb_new)
    parts = []
    for bi, (w, r) in enumerate(B_BRANCHES):
        lo = lb - min(lb, -(-w // LANES) * LANES)
        lc = lg_c[:, lo:] + bias_b_ref[bi, :, lo:lb]
        ln = lg_n + bias_b_ref[bi, :, lb:]
        m = jnp.maximum(jnp.max(lc, axis=-1, keepdims=True), jnp.max(ln, axis=-1, keepdims=True))
        pc = jnp.exp(lc - m)
        pn = jnp.exp(ln - m)
        s = jnp.sum(pc, axis=-1, keepdims=True) + jnp.sum(pn, axis=-1, keepdims=True)
        parts.append((lo, pc, pn, s, m + jnp.log(s)))
    m_all = functools.reduce(jnp.maximum, [p[4] for p in parts])
    ws = [jnp.exp(p[4] - m_all) for p in parts]
    den = sum(ws)
    p_new = None
    los = sorted({p[0] for p in parts} | {lb})
    segs = [None] * (len(los) - 1)
    for (lo, pc, pn, s, _), w in zip(parts, ws):
        coef = w / (den * s)
        p_new = coef * pn if p_new is None else p_new + coef * pn
        for si in range(len(segs)):
            a0, a1 = los[si], los[si + 1]
            if a0 >= lo:
                piece = coef * pc[:, a0 - lo:a1 - lo]
                segs[si] = piece if segs[si] is None else segs[si] + piece
    p_cache = jnp.concatenate(segs, axis=1) if len(segs) > 1 else segs[0]
    ob = (jnp.dot(p_cache.astype(BF16), cbv_ref[...].astype(BF16), preferred_element_type=F32)
          + jnp.dot(p_new.astype(BF16), vb_new, preferred_element_type=F32))
    ob_rows = sum(jnp.where(head_of_lane == h, ob[h * n_new:(h + 1) * n_new, :], 0.0) for h in range(B_HEADS))

    cz = z[:, COL_CONV:COL_CONV + 3 * C_WIDTH]
    oc, u = _gated_conv(cz[:, :C_WIDTH], cz[:, C_WIDTH:2 * C_WIDTH], cz[:, 2 * C_WIDTH:], st_ref[...], cw_ref[...])
    conv_ref[...] = u[n_new - (CONV_WIDTH - 1):, :]
    mix_ref[...] = jnp.concatenate(oa_blocks + [ob_rows, oc], axis=1)


def _sample_mixer(z, row0, cak, cav, cbk, cbv, state, cw, sink_rows, bias_a, bias_b, n_seq, n_new):
    la, lb = cak.shape[1], cbk.shape[1]
    blk0 = row0 // n_new
    per_seq = lambda shape: pl.BlockSpec((None,) + shape, lambda b: (b, 0, 0))
    const = lambda a: pl.BlockSpec(a.shape, lambda b: (0,) * a.ndim)
    return pl.pallas_call(
        functools.partial(_sample_kernel, n_new=n_new),
        out_shape=(jax.ShapeDtypeStruct((n_seq * n_new, D_MODEL), F32),
                   jax.ShapeDtypeStruct((n_seq, CONV_WIDTH - 1, C_WIDTH), F32)),
        grid=(n_seq,),
        in_specs=[pl.BlockSpec((n_new, IN_WIDTH), lambda b: (blk0 + b, 0)),
                  per_seq((la, A_KV_W)), per_seq((la, A_KV_W)), per_seq((lb, B_W)), per_seq((lb, B_W)),
                  per_seq((CONV_WIDTH - 1, C_WIDTH)), const(cw), const(sink_rows), const(bias_a), const(bias_b)],
        out_specs=(pl.BlockSpec((n_new, D_MODEL), lambda b: (b, 0)), per_seq((CONV_WIDTH - 1, C_WIDTH))),
        compiler_params=_params("parallel"),
    )(z, cak, cav, cbk, cbv, state, cw, sink_rows, bias_a, bias_b)


def _first_index(vals, best):
    idx = jnp.full(best.shape, len(vals) - 1, jnp.int32)
    for j in range(len(vals) - 2, -1, -1):
        idx = jnp.where(vals[j] == best, j, idx)
    return idx


def _route(lt):
    g = [lt[k:k + 1, :] for k in range(N_GROUPS)]
    g_max = functools.reduce(jnp.maximum, g)
    g_idx = _first_index(g, g_max)
    g_w = 1.0 / sum(jnp.exp(v - g_max) for v in g)
    e = []
    for j in range(EXPERTS_PER_GROUP):
        v = lt[N_GROUPS + j:N_GROUPS + j + 1, :]
        for gi in range(1, N_GROUPS):
            row = N_GROUPS + gi * EXPERTS_PER_GROUP + j
            v = jnp.where(g_idx == gi, lt[row:row + 1, :], v)
        e.append(v)
    e1 = functools.reduce(jnp.maximum, e)
    i1 = _first_index(e, e1)
    rest = [jnp.where(i1 == j, -jnp.inf, e[j]) for j in range(EXPERTS_PER_GROUP)]
    e2 = functools.reduce(jnp.maximum, rest)
    i2 = _first_index(rest, e2)
    t = jnp.exp(e2 - e1)
    w1 = g_w / (1.0 + t)
    w2 = g_w * t / (1.0 + t)
    swap = i2 < i1
    lo = jnp.where(swap, i2, i1)
    hi = jnp.where(swap, i1, i2)
    pair = jnp.where(lo == 0, hi - 1, jnp.where(lo == 1, hi + 1, len(PAIRS) - 1))
    bucket = g_idx * len(PAIRS) + pair
    return bucket, jnp.where(swap, w2, w1), jnp.where(swap, w1, w2)


def _outproj_kernel(y_ref, oa_ref, ob_ref, zc_ref, zh_ref, ms_ref, wout_ref, cw_ref, gn_ref, wr_ref, br_ref,
                    y1_ref, xn_ref, info_ref, cnt_ref, ut_ref, acc_scr, carry_scr, *, prompt_tiles, tiles_per_seq):
    i = pl.program_id(0)
    tile = y_ref.shape[0]

    @pl.when(i == 0)
    def _():
        carry_scr[...] = jnp.zeros_like(carry_scr)

    @pl.when(i < prompt_tiles)
    def _():
        zc = zc_ref[...]
        zh = zh_ref[...]
        halo = zh[:, 2 * C_WIDTH:] * zh[:, :C_WIDTH]
        halo = jnp.where(i % tiles_per_seq == 0, 0.0, halo)
        oc, u = _gated_conv(zc[:, :C_WIDTH], zc[:, C_WIDTH:2 * C_WIDTH], zc[:, 2 * C_WIDTH:], halo, cw_ref[...])
        ut_ref[...] = u[tile - 8:, :]
        acc_scr[...] = (
            jnp.dot(oa_ref[...], wout_ref[0:A_Q_W, :], preferred_element_type=F32)
            + jnp.dot(ob_ref[...], wout_ref[A_Q_W:A_Q_W + B_W, :], preferred_element_type=F32)
            + jnp.dot(oc.astype(BF16), wout_ref[A_Q_W + B_W:, :], preferred_element_type=F32))

    @pl.when(i >= prompt_tiles)
    def _():
        ut_ref[...] = jnp.zeros_like(ut_ref)
        acc_scr[...] = jnp.dot(ms_ref[...].astype(BF16), wout_ref[...], preferred_element_type=F32)

    y1 = y_ref[...] + acc_scr[...]
    y1_ref[...] = y1
    xn = y1 * lax.rsqrt(jnp.mean(y1 * y1, axis=-1, keepdims=True) + RMS_EPS) * gn_ref[...]
    xn_ref[...] = xn.astype(BF16)

    lt = lax.dot_general(wr_ref[...], xn, (((1,), (1,)), ((), ())), precision=lax.Precision.HIGHEST,
                         preferred_element_type=F32) + br_ref[...]
    bucket, w_lo, w_hi = _route(lt)
    onehot = (lax.broadcasted_iota(jnp.int32, (ROUTE_ROWS, tile), 0) == bucket).astype(F32)
    upper = (lax.broadcasted_iota(jnp.int32, (tile, tile), 0)
             <= lax.broadcasted_iota(jnp.int32, (tile, tile), 1)).astype(BF16)
    running = jnp.dot(onehot.astype(BF16), upper, preferred_element_type=F32)
    carry = carry_scr[...]
    rank = jnp.sum(onehot * (running - 1.0 + carry), axis=0, keepdims=True)
    carry = carry + jnp.sum(onehot, axis=1, keepdims=True)
    carry_scr[...] = carry
    cnt_ref[...] = jnp.broadcast_to(carry, cnt_ref.shape)
    info_ref[...] = jnp.concatenate(
        [bucket.astype(F32), rank, w_lo, w_hi, jnp.zeros((4, tile), F32)], axis=0)


def _outproj_route(y, oa, ob, z, mix_s, w_out, cw, gain, w_route, b_route, n_prompt, seq_len):
    n = y.shape[0]
    tiles = n // ROW_TILE
    p_tiles = n_prompt // ROW_TILE
    halo_blocks = ROW_TILE // 8
    conv_w = 3 * C_WIDTH
    pidx = lambda i: jnp.minimum(i, p_tiles - 1)
    const = lambda a: pl.BlockSpec(a.shape, lambda i: (0,) * a.ndim)
    gain = gain.reshape(1, D_MODEL)
    return pl.pallas_call(
        functools.partial(_outproj_kernel, prompt_tiles=p_tiles, tiles_per_seq=seq_len // ROW_TILE),
        out_shape=(jax.ShapeDtypeStruct((n, D_MODEL), F32),
                   jax.ShapeDtypeStruct((n, D_MODEL), BF16),
                   jax.ShapeDtypeStruct((8, n), F32),
                   jax.ShapeDtypeStruct((ROUTE_ROWS, LANES), F32),
                   jax.ShapeDtypeStruct((tiles * 8, C_WIDTH), F32)),
        grid=(tiles,),
        in_specs=[pl.BlockSpec((ROW_TILE, D_MODEL), lambda i: (i, 0)),
                  pl.BlockSpec((ROW_TILE, A_Q_W), lambda i: (pidx(i), 0)),
                  pl.BlockSpec((ROW_TILE, B_W), lambda i: (pidx(i), 0)),
                  pl.BlockSpec((ROW_TILE, conv_w), lambda i: (pidx(i), 0)),
                  pl.BlockSpec((8, conv_w), lambda i: (jnp.maximum(pidx(i) * halo_blocks - 1, 0), 0)),
                  pl.BlockSpec((ROW_TILE, D_MODEL), lambda i: (jnp.maximum(i - p_tiles, 0), 0)),
                  const(w_out), const(cw), const(gain), const(w_route), const(b_route)],
        out_specs=(pl.BlockSpec((ROW_TILE, D_MODEL), lambda i: (i, 0)),
                   pl.BlockSpec((ROW_TILE, D_MODEL), lambda i: (i, 0)),
                   pl.BlockSpec((8, ROW_TILE), lambda i: (0, i)),
                   pl.BlockSpec((ROUTE_ROWS, LANES), lambda i: (0, 0)),
                   pl.BlockSpec((8, C_WIDTH), lambda i: (i, 0))),
        scratch_shapes=[pltpu.VMEM((ROW_TILE, D_MODEL), F32), pltpu.VMEM((ROUTE_ROWS, 1), F32)],
        compiler_params=_params("arbitrary"),
    )(y, oa, ob, z, z, mix_s, w_out, cw, gain, w_route, b_route)


def _moe_kernel(e_lo_ref, e_hi_ref, used_ref, x_ref, w_ref, wg1, wu1, wd1, wg2, wu2, wd2, o_ref):
    t = pl.program_id(0)

    @pl.when(used_ref[t] > 0)
    def _():
        x = x_ref[...]
        w = w_ref[...]

        def expert(wg, wu, wd, scale):
            g = jnp.dot(x, wg[...], preferred_element_type=F32)
            u = jnp.dot(x, wu[...], preferred_element_type=F32)
            h = g * jax.nn.sigmoid(g) * u * scale
            return jnp.dot(h.astype(BF16), wd[...], preferred_element_type=F32)

        o_ref[...] = expert(wg1, wu1, wd1, w[:, 0:1]) + expert(wg2, wu2, wd2, w[:, 1:2])

    @pl.when(used_ref[t] == 0)
    def _():
        o_ref[...] = jnp.zeros_like(o_ref)


def _experts(xs, ws, e_lo, e_hi, used, w_gate, w_up, w_down, layer):
    n_tiles = xs.shape[0] // MOE_TILE
    up_spec = lambda sel: pl.BlockSpec((None, None, D_MODEL, EXPERT_FF),
                                       lambda t, lo, hi, u: (layer, (lo, hi)[sel][t], 0, 0))
    down_spec = lambda sel: pl.BlockSpec((None, None, EXPERT_FF, D_MODEL),
                                         lambda t, lo, hi, u: (layer, (lo, hi)[sel][t], 0, 0))
    grid_spec = pltpu.PrefetchScalarGridSpec(
        num_scalar_prefetch=3,
        grid=(n_tiles,),
        in_specs=[pl.BlockSpec((MOE_TILE, D_MODEL), lambda t, lo, hi, u: (t, 0)),
                  pl.BlockSpec((MOE_TILE, 2), lambda t, lo, hi, u: (t, 0)),
                  up_spec(0), up_spec(0), down_spec(0), up_spec(1), up_spec(1), down_spec(1)],
        out_specs=pl.BlockSpec((MOE_TILE, D_MODEL), lambda t, lo, hi, u: (t, 0)))
    return pl.pallas_call(
        _moe_kernel,
        out_shape=jax.ShapeDtypeStruct((xs.shape[0], D_MODEL), F32),
        grid_spec=grid_spec,
        compiler_params=_params("arbitrary"),
    )(e_lo, e_hi, used, xs, ws, w_gate, w_up, w_down, w_gate, w_up, w_down)


def _dispatch_plan(info, counts, n):
    n_tiles = -(-n // MOE_TILE) + N_ROUTE_BUCKETS
    bucket = info[0].astype(jnp.int32)
    rank = info[1].astype(jnp.int32)
    counts = counts[:N_ROUTE_BUCKETS, 0].astype(jnp.int32)
    tiles_per_bucket = (counts + MOE_TILE - 1) // MOE_TILE
    tile_end = jnp.cumsum(tiles_per_bucket)
    row_start = (tile_end - tiles_per_bucket) * MOE_TILE
    dest = row_start[bucket] + rank
    src = jnp.zeros((n_tiles * MOE_TILE,), jnp.int32).at[dest].set(jnp.arange(n, dtype=jnp.int32))
    tile_ids = jnp.arange(n_tiles, dtype=jnp.int32)
    tile_bucket = jnp.minimum(jnp.searchsorted(tile_end, tile_ids, side="right").astype(jnp.int32),
                              N_ROUTE_BUCKETS - 1)
    used = (tile_ids < tile_end[-1]).astype(jnp.int32)
    pair = tile_bucket % len(PAIRS)
    base = (tile_bucket // len(PAIRS)) * EXPERTS_PER_GROUP
    pairs = jnp.asarray(PAIRS, jnp.int32)
    return dest, src, base + pairs[pair, 0], base + pairs[pair, 1], used


def _norm_kernel(x_ref, g_ref, o_ref):
    x = x_ref[...]
    o_ref[...] = x * lax.rsqrt(jnp.mean(x * x, axis=-1, keepdims=True) + RMS_EPS) * g_ref[...]


def _final_norm(y, gain, row0, rows):
    blk0 = row0 // ROW_TILE
    return pl.pallas_call(
        _norm_kernel,
        out_shape=jax.ShapeDtypeStruct((rows, D_MODEL), F32),
        grid=(rows // ROW_TILE,),
        in_specs=[pl.BlockSpec((ROW_TILE, D_MODEL), lambda i: (blk0 + i, 0)),
                  pl.BlockSpec((1, D_MODEL), lambda i: (0, 0))],
        out_specs=pl.BlockSpec((ROW_TILE, D_MODEL), lambda i: (i, 0)),
        compiler_params=_params("parallel"),
    )(y, gain.reshape(1, D_MODEL))


def _permute_in_columns(w):
    attn = A_Q_W + 2 * A_KV_W + 3 * B_W
    return jnp.concatenate([w[:, attn:], w[:, A_Q_W:A_Q_W + A_KV_W], w[:, :A_Q_W], w[:, A_Q_W + A_KV_W:attn]],
                           axis=1)


def kernel(x_prompt, x_sample, cache_a_k, cache_a_v, cache_b_k, cache_b_v, state_conv, rel_bias_table,
           w_in, w_out, conv_w, attn_sinks, norm_mix, norm_ffn, w_group, b_group, w_router, b_router,
           w_gate, w_up, w_down, norm_final):
    n_seq, seq_len, _ = x_prompt.shape
    dec_seq, n_new, _ = x_sample.shape
    depth = w_in.shape[0]
    n_prompt = n_seq * seq_len
    n_sample = dec_seq * n_new
    n = n_prompt + n_sample
    la, lb = cache_a_k.shape[2], cache_b_k.shape[2]
    assert seq_len % SPAN == 0 and n_prompt % ROW_TILE == 0 and n_sample % ROW_TILE == 0
    assert seq_len >= SPAN and la % LANES == 0 and lb % LANES == 0 and n_new == 8

    table_a, table_b = rel_bias_table[:, :A_Q_HEADS], rel_bias_table[:, A_Q_HEADS:]
    bias_a = _band_bias(table_a, 1)
    bias_b = jnp.stack([_band_bias(table_b, r) for r in DILATIONS])
    bias_b = bias_b.reshape(len(DILATIONS), B_HEADS // 2, 2, 2, BLOCK, BLOCK).transpose(1, 0, 2, 3, 4, 5)
    sbias_a = _sample_bias(table_a, n_new, la, A_WINDOW, 1).reshape(A_Q_HEADS * n_new, la + LANES)
    sbias_b = jnp.stack([_sample_bias(table_b, n_new, lb, w, r).reshape(B_HEADS * n_new, lb + LANES)
                         for w, r in B_BRANCHES])

    w_in_b = jnp.stack([_permute_in_columns(w_in[l]) for l in range(depth)]).astype(BF16)
    w_out_b = w_out.astype(BF16)
    w_gate_b, w_up_b, w_down_b = w_gate.astype(BF16), w_up.astype(BF16), w_down.astype(BF16)
    pad = ROUTE_ROWS - N_GROUPS - N_EXPERTS
    w_route = jnp.pad(jnp.concatenate([w_group, w_router], axis=2).transpose(0, 2, 1), ((0, 0), (0, pad), (0, 0)))
    b_route = jnp.pad(jnp.concatenate([b_group, b_router], axis=1), ((0, 0), (0, pad)))[..., None]
    sink_rows = jnp.repeat(attn_sinks, n_new, axis=1)[..., None]

    y = jnp.concatenate([x_prompt.reshape(n_prompt, D_MODEL), x_sample.reshape(n_sample, D_MODEL)], axis=0)
    states = []
    for l in range(depth):
        z = _inproj(y, norm_mix[l], w_in_b[l])
        oa = _window_attention(z, bias_a, attn_sinks[l], n_seq, seq_len)
        ob = _dilated_attention(z, bias_b, n_seq, seq_len)
        mix_s, conv_s = _sample_mixer(
            z, n_prompt, cache_a_k[l].reshape(dec_seq, la, A_KV_W), cache_a_v[l].reshape(dec_seq, la, A_KV_W),
            cache_b_k[l].reshape(dec_seq, lb, B_W), cache_b_v[l].reshape(dec_seq, lb, B_W), state_conv[l],
            conv_w[l], sink_rows[l], sbias_a, sbias_b, dec_seq, n_new)
        y1, xn, info, counts, u_tail = _outproj_route(
            y, oa, ob, z, mix_s, w_out_b[l], conv_w[l], norm_ffn[l], w_route[l], b_route[l], n_prompt, seq_len)
        dest, src, e_lo, e_hi, used = _dispatch_plan(info, counts, n)
        moe = _experts(xn[src], info[2:4].T[src], e_lo, e_hi, used, w_gate_b, w_up_b, w_down_b, l)
        y = y1 + moe[dest]

        zp = z[:n_prompt].reshape(n_seq, seq_len, IN_WIDTH)
        zs = z[n_prompt:].reshape(dec_seq, n_new, IN_WIDTH)
        lap, lbp = min(A_WINDOW, seq_len), min(SPAN, seq_len)
        u_tail = u_tail[:n_prompt // ROW_TILE * 8].reshape(n_seq, seq_len // ROW_TILE, 8, C_WIDTH)
        states.append((
            zp[:, seq_len - lap:, COL_KA:COL_KA + A_KV_W].reshape(n_seq, lap, A_KV_HEADS, HEAD_DIM),
            zp[:, seq_len - lap:, COL_VA:COL_VA + A_KV_W].reshape(n_seq, lap, A_KV_HEADS, HEAD_DIM),
            zp[:, seq_len - lbp:, COL_KB:COL_KB + B_W].reshape(n_seq, lbp, B_HEADS, HEAD_DIM),
            zp[:, seq_len - lbp:, COL_VB:COL_VB + B_W].reshape(n_seq, lbp, B_HEADS, HEAD_DIM),
            u_tail[:, -1, 8 - (CONV_WIDTH - 1):, :],
            zs[:, :, COL_KA:COL_KA + A_KV_W].reshape(dec_seq, n_new, A_KV_HEADS, HEAD_DIM),
            zs[:, :, COL_VA:COL_VA + A_KV_W].reshape(dec_seq, n_new, A_KV_HEADS, HEAD_DIM),
            zs[:, :, COL_KB:COL_KB + B_W].reshape(dec_seq, n_new, B_HEADS, HEAD_DIM),
            zs[:, :, COL_VB:COL_VB + B_W].reshape(dec_seq, n_new, B_HEADS, HEAD_DIM),
            conv_s))

    y_prompt = _final_norm(y, norm_final, 0, n_prompt).reshape(n_seq, seq_len, D_MODEL)
    y_sample = _final_norm(y, norm_final, n_prompt, n_sample).reshape(dec_seq, n_new, D_MODEL)
    st = [jnp.stack([s[k] for s in states]) for k in range(10)]
    return (y_prompt, y_sample, st[0], st[1], st[2], st[3], st[4], st[5], st[6], st[7], st[8], st[9])
```

```python
import functools
import math

import numpy as np
import jax
import jax.numpy as jnp
from jax import lax
from jax.experimental import pallas as pl
from jax.experimental.pallas import tpu as pltpu

F32 = jnp.float32
BF16 = jnp.bfloat16

D_MODEL = 1024
HEAD_DIM = 64
ATTN_SCALE = HEAD_DIM ** -0.5
BLOCK = 128
LANES = 128
SUBLANES = 8
A_Q_HEADS = 4
A_KV_HEADS = 2
A_WINDOW = 128
B_HEADS = 6
B_BRANCHES = ((128, 1), (512, 4), (2048, 16))
DILATIONS = tuple(r for _, r in B_BRANCHES)
SPAN = BLOCK * max(DILATIONS)
C_WIDTH = 6 * HEAD_DIM
CONV_WIDTH = 3
A_Q_W = A_Q_HEADS * HEAD_DIM
A_KV_W = A_KV_HEADS * HEAD_DIM
B_W = B_HEADS * HEAD_DIM
IN_WIDTH = A_Q_W + 2 * A_KV_W + 3 * B_W + 3 * C_WIDTH
N_BUCKETS = 32
MAX_DISTANCE = 2048
N_GROUPS = 4
EXPERTS_PER_GROUP = 4
N_EXPERTS = N_GROUPS * EXPERTS_PER_GROUP
EXPERT_FF = 512
RMS_EPS = 1e-6
NEG_INF = -1e30

COL_CONV = 0
COL_KA = 3 * C_WIDTH
COL_QA = COL_KA + A_KV_W
COL_VA = COL_QA + A_Q_W
COL_QB = COL_VA + A_KV_W
COL_KB = COL_QB + B_W
COL_VB = COL_KB + B_W

ROW_TILE = 512
MOE_TILE = 256
BLOCKS_IN_FLIGHT = 3
PAIRS = ((0, 1), (0, 2), (0, 3), (1, 2), (1, 3), (2, 3))
N_ROUTE_BUCKETS = N_GROUPS * len(PAIRS)
ROUTE_ROWS = 32
ROW_EXT = D_MODEL + LANES
VMEM_LIMIT = 56 * 1024 * 1024


def _params(*sem):
    return pltpu.CompilerParams(dimension_semantics=sem, vmem_limit_bytes=VMEM_LIMIT)


def _rel_bucket(dist):
    d = np.maximum(dist, 0)
    max_exact = N_BUCKETS // 2
    df = np.maximum(d, max_exact).astype(np.float32)
    large = max_exact + (np.log(df / max_exact) / math.log(MAX_DISTANCE / max_exact)
                         * (N_BUCKETS - max_exact)).astype(np.int32)
    large = np.minimum(large, N_BUCKETS - 1)
    return np.where(d < max_exact, d, large)


def _masked_bias(table, dist, valid):
    bucket = _rel_bucket(dist).reshape(-1, 1)
    onehot = (jnp.asarray(bucket, jnp.int32) == jnp.arange(N_BUCKETS, dtype=jnp.int32)[None, :]).astype(F32)
    b = jnp.dot(onehot, table.astype(F32), precision=lax.Precision.HIGHEST)
    b = jnp.where(jnp.asarray(valid.reshape(-1, 1)), b, NEG_INF)
    return jnp.moveaxis(b.reshape(dist.shape + (table.shape[1],)), -1, 0)


def _band_bias(table, scale):
    i = np.arange(BLOCK)[:, None]
    j = np.arange(BLOCK)[None, :]
    dist = np.stack([i + BLOCK - j, i - j])
    return _masked_bias(table, dist * scale, (dist >= 0) & (dist <= BLOCK))


def _sample_bias(table, n_new, first_new, positions, window, dilation):
    dist = first_new + np.arange(n_new)[:, None] - np.asarray(positions)[None, :]
    valid = (dist >= 0) & (dist <= window) & (dist % dilation == 0)
    return _masked_bias(table, dist, valid)


def _own_head_only(b):
    h, t, p = b.shape
    same = jnp.asarray(np.arange(h)[:, None] == np.arange(SUBLANES)[None, :])
    return jnp.where(same[:, None, None, :], b[..., None], NEG_INF).reshape(h * t, p * SUBLANES)


def _inproj_kernel(x_ref, g_ref, w_ref, z_ref):
    x = x_ref[...]
    h = x * lax.rsqrt(jnp.mean(x * x, axis=-1, keepdims=True) + RMS_EPS) * g_ref[...]
    z_ref[...] = jnp.dot(h.astype(BF16), w_ref[...], preferred_element_type=F32)


def _inproj(x, gain, w):
    n = x.shape[0]
    return pl.pallas_call(
        _inproj_kernel,
        out_shape=jax.ShapeDtypeStruct((n, IN_WIDTH), F32),
        grid=(n // ROW_TILE,),
        in_specs=[pl.BlockSpec((ROW_TILE, D_MODEL), lambda i: (i, 0)),
                  pl.BlockSpec((1, D_MODEL), lambda i: (0, 0)),
                  pl.BlockSpec((D_MODEL, IN_WIDTH), lambda i: (0, 0))],
        out_specs=pl.BlockSpec((ROW_TILE, IN_WIDTH), lambda i: (i, 0)),
        compiler_params=_params("parallel"),
    )(x, gain.reshape(1, D_MODEL), w)


def _lane_half(shape):
    return lax.broadcasted_iota(jnp.int32, shape, len(shape) - 1) // HEAD_DIM


def _nt_dot(a, b):
    return lax.dot_general(a, b, (((1,), (1,)), ((), ())), preferred_element_type=F32)


def _band_tile(qt, kp, kc, vp, vc, bias_p, bias_c, k_half, penalty):
    keep = _lane_half(kp.shape) == k_half
    zero = jnp.zeros_like(kp)
    lp = _nt_dot(qt, jnp.where(keep, kp, zero)) + bias_p
    if penalty is not None:
        lp = lp + penalty
    lc = _nt_dot(qt, jnp.where(keep, kc, zero)) + bias_c
    m = jnp.maximum(jnp.max(lp, axis=-1, keepdims=True), jnp.max(lc, axis=-1, keepdims=True))
    pp = jnp.exp(lp - m)
    pc = jnp.exp(lc - m)
    s = jnp.sum(pp, axis=-1, keepdims=True) + jnp.sum(pc, axis=-1, keepdims=True)
    acc = (jnp.dot(pp.astype(BF16), vp, preferred_element_type=F32)
           + jnp.dot(pc.astype(BF16), vc, preferred_element_type=F32))
    return acc / s, m + jnp.log(s)


def _rows(start, dilation):
    if dilation == 1:
        return pl.ds(start, BLOCK)
    return pl.ds(start, BLOCK, stride=dilation)


def _dilated_kernel(q_ref, kc_ref, kp_ref, vc_ref, vp_ref, bias_ref, o_ref, o_scr, l_scr):
    penalty = jnp.where(pl.program_id(1) == 0, NEG_INF, 0.0).astype(F32)
    first_half = _lane_half((BLOCK, LANES)) == 0

    for bi, r in enumerate(DILATIONS):
        step = r * BLOCK

        def block(c, n, first, bi=bi, r=r, step=step):
            q_start = c + n * step
            if not isinstance(q_start, int):
                q_start = pl.multiple_of(q_start, BLOCK) if r == 1 else q_start
            cur = _rows(q_start, r)
            qt = (q_ref[cur, :] * ATTN_SCALE).astype(BF16)
            kc = kc_ref[cur, :].astype(BF16)
            vc = vc_ref[cur, :].astype(BF16)
            if first:
                prev = _rows(SPAN - step + c, r)
                kp = kp_ref[prev, :].astype(BF16)
                vp = vp_ref[prev, :].astype(BF16)
            else:
                p_start = q_start - step
                if r == 1:
                    p_start = pl.multiple_of(p_start, BLOCK)
                prev = _rows(p_start, r)
                kp = kc_ref[prev, :].astype(BF16)
                vp = vc_ref[prev, :].astype(BF16)
            outs = [_band_tile(qt, kp, kc, vp, vc, bias_ref[bi, h, 0], bias_ref[bi, h, 1], h,
                               penalty if first else None) for h in range(2)]
            o_scr[bi, cur, :] = jnp.where(first_half, outs[0][0], outs[1][0])
            l_scr[bi, cur, :] = jnp.where(first_half, outs[0][1], outs[1][1])

        n_blocks = SPAN // step
        if r == 1:
            block(0, 0, True)
            lax.fori_loop(1, n_blocks, lambda n, _: block(0, n, False), None, unroll=BLOCKS_IN_FLIGHT)
        else:
            def per_class(c, _, block=block, n_blocks=n_blocks):
                block(c, 0, True)
                for n in range(1, n_blocks):
                    block(c, n, False)
            lax.fori_loop(0, r, per_class, None, unroll=max(1, BLOCKS_IN_FLIGHT // n_blocks))

    chunk = 2 * BLOCK

    def combine(j, _):
        rows = pl.ds(pl.multiple_of(j * chunk, chunk), chunk)
        ls = [l_scr[bi, rows, :] for bi in range(len(DILATIONS))]
        m = functools.reduce(jnp.maximum, ls)
        ws = [jnp.exp(l - m) for l in ls]
        num = sum(w * o_scr[bi, rows, :] for bi, w in enumerate(ws))
        o_ref[rows, :] = (num / sum(ws)).astype(o_ref.dtype)

    lax.fori_loop(0, SPAN // chunk, combine, None)


def _dilated_attention(z, bias, n_seq, seq_len):
    spans = seq_len // SPAN
    qb, kb, vb = COL_QB // LANES, COL_KB // LANES, COL_VB // LANES
    blk = (SPAN, LANES)
    cur = lambda col: (lambda b, s, i: (b * spans + s, col + i))
    prev = lambda col: (lambda b, s, i: (b * spans + jnp.maximum(s - 1, 0), col + i))
    return pl.pallas_call(
        _dilated_kernel,
        out_shape=jax.ShapeDtypeStruct((n_seq * seq_len, B_W), BF16),
        grid=(n_seq, spans, B_W // LANES),
        in_specs=[pl.BlockSpec(blk, cur(qb)),
                  pl.BlockSpec(blk, cur(kb)), pl.BlockSpec(blk, prev(kb)),
                  pl.BlockSpec(blk, cur(vb)), pl.BlockSpec(blk, prev(vb)),
                  pl.BlockSpec((None, len(DILATIONS), 2, 2, BLOCK, BLOCK),
                               lambda b, s, i: (i, 0, 0, 0, 0, 0))],
        out_specs=pl.BlockSpec(blk, lambda b, s, i: (b * spans + s, i)),
        scratch_shapes=[pltpu.VMEM((len(DILATIONS), SPAN, LANES), F32),
                        pltpu.VMEM((len(DILATIONS), SPAN, LANES), F32)],
        compiler_params=_params("parallel", "parallel", "parallel"),
    )(z, z, z, z, z, bias)


def _window_kernel(sink_ref, q_ref, kc_ref, kp_ref, vc_ref, vp_ref, bias_ref, o_ref):
    penalty = jnp.where(pl.program_id(1) == 0, NEG_INF, 0.0).astype(F32)
    first_half = _lane_half((BLOCK, LANES)) == 0

    def block(n, first):
        start = n * BLOCK if isinstance(n, int) else pl.multiple_of(n * BLOCK, BLOCK)
        cur = pl.ds(start, BLOCK)
        if first:
            src_k, src_v, prev = kp_ref, vp_ref, pl.ds(SPAN - BLOCK, BLOCK)
        else:
            src_k, src_v, prev = kc_ref, vc_ref, pl.ds(pl.multiple_of(start - BLOCK, BLOCK), BLOCK)
        kc = kc_ref[cur, :].astype(BF16)
        kp = src_k[prev, :].astype(BF16)
        vcf = vc_ref[cur, :]
        vpf = src_v[prev, :]
        v_same = (vpf.astype(BF16), vcf.astype(BF16))
        v_swap = (pltpu.roll(vpf, HEAD_DIM, 1).astype(BF16), pltpu.roll(vcf, HEAD_DIM, 1).astype(BF16))
        for i in range(A_KV_HEADS):
            qf = q_ref[cur, i * LANES:(i + 1) * LANES] * ATTN_SCALE
            q_same = qf.astype(BF16)
            q_swap = pltpu.roll(qf, HEAD_DIM, 1).astype(BF16)
            outs = []
            for a in range(2):
                head = 2 * i + a
                qt = q_same if a == i else q_swap
                vp, vc = v_same if a == i else v_swap
                o, lse = _band_tile(qt, kp, kc, vp, vc, bias_ref[head, 0], bias_ref[head, 1], i,
                                    penalty if first else None)
                outs.append(o * jax.nn.sigmoid(lse - sink_ref[head]))
            o_ref[cur, i * LANES:(i + 1) * LANES] = jnp.where(first_half, outs[0], outs[1]).astype(o_ref.dtype)

    block(0, True)
    lax.fori_loop(1, SPAN // BLOCK, lambda n, _: block(n, False), None, unroll=BLOCKS_IN_FLIGHT)


def _window_attention(z, bias, sinks, n_seq, seq_len):
    spans = seq_len // SPAN
    ka, va = COL_KA // LANES, COL_VA // LANES
    blk = (SPAN, LANES)
    cur = lambda col: (lambda b, s: (b * spans + s, col))
    prev = lambda col: (lambda b, s: (b * spans + jnp.maximum(s - 1, 0), col))
    return pl.pallas_call(
        _window_kernel,
        out_shape=jax.ShapeDtypeStruct((n_seq * seq_len, A_Q_W), BF16),
        grid=(n_seq, spans),
        in_specs=[pl.BlockSpec(memory_space=pltpu.SMEM),
                  pl.BlockSpec((SPAN, A_Q_W), lambda b, s: (b * spans + s, COL_QA // A_Q_W)),
                  pl.BlockSpec(blk, cur(ka)), pl.BlockSpec(blk, prev(ka)),
                  pl.BlockSpec(blk, cur(va)), pl.BlockSpec(blk, prev(va)),
                  pl.BlockSpec((A_Q_HEADS, 2, BLOCK, BLOCK), lambda b, s: (0, 0, 0, 0))],
        out_specs=pl.BlockSpec((SPAN, A_Q_W), lambda b, s: (b * spans + s, 0)),
        compiler_params=_params("parallel", "parallel"),
    )(sinks, z, z, z, z, z, bias)


def _shift_rows(u, filler, k):
    rolled = pltpu.roll(u, k, 0)
    row = lax.broadcasted_iota(jnp.int32, u.shape, 0)
    n_fill = filler.shape[0]
    for j in range(k):
        rolled = jnp.where(row == j, filler[n_fill - k + j:n_fill - k + j + 1, :], rolled)
    return rolled


def _gated_conv(xc, bg, cg, filler, cw):
    u = cg * xc
    conv = cw[0:1, :] * _shift_rows(u, filler, 2) + cw[1:2, :] * _shift_rows(u, filler, 1) + cw[2:3, :] * u
    return bg * conv, u


def _pad_rows(x, rows):
    return jnp.concatenate([x, jnp.zeros((rows - x.shape[0], x.shape[1]), x.dtype)], axis=0)


def _flat_heads(ref):
    s, c, h, d = ref.shape
    x = ref[...].reshape(s * c, h, d)
    x = jnp.concatenate([x, jnp.zeros((s * c, SUBLANES - h, d), x.dtype)], axis=1)
    return x.reshape(s * c * SUBLANES, d).astype(BF16)


def _sample_kernel(z_ref, cak_ref, cav_ref, kf_ref, kn_ref, vf_ref, vn_ref, st_ref, cw_ref, sink_ref, bias_a_ref,
                   bias_f_ref, bias_n_ref, bias_new_ref, mix_ref, conv_ref, *, n_new):
    z = z_ref[...]
    la = cak_ref.shape[0]

    ka_new = _pad_rows(z[:, COL_KA:COL_KA + A_KV_W], LANES).astype(BF16)
    va_new = _pad_rows(z[:, COL_VA:COL_VA + A_KV_W], LANES).astype(BF16)
    half = _lane_half((n_new, LANES))
    pieces = []
    for i in range(A_KV_HEADS):
        qf = z[:, COL_QA + i * LANES:COL_QA + (i + 1) * LANES] * ATTN_SCALE
        for a in range(2):
            pieces.append(jnp.where(half == i, qf if a == i else pltpu.roll(qf, HEAD_DIM, 1), 0.0))
    qa = jnp.concatenate(pieces, axis=0).astype(BF16)
    lc = _nt_dot(qa, cak_ref[...].astype(BF16)) + bias_a_ref[:, :la]
    ln = _nt_dot(qa, ka_new) + bias_a_ref[:, la:]
    m = jnp.maximum(jnp.max(lc, axis=-1, keepdims=True), jnp.max(ln, axis=-1, keepdims=True))
    pc = jnp.exp(lc - m)
    pn = jnp.exp(ln - m)
    s = jnp.sum(pc, axis=-1, keepdims=True) + jnp.sum(pn, axis=-1, keepdims=True)
    oa = (jnp.dot(pc.astype(BF16), cav_ref[...].astype(BF16), preferred_element_type=F32)
          + jnp.dot(pn.astype(BF16), va_new, preferred_element_type=F32))
    oa = oa / s * jax.nn.sigmoid(m + jnp.log(s) - sink_ref[...])
    oa_blocks = []
    for i in range(A_KV_HEADS):
        per_half = []
        for a in range(2):
            rows = oa[(2 * i + a) * n_new:(2 * i + a + 1) * n_new, :]
            per_half.append(rows if a == i else pltpu.roll(rows, HEAD_DIM, 1))
        oa_blocks.append(jnp.where(half == 0, per_half[0], per_half[1]))

    qf = z[:, COL_QB:COL_QB + B_W] * ATTN_SCALE
    head_of_lane = _lane_half((n_new, B_W))
    qb = jnp.concatenate([jnp.where(head_of_lane == h, qf, 0.0) for h in range(B_HEADS)], axis=0).astype(BF16)
    q_heads = []
    for h in range(B_HEADS):
        blk = qf[:, (h // 2) * LANES:(h // 2 + 1) * LANES]
        q_heads.append((blk if h % 2 == 0 else pltpu.roll(blk, HEAD_DIM, 1))[:, :HEAD_DIM])
    q_rows = jnp.concatenate(q_heads, axis=0).astype(BF16)
    kb_new = _pad_rows(z[:, COL_KB:COL_KB + B_W], LANES).astype(BF16)
    vb_new = _pad_rows(z[:, COL_VB:COL_VB + B_W], LANES).astype(BF16)
    lg_far = _nt_dot(q_rows, _flat_heads(kf_ref))
    lg_near = _nt_dot(q_rows, _flat_heads(kn_ref))
    lg_new = _nt_dot(qb, kb_new)
    near_cols = lg_near.shape[1]
    near_len = near_cols // SUBLANES
    parts = []
    for bi, (w, r) in enumerate(B_BRANCHES):
        lo = (near_len - min(near_len, w)) * SUBLANES
        logits = [lg_near[:, lo:] + bias_n_ref[bi, :, lo:], lg_new + bias_new_ref[bi]]
        if w > near_len:
            logits.append(lg_far + bias_f_ref[...])
        m = functools.reduce(jnp.maximum, [jnp.max(l, axis=-1, keepdims=True) for l in logits])
        ps = [jnp.exp(l - m) for l in logits]
        s = sum(jnp.sum(p, axis=-1, keepdims=True) for p in ps)
        parts.append((lo, ps, s, m + jnp.log(s)))
    m_all = functools.reduce(jnp.maximum, [p[3] for p in parts])
    ws = [jnp.exp(p[3] - m_all) for p in parts]
    den = sum(ws)
    p_new = p_far = None
    los = sorted({p[0] for p in parts} | {near_cols})
    segs = [None] * (len(los) - 1)
    for (lo, ps, s, _), w in zip(parts, ws):
        coef = w / (den * s)
        p_new = coef * ps[1] if p_new is None else p_new + coef * ps[1]
        if len(ps) > 2:
            p_far = coef * ps[2] if p_far is None else p_far + coef * ps[2]
        for si in range(len(segs)):
            a0, a1 = los[si], los[si + 1]
            if a0 >= lo:
                piece = coef * ps[0][:, a0 - lo:a1 - lo]
                segs[si] = piece if segs[si] is None else segs[si] + piece
    p_near = jnp.concatenate(segs, axis=1) if len(segs) > 1 else segs[0]
    o_cache = (jnp.dot(p_near.astype(BF16), _flat_heads(vn_ref), preferred_element_type=F32)
               + jnp.dot(p_far.astype(BF16), _flat_heads(vf_ref), preferred_element_type=F32))
    o_new = jnp.dot(p_new.astype(BF16), vb_new, preferred_element_type=F32)
    ob_rows = jnp.concatenate([o_cache[h * n_new:(h + 1) * n_new, :] for h in range(B_HEADS)], axis=1)
    ob_rows = ob_rows + sum(jnp.where(head_of_lane == h, o_new[h * n_new:(h + 1) * n_new, :], 0.0)
                            for h in range(B_HEADS))

    cz = z[:, COL_CONV:COL_CONV + 3 * C_WIDTH]
    oc, u = _gated_conv(cz[:, :C_WIDTH], cz[:, C_WIDTH:2 * C_WIDTH], cz[:, 2 * C_WIDTH:], st_ref[...], cw_ref[...])
    conv_ref[...] = u[n_new - (CONV_WIDTH - 1):, :]
    mix_ref[...] = jnp.concatenate(oa_blocks + [ob_rows, oc], axis=1)


def _sample_mixer(z, row0, cak, cav, cbk, cbv, layer, far_steps, near_steps, state, cw, sink_rows, bias_a,
                  bias_far, bias_near, bias_new, n_new):
    n_seq, la = cak.shape[0], cak.shape[1]
    n_res = cbk.shape[3]
    blk0 = row0 // n_new
    per_seq = lambda shape: pl.BlockSpec((None,) + shape, lambda b: (b, 0, 0))
    const = lambda a: pl.BlockSpec(a.shape, lambda b: (0,) * a.ndim)
    far = pl.BlockSpec((None, None, far_steps, n_new, B_HEADS, HEAD_DIM), lambda b: (layer, b, 0, 0, 0, 0))
    near = pl.BlockSpec((None, None, near_steps, n_res, B_HEADS, HEAD_DIM),
                        lambda b: (layer, b, far_steps // near_steps, 0, 0, 0))
    return pl.pallas_call(
        functools.partial(_sample_kernel, n_new=n_new),
        out_shape=(jax.ShapeDtypeStruct((n_seq * n_new, D_MODEL), F32),
                   jax.ShapeDtypeStruct((n_seq, CONV_WIDTH - 1, C_WIDTH), F32)),
        grid=(n_seq,),
        in_specs=[pl.BlockSpec((n_new, IN_WIDTH), lambda b: (blk0 + b, 0)),
                  per_seq((la, A_KV_W)), per_seq((la, A_KV_W)), far, near, far, near,
                  per_seq((CONV_WIDTH - 1, C_WIDTH)), const(cw), const(sink_rows), const(bias_a),
                  const(bias_far), const(bias_near), const(bias_new)],
        out_specs=(pl.BlockSpec((n_new, D_MODEL), lambda b: (b, 0)), per_seq((CONV_WIDTH - 1, C_WIDTH))),
        compiler_params=_params("parallel"),
    )(z, cak, cav, cbk, cbk, cbv, cbv, state, cw, sink_rows, bias_a, bias_far, bias_near, bias_new)


def _first_index(vals, best):
    idx = jnp.full(best.shape, len(vals) - 1, jnp.int32)
    for j in range(len(vals) - 2, -1, -1):
        idx = jnp.where(vals[j] == best, j, idx)
    return idx


def _route(lt):
    g = [lt[k:k + 1, :] for k in range(N_GROUPS)]
    g_max = functools.reduce(jnp.maximum, g)
    g_idx = _first_index(g, g_max)
    g_w = 1.0 / sum(jnp.exp(v - g_max) for v in g)
    e = []
    for j in range(EXPERTS_PER_GROUP):
        v = lt[N_GROUPS + j:N_GROUPS + j + 1, :]
        for gi in range(1, N_GROUPS):
            row = N_GROUPS + gi * EXPERTS_PER_GROUP + j
            v = jnp.where(g_idx == gi, lt[row:row + 1, :], v)
        e.append(v)
    e1 = functools.reduce(jnp.maximum, e)
    i1 = _first_index(e, e1)
    rest = [jnp.where(i1 == j, -jnp.inf, e[j]) for j in range(EXPERTS_PER_GROUP)]
    e2 = functools.reduce(jnp.maximum, rest)
    i2 = _first_index(rest, e2)
    t = jnp.exp(e2 - e1)
    w1 = g_w / (1.0 + t)
    w2 = g_w * t / (1.0 + t)
    swap = i2 < i1
    lo = jnp.where(swap, i2, i1)
    hi = jnp.where(swap, i1, i2)
    pair = jnp.where(lo == 0, hi - 1, jnp.where(lo == 1, hi + 1, len(PAIRS) - 1))
    bucket = g_idx * len(PAIRS) + pair
    return bucket, jnp.where(swap, w2, w1), jnp.where(swap, w1, w2)


def _outproj_kernel(y_ref, oa_ref, ob_ref, zc_ref, zh_ref, ms_ref, wout_ref, cw_ref, gn_ref, wr_ref, br_ref,
                    y1_ref, info_ref, cnt_ref, ut_ref, acc_scr, carry_scr, *, prompt_tiles, tiles_per_seq):
    i = pl.program_id(0)
    tile = y_ref.shape[0]

    @pl.when(i == 0)
    def _():
        carry_scr[...] = jnp.zeros_like(carry_scr)

    @pl.when(i < prompt_tiles)
    def _():
        zc = zc_ref[...]
        zh = zh_ref[...]
        halo = zh[:, 2 * C_WIDTH:] * zh[:, :C_WIDTH]
        halo = jnp.where(i % tiles_per_seq == 0, 0.0, halo)
        oc, u = _gated_conv(zc[:, :C_WIDTH], zc[:, C_WIDTH:2 * C_WIDTH], zc[:, 2 * C_WIDTH:], halo, cw_ref[...])
        ut_ref[...] = u[tile - 8:, :]
        acc_scr[...] = (
            jnp.dot(oa_ref[...], wout_ref[0:A_Q_W, :], preferred_element_type=F32)
            + jnp.dot(ob_ref[...], wout_ref[A_Q_W:A_Q_W + B_W, :], preferred_element_type=F32)
            + jnp.dot(oc.astype(BF16), wout_ref[A_Q_W + B_W:, :], preferred_element_type=F32))

    @pl.when(i >= prompt_tiles)
    def _():
        ut_ref[...] = jnp.zeros_like(ut_ref)
        acc_scr[...] = jnp.dot(ms_ref[...].astype(BF16), wout_ref[...], preferred_element_type=F32)

    y1 = y_ref[...] + acc_scr[...]
    y1_ref[:, :D_MODEL] = y1
    xn = y1 * lax.rsqrt(jnp.mean(y1 * y1, axis=-1, keepdims=True) + RMS_EPS) * gn_ref[...]

    lt = lax.dot_general(wr_ref[...], xn, (((1,), (1,)), ((), ())), precision=lax.Precision.HIGHEST,
                         preferred_element_type=F32) + br_ref[...]
    bucket, w_lo, w_hi = _route(lt)
    onehot = (lax.broadcasted_iota(jnp.int32, (ROUTE_ROWS, tile), 0) == bucket).astype(F32)
    upper = (lax.broadcasted_iota(jnp.int32, (tile, tile), 0)
             <= lax.broadcasted_iota(jnp.int32, (tile, tile), 1)).astype(BF16)
    running = jnp.dot(onehot.astype(BF16), upper, preferred_element_type=F32)
    carry = carry_scr[...]
    rank = jnp.sum(onehot * (running - 1.0 + carry), axis=0, keepdims=True)
    carry = carry + jnp.sum(onehot, axis=1, keepdims=True)
    carry_scr[...] = carry
    cnt_ref[...] = jnp.broadcast_to(carry, cnt_ref.shape)
    info_ref[...] = jnp.concatenate([bucket.astype(F32), rank, jnp.zeros((SUBLANES - 2, tile), F32)], axis=0)
    y1_ref[:, D_MODEL:] = jnp.concatenate([w_lo, w_hi, jnp.zeros((LANES - 2, tile), F32)], axis=0).T


def _outproj_route(y, oa, ob, z, mix_s, w_out, cw, gain, w_route, b_route, n_prompt, seq_len):
    n = y.shape[0]
    tiles = n // ROW_TILE
    p_tiles = n_prompt // ROW_TILE
    halo_blocks = ROW_TILE // 8
    conv_w = 3 * C_WIDTH
    pidx = lambda i: jnp.minimum(i, p_tiles - 1)
    const = lambda a: pl.BlockSpec(a.shape, lambda i: (0,) * a.ndim)
    gain = gain.reshape(1, D_MODEL)
    return pl.pallas_call(
        functools.partial(_outproj_kernel, prompt_tiles=p_tiles, tiles_per_seq=seq_len // ROW_TILE),
        out_shape=(jax.ShapeDtypeStruct((n, ROW_EXT), F32),
                   jax.ShapeDtypeStruct((8, n), F32),
                   jax.ShapeDtypeStruct((ROUTE_ROWS, LANES), F32),
                   jax.ShapeDtypeStruct((tiles * 8, C_WIDTH), F32)),
        grid=(tiles,),
        in_specs=[pl.BlockSpec((ROW_TILE, D_MODEL), lambda i: (i, 0)),
                  pl.BlockSpec((ROW_TILE, A_Q_W), lambda i: (pidx(i), 0)),
                  pl.BlockSpec((ROW_TILE, B_W), lambda i: (pidx(i), 0)),
                  pl.BlockSpec((ROW_TILE, conv_w), lambda i: (pidx(i), 0)),
                  pl.BlockSpec((8, conv_w), lambda i: (jnp.maximum(pidx(i) * halo_blocks - 1, 0), 0)),
                  pl.BlockSpec((ROW_TILE, D_MODEL), lambda i: (jnp.maximum(i - p_tiles, 0), 0)),
                  const(w_out), const(cw), const(gain), const(w_route), const(b_route)],
        out_specs=(pl.BlockSpec((ROW_TILE, ROW_EXT), lambda i: (i, 0)),
                   pl.BlockSpec((8, ROW_TILE), lambda i: (0, i)),
                   pl.BlockSpec((ROUTE_ROWS, LANES), lambda i: (0, 0)),
                   pl.BlockSpec((8, C_WIDTH), lambda i: (i, 0))),
        scratch_shapes=[pltpu.VMEM((ROW_TILE, D_MODEL), F32), pltpu.VMEM((ROUTE_ROWS, 1), F32)],
        compiler_params=_params("arbitrary"),
    )(y, oa, ob, z, z, mix_s, w_out, cw, gain, w_route, b_route)


def _moe_kernel(e_lo_ref, e_hi_ref, used_ref, x_ref, g_ref, wg1, wu1, wd1, wg2, wu2, wd2, o_ref):
    t = pl.program_id(0)

    @pl.when(used_ref[t] > 0)
    def _():
        y1 = x_ref[:, :D_MODEL]
        w = x_ref[:, D_MODEL:]
        xn = y1 * lax.rsqrt(jnp.mean(y1 * y1, axis=-1, keepdims=True) + RMS_EPS) * g_ref[...]
        x = xn.astype(BF16)

        def expert(wg, wu, wd, scale):
            g = jnp.dot(x, wg[...], preferred_element_type=F32)
            u = jnp.dot(x, wu[...], preferred_element_type=F32)
            h = g * jax.nn.sigmoid(g) * u * scale
            return jnp.dot(h.astype(BF16), wd[...], preferred_element_type=F32)

        o_ref[...] = y1 + expert(wg1, wu1, wd1, w[:, 0:1]) + expert(wg2, wu2, wd2, w[:, 1:2])

    @pl.when(used_ref[t] == 0)
    def _():
        o_ref[...] = jnp.zeros_like(o_ref)


def _experts(xs, gain, e_lo, e_hi, used, w_gate, w_up, w_down, layer):
    n_tiles = xs.shape[0] // MOE_TILE
    up_spec = lambda sel: pl.BlockSpec((None, None, D_MODEL, EXPERT_FF),
                                       lambda t, lo, hi, u: (layer, (lo, hi)[sel][t], 0, 0))
    down_spec = lambda sel: pl.BlockSpec((None, None, EXPERT_FF, D_MODEL),
                                         lambda t, lo, hi, u: (layer, (lo, hi)[sel][t], 0, 0))
    grid_spec = pltpu.PrefetchScalarGridSpec(
        num_scalar_prefetch=3,
        grid=(n_tiles,),
        in_specs=[pl.BlockSpec((MOE_TILE, ROW_EXT), lambda t, lo, hi, u: (t, 0)),
                  pl.BlockSpec((1, D_MODEL), lambda t, lo, hi, u: (0, 0)),
                  up_spec(0), up_spec(0), down_spec(0), up_spec(1), up_spec(1), down_spec(1)],
        out_specs=pl.BlockSpec((MOE_TILE, D_MODEL), lambda t, lo, hi, u: (t, 0)))
    return pl.pallas_call(
        _moe_kernel,
        out_shape=jax.ShapeDtypeStruct((xs.shape[0], D_MODEL), F32),
        grid_spec=grid_spec,
        compiler_params=_params("arbitrary"),
    )(e_lo, e_hi, used, xs, gain.reshape(1, D_MODEL), w_gate, w_up, w_down, w_gate, w_up, w_down)


def _dispatch_plan(info, counts, n):
    n_tiles = -(-n // MOE_TILE) + N_ROUTE_BUCKETS
    bucket = info[0].astype(jnp.int32)
    rank = info[1].astype(jnp.int32)
    counts = counts[:N_ROUTE_BUCKETS, 0].astype(jnp.int32)
    tiles_per_bucket = (counts + MOE_TILE - 1) // MOE_TILE
    tile_end = jnp.cumsum(tiles_per_bucket)
    row_start = (tile_end - tiles_per_bucket) * MOE_TILE
    dest = row_start[bucket] + rank
    src = jnp.zeros((n_tiles * MOE_TILE,), jnp.int32).at[dest].set(jnp.arange(n, dtype=jnp.int32))
    tile_ids = jnp.arange(n_tiles, dtype=jnp.int32)
    tile_bucket = jnp.minimum(jnp.searchsorted(tile_end, tile_ids, side="right").astype(jnp.int32),
                              N_ROUTE_BUCKETS - 1)
    used = (tile_ids < tile_end[-1]).astype(jnp.int32)
    pair = tile_bucket % len(PAIRS)
    base = (tile_bucket // len(PAIRS)) * EXPERTS_PER_GROUP
    pairs = jnp.asarray(PAIRS, jnp.int32)
    return dest, src, base + pairs[pair, 0], base + pairs[pair, 1], used


def _norm_kernel(x_ref, g_ref, o_ref):
    x = x_ref[...]
    o_ref[...] = x * lax.rsqrt(jnp.mean(x * x, axis=-1, keepdims=True) + RMS_EPS) * g_ref[...]


def _final_norm(y, gain, row0, rows):
    blk0 = row0 // ROW_TILE
    return pl.pallas_call(
        _norm_kernel,
        out_shape=jax.ShapeDtypeStruct((rows, D_MODEL), F32),
        grid=(rows // ROW_TILE,),
        in_specs=[pl.BlockSpec((ROW_TILE, D_MODEL), lambda i: (blk0 + i, 0)),
                  pl.BlockSpec((1, D_MODEL), lambda i: (0, 0))],
        out_specs=pl.BlockSpec((ROW_TILE, D_MODEL), lambda i: (i, 0)),
        compiler_params=_params("parallel"),
    )(y, gain.reshape(1, D_MODEL))


def _permute_in_columns(w):
    attn = A_Q_W + 2 * A_KV_W + 3 * B_W
    return jnp.concatenate([w[:, attn:], w[:, A_Q_W:A_Q_W + A_KV_W], w[:, :A_Q_W], w[:, A_Q_W + A_KV_W:attn]],
                           axis=1)


def kernel(x_prompt, x_sample, cache_a_k, cache_a_v, cache_b_k, cache_b_v, state_conv, rel_bias_table,
           w_in, w_out, conv_w, attn_sinks, norm_mix, norm_ffn, w_group, b_group, w_router, b_router,
           w_gate, w_up, w_down, norm_final):
    n_seq, seq_len, _ = x_prompt.shape
    dec_seq, n_new, _ = x_sample.shape
    depth = w_in.shape[0]
    n_prompt = n_seq * seq_len
    n_sample = dec_seq * n_new
    n = n_prompt + n_sample
    la, lb = cache_a_k.shape[2], cache_b_k.shape[2]
    assert seq_len % SPAN == 0 and n_prompt % ROW_TILE == 0 and n_sample % ROW_TILE == 0
    assert seq_len >= SPAN and la % LANES == 0 and lb % LANES == 0 and n_new == 8

    max_dil = max(DILATIONS)
    near_len = sorted(w for w, _ in B_BRANCHES)[-2]
    far_steps, near_steps = (lb - near_len) // max_dil, near_len // max_dil
    assert lb % max_dil == 0 and max_dil % n_new == 0 and lb > near_len and far_steps % near_steps == 0
    assert sorted(w for w, _ in B_BRANCHES)[-1] <= lb

    table_a, table_b = rel_bias_table[:, :A_Q_HEADS], rel_bias_table[:, A_Q_HEADS:]
    bias_a = _band_bias(table_a, 1)
    bias_b = jnp.stack([_band_bias(table_b, r) for r in DILATIONS])
    bias_b = bias_b.reshape(len(DILATIONS), B_HEADS // 2, 2, 2, BLOCK, BLOCK).transpose(1, 0, 2, 3, 4, 5)
    sbias_a = _sample_bias(table_a, n_new, la, np.arange(la + LANES), A_WINDOW, 1)
    sbias_a = sbias_a.reshape(A_Q_HEADS * n_new, la + LANES)
    far_pos = (np.arange(far_steps)[:, None] * max_dil + np.arange(n_new)[None, :]).reshape(-1)
    near_pos = lb - near_len + np.arange(near_len)
    w_far, r_far = max(B_BRANCHES)
    sbias_far = _own_head_only(_sample_bias(table_b, n_new, lb, far_pos, w_far, r_far))
    sbias_near = jnp.stack([_own_head_only(_sample_bias(table_b, n_new, lb, near_pos, w, r)) for w, r in B_BRANCHES])
    sbias_new = jnp.stack([_sample_bias(table_b, n_new, lb, lb + np.arange(LANES), w, r)
                           .reshape(B_HEADS * n_new, LANES) for w, r in B_BRANCHES])
    cbk = cache_b_k.reshape(depth, dec_seq, lb // max_dil, max_dil, B_HEADS, HEAD_DIM)
    cbv = cache_b_v.reshape(depth, dec_seq, lb // max_dil, max_dil, B_HEADS, HEAD_DIM)

    w_in_b = jnp.stack([_permute_in_columns(w_in[l]) for l in range(depth)]).astype(BF16)
    w_out_b = w_out.astype(BF16)
    w_gate_b, w_up_b, w_down_b = w_gate.astype(BF16), w_up.astype(BF16), w_down.astype(BF16)
    pad = ROUTE_ROWS - N_GROUPS - N_EXPERTS
    w_route = jnp.pad(jnp.concatenate([w_group, w_router], axis=2).transpose(0, 2, 1), ((0, 0), (0, pad), (0, 0)))
    b_route = jnp.pad(jnp.concatenate([b_group, b_router], axis=1), ((0, 0), (0, pad)))[..., None]
    sink_rows = jnp.repeat(attn_sinks, n_new, axis=1)[..., None]

    y = jnp.concatenate([x_prompt.reshape(n_prompt, D_MODEL), x_sample.reshape(n_sample, D_MODEL)], axis=0)
    states = []
    for l in range(depth):
        z = _inproj(y, norm_mix[l], w_in_b[l])
        oa = _window_attention(z, bias_a, attn_sinks[l], n_seq, seq_len)
        ob = _dilated_attention(z, bias_b, n_seq, seq_len)
        mix_s, conv_s = _sample_mixer(
            z, n_prompt, cache_a_k[l].reshape(dec_seq, la, A_KV_W), cache_a_v[l].reshape(dec_seq, la, A_KV_W),
            cbk, cbv, l, far_steps, near_steps, state_conv[l], conv_w[l], sink_rows[l], sbias_a,
            sbias_far, sbias_near, sbias_new, n_new)
        y1, info, counts, u_tail = _outproj_route(
            y, oa, ob, z, mix_s, w_out_b[l], conv_w[l], norm_ffn[l], w_route[l], b_route[l], n_prompt, seq_len)
        dest, src, e_lo, e_hi, used = _dispatch_plan(info, counts, n)
        y = _experts(y1[src], norm_ffn[l], e_lo, e_hi, used, w_gate_b, w_up_b, w_down_b, l)[dest]

        lap, lbp = min(A_WINDOW, seq_len), min(SPAN, seq_len)

        def prompt_tail(length, col, width, heads):
            rows = [z[(b + 1) * seq_len - length:(b + 1) * seq_len, col:col + width] for b in range(n_seq)]
            return jnp.stack(rows).reshape(n_seq, length, heads, HEAD_DIM)

        def sample_rows(col, width, heads):
            return z[n_prompt:, col:col + width].reshape(dec_seq, n_new, heads, HEAD_DIM)

        last_tile = [((b + 1) * seq_len // ROW_TILE - 1) * SUBLANES for b in range(n_seq)]
        conv_p = jnp.stack([u_tail[t + SUBLANES - (CONV_WIDTH - 1):t + SUBLANES, :] for t in last_tile])
        states.append((
            prompt_tail(lap, COL_KA, A_KV_W, A_KV_HEADS), prompt_tail(lap, COL_VA, A_KV_W, A_KV_HEADS),
            prompt_tail(lbp, COL_KB, B_W, B_HEADS), prompt_tail(lbp, COL_VB, B_W, B_HEADS), conv_p,
            sample_rows(COL_KA, A_KV_W, A_KV_HEADS), sample_rows(COL_VA, A_KV_W, A_KV_HEADS),
            sample_rows(COL_KB, B_W, B_HEADS), sample_rows(COL_VB, B_W, B_HEADS), conv_s))

    y_prompt = _final_norm(y, norm_final, 0, n_prompt).reshape(n_seq, seq_len, D_MODEL)
    y_sample = _final_norm(y, norm_final, n_prompt, n_sample).reshape(dec_seq, n_new, D_MODEL)
    st = [jnp.stack([s[k] for s in states]) for k in range(10)]
    return (y_prompt, y_sample, st[0], st[1], st[2], st[3], st[4], st[5], st[6], st[7], st[8], st[9])
```

```python
import functools
import math

import numpy as np
import jax
import jax.numpy as jnp
from jax import lax
from jax.experimental import pallas as pl
from jax.experimental.pallas import tpu as pltpu

F32 = jnp.float32
BF16 = jnp.bfloat16

D_MODEL = 1024
HEAD_DIM = 64
ATTN_SCALE = HEAD_DIM ** -0.5
BLOCK = 128
LANES = 128
SUBLANES = 8
A_Q_HEADS = 4
A_KV_HEADS = 2
A_WINDOW = 128
B_HEADS = 6
B_BRANCHES = ((128, 1), (512, 4), (2048, 16))
DILATIONS = tuple(r for _, r in B_BRANCHES)
SPAN = BLOCK * max(DILATIONS)
C_WIDTH = 6 * HEAD_DIM
CONV_WIDTH = 3
A_Q_W = A_Q_HEADS * HEAD_DIM
A_KV_W = A_KV_HEADS * HEAD_DIM
B_W = B_HEADS * HEAD_DIM
IN_WIDTH = A_Q_W + 2 * A_KV_W + 3 * B_W + 3 * C_WIDTH
N_BUCKETS = 32
MAX_DISTANCE = 2048
N_GROUPS = 4
EXPERTS_PER_GROUP = 4
N_EXPERTS = N_GROUPS * EXPERTS_PER_GROUP
EXPERT_FF = 512
RMS_EPS = 1e-6
NEG_INF = -1e30

COL_CONV = 0
COL_KA = 3 * C_WIDTH
COL_QA = COL_KA + A_KV_W
COL_VA = COL_QA + A_Q_W
COL_QB = COL_VA + A_KV_W
COL_KB = COL_QB + B_W
COL_VB = COL_KB + B_W

ROW_TILE = 512
MOE_TILE = 256
BLOCKS_IN_FLIGHT = 3
PAIRS = ((0, 1), (0, 2), (0, 3), (1, 2), (1, 3), (2, 3))
N_ROUTE_BUCKETS = N_GROUPS * len(PAIRS)
ROUTE_ROWS = 32
ROW_EXT = D_MODEL + LANES
VMEM_LIMIT = 56 * 1024 * 1024


def _params(*sem):
    return pltpu.CompilerParams(dimension_semantics=sem, vmem_limit_bytes=VMEM_LIMIT)


def _rel_bucket(dist):
    d = np.maximum(dist, 0)
    max_exact = N_BUCKETS // 2
    df = np.maximum(d, max_exact).astype(np.float32)
    large = max_exact + (np.log(df / max_exact) / math.log(MAX_DISTANCE / max_exact)
                         * (N_BUCKETS - max_exact)).astype(np.int32)
    large = np.minimum(large, N_BUCKETS - 1)
    return np.where(d < max_exact, d, large)


def _masked_bias(table, dist, valid):
    bucket = _rel_bucket(dist).reshape(-1, 1)
    onehot = (jnp.asarray(bucket, jnp.int32) == jnp.arange(N_BUCKETS, dtype=jnp.int32)[None, :]).astype(F32)
    b = jnp.dot(onehot, table.astype(F32), precision=lax.Precision.HIGHEST)
    b = jnp.where(jnp.asarray(valid.reshape(-1, 1)), b, NEG_INF)
    return jnp.moveaxis(b.reshape(dist.shape + (table.shape[1],)), -1, 0)


def _band_bias(table, scale):
    i = np.arange(BLOCK)[:, None]
    j = np.arange(2 * BLOCK)[None, :]
    dist = i + BLOCK - j
    b = _masked_bias(table, dist * scale, (dist >= 0) & (dist <= BLOCK))
    return b.reshape(table.shape[1] // 2, 2 * BLOCK, 2 * BLOCK)


def _sample_bias(table, n_new, first_new, positions, window, dilation):
    dist = first_new + np.arange(n_new)[:, None] - np.asarray(positions)[None, :]
    valid = (dist >= 0) & (dist <= window) & (dist % dilation == 0)
    return _masked_bias(table, dist, valid)


def _inproj_kernel(x_ref, g_ref, w_ref, z_ref):
    x = x_ref[...]
    h = x * lax.rsqrt(jnp.mean(x * x, axis=-1, keepdims=True) + RMS_EPS) * g_ref[...]
    z_ref[...] = jnp.dot(h.astype(BF16), w_ref[...], preferred_element_type=F32)


def _inproj(x, gain, w):
    n = x.shape[0]
    return pl.pallas_call(
        _inproj_kernel,
        out_shape=jax.ShapeDtypeStruct((n, IN_WIDTH), F32),
        grid=(n // ROW_TILE,),
        in_specs=[pl.BlockSpec((ROW_TILE, D_MODEL), lambda i: (i, 0)),
                  pl.BlockSpec((1, D_MODEL), lambda i: (0, 0)),
                  pl.BlockSpec((D_MODEL, IN_WIDTH), lambda i: (0, 0))],
        out_specs=pl.BlockSpec((ROW_TILE, IN_WIDTH), lambda i: (i, 0)),
        compiler_params=_params("parallel"),
    )(x, gain.reshape(1, D_MODEL), w)


def _lane_half(shape):
    return lax.broadcasted_iota(jnp.int32, shape, len(shape) - 1) // HEAD_DIM


def _nt_dot(a, b):
    return lax.dot_general(a, b, (((1,), (1,)), ((), ())), preferred_element_type=F32)


def _band_pair(q2, k2, v2, bias2, penalty):
    lg = _nt_dot(q2, k2) + bias2
    if penalty is not None:
        lg = lg + penalty
    m = jnp.max(lg, axis=-1, keepdims=True)
    p = jnp.exp(lg - m)
    s = jnp.sum(p, axis=-1, keepdims=True)
    acc = jnp.dot(p.astype(BF16), v2, preferred_element_type=F32)
    return acc / s, m + jnp.log(s)


def _prev_key_penalty(first_span):
    col = lax.broadcasted_iota(jnp.int32, (1, 2 * BLOCK), 1)
    return jnp.where((col < BLOCK) & first_span, NEG_INF, 0.0).astype(F32)


def _rows(start, dilation):
    if dilation == 1:
        return pl.ds(start, BLOCK)
    return pl.ds(start, BLOCK, stride=dilation)


def _dilated_kernel(q_ref, kc_ref, kp_ref, vc_ref, vp_ref, bias_ref, o_ref, o_scr, l_scr):
    penalty = _prev_key_penalty(pl.program_id(1) == 0)
    first_half = _lane_half((BLOCK, LANES)) == 0

    for bi, r in enumerate(DILATIONS):
        step = r * BLOCK

        def block(c, n, first, bi=bi, r=r, step=step):
            q_start = c + n * step
            if not isinstance(q_start, int):
                q_start = pl.multiple_of(q_start, BLOCK) if r == 1 else q_start
            cur = _rows(q_start, r)
            qt = (q_ref[cur, :] * ATTN_SCALE).astype(BF16)
            if first:
                src_k, src_v, prev = kp_ref, vp_ref, _rows(SPAN - step + c, r)
            else:
                p_start = q_start - step
                if r == 1:
                    p_start = pl.multiple_of(p_start, BLOCK)
                src_k, src_v, prev = kc_ref, vc_ref, _rows(p_start, r)
            k2 = jnp.concatenate([src_k[prev, :], kc_ref[cur, :]], axis=0).astype(BF16)
            v2 = jnp.concatenate([src_v[prev, :], vc_ref[cur, :]], axis=0).astype(BF16)
            zero = jnp.zeros_like(qt)
            q2 = jnp.concatenate([jnp.where(first_half, qt, zero), jnp.where(first_half, zero, qt)], axis=0)
            o, lse = _band_pair(q2, k2, v2, bias_ref[bi], penalty if first else None)
            o_scr[bi, cur, :] = jnp.where(first_half, o[:BLOCK], o[BLOCK:])
            l_scr[bi, cur, :] = jnp.where(first_half, lse[:BLOCK], lse[BLOCK:])

        n_blocks = SPAN // step
        if r == 1:
            block(0, 0, True)
            lax.fori_loop(1, n_blocks, lambda n, _: block(0, n, False), None, unroll=BLOCKS_IN_FLIGHT)
        else:
            def per_class(c, _, block=block, n_blocks=n_blocks):
                block(c, 0, True)
                for n in range(1, n_blocks):
                    block(c, n, False)
            lax.fori_loop(0, r, per_class, None, unroll=max(1, BLOCKS_IN_FLIGHT // n_blocks))

    chunk = 2 * BLOCK

    def combine(j, _):
        rows = pl.ds(pl.multiple_of(j * chunk, chunk), chunk)
        ls = [l_scr[bi, rows, :] for bi in range(len(DILATIONS))]
        m = functools.reduce(jnp.maximum, ls)
        ws = [jnp.exp(l - m) for l in ls]
        num = sum(w * o_scr[bi, rows, :] for bi, w in enumerate(ws))
        o_ref[rows, :] = (num / sum(ws)).astype(o_ref.dtype)

    lax.fori_loop(0, SPAN // chunk, combine, None)


def _dilated_attention(z, bias, n_seq, seq_len):
    spans = seq_len // SPAN
    qb, kb, vb = COL_QB // LANES, COL_KB // LANES, COL_VB // LANES
    blk = (SPAN, LANES)
    cur = lambda col: (lambda b, s, i: (b * spans + s, col + i))
    prev = lambda col: (lambda b, s, i: (b * spans + jnp.maximum(s - 1, 0), col + i))
    return pl.pallas_call(
        _dilated_kernel,
        out_shape=jax.ShapeDtypeStruct((n_seq * seq_len, B_W), BF16),
        grid=(n_seq, spans, B_W // LANES),
        in_specs=[pl.BlockSpec(blk, cur(qb)),
                  pl.BlockSpec(blk, cur(kb)), pl.BlockSpec(blk, prev(kb)),
                  pl.BlockSpec(blk, cur(vb)), pl.BlockSpec(blk, prev(vb)),
                  pl.BlockSpec((None, len(DILATIONS), 2 * BLOCK, 2 * BLOCK), lambda b, s, i: (i, 0, 0, 0))],
        out_specs=pl.BlockSpec(blk, lambda b, s, i: (b * spans + s, i)),
        scratch_shapes=[pltpu.VMEM((len(DILATIONS), SPAN, LANES), F32),
                        pltpu.VMEM((len(DILATIONS), SPAN, LANES), F32)],
        compiler_params=_params("parallel", "parallel", "parallel"),
    )(z, z, z, z, z, bias)


def _window_kernel(sink_ref, q_ref, kc_ref, kp_ref, vc_ref, vp_ref, bias_ref, o_ref):
    penalty = _prev_key_penalty(pl.program_id(1) == 0)
    first_half = _lane_half((BLOCK, LANES)) == 0
    first_half2 = _lane_half((2 * BLOCK, LANES)) == 0

    def block(n, first):
        start = n * BLOCK if isinstance(n, int) else pl.multiple_of(n * BLOCK, BLOCK)
        cur = pl.ds(start, BLOCK)
        if first:
            src_k, src_v, prev = kp_ref, vp_ref, pl.ds(SPAN - BLOCK, BLOCK)
        else:
            src_k, src_v, prev = kc_ref, vc_ref, pl.ds(pl.multiple_of(start - BLOCK, BLOCK), BLOCK)
        k2 = jnp.concatenate([src_k[prev, :], kc_ref[cur, :]], axis=0).astype(BF16)
        vf = jnp.concatenate([src_v[prev, :], vc_ref[cur, :]], axis=0)
        v_swap = pltpu.roll(vf, HEAD_DIM, 1)
        for i in range(A_KV_HEADS):
            qf = q_ref[cur, i * LANES:(i + 1) * LANES] * ATTN_SCALE
            q_swap = pltpu.roll(qf, HEAD_DIM, 1)
            q2 = jnp.concatenate([jnp.where(first_half == (i == 0), qf if a == i else q_swap, 0.0)
                                  for a in range(2)], axis=0).astype(BF16)
            v2 = jnp.where(first_half2 == (i == 0), vf, v_swap).astype(BF16)
            o, lse = _band_pair(q2, k2, v2, bias_ref[i], penalty if first else None)
            outs = [o[a * BLOCK:(a + 1) * BLOCK] * jax.nn.sigmoid(lse[a * BLOCK:(a + 1) * BLOCK] - sink_ref[2 * i + a])
                    for a in range(2)]
            o_ref[cur, i * LANES:(i + 1) * LANES] = jnp.where(first_half, outs[0], outs[1]).astype(o_ref.dtype)

    block(0, True)
    lax.fori_loop(1, SPAN // BLOCK, lambda n, _: block(n, False), None, unroll=BLOCKS_IN_FLIGHT)


def _window_attention(z, bias, sinks, n_seq, seq_len):
    spans = seq_len // SPAN
    ka, va = COL_KA // LANES, COL_VA // LANES
    blk = (SPAN, LANES)
    cur = lambda col: (lambda b, s: (b * spans + s, col))
    prev = lambda col: (lambda b, s: (b * spans + jnp.maximum(s - 1, 0), col))
    return pl.pallas_call(
        _window_kernel,
        out_shape=jax.ShapeDtypeStruct((n_seq * seq_len, A_Q_W), BF16),
        grid=(n_seq, spans),
        in_specs=[pl.BlockSpec(memory_space=pltpu.SMEM),
                  pl.BlockSpec((SPAN, A_Q_W), lambda b, s: (b * spans + s, COL_QA // A_Q_W)),
                  pl.BlockSpec(blk, cur(ka)), pl.BlockSpec(blk, prev(ka)),
                  pl.BlockSpec(blk, cur(va)), pl.BlockSpec(blk, prev(va)),
                  pl.BlockSpec((A_KV_HEADS, 2 * BLOCK, 2 * BLOCK), lambda b, s: (0, 0, 0))],
        out_specs=pl.BlockSpec((SPAN, A_Q_W), lambda b, s: (b * spans + s, 0)),
        compiler_params=_params("parallel", "parallel"),
    )(sinks, z, z, z, z, z, bias)


def _shift_rows(u, filler, k):
    rolled = pltpu.roll(u, k, 0)
    row = lax.broadcasted_iota(jnp.int32, u.shape, 0)
    n_fill = filler.shape[0]
    for j in range(k):
        rolled = jnp.where(row == j, filler[n_fill - k + j:n_fill - k + j + 1, :], rolled)
    return rolled


def _gated_conv(xc, bg, cg, filler, cw):
    u = cg * xc
    conv = cw[0:1, :] * _shift_rows(u, filler, 2) + cw[1:2, :] * _shift_rows(u, filler, 1) + cw[2:3, :] * u
    return bg * conv, u


def _pad_rows(x, rows):
    return jnp.concatenate([x, jnp.zeros((rows - x.shape[0], x.shape[1]), x.dtype)], axis=0)


def _heads_by_dim(ref):
    h, d, length = ref.shape
    return ref[...].reshape(h * d, length).astype(BF16)


def _sample_kernel(z_ref, cak_ref, cav_ref, cbk_ref, cbv_ref, st_ref, cw_ref, sink_ref, bias_a_ref,
                   bias_b_ref, mix_ref, conv_ref, *, n_new):
    z = z_ref[...]
    la = cak_ref.shape[2]
    lb = cbk_ref.shape[2]

    ka_new = _pad_rows(z[:, COL_KA:COL_KA + A_KV_W], LANES).astype(BF16)
    va_new = _pad_rows(z[:, COL_VA:COL_VA + A_KV_W], LANES).astype(BF16)
    half = _lane_half((n_new, LANES))
    pieces = []
    for i in range(A_KV_HEADS):
        qf = z[:, COL_QA + i * LANES:COL_QA + (i + 1) * LANES] * ATTN_SCALE
        for a in range(2):
            pieces.append(jnp.where(half == i, qf if a == i else pltpu.roll(qf, HEAD_DIM, 1), 0.0))
    qa = jnp.concatenate(pieces, axis=0).astype(BF16)
    lc = jnp.dot(qa, _heads_by_dim(cak_ref), preferred_element_type=F32) + bias_a_ref[:, :la]
    ln = _nt_dot(qa, ka_new) + bias_a_ref[:, la:]
    m = jnp.maximum(jnp.max(lc, axis=-1, keepdims=True), jnp.max(ln, axis=-1, keepdims=True))
    pc = jnp.exp(lc - m)
    pn = jnp.exp(ln - m)
    s = jnp.sum(pc, axis=-1, keepdims=True) + jnp.sum(pn, axis=-1, keepdims=True)
    oa = (_nt_dot(pc.astype(BF16), _heads_by_dim(cav_ref))
          + jnp.dot(pn.astype(BF16), va_new, preferred_element_type=F32))
    oa = oa / s * jax.nn.sigmoid(m + jnp.log(s) - sink_ref[...])
    oa_blocks = []
    for i in range(A_KV_HEADS):
        per_half = []
        for a in range(2):
            rows = oa[(2 * i + a) * n_new:(2 * i + a + 1) * n_new, :]
            per_half.append(rows if a == i else pltpu.roll(rows, HEAD_DIM, 1))
        oa_blocks.append(jnp.where(half == 0, per_half[0], per_half[1]))

    qf = z[:, COL_QB:COL_QB + B_W] * ATTN_SCALE
    head_of_lane = _lane_half((n_new, B_W))
    qb = jnp.concatenate([jnp.where(head_of_lane == h, qf, 0.0) for h in range(B_HEADS)], axis=0).astype(BF16)
    kb_new = _pad_rows(z[:, COL_KB:COL_KB + B_W], LANES).astype(BF16)
    vb_new = _pad_rows(z[:, COL_VB:COL_VB + B_W], LANES).astype(BF16)
    lg_c = jnp.dot(qb, _heads_by_dim(cbk_ref), preferred_element_type=F32)
    lg_n = _nt_dot(qb, kb_new)
    parts = []
    for bi, (w, r) in enumerate(B_BRANCHES):
        lo = lb - min(lb, -(-w // LANES) * LANES)
        lc = lg_c[:, lo:] + bias_b_ref[bi, :, lo:lb]
        ln = lg_n + bias_b_ref[bi, :, lb:]
        m = jnp.maximum(jnp.max(lc, axis=-1, keepdims=True), jnp.max(ln, axis=-1, keepdims=True))
        pc = jnp.exp(lc - m)
        pn = jnp.exp(ln - m)
        s = jnp.sum(pc, axis=-1, keepdims=True) + jnp.sum(pn, axis=-1, keepdims=True)
        parts.append((lo, pc, pn, s, m + jnp.log(s)))
    m_all = functools.reduce(jnp.maximum, [p[4] for p in parts])
    ws = [jnp.exp(p[4] - m_all) for p in parts]
    den = sum(ws)
    p_new = None
    los = sorted({p[0] for p in parts} | {lb})
    segs = [None] * (len(los) - 1)
    for (lo, pc, pn, s, _), w in zip(parts, ws):
        coef = w / (den * s)
        p_new = coef * pn if p_new is None else p_new + coef * pn
        for si in range(len(segs)):
            a0, a1 = los[si], los[si + 1]
            if a0 >= lo:
                piece = coef * pc[:, a0 - lo:a1 - lo]
                segs[si] = piece if segs[si] is None else segs[si] + piece
    p_cache = jnp.concatenate(segs, axis=1) if len(segs) > 1 else segs[0]
    ob = (_nt_dot(p_cache.astype(BF16), _heads_by_dim(cbv_ref))
          + jnp.dot(p_new.astype(BF16), vb_new, preferred_element_type=F32))
    ob_rows = sum(jnp.where(head_of_lane == h, ob[h * n_new:(h + 1) * n_new, :], 0.0) for h in range(B_HEADS))

    cz = z[:, COL_CONV:COL_CONV + 3 * C_WIDTH]
    oc, u = _gated_conv(cz[:, :C_WIDTH], cz[:, C_WIDTH:2 * C_WIDTH], cz[:, 2 * C_WIDTH:], st_ref[...], cw_ref[...])
    conv_ref[...] = u[n_new - (CONV_WIDTH - 1):, :]
    mix_ref[...] = jnp.concatenate(oa_blocks + [ob_rows, oc], axis=1)


def _sample_mixer(z, row0, cak, cav, cbk, cbv, layer, state, cw, sink_rows, bias_a, bias_b, n_new):
    n_seq = cak.shape[1]
    blk0 = row0 // n_new
    cache = lambda a: pl.BlockSpec((None, None) + a.shape[2:], lambda b: (layer, b, 0, 0, 0))
    per_seq = lambda shape: pl.BlockSpec((None,) + shape, lambda b: (b, 0, 0))
    const = lambda a: pl.BlockSpec(a.shape, lambda b: (0,) * a.ndim)
    return pl.pallas_call(
        functools.partial(_sample_kernel, n_new=n_new),
        out_shape=(jax.ShapeDtypeStruct((n_seq * n_new, D_MODEL), F32),
                   jax.ShapeDtypeStruct((n_seq, CONV_WIDTH - 1, C_WIDTH), F32)),
        grid=(n_seq,),
        in_specs=[pl.BlockSpec((n_new, IN_WIDTH), lambda b: (blk0 + b, 0)),
                  cache(cak), cache(cav), cache(cbk), cache(cbv),
                  per_seq((CONV_WIDTH - 1, C_WIDTH)), const(cw), const(sink_rows), const(bias_a), const(bias_b)],
        out_specs=(pl.BlockSpec((n_new, D_MODEL), lambda b: (b, 0)), per_seq((CONV_WIDTH - 1, C_WIDTH))),
        compiler_params=_params("parallel"),
    )(z, cak, cav, cbk, cbv, state, cw, sink_rows, bias_a, bias_b)


def _first_index(vals, best):
    idx = jnp.full(best.shape, len(vals) - 1, jnp.int32)
    for j in range(len(vals) - 2, -1, -1):
        idx = jnp.where(vals[j] == best, j, idx)
    return idx


def _route(lt):
    g = [lt[k:k + 1, :] for k in range(N_GROUPS)]
    g_max = functools.reduce(jnp.maximum, g)
    g_idx = _first_index(g, g_max)
    g_w = 1.0 / sum(jnp.exp(v - g_max) for v in g)
    e = []
    for j in range(EXPERTS_PER_GROUP):
        v = lt[N_GROUPS + j:N_GROUPS + j + 1, :]
        for gi in range(1, N_GROUPS):
            row = N_GROUPS + gi * EXPERTS_PER_GROUP + j
            v = jnp.where(g_idx == gi, lt[row:row + 1, :], v)
        e.append(v)
    e1 = functools.reduce(jnp.maximum, e)
    i1 = _first_index(e, e1)
    rest = [jnp.where(i1 == j, -jnp.inf, e[j]) for j in range(EXPERTS_PER_GROUP)]
    e2 = functools.reduce(jnp.maximum, rest)
    i2 = _first_index(rest, e2)
    t = jnp.exp(e2 - e1)
    w1 = g_w / (1.0 + t)
    w2 = g_w * t / (1.0 + t)
    swap = i2 < i1
    lo = jnp.where(swap, i2, i1)
    hi = jnp.where(swap, i1, i2)
    pair = jnp.where(lo == 0, hi - 1, jnp.where(lo == 1, hi + 1, len(PAIRS) - 1))
    bucket = g_idx * len(PAIRS) + pair
    return bucket, jnp.where(swap, w2, w1), jnp.where(swap, w1, w2)


def _outproj_kernel(y_ref, oa_ref, ob_ref, zc_ref, zh_ref, ms_ref, wout_ref, cw_ref, gn_ref, wr_ref, br_ref,
                    y1_ref, info_ref, cnt_ref, ut_ref, acc_scr, carry_scr, *, prompt_tiles, tiles_per_seq):
    i = pl.program_id(0)
    tile = y_ref.shape[0]

    @pl.when(i == 0)
    def _():
        carry_scr[...] = jnp.zeros_like(carry_scr)

    @pl.when(i < prompt_tiles)
    def _():
        zc = zc_ref[...]
        zh = zh_ref[...]
        halo = zh[:, 2 * C_WIDTH:] * zh[:, :C_WIDTH]
        halo = jnp.where(i % tiles_per_seq == 0, 0.0, halo)
        oc, u = _gated_conv(zc[:, :C_WIDTH], zc[:, C_WIDTH:2 * C_WIDTH], zc[:, 2 * C_WIDTH:], halo, cw_ref[...])
        ut_ref[...] = u[tile - 8:, :]
        acc_scr[...] = (
            jnp.dot(oa_ref[...], wout_ref[0:A_Q_W, :], preferred_element_type=F32)
            + jnp.dot(ob_ref[...], wout_ref[A_Q_W:A_Q_W + B_W, :], preferred_element_type=F32)
            + jnp.dot(oc.astype(BF16), wout_ref[A_Q_W + B_W:, :], preferred_element_type=F32))

    @pl.when(i >= prompt_tiles)
    def _():
        ut_ref[...] = jnp.zeros_like(ut_ref)
        acc_scr[...] = jnp.dot(ms_ref[...].astype(BF16), wout_ref[...], preferred_element_type=F32)

    y1 = y_ref[...] + acc_scr[...]
    y1_ref[:, :D_MODEL] = y1
    xn = y1 * lax.rsqrt(jnp.mean(y1 * y1, axis=-1, keepdims=True) + RMS_EPS) * gn_ref[...]

    lt = lax.dot_general(wr_ref[...], xn, (((1,), (1,)), ((), ())), precision=lax.Precision.HIGHEST,
                         preferred_element_type=F32) + br_ref[...]
    bucket, w_lo, w_hi = _route(lt)
    onehot = (lax.broadcasted_iota(jnp.int32, (ROUTE_ROWS, tile), 0) == bucket).astype(F32)
    upper = (lax.broadcasted_iota(jnp.int32, (tile, tile), 0)
             <= lax.broadcasted_iota(jnp.int32, (tile, tile), 1)).astype(BF16)
    running = jnp.dot(onehot.astype(BF16), upper, preferred_element_type=F32)
    carry = carry_scr[...]
    rank = jnp.sum(onehot * (running - 1.0 + carry), axis=0, keepdims=True)
    carry = carry + jnp.sum(onehot, axis=1, keepdims=True)
    carry_scr[...] = carry
    cnt_ref[...] = jnp.broadcast_to(carry, cnt_ref.shape)
    info_ref[...] = jnp.concatenate([bucket.astype(F32), rank, jnp.zeros((SUBLANES - 2, tile), F32)], axis=0)
    y1_ref[:, D_MODEL:] = jnp.concatenate([w_lo, w_hi, jnp.zeros((LANES - 2, tile), F32)], axis=0).T


def _outproj_route(y, oa, ob, z, mix_s, w_out, cw, gain, w_route, b_route, n_prompt, seq_len):
    n = y.shape[0]
    tiles = n // ROW_TILE
    p_tiles = n_prompt // ROW_TILE
    halo_blocks = ROW_TILE // 8
    conv_w = 3 * C_WIDTH
    pidx = lambda i: jnp.minimum(i, p_tiles - 1)
    const = lambda a: pl.BlockSpec(a.shape, lambda i: (0,) * a.ndim)
    gain = gain.reshape(1, D_MODEL)
    return pl.pallas_call(
        functools.partial(_outproj_kernel, prompt_tiles=p_tiles, tiles_per_seq=seq_len // ROW_TILE),
        out_shape=(jax.ShapeDtypeStruct((n, ROW_EXT), F32),
                   jax.ShapeDtypeStruct((8, n), F32),
                   jax.ShapeDtypeStruct((ROUTE_ROWS, LANES), F32),
                   jax.ShapeDtypeStruct((tiles * 8, C_WIDTH), F32)),
        grid=(tiles,),
        in_specs=[pl.BlockSpec((ROW_TILE, D_MODEL), lambda i: (i, 0)),
                  pl.BlockSpec((ROW_TILE, A_Q_W), lambda i: (pidx(i), 0)),
                  pl.BlockSpec((ROW_TILE, B_W), lambda i: (pidx(i), 0)),
                  pl.BlockSpec((ROW_TILE, conv_w), lambda i: (pidx(i), 0)),
                  pl.BlockSpec((8, conv_w), lambda i: (jnp.maximum(pidx(i) * halo_blocks - 1, 0), 0)),
                  pl.BlockSpec((ROW_TILE, D_MODEL), lambda i: (jnp.maximum(i - p_tiles, 0), 0)),
                  const(w_out), const(cw), const(gain), const(w_route), const(b_route)],
        out_specs=(pl.BlockSpec((ROW_TILE, ROW_EXT), lambda i: (i, 0)),
                   pl.BlockSpec((8, ROW_TILE), lambda i: (0, i)),
                   pl.BlockSpec((ROUTE_ROWS, LANES), lambda i: (0, 0)),
                   pl.BlockSpec((8, C_WIDTH), lambda i: (i, 0))),
        scratch_shapes=[pltpu.VMEM((ROW_TILE, D_MODEL), F32), pltpu.VMEM((ROUTE_ROWS, 1), F32)],
        compiler_params=_params("arbitrary"),
    )(y, oa, ob, z, z, mix_s, w_out, cw, gain, w_route, b_route)


def _moe_kernel(e_lo_ref, e_hi_ref, used_ref, x_ref, g_ref, wg1, wu1, wd1, wg2, wu2, wd2, o_ref):
    t = pl.program_id(0)

    @pl.when(used_ref[t] > 0)
    def _():
        y1 = x_ref[:, :D_MODEL]
        w = x_ref[:, D_MODEL:]
        xn = y1 * lax.rsqrt(jnp.mean(y1 * y1, axis=-1, keepdims=True) + RMS_EPS) * g_ref[...]
        x = xn.astype(BF16)

        def expert(wg, wu, wd, scale):
            g = jnp.dot(x, wg[...], preferred_element_type=F32)
            u = jnp.dot(x, wu[...], preferred_element_type=F32)
            h = g * jax.nn.sigmoid(g) * u * scale
            return jnp.dot(h.astype(BF16), wd[...], preferred_element_type=F32)

        o_ref[...] = y1 + expert(wg1, wu1, wd1, w[:, 0:1]) + expert(wg2, wu2, wd2, w[:, 1:2])

    @pl.when(used_ref[t] == 0)
    def _():
        o_ref[...] = jnp.zeros_like(o_ref)


def _experts(xs, gain, e_lo, e_hi, used, w_gate, w_up, w_down, layer):
    n_tiles = xs.shape[0] // MOE_TILE
    up_spec = lambda sel: pl.BlockSpec((None, None, D_MODEL, EXPERT_FF),
                                       lambda t, lo, hi, u: (layer, (lo, hi)[sel][t], 0, 0))
    down_spec = lambda sel: pl.BlockSpec((None, None, EXPERT_FF, D_MODEL),
                                         lambda t, lo, hi, u: (layer, (lo, hi)[sel][t], 0, 0))
    grid_spec = pltpu.PrefetchScalarGridSpec(
        num_scalar_prefetch=3,
        grid=(n_tiles,),
        in_specs=[pl.BlockSpec((MOE_TILE, ROW_EXT), lambda t, lo, hi, u: (t, 0)),
                  pl.BlockSpec((1, D_MODEL), lambda t, lo, hi, u: (0, 0)),
                  up_spec(0), up_spec(0), down_spec(0), up_spec(1), up_spec(1), down_spec(1)],
        out_specs=pl.BlockSpec((MOE_TILE, D_MODEL), lambda t, lo, hi, u: (t, 0)))
    return pl.pallas_call(
        _moe_kernel,
        out_shape=jax.ShapeDtypeStruct((xs.shape[0], D_MODEL), F32),
        grid_spec=grid_spec,
        compiler_params=_params("arbitrary"),
    )(e_lo, e_hi, used, xs, gain.reshape(1, D_MODEL), w_gate, w_up, w_down, w_gate, w_up, w_down)


def _dispatch_plan(info, counts, n):
    n_tiles = -(-n // MOE_TILE) + N_ROUTE_BUCKETS
    bucket = info[0].astype(jnp.int32)
    rank = info[1].astype(jnp.int32)
    counts = counts[:N_ROUTE_BUCKETS, 0].astype(jnp.int32)
    tiles_per_bucket = (counts + MOE_TILE - 1) // MOE_TILE
    tile_end = jnp.cumsum(tiles_per_bucket)
    row_start = (tile_end - tiles_per_bucket) * MOE_TILE
    dest = row_start[bucket] + rank
    src = (jnp.arange(n_tiles * MOE_TILE, dtype=jnp.int32) % n).at[dest].set(jnp.arange(n, dtype=jnp.int32))
    tile_ids = jnp.arange(n_tiles, dtype=jnp.int32)
    tile_bucket = jnp.minimum(jnp.searchsorted(tile_end, tile_ids, side="right").astype(jnp.int32),
                              N_ROUTE_BUCKETS - 1)
    used = (tile_ids < tile_end[-1]).astype(jnp.int32)
    pair = tile_bucket % len(PAIRS)
    base = (tile_bucket // len(PAIRS)) * EXPERTS_PER_GROUP
    pairs = jnp.asarray(PAIRS, jnp.int32)
    return dest, src, base + pairs[pair, 0], base + pairs[pair, 1], used


def _norm_kernel(x_ref, g_ref, o_ref):
    x = x_ref[...]
    o_ref[...] = x * lax.rsqrt(jnp.mean(x * x, axis=-1, keepdims=True) + RMS_EPS) * g_ref[...]


def _final_norm(y, gain, row0, rows):
    blk0 = row0 // ROW_TILE
    return pl.pallas_call(
        _norm_kernel,
        out_shape=jax.ShapeDtypeStruct((rows, D_MODEL), F32),
        grid=(rows // ROW_TILE,),
        in_specs=[pl.BlockSpec((ROW_TILE, D_MODEL), lambda i: (blk0 + i, 0)),
                  pl.BlockSpec((1, D_MODEL), lambda i: (0, 0))],
        out_specs=pl.BlockSpec((ROW_TILE, D_MODEL), lambda i: (i, 0)),
        compiler_params=_params("parallel"),
    )(y, gain.reshape(1, D_MODEL))


def _permute_in_columns(w):
    attn = A_Q_W + 2 * A_KV_W + 3 * B_W
    return jnp.concatenate([w[:, attn:], w[:, A_Q_W:A_Q_W + A_KV_W], w[:, :A_Q_W], w[:, A_Q_W + A_KV_W:attn]],
                           axis=1)


def kernel(x_prompt, x_sample, cache_a_k, cache_a_v, cache_b_k, cache_b_v, state_conv, rel_bias_table,
           w_in, w_out, conv_w, attn_sinks, norm_mix, norm_ffn, w_group, b_group, w_router, b_router,
           w_gate, w_up, w_down, norm_final):
    n_seq, seq_len, _ = x_prompt.shape
    dec_seq, n_new, _ = x_sample.shape
    depth = w_in.shape[0]
    n_prompt = n_seq * seq_len
    n_sample = dec_seq * n_new
    n = n_prompt + n_sample
    la, lb = cache_a_k.shape[2], cache_b_k.shape[2]
    assert seq_len % SPAN == 0 and n_prompt % ROW_TILE == 0 and n_sample % ROW_TILE == 0
    assert seq_len >= SPAN and la % LANES == 0 and lb % LANES == 0 and n_new == 8

    table_a, table_b = rel_bias_table[:, :A_Q_HEADS], rel_bias_table[:, A_Q_HEADS:]
    bias_a = _band_bias(table_a, 1)
    bias_b = jnp.stack([_band_bias(table_b, r) for r in DILATIONS], axis=1)
    sbias_a = _sample_bias(table_a, n_new, la, np.arange(la + LANES), A_WINDOW, 1)
    sbias_a = sbias_a.reshape(A_Q_HEADS * n_new, la + LANES)
    sbias_b = jnp.stack([_sample_bias(table_b, n_new, lb, np.arange(lb + LANES), w, r)
                         .reshape(B_HEADS * n_new, lb + LANES) for w, r in B_BRANCHES])
    cak, cav, cbk, cbv = (c.transpose(0, 1, 3, 4, 2) for c in (cache_a_k, cache_a_v, cache_b_k, cache_b_v))

    w_in_b = jnp.stack([_permute_in_columns(w_in[l]) for l in range(depth)]).astype(BF16)
    w_out_b = w_out.astype(BF16)
    w_gate_b, w_up_b, w_down_b = w_gate.astype(BF16), w_up.astype(BF16), w_down.astype(BF16)
    pad = ROUTE_ROWS - N_GROUPS - N_EXPERTS
    w_route = jnp.pad(jnp.concatenate([w_group, w_router], axis=2).transpose(0, 2, 1), ((0, 0), (0, pad), (0, 0)))
    b_route = jnp.pad(jnp.concatenate([b_group, b_router], axis=1), ((0, 0), (0, pad)))[..., None]
    sink_rows = jnp.repeat(attn_sinks, n_new, axis=1)[..., None]

    y = jnp.concatenate([x_prompt.reshape(n_prompt, D_MODEL), x_sample.reshape(n_sample, D_MODEL)], axis=0)
    states = []
    for l in range(depth):
        z = _inproj(y, norm_mix[l], w_in_b[l])
        oa = _window_attention(z, bias_a, attn_sinks[l], n_seq, seq_len)
        ob = _dilated_attention(z, bias_b, n_seq, seq_len)
        mix_s, conv_s = _sample_mixer(z, n_prompt, cak, cav, cbk, cbv, l, state_conv[l], conv_w[l], sink_rows[l],
                                      sbias_a, sbias_b, n_new)
        y1, info, counts, u_tail = _outproj_route(
            y, oa, ob, z, mix_s, w_out_b[l], conv_w[l], norm_ffn[l], w_route[l], b_route[l], n_prompt, seq_len)
        dest, src, e_lo, e_hi, used = _dispatch_plan(info, counts, n)
        y = _experts(y1[src], norm_ffn[l], e_lo, e_hi, used, w_gate_b, w_up_b, w_down_b, l)[dest]

        lap, lbp = min(A_WINDOW, seq_len), min(SPAN, seq_len)

        def prompt_tail(length, col, width, heads):
            rows = [z[(b + 1) * seq_len - length:(b + 1) * seq_len, col:col + width] for b in range(n_seq)]
            return jnp.stack(rows).reshape(n_seq, length, heads, HEAD_DIM)

        def sample_rows(col, width, heads):
            return z[n_prompt:, col:col + width].reshape(dec_seq, n_new, heads, HEAD_DIM)

        last_tile = [((b + 1) * seq_len // ROW_TILE - 1) * SUBLANES for b in range(n_seq)]
        conv_p = jnp.stack([u_tail[t + SUBLANES - (CONV_WIDTH - 1):t + SUBLANES, :] for t in last_tile])
        states.append((
            prompt_tail(lap, COL_KA, A_KV_W, A_KV_HEADS), prompt_tail(lap, COL_VA, A_KV_W, A_KV_HEADS),
            prompt_tail(lbp, COL_KB, B_W, B_HEADS), prompt_tail(lbp, COL_VB, B_W, B_HEADS), conv_p,
            sample_rows(COL_KA, A_KV_W, A_KV_HEADS), sample_rows(COL_VA, A_KV_W, A_KV_HEADS),
            sample_rows(COL_KB, B_W, B_HEADS), sample_rows(COL_VB, B_W, B_HEADS), conv_s))

    y_prompt = _final_norm(y, norm_final, 0, n_prompt).reshape(n_seq, seq_len, D_MODEL)
    y_sample = _final_norm(y, norm_final, n_prompt, n_sample).reshape(dec_seq, n_new, D_MODEL)
    st = [jnp.stack([s[k] for s in states]) for k in range(10)]
    return (y_prompt, y_sample, st[0], st[1], st[2], st[3], st[4], st[5], st[6], st[7], st[8], st[9])
```

```python
import functools
import math

import numpy as np
import jax
import jax.numpy as jnp
from jax import lax
from jax.experimental import pallas as pl
from jax.experimental.pallas import tpu as pltpu

F32 = jnp.float32
BF16 = jnp.bfloat16

D_MODEL = 1024
HEAD_DIM = 64
ATTN_SCALE = HEAD_DIM ** -0.5
BLOCK = 128
LANES = 128
SUBLANES = 8
A_Q_HEADS = 4
A_KV_HEADS = 2
A_WINDOW = 128
B_HEADS = 6
B_BRANCHES = ((128, 1), (512, 4), (2048, 16))
DILATIONS = tuple(r for _, r in B_BRANCHES)
SPAN = BLOCK * max(DILATIONS)
C_WIDTH = 6 * HEAD_DIM
CONV_WIDTH = 3
A_Q_W = A_Q_HEADS * HEAD_DIM
A_KV_W = A_KV_HEADS * HEAD_DIM
B_W = B_HEADS * HEAD_DIM
IN_WIDTH = A_Q_W + 2 * A_KV_W + 3 * B_W + 3 * C_WIDTH
N_BUCKETS = 32
MAX_DISTANCE = 2048
N_GROUPS = 4
EXPERTS_PER_GROUP = 4
N_EXPERTS = N_GROUPS * EXPERTS_PER_GROUP
EXPERT_FF = 512
RMS_EPS = 1e-6
NEG_INF = -1e30

COL_CONV = 0
COL_KA = 3 * C_WIDTH
COL_QA = COL_KA + A_KV_W
COL_VA = COL_QA + A_Q_W
COL_QB = COL_VA + A_KV_W
COL_KB = COL_QB + B_W
COL_VB = COL_KB + B_W

ROW_TILE = 512
IN_TILE = 1024
MOE_TILE = 256
PAIRS = ((0, 1), (0, 2), (0, 3), (1, 2), (1, 3), (2, 3))
N_ROUTE_BUCKETS = N_GROUPS * len(PAIRS)
ROUTE_ROWS = 32
ROW_EXT = D_MODEL + LANES
VMEM_LIMIT = 56 * 1024 * 1024


def _params(*sem):
    return pltpu.CompilerParams(dimension_semantics=sem, vmem_limit_bytes=VMEM_LIMIT)


def _rel_bucket(dist):
    d = np.maximum(dist, 0)
    max_exact = N_BUCKETS // 2
    df = np.maximum(d, max_exact).astype(np.float32)
    large = max_exact + (np.log(df / max_exact) / math.log(MAX_DISTANCE / max_exact)
                         * (N_BUCKETS - max_exact)).astype(np.int32)
    large = np.minimum(large, N_BUCKETS - 1)
    return np.where(d < max_exact, d, large)


def _masked_bias(table, dist, valid):
    bucket = _rel_bucket(dist).reshape(-1, 1)
    onehot = (jnp.asarray(bucket, jnp.int32) == jnp.arange(N_BUCKETS, dtype=jnp.int32)[None, :]).astype(F32)
    b = jnp.dot(onehot, table.astype(F32), precision=lax.Precision.HIGHEST)
    b = jnp.where(jnp.asarray(valid.reshape(-1, 1)), b, NEG_INF)
    return jnp.moveaxis(b.reshape(dist.shape + (table.shape[1],)), -1, 0)


def _band_bias(table, scale):
    i = np.arange(BLOCK)[:, None]
    j = np.arange(2 * BLOCK)[None, :]
    dist = i + BLOCK - j
    b = _masked_bias(table, dist * scale, (dist >= 0) & (dist <= BLOCK))
    return b.reshape(table.shape[1] // 2, 2 * BLOCK, 2 * BLOCK)


def _sample_bias(table, n_new, first_new, positions, window, dilation):
    dist = first_new + np.arange(n_new)[:, None] - np.asarray(positions)[None, :]
    valid = (dist >= 0) & (dist <= window) & (dist % dilation == 0)
    return _masked_bias(table, dist, valid)


def _rms(x, gain):
    return x * lax.rsqrt(jnp.mean(x * x, axis=-1, keepdims=True) + RMS_EPS) * gain


def _inproj_kernel(x_ref, g_ref, w_ref, z_ref):
    h = _rms(x_ref[...], g_ref[...])
    z_ref[...] = jnp.dot(h.astype(BF16), w_ref[...], preferred_element_type=F32)


def _inproj(x, gain, w):
    n = x.shape[0]
    return pl.pallas_call(
        _inproj_kernel,
        out_shape=jax.ShapeDtypeStruct((n, IN_WIDTH), F32),
        grid=(n // IN_TILE,),
        in_specs=[pl.BlockSpec((IN_TILE, D_MODEL), lambda i: (i, 0)),
                  pl.BlockSpec((1, D_MODEL), lambda i: (0, 0)),
                  pl.BlockSpec((D_MODEL, IN_WIDTH), lambda i: (0, 0))],
        out_specs=pl.BlockSpec((IN_TILE, IN_WIDTH), lambda i: (i, 0)),
        compiler_params=_params("parallel"),
    )(x, gain.reshape(1, D_MODEL), w)


def _lane_half(shape):
    return lax.broadcasted_iota(jnp.int32, shape, len(shape) - 1) // HEAD_DIM


def _nt_dot(a, b):
    return lax.dot_general(a, b, (((1,), (1,)), ((), ())), preferred_element_type=F32)


def _band_pair(q2, k2, v2, bias2, penalty):
    lg = _nt_dot(q2, k2) + bias2
    if penalty is not None:
        lg = lg + penalty
    m = jnp.max(lg, axis=-1, keepdims=True)
    p = jnp.exp(lg - m)
    s = jnp.sum(p, axis=-1, keepdims=True)
    acc = jnp.dot(p.astype(BF16), v2, preferred_element_type=F32)
    return acc / s, m + jnp.log(s)


def _prev_key_penalty(first_span):
    col = lax.broadcasted_iota(jnp.int32, (1, 2 * BLOCK), 1)
    return jnp.where((col < BLOCK) & first_span, NEG_INF, 0.0).astype(F32)


def _rows(start, dilation):
    if dilation == 1:
        return pl.ds(start, BLOCK)
    return pl.ds(start, BLOCK, stride=dilation)


def _dilated_kernel(q_ref, kc_ref, kp_ref, vc_ref, vp_ref, bias_ref, o_ref, o_scr, l_scr):
    penalty = _prev_key_penalty(pl.program_id(1) == 0)
    first_half = _lane_half((BLOCK, LANES)) == 0

    for bi, r in enumerate(DILATIONS):
        step = r * BLOCK

        def block(c, n, first, bi=bi, r=r, step=step):
            q_start = c + n * step
            if not isinstance(q_start, int):
                q_start = pl.multiple_of(q_start, BLOCK) if r == 1 else q_start
            cur = _rows(q_start, r)
            qt = (q_ref[cur, :] * ATTN_SCALE).astype(BF16)
            if first:
                src_k, src_v, prev = kp_ref, vp_ref, _rows(SPAN - step + c, r)
            else:
                p_start = q_start - step
                if r == 1:
                    p_start = pl.multiple_of(p_start, BLOCK)
                src_k, src_v, prev = kc_ref, vc_ref, _rows(p_start, r)
            k2 = jnp.concatenate([src_k[prev, :], kc_ref[cur, :]], axis=0).astype(BF16)
            v2 = jnp.concatenate([src_v[prev, :], vc_ref[cur, :]], axis=0).astype(BF16)
            zero = jnp.zeros_like(qt)
            q2 = jnp.concatenate([jnp.where(first_half, qt, zero), jnp.where(first_half, zero, qt)], axis=0)
            o, lse = _band_pair(q2, k2, v2, bias_ref[bi], penalty if first else None)
            o_scr[bi, cur, :] = jnp.where(first_half, o[:BLOCK], o[BLOCK:])
            l_scr[bi, cur, :] = jnp.where(first_half, lse[:BLOCK], lse[BLOCK:])

        n_blocks = SPAN // step
        if r == 1:
            block(0, 0, True)
            lax.fori_loop(1, n_blocks, lambda n, _: block(0, n, False), None, unroll=True)
        else:
            def per_class(c, _, block=block, n_blocks=n_blocks):
                block(c, 0, True)
                for n in range(1, n_blocks):
                    block(c, n, False)
            lax.fori_loop(0, r, per_class, None, unroll=True)

    chunk = 2 * BLOCK

    def combine(j, _):
        rows = pl.ds(pl.multiple_of(j * chunk, chunk), chunk)
        ls = [l_scr[bi, rows, :] for bi in range(len(DILATIONS))]
        m = functools.reduce(jnp.maximum, ls)
        ws = [jnp.exp(l - m) for l in ls]
        num = sum(w * o_scr[bi, rows, :] for bi, w in enumerate(ws))
        o_ref[rows, :] = (num / sum(ws)).astype(o_ref.dtype)

    lax.fori_loop(0, SPAN // chunk, combine, None)


def _dilated_attention(z, bias, n_seq, seq_len):
    spans = seq_len // SPAN
    qb, kb, vb = COL_QB // LANES, COL_KB // LANES, COL_VB // LANES
    blk = (SPAN, LANES)
    cur = lambda col: (lambda b, s, i: (b * spans + s, col + i))
    prev = lambda col: (lambda b, s, i: (b * spans + jnp.maximum(s - 1, 0), col + i))
    return pl.pallas_call(
        _dilated_kernel,
        out_shape=jax.ShapeDtypeStruct((n_seq * seq_len, B_W), BF16),
        grid=(n_seq, spans, B_W // LANES),
        in_specs=[pl.BlockSpec(blk, cur(qb)),
                  pl.BlockSpec(blk, cur(kb)), pl.BlockSpec(blk, prev(kb)),
                  pl.BlockSpec(blk, cur(vb)), pl.BlockSpec(blk, prev(vb)),
                  pl.BlockSpec((None, len(DILATIONS), 2 * BLOCK, 2 * BLOCK), lambda b, s, i: (i, 0, 0, 0))],
        out_specs=pl.BlockSpec(blk, lambda b, s, i: (b * spans + s, i)),
        scratch_shapes=[pltpu.VMEM((len(DILATIONS), SPAN, LANES), F32),
                        pltpu.VMEM((len(DILATIONS), SPAN, LANES), F32)],
        compiler_params=_params("parallel", "parallel", "parallel"),
    )(z, z, z, z, z, bias)


def _window_kernel(sink_ref, q_ref, kc_ref, kp_ref, vc_ref, vp_ref, bias_ref, o_ref):
    penalty = _prev_key_penalty(pl.program_id(1) == 0)
    first_half = _lane_half((BLOCK, LANES)) == 0
    first_half2 = _lane_half((2 * BLOCK, LANES)) == 0

    def block(n, first):
        start = n * BLOCK if isinstance(n, int) else pl.multiple_of(n * BLOCK, BLOCK)
        cur = pl.ds(start, BLOCK)
        if first:
            src_k, src_v, prev = kp_ref, vp_ref, pl.ds(SPAN - BLOCK, BLOCK)
        else:
            src_k, src_v, prev = kc_ref, vc_ref, pl.ds(pl.multiple_of(start - BLOCK, BLOCK), BLOCK)
        k2 = jnp.concatenate([src_k[prev, :], kc_ref[cur, :]], axis=0).astype(BF16)
        vf = jnp.concatenate([src_v[prev, :], vc_ref[cur, :]], axis=0)
        v_swap = pltpu.roll(vf, HEAD_DIM, 1)
        for i in range(A_KV_HEADS):
            qf = q_ref[cur, i * LANES:(i + 1) * LANES] * ATTN_SCALE
            q_swap = pltpu.roll(qf, HEAD_DIM, 1)
            q2 = jnp.concatenate([jnp.where(first_half == (i == 0), qf if a == i else q_swap, 0.0)
                                  for a in range(2)], axis=0).astype(BF16)
            v2 = jnp.where(first_half2 == (i == 0), vf, v_swap).astype(BF16)
            o, lse = _band_pair(q2, k2, v2, bias_ref[i], penalty if first else None)
            outs = [o[a * BLOCK:(a + 1) * BLOCK] * jax.nn.sigmoid(lse[a * BLOCK:(a + 1) * BLOCK] - sink_ref[2 * i + a])
                    for a in range(2)]
            o_ref[cur, i * LANES:(i + 1) * LANES] = jnp.where(first_half, outs[0], outs[1]).astype(o_ref.dtype)

    block(0, True)
    lax.fori_loop(1, SPAN // BLOCK, lambda n, _: block(n, False), None, unroll=True)


def _window_attention(z, bias, sinks, n_seq, seq_len):
    spans = seq_len // SPAN
    ka, va = COL_KA // LANES, COL_VA // LANES
    blk = (SPAN, LANES)
    cur = lambda col: (lambda b, s: (b * spans + s, col))
    prev = lambda col: (lambda b, s: (b * spans + jnp.maximum(s - 1, 0), col))
    return pl.pallas_call(
        _window_kernel,
        out_shape=jax.ShapeDtypeStruct((n_seq * seq_len, A_Q_W), BF16),
        grid=(n_seq, spans),
        in_specs=[pl.BlockSpec(memory_space=pltpu.SMEM),
                  pl.BlockSpec((SPAN, A_Q_W), lambda b, s: (b * spans + s, COL_QA // A_Q_W)),
                  pl.BlockSpec(blk, cur(ka)), pl.BlockSpec(blk, prev(ka)),
                  pl.BlockSpec(blk, cur(va)), pl.BlockSpec(blk, prev(va)),
                  pl.BlockSpec((A_KV_HEADS, 2 * BLOCK, 2 * BLOCK), lambda b, s: (0, 0, 0))],
        out_specs=pl.BlockSpec((SPAN, A_Q_W), lambda b, s: (b * spans + s, 0)),
        compiler_params=_params("parallel", "parallel"),
    )(sinks, z, z, z, z, z, bias)


def _shift_rows(u, filler, k):
    rolled = pltpu.roll(u, k, 0)
    row = lax.broadcasted_iota(jnp.int32, u.shape, 0)
    n_fill = filler.shape[0]
    for j in range(k):
        rolled = jnp.where(row == j, filler[n_fill - k + j:n_fill - k + j + 1, :], rolled)
    return rolled


def _gated_conv(xc, bg, cg, filler, cw):
    u = cg * xc
    conv = cw[0:1, :] * _shift_rows(u, filler, 2) + cw[1:2, :] * _shift_rows(u, filler, 1) + cw[2:3, :] * u
    return bg * conv, u


def _pad_rows(x, rows):
    return jnp.concatenate([x, jnp.zeros((rows - x.shape[0], x.shape[1]), x.dtype)], axis=0)


def _heads_by_dim(ref):
    h, d, length = ref.shape
    return ref[...].reshape(h * d, length).astype(BF16)


def _sample_kernel(z_ref, cak_ref, cav_ref, cbk_ref, cbv_ref, st_ref, cw_ref, sink_ref, bias_a_ref,
                   bias_b_ref, mix_ref, conv_ref, *, n_new):
    z = z_ref[...]
    la = cak_ref.shape[2]
    lb = cbk_ref.shape[2]

    ka_new = _pad_rows(z[:, COL_KA:COL_KA + A_KV_W], LANES).astype(BF16)
    va_new = _pad_rows(z[:, COL_VA:COL_VA + A_KV_W], LANES).astype(BF16)
    half = _lane_half((n_new, LANES))
    pieces = []
    for i in range(A_KV_HEADS):
        qf = z[:, COL_QA + i * LANES:COL_QA + (i + 1) * LANES] * ATTN_SCALE
        for a in range(2):
            pieces.append(jnp.where(half == i, qf if a == i else pltpu.roll(qf, HEAD_DIM, 1), 0.0))
    qa = jnp.concatenate(pieces, axis=0).astype(BF16)
    lc = jnp.dot(qa, _heads_by_dim(cak_ref), preferred_element_type=F32) + bias_a_ref[:, :la]
    ln = _nt_dot(qa, ka_new) + bias_a_ref[:, la:]
    m = jnp.maximum(jnp.max(lc, axis=-1, keepdims=True), jnp.max(ln, axis=-1, keepdims=True))
    pc = jnp.exp(lc - m)
    pn = jnp.exp(ln - m)
    s = jnp.sum(pc, axis=-1, keepdims=True) + jnp.sum(pn, axis=-1, keepdims=True)
    oa = (_nt_dot(pc.astype(BF16), _heads_by_dim(cav_ref))
          + jnp.dot(pn.astype(BF16), va_new, preferred_element_type=F32))
    oa = oa / s * jax.nn.sigmoid(m + jnp.log(s) - sink_ref[...])
    oa_blocks = []
    for i in range(A_KV_HEADS):
        per_half = []
        for a in range(2):
            rows = oa[(2 * i + a) * n_new:(2 * i + a + 1) * n_new, :]
            per_half.append(rows if a == i else pltpu.roll(rows, HEAD_DIM, 1))
        oa_blocks.append(jnp.where(half == 0, per_half[0], per_half[1]))

    qf = z[:, COL_QB:COL_QB + B_W] * ATTN_SCALE
    head_of_lane = _lane_half((n_new, B_W))
    qb = jnp.concatenate([jnp.where(head_of_lane == h, qf, 0.0) for h in range(B_HEADS)], axis=0).astype(BF16)
    kb_new = _pad_rows(z[:, COL_KB:COL_KB + B_W], LANES).astype(BF16)
    vb_new = _pad_rows(z[:, COL_VB:COL_VB + B_W], LANES).astype(BF16)
    lg_c = jnp.dot(qb, _heads_by_dim(cbk_ref), preferred_element_type=F32)
    lg_n = _nt_dot(qb, kb_new)
    parts = []
    for bi, (w, r) in enumerate(B_BRANCHES):
        lo = lb - min(lb, -(-w // LANES) * LANES)
        lc = lg_c[:, lo:] + bias_b_ref[bi, :, lo:lb]
        ln = lg_n + bias_b_ref[bi, :, lb:]
        m = jnp.maximum(jnp.max(lc, axis=-1, keepdims=True), jnp.max(ln, axis=-1, keepdims=True))
        pc = jnp.exp(lc - m)
        pn = jnp.exp(ln - m)
        s = jnp.sum(pc, axis=-1, keepdims=True) + jnp.sum(pn, axis=-1, keepdims=True)
        parts.append((lo, pc, pn, s, m + jnp.log(s)))
    m_all = functools.reduce(jnp.maximum, [p[4] for p in parts])
    ws = [jnp.exp(p[4] - m_all) for p in parts]
    den = sum(ws)
    p_new = None
    los = sorted({p[0] for p in parts} | {lb})
    segs = [None] * (len(los) - 1)
    for (lo, pc, pn, s, _), w in zip(parts, ws):
        coef = w / (den * s)
        p_new = coef * pn if p_new is None else p_new + coef * pn
        for si in range(len(segs)):
            a0, a1 = los[si], los[si + 1]
            if a0 >= lo:
                piece = coef * pc[:, a0 - lo:a1 - lo]
                segs[si] = piece if segs[si] is None else segs[si] + piece
    p_cache = jnp.concatenate(segs, axis=1) if len(segs) > 1 else segs[0]
    ob = (_nt_dot(p_cache.astype(BF16), _heads_by_dim(cbv_ref))
          + jnp.dot(p_new.astype(BF16), vb_new, preferred_element_type=F32))
    ob_rows = sum(jnp.where(head_of_lane == h, ob[h * n_new:(h + 1) * n_new, :], 0.0) for h in range(B_HEADS))

    cz = z[:, COL_CONV:COL_CONV + 3 * C_WIDTH]
    oc, u = _gated_conv(cz[:, :C_WIDTH], cz[:, C_WIDTH:2 * C_WIDTH], cz[:, 2 * C_WIDTH:], st_ref[...], cw_ref[...])
    conv_ref[...] = u[n_new - (CONV_WIDTH - 1):, :]
    mix_ref[...] = jnp.concatenate(oa_blocks + [ob_rows, oc], axis=1)


def _sample_mixer(z, row0, cak, cav, cbk, cbv, layer, state, cw, sink_rows, bias_a, bias_b, n_new):
    n_seq = cak.shape[1]
    blk0 = row0 // n_new
    cache = lambda a: pl.BlockSpec((None, None) + a.shape[2:], lambda b: (layer, b, 0, 0, 0))
    per_seq = lambda shape: pl.BlockSpec((None,) + shape, lambda b: (b, 0, 0))
    const = lambda a: pl.BlockSpec(a.shape, lambda b: (0,) * a.ndim)
    return pl.pallas_call(
        functools.partial(_sample_kernel, n_new=n_new),
        out_shape=(jax.ShapeDtypeStruct((n_seq * n_new, D_MODEL), F32),
                   jax.ShapeDtypeStruct((n_seq, CONV_WIDTH - 1, C_WIDTH), F32)),
        grid=(n_seq,),
        in_specs=[pl.BlockSpec((n_new, IN_WIDTH), lambda b: (blk0 + b, 0)),
                  cache(cak), cache(cav), cache(cbk), cache(cbv),
                  per_seq((CONV_WIDTH - 1, C_WIDTH)), const(cw), const(sink_rows), const(bias_a), const(bias_b)],
        out_specs=(pl.BlockSpec((n_new, D_MODEL), lambda b: (b, 0)), per_seq((CONV_WIDTH - 1, C_WIDTH))),
        compiler_params=_params("parallel"),
    )(z, cak, cav, cbk, cbv, state, cw, sink_rows, bias_a, bias_b)


def _first_index(vals, best):
    idx = jnp.full(best.shape, len(vals) - 1, jnp.int32)
    for j in range(len(vals) - 2, -1, -1):
        idx = jnp.where(vals[j] == best, j, idx)
    return idx


def _route(lt):
    g = [lt[k:k + 1, :] for k in range(N_GROUPS)]
    g_max = functools.reduce(jnp.maximum, g)
    g_idx = _first_index(g, g_max)
    g_w = 1.0 / sum(jnp.exp(v - g_max) for v in g)
    e = []
    for j in range(EXPERTS_PER_GROUP):
        v = lt[N_GROUPS + j:N_GROUPS + j + 1, :]
        for gi in range(1, N_GROUPS):
            row = N_GROUPS + gi * EXPERTS_PER_GROUP + j
            v = jnp.where(g_idx == gi, lt[row:row + 1, :], v)
        e.append(v)
    e1 = functools.reduce(jnp.maximum, e)
    i1 = _first_index(e, e1)
    rest = [jnp.where(i1 == j, -jnp.inf, e[j]) for j in range(EXPERTS_PER_GROUP)]
    e2 = functools.reduce(jnp.maximum, rest)
    i2 = _first_index(rest, e2)
    t = jnp.exp(e2 - e1)
    w1 = g_w / (1.0 + t)
    w2 = g_w * t / (1.0 + t)
    swap = i2 < i1
    lo = jnp.where(swap, i2, i1)
    hi = jnp.where(swap, i1, i2)
    pair = jnp.where(lo == 0, hi - 1, jnp.where(lo == 1, hi + 1, len(PAIRS) - 1))
    bucket = g_idx * len(PAIRS) + pair
    return bucket, jnp.where(swap, w2, w1), jnp.where(swap, w1, w2)


def _outproj_kernel(y_ref, oa_ref, ob_ref, zc_ref, zh_ref, ms_ref, wout_ref, cw_ref, gn_ref, wr_ref, br_ref,
                    y1_ref, info_ref, cnt_ref, ut_ref, acc_scr, carry_scr, *, prompt_tiles, tiles_per_seq):
    i = pl.program_id(0)
    tile = y_ref.shape[0]

    @pl.when(i == 0)
    def _():
        carry_scr[...] = jnp.zeros_like(carry_scr)

    @pl.when(i < prompt_tiles)
    def _():
        zc = zc_ref[...]
        zh = zh_ref[...]
        halo = zh[:, 2 * C_WIDTH:] * zh[:, :C_WIDTH]
        halo = jnp.where(i % tiles_per_seq == 0, 0.0, halo)
        oc, u = _gated_conv(zc[:, :C_WIDTH], zc[:, C_WIDTH:2 * C_WIDTH], zc[:, 2 * C_WIDTH:], halo, cw_ref[...])
        ut_ref[...] = u[tile - 8:, :]
        acc_scr[...] = (
            jnp.dot(oa_ref[...], wout_ref[0:A_Q_W, :], preferred_element_type=F32)
            + jnp.dot(ob_ref[...], wout_ref[A_Q_W:A_Q_W + B_W, :], preferred_element_type=F32)
            + jnp.dot(oc.astype(BF16), wout_ref[A_Q_W + B_W:, :], preferred_element_type=F32))

    @pl.when(i >= prompt_tiles)
    def _():
        ut_ref[...] = jnp.zeros_like(ut_ref)
        acc_scr[...] = jnp.dot(ms_ref[...].astype(BF16), wout_ref[...], preferred_element_type=F32)

    y1 = y_ref[...] + acc_scr[...]
    y1_ref[:, :D_MODEL] = y1
    xn = _rms(y1, gn_ref[...])

    lt = lax.dot_general(wr_ref[...], xn, (((1,), (1,)), ((), ())), precision=lax.Precision.HIGHEST,
                         preferred_element_type=F32) + br_ref[...]
    bucket, w_lo, w_hi = _route(lt)
    onehot = (lax.broadcasted_iota(jnp.int32, (ROUTE_ROWS, tile), 0) == bucket).astype(F32)
    upper = (lax.broadcasted_iota(jnp.int32, (tile, tile), 0)
             <= lax.broadcasted_iota(jnp.int32, (tile, tile), 1)).astype(BF16)
    running = jnp.dot(onehot.astype(BF16), upper, preferred_element_type=F32)
    carry = carry_scr[...]
    rank = jnp.sum(onehot * (running - 1.0 + carry), axis=0, keepdims=True)
    carry = carry + jnp.sum(onehot, axis=1, keepdims=True)
    carry_scr[...] = carry
    cnt_ref[...] = jnp.broadcast_to(carry, cnt_ref.shape)
    info_ref[...] = jnp.concatenate([bucket.astype(F32), rank, jnp.zeros((SUBLANES - 2, tile), F32)], axis=0)
    y1_ref[:, D_MODEL:] = jnp.concatenate([w_lo, w_hi, jnp.zeros((LANES - 2, tile), F32)], axis=0).T


def _outproj_route(y, oa, ob, z, mix_s, w_out, cw, gain, w_route, b_route, n_prompt, seq_len):
    n = y.shape[0]
    tiles = n // ROW_TILE
    p_tiles = n_prompt // ROW_TILE
    halo_blocks = ROW_TILE // 8
    conv_w = 3 * C_WIDTH
    pidx = lambda i: jnp.minimum(i, p_tiles - 1)
    const = lambda a: pl.BlockSpec(a.shape, lambda i: (0,) * a.ndim)
    gain = gain.reshape(1, D_MODEL)
    return pl.pallas_call(
        functools.partial(_outproj_kernel, prompt_tiles=p_tiles, tiles_per_seq=seq_len // ROW_TILE),
        out_shape=(jax.ShapeDtypeStruct((n, ROW_EXT), F32),
                   jax.ShapeDtypeStruct((8, n), F32),
                   jax.ShapeDtypeStruct((ROUTE_ROWS, LANES), F32),
                   jax.ShapeDtypeStruct((tiles * 8, C_WIDTH), F32)),
        grid=(tiles,),
        in_specs=[pl.BlockSpec((ROW_TILE, D_MODEL), lambda i: (i, 0)),
                  pl.BlockSpec((ROW_TILE, A_Q_W), lambda i: (pidx(i), 0)),
                  pl.BlockSpec((ROW_TILE, B_W), lambda i: (pidx(i), 0)),
                  pl.BlockSpec((ROW_TILE, conv_w), lambda i: (pidx(i), 0)),
                  pl.BlockSpec((8, conv_w), lambda i: (jnp.maximum(pidx(i) * halo_blocks - 1, 0), 0)),
                  pl.BlockSpec((ROW_TILE, D_MODEL), lambda i: (jnp.maximum(i - p_tiles, 0), 0)),
                  const(w_out), const(cw), const(gain), const(w_route), const(b_route)],
        out_specs=(pl.BlockSpec((ROW_TILE, ROW_EXT), lambda i: (i, 0)),
                   pl.BlockSpec((8, ROW_TILE), lambda i: (0, i)),
                   pl.BlockSpec((ROUTE_ROWS, LANES), lambda i: (0, 0)),
                   pl.BlockSpec((8, C_WIDTH), lambda i: (i, 0))),
        scratch_shapes=[pltpu.VMEM((ROW_TILE, D_MODEL), F32), pltpu.VMEM((ROUTE_ROWS, 1), F32)],
        compiler_params=_params("arbitrary"),
    )(y, oa, ob, z, z, mix_s, w_out, cw, gain, w_route, b_route)


def _moe_kernel(e_lo_ref, e_hi_ref, used_ref, x_ref, g_ref, wg1, wu1, wd1, wg2, wu2, wd2, *rest):
    o_ref = rest[-1]
    t = pl.program_id(0)

    @pl.when(used_ref[t] > 0)
    def _():
        y1 = x_ref[:, :D_MODEL]
        w = x_ref[:, D_MODEL:]
        x = _rms(y1, g_ref[...]).astype(BF16)

        def expert(wg, wu, wd, scale):
            g = jnp.dot(x, wg[...], preferred_element_type=F32)
            u = jnp.dot(x, wu[...], preferred_element_type=F32)
            h = g * jax.nn.sigmoid(g) * u * scale
            return jnp.dot(h.astype(BF16), wd[...], preferred_element_type=F32)

        y2 = y1 + expert(wg1, wu1, wd1, w[:, 0:1]) + expert(wg2, wu2, wd2, w[:, 1:2])
        o_ref[...] = _rms(y2, rest[0][...]) if len(rest) > 1 else y2

    @pl.when(used_ref[t] == 0)
    def _():
        o_ref[...] = jnp.zeros_like(o_ref)


def _experts(xs, gain, e_lo, e_hi, used, w_gate, w_up, w_down, layer, final_gain=None):
    n_tiles = xs.shape[0] // MOE_TILE
    up_spec = lambda sel: pl.BlockSpec((None, None, D_MODEL, EXPERT_FF),
                                       lambda t, lo, hi, u: (layer, (lo, hi)[sel][t], 0, 0))
    down_spec = lambda sel: pl.BlockSpec((None, None, EXPERT_FF, D_MODEL),
                                         lambda t, lo, hi, u: (layer, (lo, hi)[sel][t], 0, 0))
    gain_spec = pl.BlockSpec((1, D_MODEL), lambda t, lo, hi, u: (0, 0))
    extra = [] if final_gain is None else [final_gain.reshape(1, D_MODEL)]
    grid_spec = pltpu.PrefetchScalarGridSpec(
        num_scalar_prefetch=3,
        grid=(n_tiles,),
        in_specs=[pl.BlockSpec((MOE_TILE, ROW_EXT), lambda t, lo, hi, u: (t, 0)), gain_spec,
                  up_spec(0), up_spec(0), down_spec(0), up_spec(1), up_spec(1), down_spec(1)]
                 + [gain_spec] * len(extra),
        out_specs=pl.BlockSpec((MOE_TILE, D_MODEL), lambda t, lo, hi, u: (t, 0)))
    return pl.pallas_call(
        _moe_kernel,
        out_shape=jax.ShapeDtypeStruct((xs.shape[0], D_MODEL), F32),
        grid_spec=grid_spec,
        compiler_params=_params("arbitrary"),
    )(e_lo, e_hi, used, xs, gain.reshape(1, D_MODEL), w_gate, w_up, w_down, w_gate, w_up, w_down, *extra)


def _dispatch_plan(info, counts, n):
    n_tiles = -(-n // MOE_TILE) + N_ROUTE_BUCKETS
    bucket = info[0].astype(jnp.int32)
    rank = info[1].astype(jnp.int32)
    counts = counts[:N_ROUTE_BUCKETS, 0].astype(jnp.int32)
    tiles_per_bucket = (counts + MOE_TILE - 1) // MOE_TILE
    tile_end = jnp.cumsum(tiles_per_bucket)
    row_start = (tile_end - tiles_per_bucket) * MOE_TILE
    dest = row_start[bucket] + rank
    src = (jnp.arange(n_tiles * MOE_TILE, dtype=jnp.int32) % n).at[dest].set(jnp.arange(n, dtype=jnp.int32))
    tile_ids = jnp.arange(n_tiles, dtype=jnp.int32)
    tile_bucket = jnp.minimum(jnp.searchsorted(tile_end, tile_ids, side="right").astype(jnp.int32),
                              N_ROUTE_BUCKETS - 1)
    used = (tile_ids < tile_end[-1]).astype(jnp.int32)
    pair = tile_bucket % len(PAIRS)
    base = (tile_bucket // len(PAIRS)) * EXPERTS_PER_GROUP
    pairs = jnp.asarray(PAIRS, jnp.int32)
    return dest, src, base + pairs[pair, 0], base + pairs[pair, 1], used


def _permute_in_columns(w):
    attn = A_Q_W + 2 * A_KV_W + 3 * B_W
    return jnp.concatenate([w[:, attn:], w[:, A_Q_W:A_Q_W + A_KV_W], w[:, :A_Q_W], w[:, A_Q_W + A_KV_W:attn]],
                           axis=1)


def kernel(x_prompt, x_sample, cache_a_k, cache_a_v, cache_b_k, cache_b_v, state_conv, rel_bias_table,
           w_in, w_out, conv_w, attn_sinks, norm_mix, norm_ffn, w_group, b_group, w_router, b_router,
           w_gate, w_up, w_down, norm_final):
    n_seq, seq_len, _ = x_prompt.shape
    dec_seq, n_new, _ = x_sample.shape
    depth = w_in.shape[0]
    n_prompt = n_seq * seq_len
    n_sample = dec_seq * n_new
    n = n_prompt + n_sample
    la, lb = cache_a_k.shape[2], cache_b_k.shape[2]
    assert seq_len % SPAN == 0 and n_prompt % ROW_TILE == 0 and n_sample % ROW_TILE == 0 and n % IN_TILE == 0
    assert seq_len >= SPAN and la % LANES == 0 and lb % LANES == 0 and n_new == 8

    table_a, table_b = rel_bias_table[:, :A_Q_HEADS], rel_bias_table[:, A_Q_HEADS:]
    bias_a = _band_bias(table_a, 1)
    bias_b = jnp.stack([_band_bias(table_b, r) for r in DILATIONS], axis=1)
    sbias_a = _sample_bias(table_a, n_new, la, np.arange(la + LANES), A_WINDOW, 1)
    sbias_a = sbias_a.reshape(A_Q_HEADS * n_new, la + LANES)
    sbias_b = jnp.stack([_sample_bias(table_b, n_new, lb, np.arange(lb + LANES), w, r)
                         .reshape(B_HEADS * n_new, lb + LANES) for w, r in B_BRANCHES])
    cak, cav, cbk, cbv = (c.transpose(0, 1, 3, 4, 2) for c in (cache_a_k, cache_a_v, cache_b_k, cache_b_v))

    w_in_b = jnp.stack([_permute_in_columns(w_in[l]) for l in range(depth)]).astype(BF16)
    w_out_b = w_out.astype(BF16)
    w_gate_b, w_up_b, w_down_b = w_gate.astype(BF16), w_up.astype(BF16), w_down.astype(BF16)
    pad = ROUTE_ROWS - N_GROUPS - N_EXPERTS
    w_route = jnp.pad(jnp.concatenate([w_group, w_router], axis=2).transpose(0, 2, 1), ((0, 0), (0, pad), (0, 0)))
    b_route = jnp.pad(jnp.concatenate([b_group, b_router], axis=1), ((0, 0), (0, pad)))[..., None]
    sink_rows = jnp.repeat(attn_sinks, n_new, axis=1)[..., None]

    y = jnp.concatenate([x_prompt.reshape(n_prompt, D_MODEL), x_sample.reshape(n_sample, D_MODEL)], axis=0)
    states = []
    for l in range(depth):
        z = _inproj(y, norm_mix[l], w_in_b[l])
        oa = _window_attention(z, bias_a, attn_sinks[l], n_seq, seq_len)
        ob = _dilated_attention(z, bias_b, n_seq, seq_len)
        mix_s, conv_s = _sample_mixer(z, n_prompt, cak, cav, cbk, cbv, l, state_conv[l], conv_w[l], sink_rows[l],
                                      sbias_a, sbias_b, n_new)
        y1, info, counts, u_tail = _outproj_route(
            y, oa, ob, z, mix_s, w_out_b[l], conv_w[l], norm_ffn[l], w_route[l], b_route[l], n_prompt, seq_len)
        dest, src, e_lo, e_hi, used = _dispatch_plan(info, counts, n)
        y_sorted = _experts(y1[src], norm_ffn[l], e_lo, e_hi, used, w_gate_b, w_up_b, w_down_b, l,
                            final_gain=norm_final if l == depth - 1 else None)
        if l < depth - 1:
            y = y_sorted[dest]

        lap, lbp = min(A_WINDOW, seq_len), min(SPAN, seq_len)

        def prompt_tail(length, col, width, heads):
            rows = [z[(b + 1) * seq_len - length:(b + 1) * seq_len, col:col + width] for b in range(n_seq)]
            return jnp.stack(rows).reshape(n_seq, length, heads, HEAD_DIM)

        def sample_rows(col, width, heads):
            return z[n_prompt:, col:col + width].reshape(dec_seq, n_new, heads, HEAD_DIM)

        last_tile = [((b + 1) * seq_len // ROW_TILE - 1) * SUBLANES for b in range(n_seq)]
        conv_p = jnp.stack([u_tail[t + SUBLANES - (CONV_WIDTH - 1):t + SUBLANES, :] for t in last_tile])
        states.append((
            prompt_tail(lap, COL_KA, A_KV_W, A_KV_HEADS), prompt_tail(lap, COL_VA, A_KV_W, A_KV_HEADS),
            prompt_tail(lbp, COL_KB, B_W, B_HEADS), prompt_tail(lbp, COL_VB, B_W, B_HEADS), conv_p,
            sample_rows(COL_KA, A_KV_W, A_KV_HEADS), sample_rows(COL_VA, A_KV_W, A_KV_HEADS),
            sample_rows(COL_KB, B_W, B_HEADS), sample_rows(COL_VB, B_W, B_HEADS), conv_s))

    y_prompt = y_sorted[dest[:n_prompt]].reshape(n_seq, seq_len, D_MODEL)
    y_sample = y_sorted[dest[n_prompt:]].reshape(dec_seq, n_new, D_MODEL)
    st = [jnp.stack([s[k] for s in states]) for k in range(10)]
    return (y_prompt, y_sample, st[0], st[1], st[2], st[3], st[4], st[5], st[6], st[7], st[8], st[9])
```

```python
import functools
import math

import numpy as np
import jax
import jax.numpy as jnp
from jax import lax
from jax.experimental import pallas as pl
from jax.experimental.pallas import tpu as pltpu

F32 = jnp.float32
BF16 = jnp.bfloat16

D_MODEL = 1024
HEAD_DIM = 64
ATTN_SCALE = HEAD_DIM ** -0.5
LOG2E = math.log2(math.e)
LN2 = math.log(2.0)
BAND_Q_SCALE = ATTN_SCALE * LOG2E
BLOCK = 128
LANES = 128
SUBLANES = 8
A_Q_HEADS = 4
A_KV_HEADS = 2
A_WINDOW = 128
B_HEADS = 6
B_BRANCHES = ((128, 1), (512, 4), (2048, 16))
DILATIONS = tuple(r for _, r in B_BRANCHES)
SPAN = BLOCK * max(DILATIONS)
C_WIDTH = 6 * HEAD_DIM
CONV_WIDTH = 3
A_Q_W = A_Q_HEADS * HEAD_DIM
A_KV_W = A_KV_HEADS * HEAD_DIM
B_W = B_HEADS * HEAD_DIM
IN_WIDTH = A_Q_W + 2 * A_KV_W + 3 * B_W + 3 * C_WIDTH
N_BUCKETS = 32
MAX_DISTANCE = 2048
N_GROUPS = 4
EXPERTS_PER_GROUP = 4
N_EXPERTS = N_GROUPS * EXPERTS_PER_GROUP
EXPERT_FF = 512
RMS_EPS = 1e-6
NEG_INF = -1e30

COL_CONV = 0
COL_KA = 3 * C_WIDTH
COL_QA = COL_KA + A_KV_W
COL_VA = COL_QA + A_Q_W
COL_QB = COL_VA + A_KV_W
COL_KB = COL_QB + B_W
COL_VB = COL_KB + B_W

ROW_TILE = 512
IN_TILE = 1024
MOE_TILE = 256
SEQS_PER_STEP = 2
PAIRS = ((0, 1), (0, 2), (0, 3), (1, 2), (1, 3), (2, 3))
N_ROUTE_BUCKETS = N_GROUPS * len(PAIRS)
ROUTE_ROWS = 32
ROW_EXT = D_MODEL + LANES
VMEM_LIMIT = 56 * 1024 * 1024


def _params(*sem):
    return pltpu.CompilerParams(dimension_semantics=sem, vmem_limit_bytes=VMEM_LIMIT)


def _rel_bucket(dist):
    d = np.maximum(dist, 0)
    max_exact = N_BUCKETS // 2
    df = np.maximum(d, max_exact).astype(np.float32)
    large = max_exact + (np.log(df / max_exact) / math.log(MAX_DISTANCE / max_exact)
                         * (N_BUCKETS - max_exact)).astype(np.int32)
    large = np.minimum(large, N_BUCKETS - 1)
    return np.where(d < max_exact, d, large)


def _masked_bias(table, dist, valid):
    bucket = _rel_bucket(dist).reshape(-1, 1)
    onehot = (jnp.asarray(bucket, jnp.int32) == jnp.arange(N_BUCKETS, dtype=jnp.int32)[None, :]).astype(F32)
    b = jnp.dot(onehot, table.astype(F32), precision=lax.Precision.HIGHEST)
    b = jnp.where(jnp.asarray(valid.reshape(-1, 1)), b, NEG_INF)
    return jnp.moveaxis(b.reshape(dist.shape + (table.shape[1],)), -1, 0)


def _band_bias(table, scale):
    i = np.arange(BLOCK)[:, None]
    j = np.arange(2 * BLOCK)[None, :]
    dist = i + BLOCK - j
    b = _masked_bias(table, dist * scale, (dist >= 0) & (dist <= BLOCK)) * LOG2E
    return b.reshape(table.shape[1] // 2, 2 * BLOCK, 2 * BLOCK)


def _sample_bias(table, n_new, first_new, positions, window, dilation):
    dist = first_new + np.arange(n_new)[:, None] - np.asarray(positions)[None, :]
    valid = (dist >= 0) & (dist <= window) & (dist % dilation == 0)
    return _masked_bias(table, dist, valid)


def _rms(x, gain):
    return x * lax.rsqrt(jnp.mean(x * x, axis=-1, keepdims=True) + RMS_EPS) * gain


def _inproj_kernel(x_ref, g_ref, w_ref, z_ref):
    h = _rms(x_ref[...], g_ref[...])
    z_ref[...] = jnp.dot(h.astype(BF16), w_ref[...], preferred_element_type=F32)


def _inproj(x, gain, w):
    n = x.shape[0]
    return pl.pallas_call(
        _inproj_kernel,
        out_shape=jax.ShapeDtypeStruct((n, IN_WIDTH), F32),
        grid=(n // IN_TILE,),
        in_specs=[pl.BlockSpec((IN_TILE, D_MODEL), lambda i: (i, 0)),
                  pl.BlockSpec((1, D_MODEL), lambda i: (0, 0)),
                  pl.BlockSpec((D_MODEL, IN_WIDTH), lambda i: (0, 0))],
        out_specs=pl.BlockSpec((IN_TILE, IN_WIDTH), lambda i: (i, 0)),
        compiler_params=_params("parallel"),
    )(x, gain.reshape(1, D_MODEL), w)


def _lane_half(shape):
    return lax.broadcasted_iota(jnp.int32, shape, len(shape) - 1) // HEAD_DIM


def _nt_dot(a, b):
    return lax.dot_general(a, b, (((1,), (1,)), ((), ())), preferred_element_type=F32)


def _band_pair(q2, k2, v2, bias2, penalty):
    lg = _nt_dot(q2, k2) + bias2
    if penalty is not None:
        lg = lg + penalty
    m = jnp.max(lg, axis=-1, keepdims=True)
    p = jnp.exp2(lg - m)
    s = jnp.sum(p, axis=-1, keepdims=True)
    acc = jnp.dot(p.astype(BF16), v2, preferred_element_type=F32)
    return acc / s, (m + jnp.log2(s)) * LN2


def _prev_key_penalty(first_span):
    col = lax.broadcasted_iota(jnp.int32, (1, 2 * BLOCK), 1)
    return jnp.where((col < BLOCK) & first_span, NEG_INF, 0.0).astype(F32)


def _rows(start, dilation):
    if dilation == 1:
        return pl.ds(start, BLOCK)
    return pl.ds(start, BLOCK, stride=dilation)


def _dilated_kernel(q_ref, kc_ref, kp_ref, vc_ref, vp_ref, bias_ref, o_ref, o_scr, l_scr, first_bias_scr):
    first_half = _lane_half((BLOCK, LANES)) == 0
    penalty = _prev_key_penalty(pl.program_id(1) == 0)
    for bi in range(len(DILATIONS)):
        first_bias_scr[bi] = bias_ref[bi] + penalty

    for bi, r in enumerate(DILATIONS):
        step = r * BLOCK

        def block(c, n, first, bi=bi, r=r, step=step):
            q_start = c + n * step
            if not isinstance(q_start, int):
                q_start = pl.multiple_of(q_start, BLOCK) if r == 1 else q_start
            cur = _rows(q_start, r)
            qt = (q_ref[cur, :] * BAND_Q_SCALE).astype(BF16)
            if first:
                src_k, src_v, prev = kp_ref, vp_ref, _rows(SPAN - step + c, r)
            else:
                p_start = q_start - step
                if r == 1:
                    p_start = pl.multiple_of(p_start, BLOCK)
                src_k, src_v, prev = kc_ref, vc_ref, _rows(p_start, r)
            k2 = jnp.concatenate([src_k[prev, :], kc_ref[cur, :]], axis=0).astype(BF16)
            v2 = jnp.concatenate([src_v[prev, :], vc_ref[cur, :]], axis=0).astype(BF16)
            zero = jnp.zeros_like(qt)
            q2 = jnp.concatenate([jnp.where(first_half, qt, zero), jnp.where(first_half, zero, qt)], axis=0)
            o, lse = _band_pair(q2, k2, v2, first_bias_scr[bi] if first else bias_ref[bi], None)
            o_scr[bi, cur, :] = jnp.where(first_half, o[:BLOCK], o[BLOCK:])
            l_scr[bi, cur, :] = jnp.where(first_half, lse[:BLOCK], lse[BLOCK:])

        n_blocks = SPAN // step
        if r == 1:
            block(0, 0, True)
            lax.fori_loop(1, n_blocks, lambda n, _: block(0, n, False), None, unroll=True)
        else:
            def per_class(c, _, block=block, n_blocks=n_blocks):
                block(c, 0, True)
                for n in range(1, n_blocks):
                    block(c, n, False)
            lax.fori_loop(0, r, per_class, None, unroll=True)

    chunk = 2 * BLOCK

    def combine(j, _):
        rows = pl.ds(pl.multiple_of(j * chunk, chunk), chunk)
        ls = [l_scr[bi, rows, :] for bi in range(len(DILATIONS))]
        m = functools.reduce(jnp.maximum, ls)
        ws = [jnp.exp(l - m) for l in ls]
        num = sum(w * o_scr[bi, rows, :] for bi, w in enumerate(ws))
        o_ref[rows, :] = (num / sum(ws)).astype(o_ref.dtype)

    lax.fori_loop(0, SPAN // chunk, combine, None)


def _dilated_attention(z, bias, n_seq, seq_len):
    spans = seq_len // SPAN
    qb, kb, vb = COL_QB // LANES, COL_KB // LANES, COL_VB // LANES
    blk = (SPAN, LANES)
    cur = lambda col: (lambda b, s, i: (b * spans + s, col + i))
    prev = lambda col: (lambda b, s, i: (b * spans + jnp.maximum(s - 1, 0), col + i))
    return pl.pallas_call(
        _dilated_kernel,
        out_shape=jax.ShapeDtypeStruct((n_seq * seq_len, B_W), BF16),
        grid=(n_seq, spans, B_W // LANES),
        in_specs=[pl.BlockSpec(blk, cur(qb)),
                  pl.BlockSpec(blk, cur(kb)), pl.BlockSpec(blk, prev(kb)),
                  pl.BlockSpec(blk, cur(vb)), pl.BlockSpec(blk, prev(vb)),
                  pl.BlockSpec((None, len(DILATIONS), 2 * BLOCK, 2 * BLOCK), lambda b, s, i: (i, 0, 0, 0))],
        out_specs=pl.BlockSpec(blk, lambda b, s, i: (b * spans + s, i)),
        scratch_shapes=[pltpu.VMEM((len(DILATIONS), SPAN, LANES), F32),
                        pltpu.VMEM((len(DILATIONS), SPAN, LANES), F32),
                        pltpu.VMEM((len(DILATIONS), 2 * BLOCK, 2 * BLOCK), F32)],
        compiler_params=_params("parallel", "parallel", "parallel"),
    )(z, z, z, z, z, bias)


def _window_kernel(sink_ref, q_ref, kc_ref, kp_ref, vc_ref, vp_ref, bias_ref, o_ref):
    penalty = _prev_key_penalty(pl.program_id(1) == 0)
    first_half = _lane_half((BLOCK, LANES)) == 0
    first_half2 = _lane_half((2 * BLOCK, LANES)) == 0

    def block(n, first):
        start = n * BLOCK if isinstance(n, int) else pl.multiple_of(n * BLOCK, BLOCK)
        cur = pl.ds(start, BLOCK)
        if first:
            src_k, src_v, prev = kp_ref, vp_ref, pl.ds(SPAN - BLOCK, BLOCK)
        else:
            src_k, src_v, prev = kc_ref, vc_ref, pl.ds(pl.multiple_of(start - BLOCK, BLOCK), BLOCK)
        k2 = jnp.concatenate([src_k[prev, :], kc_ref[cur, :]], axis=0).astype(BF16)
        vf = jnp.concatenate([src_v[prev, :], vc_ref[cur, :]], axis=0)
        v_swap = pltpu.roll(vf, HEAD_DIM, 1)
        for i in range(A_KV_HEADS):
            qf = q_ref[cur, i * LANES:(i + 1) * LANES] * BAND_Q_SCALE
            q_swap = pltpu.roll(qf, HEAD_DIM, 1)
            q2 = jnp.concatenate([jnp.where(first_half == (i == 0), qf if a == i else q_swap, 0.0)
                                  for a in range(2)], axis=0).astype(BF16)
            v2 = jnp.where(first_half2 == (i == 0), vf, v_swap).astype(BF16)
            o, lse = _band_pair(q2, k2, v2, bias_ref[i], penalty if first else None)
            outs = [o[a * BLOCK:(a + 1) * BLOCK] * jax.nn.sigmoid(lse[a * BLOCK:(a + 1) * BLOCK] - sink_ref[2 * i + a])
                    for a in range(2)]
            o_ref[cur, i * LANES:(i + 1) * LANES] = jnp.where(first_half, outs[0], outs[1]).astype(o_ref.dtype)

    block(0, True)
    lax.fori_loop(1, SPAN // BLOCK, lambda n, _: block(n, False), None, unroll=5)


def _window_attention(z, bias, sinks, n_seq, seq_len):
    spans = seq_len // SPAN
    ka, va = COL_KA // LANES, COL_VA // LANES
    blk = (SPAN, LANES)
    cur = lambda col: (lambda b, s: (b * spans + s, col))
    prev = lambda col: (lambda b, s: (b * spans + jnp.maximum(s - 1, 0), col))
    return pl.pallas_call(
        _window_kernel,
        out_shape=jax.ShapeDtypeStruct((n_seq * seq_len, A_Q_W), BF16),
        grid=(n_seq, spans),
        in_specs=[pl.BlockSpec(memory_space=pltpu.SMEM),
                  pl.BlockSpec((SPAN, A_Q_W), lambda b, s: (b * spans + s, COL_QA // A_Q_W)),
                  pl.BlockSpec(blk, cur(ka)), pl.BlockSpec(blk, prev(ka)),
                  pl.BlockSpec(blk, cur(va)), pl.BlockSpec(blk, prev(va)),
                  pl.BlockSpec((A_KV_HEADS, 2 * BLOCK, 2 * BLOCK), lambda b, s: (0, 0, 0))],
        out_specs=pl.BlockSpec((SPAN, A_Q_W), lambda b, s: (b * spans + s, 0)),
        compiler_params=_params("parallel", "parallel"),
    )(sinks, z, z, z, z, z, bias)


def _shift_rows(u, filler, k):
    rolled = pltpu.roll(u, k, 0)
    row = lax.broadcasted_iota(jnp.int32, u.shape, 0)
    n_fill = filler.shape[0]
    for j in range(k):
        rolled = jnp.where(row == j, filler[n_fill - k + j:n_fill - k + j + 1, :], rolled)
    return rolled


def _gated_conv(xc, bg, cg, filler, cw):
    u = cg * xc
    conv = cw[0:1, :] * _shift_rows(u, filler, 2) + cw[1:2, :] * _shift_rows(u, filler, 1) + cw[2:3, :] * u
    return bg * conv, u


def _pad_rows(x, rows):
    return jnp.concatenate([x, jnp.zeros((rows - x.shape[0], x.shape[1]), x.dtype)], axis=0)


def _heads_by_dim(ref, j):
    _, h, d, length = ref.shape
    return ref[j].reshape(h * d, length).astype(BF16)


def _sample_kernel(z_ref, cak_ref, cav_ref, cbk_ref, cbv_ref, st_ref, cw_ref, sink_ref, bias_a_ref,
                   bias_b_ref, mix_ref, conv_ref, *, n_new):
    for j in range(z_ref.shape[0] // n_new):
        _sample_sequence(j, z_ref, cak_ref, cav_ref, cbk_ref, cbv_ref, st_ref, cw_ref, sink_ref, bias_a_ref,
                         bias_b_ref, mix_ref, conv_ref, n_new)


def _sample_sequence(j, z_ref, cak_ref, cav_ref, cbk_ref, cbv_ref, st_ref, cw_ref, sink_ref, bias_a_ref,
                     bias_b_ref, mix_ref, conv_ref, n_new):
    tokens = slice(j * n_new, (j + 1) * n_new)
    z = z_ref[tokens, :]
    la = cak_ref.shape[3]
    lb = cbk_ref.shape[3]

    ka_new = _pad_rows(z[:, COL_KA:COL_KA + A_KV_W], LANES).astype(BF16)
    va_new = _pad_rows(z[:, COL_VA:COL_VA + A_KV_W], LANES).astype(BF16)
    half = _lane_half((n_new, LANES))
    pieces = []
    for i in range(A_KV_HEADS):
        qf = z[:, COL_QA + i * LANES:COL_QA + (i + 1) * LANES] * ATTN_SCALE
        for a in range(2):
            pieces.append(jnp.where(half == i, qf if a == i else pltpu.roll(qf, HEAD_DIM, 1), 0.0))
    qa = jnp.concatenate(pieces, axis=0).astype(BF16)
    lc = jnp.dot(qa, _heads_by_dim(cak_ref, j), preferred_element_type=F32) + bias_a_ref[:, :la]
    ln = _nt_dot(qa, ka_new) + bias_a_ref[:, la:]
    m = jnp.maximum(jnp.max(lc, axis=-1, keepdims=True), jnp.max(ln, axis=-1, keepdims=True))
    pc = jnp.exp(lc - m)
    pn = jnp.exp(ln - m)
    s = jnp.sum(pc, axis=-1, keepdims=True) + jnp.sum(pn, axis=-1, keepdims=True)
    oa = (_nt_dot(pc.astype(BF16), _heads_by_dim(cav_ref, j))
          + jnp.dot(pn.astype(BF16), va_new, preferred_element_type=F32))
    oa = oa / s * jax.nn.sigmoid(m + jnp.log(s) - sink_ref[...])
    oa_blocks = []
    for i in range(A_KV_HEADS):
        per_half = []
        for a in range(2):
            rows = oa[(2 * i + a) * n_new:(2 * i + a + 1) * n_new, :]
            per_half.append(rows if a == i else pltpu.roll(rows, HEAD_DIM, 1))
        oa_blocks.append(jnp.where(half == 0, per_half[0], per_half[1]))

    qf = z[:, COL_QB:COL_QB + B_W] * ATTN_SCALE
    head_of_lane = _lane_half((n_new, B_W))
    qb = jnp.concatenate([jnp.where(head_of_lane == h, qf, 0.0) for h in range(B_HEADS)], axis=0).astype(BF16)
    kb_new = _pad_rows(z[:, COL_KB:COL_KB + B_W], LANES).astype(BF16)
    vb_new = _pad_rows(z[:, COL_VB:COL_VB + B_W], LANES).astype(BF16)
    lg_c = jnp.dot(qb, _heads_by_dim(cbk_ref, j), preferred_element_type=F32)
    lg_n = _nt_dot(qb, kb_new)
    parts = []
    for bi, (w, r) in enumerate(B_BRANCHES):
        lo = lb - min(lb, -(-w // LANES) * LANES)
        lc = lg_c[:, lo:] + bias_b_ref[bi, :, lo:lb]
        ln = lg_n + bias_b_ref[bi, :, lb:]
        m = jnp.maximum(jnp.max(lc, axis=-1, keepdims=True), jnp.max(ln, axis=-1, keepdims=True))
        pc = jnp.exp(lc - m)
        pn = jnp.exp(ln - m)
        s = jnp.sum(pc, axis=-1, keepdims=True) + jnp.sum(pn, axis=-1, keepdims=True)
        parts.append((lo, pc, pn, s, m + jnp.log(s)))
    m_all = functools.reduce(jnp.maximum, [p[4] for p in parts])
    ws = [jnp.exp(p[4] - m_all) for p in parts]
    den = sum(ws)
    p_new = None
    los = sorted({p[0] for p in parts} | {lb})
    segs = [None] * (len(los) - 1)
    for (lo, pc, pn, s, _), w in zip(parts, ws):
        coef = w / (den * s)
        p_new = coef * pn if p_new is None else p_new + coef * pn
        for si in range(len(segs)):
            a0, a1 = los[si], los[si + 1]
            if a0 >= lo:
                piece = coef * pc[:, a0 - lo:a1 - lo]
                segs[si] = piece if segs[si] is None else segs[si] + piece
    p_cache = jnp.concatenate(segs, axis=1) if len(segs) > 1 else segs[0]
    ob = (_nt_dot(p_cache.astype(BF16), _heads_by_dim(cbv_ref, j))
          + jnp.dot(p_new.astype(BF16), vb_new, preferred_element_type=F32))
    ob_rows = sum(jnp.where(head_of_lane == h, ob[h * n_new:(h + 1) * n_new, :], 0.0) for h in range(B_HEADS))

    cz = z[:, COL_CONV:COL_CONV + 3 * C_WIDTH]
    oc, u = _gated_conv(cz[:, :C_WIDTH], cz[:, C_WIDTH:2 * C_WIDTH], cz[:, 2 * C_WIDTH:], st_ref[j], cw_ref[...])
    conv_ref[j] = u[n_new - (CONV_WIDTH - 1):, :]
    mix_ref[tokens, :] = jnp.concatenate(oa_blocks + [ob_rows, oc], axis=1)


def _sample_mixer(z, row0, cak, cav, cbk, cbv, layer, state, cw, sink_rows, bias_a, bias_b, n_new):
    n_seq = cak.shape[1]
    g = SEQS_PER_STEP
    rows = g * n_new
    blk0 = row0 // rows
    cache = lambda a: pl.BlockSpec((None, g) + a.shape[2:], lambda b: (layer, b, 0, 0, 0))
    per_seq = lambda shape: pl.BlockSpec((g,) + shape, lambda b: (b, 0, 0))
    const = lambda a: pl.BlockSpec(a.shape, lambda b: (0,) * a.ndim)
    return pl.pallas_call(
        functools.partial(_sample_kernel, n_new=n_new),
        out_shape=(jax.ShapeDtypeStruct((n_seq * n_new, D_MODEL), F32),
                   jax.ShapeDtypeStruct((n_seq, CONV_WIDTH - 1, C_WIDTH), F32)),
        grid=(n_seq // g,),
        in_specs=[pl.BlockSpec((rows, IN_WIDTH), lambda b: (blk0 + b, 0)),
                  cache(cak), cache(cav), cache(cbk), cache(cbv),
                  per_seq((CONV_WIDTH - 1, C_WIDTH)), const(cw), const(sink_rows), const(bias_a), const(bias_b)],
        out_specs=(pl.BlockSpec((rows, D_MODEL), lambda b: (b, 0)), per_seq((CONV_WIDTH - 1, C_WIDTH))),
        compiler_params=_params("parallel"),
    )(z, cak, cav, cbk, cbv, state, cw, sink_rows, bias_a, bias_b)


def _first_index(vals, best):
    idx = jnp.full(best.shape, len(vals) - 1, jnp.int32)
    for j in range(len(vals) - 2, -1, -1):
        idx = jnp.where(vals[j] == best, j, idx)
    return idx


def _route(lt):
    g = [lt[k:k + 1, :] for k in range(N_GROUPS)]
    g_max = functools.reduce(jnp.maximum, g)
    g_idx = _first_index(g, g_max)
    g_w = 1.0 / sum(jnp.exp(v - g_max) for v in g)
    e = []
    for j in range(EXPERTS_PER_GROUP):
        v = lt[N_GROUPS + j:N_GROUPS + j + 1, :]
        for gi in range(1, N_GROUPS):
            row = N_GROUPS + gi * EXPERTS_PER_GROUP + j
            v = jnp.where(g_idx == gi, lt[row:row + 1, :], v)
        e.append(v)
    e1 = functools.reduce(jnp.maximum, e)
    i1 = _first_index(e, e1)
    rest = [jnp.where(i1 == j, -jnp.inf, e[j]) for j in range(EXPERTS_PER_GROUP)]
    e2 = functools.reduce(jnp.maximum, rest)
    i2 = _first_index(rest, e2)
    t = jnp.exp(e2 - e1)
    w1 = g_w / (1.0 + t)
    w2 = g_w * t / (1.0 + t)
    swap = i2 < i1
    lo = jnp.where(swap, i2, i1)
    hi = jnp.where(swap, i1, i2)
    pair = jnp.where(lo == 0, hi - 1, jnp.where(lo == 1, hi + 1, len(PAIRS) - 1))
    bucket = g_idx * len(PAIRS) + pair
    return bucket, jnp.where(swap, w2, w1), jnp.where(swap, w1, w2)


def _outproj_kernel(y_ref, oa_ref, ob_ref, zc_ref, zh_ref, ms_ref, wout_ref, cw_ref, gn_ref, wr_ref, br_ref,
                    y1_ref, info_ref, cnt_ref, ut_ref, acc_scr, carry_scr, *, prompt_tiles, tiles_per_seq):
    i = pl.program_id(0)
    tile = y_ref.shape[0]

    @pl.when(i == 0)
    def _():
        carry_scr[...] = jnp.zeros_like(carry_scr)

    @pl.when(i < prompt_tiles)
    def _():
        zc = zc_ref[...]
        zh = zh_ref[...]
        halo = zh[:, 2 * C_WIDTH:] * zh[:, :C_WIDTH]
        halo = jnp.where(i % tiles_per_seq == 0, 0.0, halo)
        oc, u = _gated_conv(zc[:, :C_WIDTH], zc[:, C_WIDTH:2 * C_WIDTH], zc[:, 2 * C_WIDTH:], halo, cw_ref[...])
        ut_ref[...] = u[tile - 8:, :]
        acc_scr[...] = (
            jnp.dot(oa_ref[...], wout_ref[0:A_Q_W, :], preferred_element_type=F32)
            + jnp.dot(ob_ref[...], wout_ref[A_Q_W:A_Q_W + B_W, :], preferred_element_type=F32)
            + jnp.dot(oc.astype(BF16), wout_ref[A_Q_W + B_W:, :], preferred_element_type=F32))

    @pl.when(i >= prompt_tiles)
    def _():
        ut_ref[...] = jnp.zeros_like(ut_ref)
        acc_scr[...] = jnp.dot(ms_ref[...].astype(BF16), wout_ref[...], preferred_element_type=F32)

    y1 = y_ref[...] + acc_scr[...]
    y1_ref[:, :D_MODEL] = y1
    xn = _rms(y1, gn_ref[...])

    lt = lax.dot_general(wr_ref[...], xn, (((1,), (1,)), ((), ())), precision=lax.Precision.HIGHEST,
                         preferred_element_type=F32) + br_ref[...]
    bucket, w_lo, w_hi = _route(lt)
    onehot = (lax.broadcasted_iota(jnp.int32, (ROUTE_ROWS, tile), 0) == bucket).astype(F32)
    upper = (lax.broadcasted_iota(jnp.int32, (tile, tile), 0)
             <= lax.broadcasted_iota(jnp.int32, (tile, tile), 1)).astype(BF16)
    running = jnp.dot(onehot.astype(BF16), upper, preferred_element_type=F32)
    carry = carry_scr[...]
    rank = jnp.sum(onehot * (running - 1.0 + carry), axis=0, keepdims=True)
    carry = carry + jnp.sum(onehot, axis=1, keepdims=True)
    carry_scr[...] = carry
    cnt_ref[...] = jnp.broadcast_to(carry, cnt_ref.shape)
    info_ref[...] = jnp.concatenate([bucket.astype(F32), rank, jnp.zeros((SUBLANES - 2, tile), F32)], axis=0)
    y1_ref[:, D_MODEL:] = jnp.concatenate([w_lo, w_hi, jnp.zeros((LANES - 2, tile), F32)], axis=0).T


def _outproj_route(y, oa, ob, z, mix_s, w_out, cw, gain, w_route, b_route, n_prompt, seq_len):
    n = y.shape[0]
    tiles = n // ROW_TILE
    p_tiles = n_prompt // ROW_TILE
    halo_blocks = ROW_TILE // 8
    conv_w = 3 * C_WIDTH
    pidx = lambda i: jnp.minimum(i, p_tiles - 1)
    const = lambda a: pl.BlockSpec(a.shape, lambda i: (0,) * a.ndim)
    gain = gain.reshape(1, D_MODEL)
    return pl.pallas_call(
        functools.partial(_outproj_kernel, prompt_tiles=p_tiles, tiles_per_seq=seq_len // ROW_TILE),
        out_shape=(jax.ShapeDtypeStruct((n, ROW_EXT), F32),
                   jax.ShapeDtypeStruct((8, n), F32),
                   jax.ShapeDtypeStruct((ROUTE_ROWS, LANES), F32),
                   jax.ShapeDtypeStruct((tiles * 8, C_WIDTH), F32)),
        grid=(tiles,),
        in_specs=[pl.BlockSpec((ROW_TILE, D_MODEL), lambda i: (i, 0)),
                  pl.BlockSpec((ROW_TILE, A_Q_W), lambda i: (pidx(i), 0)),
                  pl.BlockSpec((ROW_TILE, B_W), lambda i: (pidx(i), 0)),
                  pl.BlockSpec((ROW_TILE, conv_w), lambda i: (pidx(i), 0)),
                  pl.BlockSpec((8, conv_w), lambda i: (jnp.maximum(pidx(i) * halo_blocks - 1, 0), 0)),
                  pl.BlockSpec((ROW_TILE, D_MODEL), lambda i: (jnp.maximum(i - p_tiles, 0), 0)),
                  const(w_out), const(cw), const(gain), const(w_route), const(b_route)],
        out_specs=(pl.BlockSpec((ROW_TILE, ROW_EXT), lambda i: (i, 0)),
                   pl.BlockSpec((8, ROW_TILE), lambda i: (0, i)),
                   pl.BlockSpec((ROUTE_ROWS, LANES), lambda i: (0, 0)),
                   pl.BlockSpec((8, C_WIDTH), lambda i: (i, 0))),
        scratch_shapes=[pltpu.VMEM((ROW_TILE, D_MODEL), F32), pltpu.VMEM((ROUTE_ROWS, 1), F32)],
        compiler_params=_params("arbitrary"),
    )(y, oa, ob, z, z, mix_s, w_out, cw, gain, w_route, b_route)


def _moe_kernel(e_lo_ref, e_hi_ref, used_ref, x_ref, g_ref, wg1, wu1, wd1, wg2, wu2, wd2, *rest):
    o_ref = rest[-1]
    t = pl.program_id(0)

    @pl.when(used_ref[t] > 0)
    def _():
        y1 = x_ref[:, :D_MODEL]
        w = x_ref[:, D_MODEL:]
        x = _rms(y1, g_ref[...]).astype(BF16)

        def expert(wg, wu, wd, scale):
            g = jnp.dot(x, wg[...], preferred_element_type=F32)
            u = jnp.dot(x, wu[...], preferred_element_type=F32)
            h = g * jax.nn.sigmoid(g) * u * scale
            return jnp.dot(h.astype(BF16), wd[...], preferred_element_type=F32)

        y2 = y1 + expert(wg1, wu1, wd1, w[:, 0:1]) + expert(wg2, wu2, wd2, w[:, 1:2])
        o_ref[...] = _rms(y2, rest[0][...]) if len(rest) > 1 else y2

    @pl.when(used_ref[t] == 0)
    def _():
        o_ref[...] = jnp.zeros_like(o_ref)


def _experts(xs, gain, e_lo, e_hi, used, w_gate, w_up, w_down, final_gain=None):
    n_tiles = xs.shape[0] // MOE_TILE
    up_spec = lambda sel: pl.BlockSpec((None, D_MODEL, EXPERT_FF), lambda t, lo, hi, u: ((lo, hi)[sel][t], 0, 0))
    down_spec = lambda sel: pl.BlockSpec((None, EXPERT_FF, D_MODEL), lambda t, lo, hi, u: ((lo, hi)[sel][t], 0, 0))
    gain_spec = pl.BlockSpec((1, D_MODEL), lambda t, lo, hi, u: (0, 0))
    extra = [] if final_gain is None else [final_gain.reshape(1, D_MODEL)]
    grid_spec = pltpu.PrefetchScalarGridSpec(
        num_scalar_prefetch=3,
        grid=(n_tiles,),
        in_specs=[pl.BlockSpec((MOE_TILE, ROW_EXT), lambda t, lo, hi, u: (t, 0)), gain_spec,
                  up_spec(0), up_spec(0), down_spec(0), up_spec(1), up_spec(1), down_spec(1)]
                 + [gain_spec] * len(extra),
        out_specs=pl.BlockSpec((MOE_TILE, D_MODEL), lambda t, lo, hi, u: (t, 0)))
    return pl.pallas_call(
        _moe_kernel,
        out_shape=jax.ShapeDtypeStruct((xs.shape[0], D_MODEL), F32),
        grid_spec=grid_spec,
        compiler_params=_params("arbitrary"),
    )(e_lo, e_hi, used, xs, gain.reshape(1, D_MODEL), w_gate, w_up, w_down, w_gate, w_up, w_down, *extra)


def _dispatch_plan(info, counts, n):
    n_tiles = -(-n // MOE_TILE) + N_ROUTE_BUCKETS
    bucket = info[0].astype(jnp.int32)
    rank = info[1].astype(jnp.int32)
    counts = counts[:N_ROUTE_BUCKETS, 0].astype(jnp.int32)
    tiles_per_bucket = (counts + MOE_TILE - 1) // MOE_TILE
    tile_end = jnp.cumsum(tiles_per_bucket)
    row_start = (tile_end - tiles_per_bucket) * MOE_TILE
    dest = row_start[bucket] + rank
    src = (jnp.arange(n_tiles * MOE_TILE, dtype=jnp.int32) % n).at[dest].set(jnp.arange(n, dtype=jnp.int32))
    tile_ids = jnp.arange(n_tiles, dtype=jnp.int32)
    tile_bucket = jnp.minimum(jnp.searchsorted(tile_end, tile_ids, side="right").astype(jnp.int32),
                              N_ROUTE_BUCKETS - 1)
    used = (tile_ids < tile_end[-1]).astype(jnp.int32)
    pair = tile_bucket % len(PAIRS)
    base = (tile_bucket // len(PAIRS)) * EXPERTS_PER_GROUP
    pairs = jnp.asarray(PAIRS, jnp.int32)
    return dest, src, base + pairs[pair, 0], base + pairs[pair, 1], used


def _permute_in_columns(w):
    attn = A_Q_W + 2 * A_KV_W + 3 * B_W
    return jnp.concatenate([w[:, attn:], w[:, A_Q_W:A_Q_W + A_KV_W], w[:, :A_Q_W], w[:, A_Q_W + A_KV_W:attn]],
                           axis=1)


def kernel(x_prompt, x_sample, cache_a_k, cache_a_v, cache_b_k, cache_b_v, state_conv, rel_bias_table,
           w_in, w_out, conv_w, attn_sinks, norm_mix, norm_ffn, w_group, b_group, w_router, b_router,
           w_gate, w_up, w_down, norm_final):
    n_seq, seq_len, _ = x_prompt.shape
    dec_seq, n_new, _ = x_sample.shape
    depth = w_in.shape[0]
    n_prompt = n_seq * seq_len
    n_sample = dec_seq * n_new
    n = n_prompt + n_sample
    la, lb = cache_a_k.shape[2], cache_b_k.shape[2]
    assert seq_len % SPAN == 0 and n_prompt % ROW_TILE == 0 and n_sample % ROW_TILE == 0 and n % IN_TILE == 0
    assert seq_len >= SPAN and la % LANES == 0 and lb % LANES == 0 and n_new == 8

    table_a, table_b = rel_bias_table[:, :A_Q_HEADS], rel_bias_table[:, A_Q_HEADS:]
    bias_a = _band_bias(table_a, 1)
    bias_b = jnp.stack([_band_bias(table_b, r) for r in DILATIONS], axis=1)
    sbias_a = _sample_bias(table_a, n_new, la, np.arange(la + LANES), A_WINDOW, 1)
    sbias_a = sbias_a.reshape(A_Q_HEADS * n_new, la + LANES)
    sbias_b = jnp.stack([_sample_bias(table_b, n_new, lb, np.arange(lb + LANES), w, r)
                         .reshape(B_HEADS * n_new, lb + LANES) for w, r in B_BRANCHES])
    cak, cav, cbk, cbv = (c.transpose(0, 1, 3, 4, 2) for c in (cache_a_k, cache_a_v, cache_b_k, cache_b_v))

    w_in_b = jnp.stack([_permute_in_columns(w_in[l]) for l in range(depth)]).astype(BF16)
    w_out_b = w_out.astype(BF16)
    expert_w = [tuple(w[l].astype(BF16) for w in (w_gate, w_up, w_down)) for l in range(depth)]
    pad = ROUTE_ROWS - N_GROUPS - N_EXPERTS
    w_route = jnp.pad(jnp.concatenate([w_group, w_router], axis=2).transpose(0, 2, 1), ((0, 0), (0, pad), (0, 0)))
    b_route = jnp.pad(jnp.concatenate([b_group, b_router], axis=1), ((0, 0), (0, pad)))[..., None]
    sink_rows = jnp.repeat(attn_sinks, n_new, axis=1)[..., None]

    y = jnp.concatenate([x_prompt.reshape(n_prompt, D_MODEL), x_sample.reshape(n_sample, D_MODEL)], axis=0)
    states = []
    for l in range(depth):
        z = _inproj(y, norm_mix[l], w_in_b[l])
        oa = _window_attention(z, bias_a, attn_sinks[l], n_seq, seq_len)
        ob = _dilated_attention(z, bias_b, n_seq, seq_len)
        mix_s, conv_s = _sample_mixer(z, n_prompt, cak, cav, cbk, cbv, l, state_conv[l], conv_w[l], sink_rows[l],
                                      sbias_a, sbias_b, n_new)
        y1, info, counts, u_tail = _outproj_route(
            y, oa, ob, z, mix_s, w_out_b[l], conv_w[l], norm_ffn[l], w_route[l], b_route[l], n_prompt, seq_len)
        dest, src, e_lo, e_hi, used = _dispatch_plan(info, counts, n)
        y_sorted = _experts(y1[src], norm_ffn[l], e_lo, e_hi, used, *expert_w[l],
                            final_gain=norm_final if l == depth - 1 else None)
        if l < depth - 1:
            y = y_sorted[dest]

        lap, lbp = min(A_WINDOW, seq_len), min(SPAN, seq_len)

        def prompt_tail(length, col, width, heads):
            rows = [z[(b + 1) * seq_len - length:(b + 1) * seq_len, col:col + width] for b in range(n_seq)]
            return jnp.stack(rows).reshape(n_seq, length, heads, HEAD_DIM)

        def sample_rows(col, width, heads):
            return z[n_prompt:, col:col + width].reshape(dec_seq, n_new, heads, HEAD_DIM)

        last_tile = [((b + 1) * seq_len // ROW_TILE - 1) * SUBLANES for b in range(n_seq)]
        conv_p = jnp.stack([u_tail[t + SUBLANES - (CONV_WIDTH - 1):t + SUBLANES, :] for t in last_tile])
        states.append((
            prompt_tail(lap, COL_KA, A_KV_W, A_KV_HEADS), prompt_tail(lap, COL_VA, A_KV_W, A_KV_HEADS),
            prompt_tail(lbp, COL_KB, B_W, B_HEADS), prompt_tail(lbp, COL_VB, B_W, B_HEADS), conv_p,
            sample_rows(COL_KA, A_KV_W, A_KV_HEADS), sample_rows(COL_VA, A_KV_W, A_KV_HEADS),
            sample_rows(COL_KB, B_W, B_HEADS), sample_rows(COL_VB, B_W, B_HEADS), conv_s))

    y_prompt = y_sorted[dest[:n_prompt]].reshape(n_seq, seq_len, D_MODEL)
    y_sample = y_sorted[dest[n_prompt:]].reshape(dec_seq, n_new, D_MODEL)
    st = [jnp.stack([s[k] for s in states]) for k in range(10)]
    return (y_prompt, y_sample, st[0], st[1], st[2], st[3], st[4], st[5], st[6], st[7], st[8], st[9])
```

```python
import functools
import math

import numpy as np
import jax
import jax.numpy as jnp
from jax import lax
from jax.experimental import pallas as pl
from jax.experimental.pallas import tpu as pltpu

F32 = jnp.float32
BF16 = jnp.bfloat16

D_MODEL = 1024
HEAD_DIM = 64
ATTN_SCALE = HEAD_DIM ** -0.5
LOG2E = math.log2(math.e)
LN2 = math.log(2.0)
BAND_Q_SCALE = ATTN_SCALE * LOG2E
BLOCK = 128
LANES = 128
SUBLANES = 8
A_Q_HEADS = 4
A_KV_HEADS = 2
A_WINDOW = 128
B_HEADS = 6
B_BRANCHES = ((128, 1), (512, 4), (2048, 16))
DILATIONS = tuple(r for _, r in B_BRANCHES)
SPAN = BLOCK * max(DILATIONS)
C_WIDTH = 6 * HEAD_DIM
CONV_WIDTH = 3
A_Q_W = A_Q_HEADS * HEAD_DIM
A_KV_W = A_KV_HEADS * HEAD_DIM
B_W = B_HEADS * HEAD_DIM
IN_WIDTH = A_Q_W + 2 * A_KV_W + 3 * B_W + 3 * C_WIDTH
N_BUCKETS = 32
MAX_DISTANCE = 2048
N_GROUPS = 4
EXPERTS_PER_GROUP = 4
N_EXPERTS = N_GROUPS * EXPERTS_PER_GROUP
EXPERT_FF = 512
RMS_EPS = 1e-6
NEG_INF = -1e30

COL_CONV = 0
COL_KA = 3 * C_WIDTH
COL_QA = COL_KA + A_KV_W
COL_VA = COL_QA + A_Q_W
COL_QB = COL_VA + A_KV_W
COL_KB = COL_QB + B_W
COL_VB = COL_KB + B_W

ROW_TILE = 512
IN_TILE = 1024
MOE_TILE = 256
SEQS_PER_STEP = 2
PAIRS = ((0, 1), (0, 2), (0, 3), (1, 2), (1, 3), (2, 3))
N_ROUTE_BUCKETS = N_GROUPS * len(PAIRS)
ROUTE_ROWS = 32
ROW_EXT = D_MODEL + LANES
VMEM_LIMIT = 56 * 1024 * 1024


def _params(*sem):
    return pltpu.CompilerParams(dimension_semantics=sem, vmem_limit_bytes=VMEM_LIMIT)


def _rel_bucket(dist):
    d = np.maximum(dist, 0)
    max_exact = N_BUCKETS // 2
    df = np.maximum(d, max_exact).astype(np.float32)
    large = max_exact + (np.log(df / max_exact) / math.log(MAX_DISTANCE / max_exact)
                         * (N_BUCKETS - max_exact)).astype(np.int32)
    large = np.minimum(large, N_BUCKETS - 1)
    return np.where(d < max_exact, d, large)


def _masked_bias(table, dist, valid):
    bucket = _rel_bucket(dist).reshape(-1, 1)
    onehot = (jnp.asarray(bucket, jnp.int32) == jnp.arange(N_BUCKETS, dtype=jnp.int32)[None, :]).astype(F32)
    b = jnp.dot(onehot, table.astype(F32), precision=lax.Precision.HIGHEST)
    b = jnp.where(jnp.asarray(valid.reshape(-1, 1)), b, NEG_INF)
    return jnp.moveaxis(b.reshape(dist.shape + (table.shape[1],)), -1, 0)


def _band_bias(table, scale):
    i = np.arange(BLOCK)[:, None]
    j = np.arange(2 * BLOCK)[None, :]
    dist = i + BLOCK - j
    b = _masked_bias(table, dist * scale, (dist >= 0) & (dist <= BLOCK)) * LOG2E
    return b.reshape(table.shape[1] // 2, 2 * BLOCK, 2 * BLOCK)


def _sample_bias(table, n_new, first_new, positions, window, dilation):
    dist = first_new + np.arange(n_new)[:, None] - np.asarray(positions)[None, :]
    valid = (dist >= 0) & (dist <= window) & (dist % dilation == 0)
    return _masked_bias(table, dist, valid)


def _rms(x, gain):
    return x * lax.rsqrt(jnp.mean(x * x, axis=-1, keepdims=True) + RMS_EPS) * gain


def _inproj_kernel(xp_ref, xs_ref, g_ref, w_ref, z_ref, *, prompt_tiles):
    def project(x_ref):
        h = _rms(x_ref[...], g_ref[...])
        z_ref[...] = jnp.dot(h.astype(BF16), w_ref[...], preferred_element_type=F32)

    pl.when(pl.program_id(0) < prompt_tiles)(lambda: project(xp_ref))
    pl.when(pl.program_id(0) >= prompt_tiles)(lambda: project(xs_ref))


def _inproj(xp, xs, sample_row0, n_prompt, n_sample, gain, w):
    p_tiles, s_tile0 = n_prompt // IN_TILE, sample_row0 // IN_TILE
    return pl.pallas_call(
        functools.partial(_inproj_kernel, prompt_tiles=p_tiles),
        out_shape=jax.ShapeDtypeStruct((n_prompt + n_sample, IN_WIDTH), F32),
        grid=((n_prompt + n_sample) // IN_TILE,),
        in_specs=[pl.BlockSpec((IN_TILE, D_MODEL), lambda i: (jnp.minimum(i, p_tiles - 1), 0)),
                  pl.BlockSpec((IN_TILE, D_MODEL), lambda i: (s_tile0 + jnp.maximum(i - p_tiles, 0), 0)),
                  pl.BlockSpec((1, D_MODEL), lambda i: (0, 0)),
                  pl.BlockSpec((D_MODEL, IN_WIDTH), lambda i: (0, 0))],
        out_specs=pl.BlockSpec((IN_TILE, IN_WIDTH), lambda i: (i, 0)),
        compiler_params=_params("parallel"),
    )(xp, xs, gain.reshape(1, D_MODEL), w)


def _lane_half(shape):
    return lax.broadcasted_iota(jnp.int32, shape, len(shape) - 1) // HEAD_DIM


def _nt_dot(a, b):
    return lax.dot_general(a, b, (((1,), (1,)), ((), ())), preferred_element_type=F32)


def _band_pair(q2, k2, v2, bias2, penalty):
    lg = _nt_dot(q2, k2) + bias2
    if penalty is not None:
        lg = lg + penalty
    m = jnp.max(lg, axis=-1, keepdims=True)
    p = jnp.exp2(lg - m)
    s = jnp.sum(p, axis=-1, keepdims=True)
    acc = jnp.dot(p.astype(BF16), v2, preferred_element_type=F32)
    return acc / s, (m + jnp.log2(s)) * LN2


def _prev_key_penalty(first_span):
    col = lax.broadcasted_iota(jnp.int32, (1, 2 * BLOCK), 1)
    return jnp.where((col < BLOCK) & first_span, NEG_INF, 0.0).astype(F32)


def _rows(start, dilation):
    if dilation == 1:
        return pl.ds(start, BLOCK)
    return pl.ds(start, BLOCK, stride=dilation)


def _dilated_kernel(q_ref, kc_ref, kp_ref, vc_ref, vp_ref, bias_ref, o_ref, o_scr, l_scr, first_bias_scr):
    first_half = _lane_half((BLOCK, LANES)) == 0
    penalty = _prev_key_penalty(pl.program_id(1) == 0)
    for bi in range(len(DILATIONS)):
        first_bias_scr[bi] = bias_ref[bi] + penalty

    for bi, r in enumerate(DILATIONS):
        step = r * BLOCK

        def block(c, n, first, bi=bi, r=r, step=step):
            q_start = c + n * step
            if not isinstance(q_start, int):
                q_start = pl.multiple_of(q_start, BLOCK) if r == 1 else q_start
            cur = _rows(q_start, r)
            qt = (q_ref[cur, :] * BAND_Q_SCALE).astype(BF16)
            if first:
                src_k, src_v, prev = kp_ref, vp_ref, _rows(SPAN - step + c, r)
            else:
                p_start = q_start - step
                if r == 1:
                    p_start = pl.multiple_of(p_start, BLOCK)
                src_k, src_v, prev = kc_ref, vc_ref, _rows(p_start, r)
            k2 = jnp.concatenate([src_k[prev, :], kc_ref[cur, :]], axis=0).astype(BF16)
            v2 = jnp.concatenate([src_v[prev, :], vc_ref[cur, :]], axis=0).astype(BF16)
            zero = jnp.zeros_like(qt)
            q2 = jnp.concatenate([jnp.where(first_half, qt, zero), jnp.where(first_half, zero, qt)], axis=0)
            o, lse = _band_pair(q2, k2, v2, first_bias_scr[bi] if first else bias_ref[bi], None)
            o_scr[bi, cur, :] = jnp.where(first_half, o[:BLOCK], o[BLOCK:])
            l_scr[bi, cur, :] = jnp.where(first_half, lse[:BLOCK], lse[BLOCK:])

        n_blocks = SPAN // step
        if r == 1:
            block(0, 0, True)
            lax.fori_loop(1, n_blocks, lambda n, _: block(0, n, False), None, unroll=True)
        else:
            def per_class(c, _, block=block, n_blocks=n_blocks):
                block(c, 0, True)
                for n in range(1, n_blocks):
                    block(c, n, False)
            lax.fori_loop(0, r, per_class, None, unroll=True)

    chunk = 2 * BLOCK

    def combine(j, _):
        rows = pl.ds(pl.multiple_of(j * chunk, chunk), chunk)
        ls = [l_scr[bi, rows, :] for bi in range(len(DILATIONS))]
        m = functools.reduce(jnp.maximum, ls)
        ws = [jnp.exp(l - m) for l in ls]
        num = sum(w * o_scr[bi, rows, :] for bi, w in enumerate(ws))
        o_ref[rows, :] = (num / sum(ws)).astype(o_ref.dtype)

    lax.fori_loop(0, SPAN // chunk, combine, None)


def _dilated_attention(z, bias, n_seq, seq_len):
    spans = seq_len // SPAN
    qb, kb, vb = COL_QB // LANES, COL_KB // LANES, COL_VB // LANES
    blk = (SPAN, LANES)
    cur = lambda col: (lambda b, s, i: (b * spans + s, col + i))
    prev = lambda col: (lambda b, s, i: (b * spans + jnp.maximum(s - 1, 0), col + i))
    return pl.pallas_call(
        _dilated_kernel,
        out_shape=jax.ShapeDtypeStruct((n_seq * seq_len, B_W), BF16),
        grid=(n_seq, spans, B_W // LANES),
        in_specs=[pl.BlockSpec(blk, cur(qb)),
                  pl.BlockSpec(blk, cur(kb)), pl.BlockSpec(blk, prev(kb)),
                  pl.BlockSpec(blk, cur(vb)), pl.BlockSpec(blk, prev(vb)),
                  pl.BlockSpec((None, len(DILATIONS), 2 * BLOCK, 2 * BLOCK), lambda b, s, i: (i, 0, 0, 0))],
        out_specs=pl.BlockSpec(blk, lambda b, s, i: (b * spans + s, i)),
        scratch_shapes=[pltpu.VMEM((len(DILATIONS), SPAN, LANES), F32),
                        pltpu.VMEM((len(DILATIONS), SPAN, LANES), F32),
                        pltpu.VMEM((len(DILATIONS), 2 * BLOCK, 2 * BLOCK), F32)],
        compiler_params=_params("parallel", "parallel", "parallel"),
    )(z, z, z, z, z, bias)


def _window_kernel(sink_ref, q_ref, kc_ref, kp_ref, vc_ref, vp_ref, bias_ref, o_ref):
    penalty = _prev_key_penalty(pl.program_id(1) == 0)
    first_half = _lane_half((BLOCK, LANES)) == 0
    first_half2 = _lane_half((2 * BLOCK, LANES)) == 0

    def block(n, first):
        start = n * BLOCK if isinstance(n, int) else pl.multiple_of(n * BLOCK, BLOCK)
        cur = pl.ds(start, BLOCK)
        if first:
            src_k, src_v, prev = kp_ref, vp_ref, pl.ds(SPAN - BLOCK, BLOCK)
        else:
            src_k, src_v, prev = kc_ref, vc_ref, pl.ds(pl.multiple_of(start - BLOCK, BLOCK), BLOCK)
        k2 = jnp.concatenate([src_k[prev, :], kc_ref[cur, :]], axis=0).astype(BF16)
        vf = jnp.concatenate([src_v[prev, :], vc_ref[cur, :]], axis=0)
        v_swap = pltpu.roll(vf, HEAD_DIM, 1)
        for i in range(A_KV_HEADS):
            qf = q_ref[cur, i * LANES:(i + 1) * LANES] * BAND_Q_SCALE
            q_swap = pltpu.roll(qf, HEAD_DIM, 1)
            q2 = jnp.concatenate([jnp.where(first_half == (i == 0), qf if a == i else q_swap, 0.0)
                                  for a in range(2)], axis=0).astype(BF16)
            v2 = jnp.where(first_half2 == (i == 0), vf, v_swap).astype(BF16)
            o, lse = _band_pair(q2, k2, v2, bias_ref[i], penalty if first else None)
            outs = [o[a * BLOCK:(a + 1) * BLOCK] * jax.nn.sigmoid(lse[a * BLOCK:(a + 1) * BLOCK] - sink_ref[2 * i + a])
                    for a in range(2)]
            o_ref[cur, i * LANES:(i + 1) * LANES] = jnp.where(first_half, outs[0], outs[1]).astype(o_ref.dtype)

    block(0, True)
    lax.fori_loop(1, SPAN // BLOCK, lambda n, _: block(n, False), None, unroll=5)


def _window_attention(z, bias, sinks, n_seq, seq_len):
    spans = seq_len // SPAN
    ka, va = COL_KA // LANES, COL_VA // LANES
    blk = (SPAN, LANES)
    cur = lambda col: (lambda b, s: (b * spans + s, col))
    prev = lambda col: (lambda b, s: (b * spans + jnp.maximum(s - 1, 0), col))
    return pl.pallas_call(
        _window_kernel,
        out_shape=jax.ShapeDtypeStruct((n_seq * seq_len, A_Q_W), BF16),
        grid=(n_seq, spans),
        in_specs=[pl.BlockSpec(memory_space=pltpu.SMEM),
                  pl.BlockSpec((SPAN, A_Q_W), lambda b, s: (b * spans + s, COL_QA // A_Q_W)),
                  pl.BlockSpec(blk, cur(ka)), pl.BlockSpec(blk, prev(ka)),
                  pl.BlockSpec(blk, cur(va)), pl.BlockSpec(blk, prev(va)),
                  pl.BlockSpec((A_KV_HEADS, 2 * BLOCK, 2 * BLOCK), lambda b, s: (0, 0, 0))],
        out_specs=pl.BlockSpec((SPAN, A_Q_W), lambda b, s: (b * spans + s, 0)),
        compiler_params=_params("parallel", "parallel"),
    )(sinks, z, z, z, z, z, bias)


def _shift_rows(u, filler, k):
    rolled = pltpu.roll(u, k, 0)
    row = lax.broadcasted_iota(jnp.int32, u.shape, 0)
    n_fill = filler.shape[0]
    for j in range(k):
        rolled = jnp.where(row == j, filler[n_fill - k + j:n_fill - k + j + 1, :], rolled)
    return rolled


def _gated_conv(xc, bg, cg, filler, cw):
    u = cg * xc
    conv = cw[0:1, :] * _shift_rows(u, filler, 2) + cw[1:2, :] * _shift_rows(u, filler, 1) + cw[2:3, :] * u
    return bg * conv, u


def _pad_rows(x, rows):
    return jnp.concatenate([x, jnp.zeros((rows - x.shape[0], x.shape[1]), x.dtype)], axis=0)


def _heads_by_dim(ref, j):
    _, h, d, length = ref.shape
    return ref[j].reshape(h * d, length).astype(BF16)


def _sample_kernel(z_ref, cak_ref, cav_ref, cbk_ref, cbv_ref, st_ref, cw_ref, sink_ref, bias_a_ref,
                   bias_b_ref, mix_ref, conv_ref, *, n_new):
    for j in range(z_ref.shape[0] // n_new):
        _sample_sequence(j, z_ref, cak_ref, cav_ref, cbk_ref, cbv_ref, st_ref, cw_ref, sink_ref, bias_a_ref,
                         bias_b_ref, mix_ref, conv_ref, n_new)


def _sample_sequence(j, z_ref, cak_ref, cav_ref, cbk_ref, cbv_ref, st_ref, cw_ref, sink_ref, bias_a_ref,
                     bias_b_ref, mix_ref, conv_ref, n_new):
    tokens = slice(j * n_new, (j + 1) * n_new)
    z = z_ref[tokens, :]
    la = cak_ref.shape[3]
    lb = cbk_ref.shape[3]

    ka_new = _pad_rows(z[:, COL_KA:COL_KA + A_KV_W], LANES).astype(BF16)
    va_new = _pad_rows(z[:, COL_VA:COL_VA + A_KV_W], LANES).astype(BF16)
    half = _lane_half((n_new, LANES))
    pieces = []
    for i in range(A_KV_HEADS):
        qf = z[:, COL_QA + i * LANES:COL_QA + (i + 1) * LANES] * ATTN_SCALE
        for a in range(2):
            pieces.append(jnp.where(half == i, qf if a == i else pltpu.roll(qf, HEAD_DIM, 1), 0.0))
    qa = jnp.concatenate(pieces, axis=0).astype(BF16)
    lc = jnp.dot(qa, _heads_by_dim(cak_ref, j), preferred_element_type=F32) + bias_a_ref[:, :la]
    ln = _nt_dot(qa, ka_new) + bias_a_ref[:, la:]
    m = jnp.maximum(jnp.max(lc, axis=-1, keepdims=True), jnp.max(ln, axis=-1, keepdims=True))
    pc = jnp.exp(lc - m)
    pn = jnp.exp(ln - m)
    s = jnp.sum(pc, axis=-1, keepdims=True) + jnp.sum(pn, axis=-1, keepdims=True)
    oa = (_nt_dot(pc.astype(BF16), _heads_by_dim(cav_ref, j))
          + jnp.dot(pn.astype(BF16), va_new, preferred_element_type=F32))
    oa = oa / s * jax.nn.sigmoid(m + jnp.log(s) - sink_ref[...])
    oa_blocks = []
    for i in range(A_KV_HEADS):
        per_half = []
        for a in range(2):
            rows = oa[(2 * i + a) * n_new:(2 * i + a + 1) * n_new, :]
            per_half.append(rows if a == i else pltpu.roll(rows, HEAD_DIM, 1))
        oa_blocks.append(jnp.where(half == 0, per_half[0], per_half[1]))

    qf = z[:, COL_QB:COL_QB + B_W] * ATTN_SCALE
    head_of_lane = _lane_half((n_new, B_W))
    qb = jnp.concatenate([jnp.where(head_of_lane == h, qf, 0.0) for h in range(B_HEADS)], axis=0).astype(BF16)
    kb_new = _pad_rows(z[:, COL_KB:COL_KB + B_W], LANES).astype(BF16)
    vb_new = _pad_rows(z[:, COL_VB:COL_VB + B_W], LANES).astype(BF16)
    lg_c = jnp.dot(qb, _heads_by_dim(cbk_ref, j), preferred_element_type=F32)
    lg_n = _nt_dot(qb, kb_new)
    parts = []
    for bi, (w, r) in enumerate(B_BRANCHES):
        lo = lb - min(lb, -(-w // LANES) * LANES)
        lc = lg_c[:, lo:] + bias_b_ref[bi, :, lo:lb]
        ln = lg_n + bias_b_ref[bi, :, lb:]
        m = jnp.maximum(jnp.max(lc, axis=-1, keepdims=True), jnp.max(ln, axis=-1, keepdims=True))
        pc = jnp.exp(lc - m)
        pn = jnp.exp(ln - m)
        s = jnp.sum(pc, axis=-1, keepdims=True) + jnp.sum(pn, axis=-1, keepdims=True)
        parts.append((lo, pc, pn, s, m + jnp.log(s)))
    m_all = functools.reduce(jnp.maximum, [p[4] for p in parts])
    ws = [jnp.exp(p[4] - m_all) for p in parts]
    den = sum(ws)
    p_new = None
    los = sorted({p[0] for p in parts} | {lb})
    segs = [None] * (len(los) - 1)
    for (lo, pc, pn, s, _), w in zip(parts, ws):
        coef = w / (den * s)
        p_new = coef * pn if p_new is None else p_new + coef * pn
        for si in range(len(segs)):
            a0, a1 = los[si], los[si + 1]
            if a0 >= lo:
                piece = coef * pc[:, a0 - lo:a1 - lo]
                segs[si] = piece if segs[si] is None else segs[si] + piece
    p_cache = jnp.concatenate(segs, axis=1) if len(segs) > 1 else segs[0]
    ob = (_nt_dot(p_cache.astype(BF16), _heads_by_dim(cbv_ref, j))
          + jnp.dot(p_new.astype(BF16), vb_new, preferred_element_type=F32))
    ob_rows = sum(jnp.where(head_of_lane == h, ob[h * n_new:(h + 1) * n_new, :], 0.0) for h in range(B_HEADS))

    cz = z[:, COL_CONV:COL_CONV + 3 * C_WIDTH]
    oc, u = _gated_conv(cz[:, :C_WIDTH], cz[:, C_WIDTH:2 * C_WIDTH], cz[:, 2 * C_WIDTH:], st_ref[j], cw_ref[...])
    conv_ref[j] = u[n_new - (CONV_WIDTH - 1):, :]
    mix_ref[tokens, :] = jnp.concatenate(oa_blocks + [ob_rows, oc], axis=1)


def _sample_mixer(z, row0, cak, cav, cbk, cbv, layer, state, cw, sink_rows, bias_a, bias_b, n_new):
    n_seq = cak.shape[1]
    g = SEQS_PER_STEP
    rows = g * n_new
    blk0 = row0 // rows
    cache = lambda a: pl.BlockSpec((None, g) + a.shape[2:], lambda b: (layer, b, 0, 0, 0))
    per_seq = lambda shape: pl.BlockSpec((g,) + shape, lambda b: (b, 0, 0))
    const = lambda a: pl.BlockSpec(a.shape, lambda b: (0,) * a.ndim)
    return pl.pallas_call(
        functools.partial(_sample_kernel, n_new=n_new),
        out_shape=(jax.ShapeDtypeStruct((n_seq * n_new, D_MODEL), F32),
                   jax.ShapeDtypeStruct((n_seq, CONV_WIDTH - 1, C_WIDTH), F32)),
        grid=(n_seq // g,),
        in_specs=[pl.BlockSpec((rows, IN_WIDTH), lambda b: (blk0 + b, 0)),
                  cache(cak), cache(cav), cache(cbk), cache(cbv),
                  per_seq((CONV_WIDTH - 1, C_WIDTH)), const(cw), const(sink_rows), const(bias_a), const(bias_b)],
        out_specs=(pl.BlockSpec((rows, D_MODEL), lambda b: (b, 0)), per_seq((CONV_WIDTH - 1, C_WIDTH))),
        compiler_params=_params("parallel"),
    )(z, cak, cav, cbk, cbv, state, cw, sink_rows, bias_a, bias_b)


def _first_index(vals, best):
    idx = jnp.full(best.shape, len(vals) - 1, jnp.int32)
    for j in range(len(vals) - 2, -1, -1):
        idx = jnp.where(vals[j] == best, j, idx)
    return idx


def _route(lt):
    g = [lt[k:k + 1, :] for k in range(N_GROUPS)]
    g_max = functools.reduce(jnp.maximum, g)
    g_idx = _first_index(g, g_max)
    g_w = 1.0 / sum(jnp.exp(v - g_max) for v in g)
    e = []
    for j in range(EXPERTS_PER_GROUP):
        v = lt[N_GROUPS + j:N_GROUPS + j + 1, :]
        for gi in range(1, N_GROUPS):
            row = N_GROUPS + gi * EXPERTS_PER_GROUP + j
            v = jnp.where(g_idx == gi, lt[row:row + 1, :], v)
        e.append(v)
    e1 = functools.reduce(jnp.maximum, e)
    i1 = _first_index(e, e1)
    rest = [jnp.where(i1 == j, -jnp.inf, e[j]) for j in range(EXPERTS_PER_GROUP)]
    e2 = functools.reduce(jnp.maximum, rest)
    i2 = _first_index(rest, e2)
    t = jnp.exp(e2 - e1)
    w1 = g_w / (1.0 + t)
    w2 = g_w * t / (1.0 + t)
    swap = i2 < i1
    lo = jnp.where(swap, i2, i1)
    hi = jnp.where(swap, i1, i2)
    pair = jnp.where(lo == 0, hi - 1, jnp.where(lo == 1, hi + 1, len(PAIRS) - 1))
    bucket = g_idx * len(PAIRS) + pair
    return bucket, jnp.where(swap, w2, w1), jnp.where(swap, w1, w2)


def _outproj_kernel(yp_ref, ys_ref, oa_ref, ob_ref, zc_ref, zh_ref, ms_ref, wout_ref, cw_ref, gn_ref, wr_ref,
                    br_ref, y1_ref, info_ref, cnt_ref, ut_ref, y1_scr, carry_scr, *, prompt_tiles, tiles_per_seq):
    i = pl.program_id(0)
    tile = yp_ref.shape[0]

    @pl.when(i == 0)
    def _():
        carry_scr[...] = jnp.zeros_like(carry_scr)

    @pl.when(i < prompt_tiles)
    def _():
        zc = zc_ref[...]
        zh = zh_ref[...]
        halo = zh[:, 2 * C_WIDTH:] * zh[:, :C_WIDTH]
        halo = jnp.where(i % tiles_per_seq == 0, 0.0, halo)
        oc, u = _gated_conv(zc[:, :C_WIDTH], zc[:, C_WIDTH:2 * C_WIDTH], zc[:, 2 * C_WIDTH:], halo, cw_ref[...])
        ut_ref[...] = u[tile - 8:, :]
        y1_scr[...] = (
            yp_ref[...]
            + jnp.dot(oa_ref[...], wout_ref[0:A_Q_W, :], preferred_element_type=F32)
            + jnp.dot(ob_ref[...], wout_ref[A_Q_W:A_Q_W + B_W, :], preferred_element_type=F32)
            + jnp.dot(oc.astype(BF16), wout_ref[A_Q_W + B_W:, :], preferred_element_type=F32))

    @pl.when(i >= prompt_tiles)
    def _():
        ut_ref[...] = jnp.zeros_like(ut_ref)
        y1_scr[...] = ys_ref[...] + jnp.dot(ms_ref[...].astype(BF16), wout_ref[...], preferred_element_type=F32)

    y1 = y1_scr[...]
    y1_ref[:, :D_MODEL] = y1
    xn = _rms(y1, gn_ref[...])

    x_hi = xn.astype(BF16)
    x_lo = (xn - x_hi.astype(F32)).astype(BF16)
    wr = wr_ref[...]
    w_hi = wr.astype(BF16)
    w_lo = (wr - w_hi.astype(F32)).astype(BF16)
    lt_hi = _nt_dot(jnp.concatenate([w_hi, w_lo], axis=0), x_hi)
    lt = lt_hi[:ROUTE_ROWS] + lt_hi[ROUTE_ROWS:] + _nt_dot(w_hi, x_lo) + br_ref[...]
    bucket, w_lo, w_hi = _route(lt)
    onehot = (lax.broadcasted_iota(jnp.int32, (ROUTE_ROWS, tile), 0) == bucket).astype(F32)
    upper = (lax.broadcasted_iota(jnp.int32, (tile, tile), 0)
             <= lax.broadcasted_iota(jnp.int32, (tile, tile), 1)).astype(BF16)
    running = jnp.dot(onehot.astype(BF16), upper, preferred_element_type=F32)
    carry = carry_scr[...]
    rank = jnp.sum(onehot * (running - 1.0 + carry), axis=0, keepdims=True)
    carry = carry + jnp.sum(onehot, axis=1, keepdims=True)
    carry_scr[...] = carry
    cnt_ref[...] = jnp.broadcast_to(carry, cnt_ref.shape)
    info_ref[...] = jnp.concatenate([bucket.astype(F32), rank, jnp.zeros((SUBLANES - 2, tile), F32)], axis=0)
    y1_ref[:, D_MODEL:] = jnp.concatenate([w_lo, w_hi, jnp.zeros((LANES - 2, tile), F32)], axis=0).T


def _outproj_route(yp, ys, sample_row0, oa, ob, z, mix_s, w_out, cw, gain, w_route, b_route, n_prompt, seq_len):
    n = n_prompt + mix_s.shape[0]
    tiles = n // ROW_TILE
    p_tiles = n_prompt // ROW_TILE
    halo_blocks = ROW_TILE // 8
    conv_w = 3 * C_WIDTH
    pidx = lambda i: jnp.minimum(i, p_tiles - 1)
    sidx = lambda i: jnp.maximum(i - p_tiles, 0)
    const = lambda a: pl.BlockSpec(a.shape, lambda i: (0,) * a.ndim)
    gain = gain.reshape(1, D_MODEL)
    return pl.pallas_call(
        functools.partial(_outproj_kernel, prompt_tiles=p_tiles, tiles_per_seq=seq_len // ROW_TILE),
        out_shape=(jax.ShapeDtypeStruct((n, ROW_EXT), F32),
                   jax.ShapeDtypeStruct((8, n), F32),
                   jax.ShapeDtypeStruct((ROUTE_ROWS, LANES), F32),
                   jax.ShapeDtypeStruct((tiles * 8, C_WIDTH), F32)),
        grid=(tiles,),
        in_specs=[pl.BlockSpec((ROW_TILE, D_MODEL), lambda i: (pidx(i), 0)),
                  pl.BlockSpec((ROW_TILE, D_MODEL), lambda i: (sample_row0 // ROW_TILE + sidx(i), 0)),
                  pl.BlockSpec((ROW_TILE, A_Q_W), lambda i: (pidx(i), 0)),
                  pl.BlockSpec((ROW_TILE, B_W), lambda i: (pidx(i), 0)),
                  pl.BlockSpec((ROW_TILE, conv_w), lambda i: (pidx(i), 0)),
                  pl.BlockSpec((8, conv_w), lambda i: (jnp.maximum(pidx(i) * halo_blocks - 1, 0), 0)),
                  pl.BlockSpec((ROW_TILE, D_MODEL), lambda i: (sidx(i), 0)),
                  const(w_out), const(cw), const(gain), const(w_route), const(b_route)],
        out_specs=(pl.BlockSpec((ROW_TILE, ROW_EXT), lambda i: (i, 0)),
                   pl.BlockSpec((8, ROW_TILE), lambda i: (0, i)),
                   pl.BlockSpec((ROUTE_ROWS, LANES), lambda i: (0, 0)),
                   pl.BlockSpec((8, C_WIDTH), lambda i: (i, 0))),
        scratch_shapes=[pltpu.VMEM((ROW_TILE, D_MODEL), F32), pltpu.VMEM((ROUTE_ROWS, 1), F32)],
        compiler_params=_params("arbitrary"),
    )(yp, ys, oa, ob, z, z, mix_s, w_out, cw, gain, w_route, b_route)


def _moe_kernel(e_lo_ref, e_hi_ref, used_ref, x_ref, g_ref, wg1, wu1, wd1, wg2, wu2, wd2, *rest):
    o_ref = rest[-1]
    t = pl.program_id(0)

    @pl.when(used_ref[t] > 0)
    def _():
        y1 = x_ref[:, :D_MODEL]
        w = x_ref[:, D_MODEL:]
        x = _rms(y1, g_ref[...]).astype(BF16)

        def expert(wg, wu, wd, scale):
            g = jnp.dot(x, wg[...], preferred_element_type=F32)
            u = jnp.dot(x, wu[...], preferred_element_type=F32)
            h = g * jax.nn.sigmoid(g) * u * scale
            return jnp.dot(h.astype(BF16), wd[...], preferred_element_type=F32)

        y2 = y1 + expert(wg1, wu1, wd1, w[:, 0:1]) + expert(wg2, wu2, wd2, w[:, 1:2])
        o_ref[...] = _rms(y2, rest[0][...]) if len(rest) > 1 else y2

    @pl.when(used_ref[t] == 0)
    def _():
        o_ref[...] = jnp.zeros_like(o_ref)


def _experts(xs, gain, e_lo, e_hi, used, w_gate, w_up, w_down, layer, final_gain=None):
    n_tiles = xs.shape[0] // MOE_TILE
    up_spec = lambda sel: pl.BlockSpec((None, None, D_MODEL, EXPERT_FF),
                                       lambda t, lo, hi, u: (layer, (lo, hi)[sel][t], 0, 0))
    down_spec = lambda sel: pl.BlockSpec((None, None, EXPERT_FF, D_MODEL),
                                         lambda t, lo, hi, u: (layer, (lo, hi)[sel][t], 0, 0))
    gain_spec = pl.BlockSpec((1, D_MODEL), lambda t, lo, hi, u: (0, 0))
    extra = [] if final_gain is None else [final_gain.reshape(1, D_MODEL)]
    grid_spec = pltpu.PrefetchScalarGridSpec(
        num_scalar_prefetch=3,
        grid=(n_tiles,),
        in_specs=[pl.BlockSpec((MOE_TILE, ROW_EXT), lambda t, lo, hi, u: (t, 0)), gain_spec,
                  up_spec(0), up_spec(0), down_spec(0), up_spec(1), up_spec(1), down_spec(1)]
                 + [gain_spec] * len(extra),
        out_specs=pl.BlockSpec((MOE_TILE, D_MODEL), lambda t, lo, hi, u: (t, 0)))
    return pl.pallas_call(
        _moe_kernel,
        out_shape=jax.ShapeDtypeStruct((xs.shape[0], D_MODEL), F32),
        grid_spec=grid_spec,
        compiler_params=_params("arbitrary"),
    )(e_lo, e_hi, used, xs, gain.reshape(1, D_MODEL), w_gate, w_up, w_down, w_gate, w_up, w_down, *extra)


def _dispatch_plan(info, counts, n):
    n_tiles = -(-n // MOE_TILE) + N_ROUTE_BUCKETS
    bucket = info[0].astype(jnp.int32)
    rank = info[1].astype(jnp.int32)
    counts = counts[:N_ROUTE_BUCKETS, 0].astype(jnp.int32)
    tiles_per_bucket = (counts + MOE_TILE - 1) // MOE_TILE
    tile_end = jnp.cumsum(tiles_per_bucket)
    row_start = (tile_end - tiles_per_bucket) * MOE_TILE
    dest = row_start[bucket] + rank
    src = (jnp.arange(n_tiles * MOE_TILE, dtype=jnp.int32) % n).at[dest].set(jnp.arange(n, dtype=jnp.int32))
    tile_ids = jnp.arange(n_tiles, dtype=jnp.int32)
    tile_bucket = jnp.minimum(jnp.searchsorted(tile_end, tile_ids, side="right").astype(jnp.int32),
                              N_ROUTE_BUCKETS - 1)
    used = (tile_ids < tile_end[-1]).astype(jnp.int32)
    pair = tile_bucket % len(PAIRS)
    base = (tile_bucket // len(PAIRS)) * EXPERTS_PER_GROUP
    pairs = jnp.asarray(PAIRS, jnp.int32)
    return dest, src, base + pairs[pair, 0], base + pairs[pair, 1], used


def _permute_in_columns(w):
    attn = A_Q_W + 2 * A_KV_W + 3 * B_W
    return jnp.concatenate([w[:, attn:], w[:, A_Q_W:A_Q_W + A_KV_W], w[:, :A_Q_W], w[:, A_Q_W + A_KV_W:attn]],
                           axis=1)


def kernel(x_prompt, x_sample, cache_a_k, cache_a_v, cache_b_k, cache_b_v, state_conv, rel_bias_table,
           w_in, w_out, conv_w, attn_sinks, norm_mix, norm_ffn, w_group, b_group, w_router, b_router,
           w_gate, w_up, w_down, norm_final):
    n_seq, seq_len, _ = x_prompt.shape
    dec_seq, n_new, _ = x_sample.shape
    depth = w_in.shape[0]
    n_prompt = n_seq * seq_len
    n_sample = dec_seq * n_new
    n = n_prompt + n_sample
    la, lb = cache_a_k.shape[2], cache_b_k.shape[2]
    assert seq_len % SPAN == 0 and n_prompt % IN_TILE == 0 and n_sample % IN_TILE == 0 and IN_TILE % ROW_TILE == 0
    assert seq_len >= SPAN and la % LANES == 0 and lb % LANES == 0 and n_new == 8

    table_a, table_b = rel_bias_table[:, :A_Q_HEADS], rel_bias_table[:, A_Q_HEADS:]
    bias_a = _band_bias(table_a, 1)
    bias_b = jnp.stack([_band_bias(table_b, r) for r in DILATIONS], axis=1)
    sbias_a = _sample_bias(table_a, n_new, la, np.arange(la + LANES), A_WINDOW, 1)
    sbias_a = sbias_a.reshape(A_Q_HEADS * n_new, la + LANES)
    sbias_b = jnp.stack([_sample_bias(table_b, n_new, lb, np.arange(lb + LANES), w, r)
                         .reshape(B_HEADS * n_new, lb + LANES) for w, r in B_BRANCHES])
    cak, cav, cbk, cbv = (c.transpose(0, 1, 3, 4, 2) for c in (cache_a_k, cache_a_v, cache_b_k, cache_b_v))

    w_in_b = jnp.stack([_permute_in_columns(w_in[l]) for l in range(depth)]).astype(BF16)
    w_out_b = w_out.astype(BF16)
    expert_w = tuple(w.astype(BF16) for w in (w_gate, w_up, w_down))
    pad = ROUTE_ROWS - N_GROUPS - N_EXPERTS
    w_route = jnp.pad(jnp.concatenate([w_group, w_router], axis=2).transpose(0, 2, 1), ((0, 0), (0, pad), (0, 0)))
    b_route = jnp.pad(jnp.concatenate([b_group, b_router], axis=1), ((0, 0), (0, pad)))[..., None]
    sink_rows = jnp.repeat(attn_sinks, n_new, axis=1)[..., None]

    yp, ys, s_row0 = x_prompt.reshape(n_prompt, D_MODEL), x_sample.reshape(n_sample, D_MODEL), 0
    states = []
    for l in range(depth):
        z = _inproj(yp, ys, s_row0, n_prompt, n_sample, norm_mix[l], w_in_b[l])
        oa = _window_attention(z, bias_a, attn_sinks[l], n_seq, seq_len)
        ob = _dilated_attention(z, bias_b, n_seq, seq_len)
        mix_s, conv_s = _sample_mixer(z, n_prompt, cak, cav, cbk, cbv, l, state_conv[l], conv_w[l], sink_rows[l],
                                      sbias_a, sbias_b, n_new)
        y1, info, counts, u_tail = _outproj_route(
            yp, ys, s_row0, oa, ob, z, mix_s, w_out_b[l], conv_w[l], norm_ffn[l], w_route[l], b_route[l], n_prompt, seq_len)
        dest, src, e_lo, e_hi, used = _dispatch_plan(info, counts, n)
        y_sorted = _experts(y1[src], norm_ffn[l], e_lo, e_hi, used, *expert_w, l,
                            final_gain=norm_final if l == depth - 1 else None)
        if l < depth - 1:
            yp = ys = y_sorted[dest]
            s_row0 = n_prompt

        lap, lbp = min(A_WINDOW, seq_len), min(SPAN, seq_len)

        def prompt_tail(length, col, width, heads):
            rows = [z[(b + 1) * seq_len - length:(b + 1) * seq_len, col:col + width] for b in range(n_seq)]
            return jnp.stack(rows).reshape(n_seq, length, heads, HEAD_DIM)

        def sample_rows(col, width, heads):
            return z[n_prompt:, col:col + width].reshape(dec_seq, n_new, heads, HEAD_DIM)

        last_tile = [((b + 1) * seq_len // ROW_TILE - 1) * SUBLANES for b in range(n_seq)]
        conv_p = jnp.stack([u_tail[t + SUBLANES - (CONV_WIDTH - 1):t + SUBLANES, :] for t in last_tile])
        states.append((
            prompt_tail(lap, COL_KA, A_KV_W, A_KV_HEADS), prompt_tail(lap, COL_VA, A_KV_W, A_KV_HEADS),
            prompt_tail(lbp, COL_KB, B_W, B_HEADS), prompt_tail(lbp, COL_VB, B_W, B_HEADS), conv_p,
            sample_rows(COL_KA, A_KV_W, A_KV_HEADS), sample_rows(COL_VA, A_KV_W, A_KV_HEADS),
            sample_rows(COL_KB, B_W, B_HEADS), sample_rows(COL_VB, B_W, B_HEADS), conv_s))

    y_prompt = y_sorted[dest[:n_prompt]].reshape(n_seq, seq_len, D_MODEL)
    y_sample = y_sorted[dest[n_prompt:]].reshape(dec_seq, n_new, D_MODEL)
    st = [jnp.stack([s[k] for s in states]) for k in range(10)]
    return (y_prompt, y_sample, st[0], st[1], st[2], st[3], st[4], st[5], st[6], st[7], st[8], st[9])
```

```python
import functools
import math

import numpy as np
import jax
import jax.numpy as jnp
from jax import lax
from jax.experimental import pallas as pl
from jax.experimental.pallas import tpu as pltpu

F32 = jnp.float32
BF16 = jnp.bfloat16

D_MODEL = 1024
HEAD_DIM = 64
ATTN_SCALE = HEAD_DIM ** -0.5
LOG2E = math.log2(math.e)
LN2 = math.log(2.0)
BAND_Q_SCALE = ATTN_SCALE * LOG2E
BLOCK = 128
LANES = 128
SUBLANES = 8
A_Q_HEADS = 4
A_KV_HEADS = 2
A_WINDOW = 128
B_HEADS = 6
B_BRANCHES = ((128, 1), (512, 4), (2048, 16))
DILATIONS = tuple(r for _, r in B_BRANCHES)
SPAN = BLOCK * max(DILATIONS)
C_WIDTH = 6 * HEAD_DIM
CONV_WIDTH = 3
A_Q_W = A_Q_HEADS * HEAD_DIM
A_KV_W = A_KV_HEADS * HEAD_DIM
B_W = B_HEADS * HEAD_DIM
IN_WIDTH = A_Q_W + 2 * A_KV_W + 3 * B_W + 3 * C_WIDTH
N_BUCKETS = 32
MAX_DISTANCE = 2048
N_GROUPS = 4
EXPERTS_PER_GROUP = 4
N_EXPERTS = N_GROUPS * EXPERTS_PER_GROUP
EXPERT_FF = 512
RMS_EPS = 1e-6
NEG_INF = -1e30

COL_CONV = 0
COL_KA = 3 * C_WIDTH
COL_QA = COL_KA + A_KV_W
COL_VA = COL_QA + A_Q_W
COL_QB = COL_VA + A_KV_W
COL_KB = COL_QB + B_W
COL_VB = COL_KB + B_W

ROW_TILE = 512
IN_TILE = 1024
MOE_TILE = 256
SEQS_PER_STEP = 2
PAIRS = ((0, 1), (0, 2), (0, 3), (1, 2), (1, 3), (2, 3))
N_ROUTE_BUCKETS = N_GROUPS * len(PAIRS)
ROUTE_ROWS = 32
ROW_EXT = D_MODEL + LANES
VMEM_LIMIT = 56 * 1024 * 1024


def _params(*sem):
    return pltpu.CompilerParams(dimension_semantics=sem, vmem_limit_bytes=VMEM_LIMIT)


def _rel_bucket(dist):
    d = np.maximum(dist, 0)
    max_exact = N_BUCKETS // 2
    df = np.maximum(d, max_exact).astype(np.float32)
    large = max_exact + (np.log(df / max_exact) / math.log(MAX_DISTANCE / max_exact)
                         * (N_BUCKETS - max_exact)).astype(np.int32)
    large = np.minimum(large, N_BUCKETS - 1)
    return np.where(d < max_exact, d, large)


def _masked_bias(table, dist, valid):
    bucket = _rel_bucket(dist).reshape(-1, 1)
    onehot = (jnp.asarray(bucket, jnp.int32) == jnp.arange(N_BUCKETS, dtype=jnp.int32)[None, :]).astype(F32)
    b = jnp.dot(onehot, table.astype(F32), precision=lax.Precision.HIGHEST)
    b = jnp.where(jnp.asarray(valid.reshape(-1, 1)), b, NEG_INF)
    return jnp.moveaxis(b.reshape(dist.shape + (table.shape[1],)), -1, 0)


def _band_bias(table, scale):
    i = np.arange(BLOCK)[:, None]
    j = np.arange(2 * BLOCK)[None, :]
    dist = i + BLOCK - j
    b = _masked_bias(table, dist * scale, (dist >= 0) & (dist <= BLOCK)) * LOG2E
    return b.reshape(table.shape[1] // 2, 2 * BLOCK, 2 * BLOCK)


def _sample_bias(table, n_new, first_new, positions, window, dilation):
    dist = first_new + np.arange(n_new)[:, None] - np.asarray(positions)[None, :]
    valid = (dist >= 0) & (dist <= window) & (dist % dilation == 0)
    return _masked_bias(table, dist, valid)


def _rms(x, gain):
    return x * lax.rsqrt(jnp.mean(x * x, axis=-1, keepdims=True) + RMS_EPS) * gain


def _inproj_kernel(xp_ref, xs_ref, g_ref, w_ref, z_ref, *, prompt_tiles):
    def project(x_ref):
        h = _rms(x_ref[...], g_ref[...])
        z_ref[...] = jnp.dot(h.astype(BF16), w_ref[...], preferred_element_type=F32)

    pl.when(pl.program_id(0) < prompt_tiles)(lambda: project(xp_ref))
    pl.when(pl.program_id(0) >= prompt_tiles)(lambda: project(xs_ref))


def _inproj(xp, xs, sample_row0, n_prompt, n_sample, gain, w):
    p_tiles, s_tile0 = n_prompt // IN_TILE, sample_row0 // IN_TILE
    return pl.pallas_call(
        functools.partial(_inproj_kernel, prompt_tiles=p_tiles),
        out_shape=jax.ShapeDtypeStruct((n_prompt + n_sample, IN_WIDTH), F32),
        grid=((n_prompt + n_sample) // IN_TILE,),
        in_specs=[pl.BlockSpec((IN_TILE, D_MODEL), lambda i: (jnp.minimum(i, p_tiles - 1), 0)),
                  pl.BlockSpec((IN_TILE, D_MODEL), lambda i: (s_tile0 + jnp.maximum(i - p_tiles, 0), 0)),
                  pl.BlockSpec((1, D_MODEL), lambda i: (0, 0)),
                  pl.BlockSpec((D_MODEL, IN_WIDTH), lambda i: (0, 0))],
        out_specs=pl.BlockSpec((IN_TILE, IN_WIDTH), lambda i: (i, 0)),
        compiler_params=_params("parallel"),
    )(xp, xs, gain.reshape(1, D_MODEL), w)


def _lane_half(shape):
    return lax.broadcasted_iota(jnp.int32, shape, len(shape) - 1) // HEAD_DIM


def _nt_dot(a, b):
    return lax.dot_general(a, b, (((1,), (1,)), ((), ())), preferred_element_type=F32)

def _band_pair(q2, k2, v2, bias2, penalty):
    lg = _nt_dot(q2, k2) + bias2
    if penalty is not None:
        lg = lg + penalty
    m = jnp.max(lg, axis=-1, keepdims=True)
    p = jnp.exp2(lg - m)
    s = jnp.sum(p, axis=-1, keepdims=True)
    acc = jnp.dot(p.astype(BF16), v2, preferred_element_type=F32)
    return acc / s, (m + jnp.log2(s)) * LN2


def _prev_key_penalty(first_span):
    col = lax.broadcasted_iota(jnp.int32, (1, 2 * BLOCK), 1)
    return jnp.where((col < BLOCK) & first_span, NEG_INF, 0.0).astype(F32)


def _rows(start, dilation):
    if dilation == 1:
        return pl.ds(start, BLOCK)
    return pl.ds(start, BLOCK, stride=dilation)


def _dilated_kernel(q_ref, kc_ref, kp_ref, vc_ref, vp_ref, bias_ref, o_ref, o_scr, l_scr, first_bias_scr):
    first_half = _lane_half((BLOCK, LANES)) == 0
    penalty = _prev_key_penalty(pl.program_id(1) == 0)
    for bi in range(len(DILATIONS)):
        first_bias_scr[bi] = bias_ref[bi] + penalty

    for bi, r in enumerate(DILATIONS):
        step = r * BLOCK

        def block(c, n, first, bi=bi, r=r, step=step):
            q_start = c + n * step
            if not isinstance(q_start, int):
                q_start = pl.multiple_of(q_start, BLOCK) if r == 1 else q_start
            cur = _rows(q_start, r)
            qt = (q_ref[cur, :] * BAND_Q_SCALE).astype(BF16)
            if first:
                src_k, src_v, prev = kp_ref, vp_ref, _rows(SPAN - step + c, r)
            else:
                p_start = q_start - step
                if r == 1:
                    p_start = pl.multiple_of(p_start, BLOCK)
                src_k, src_v, prev = kc_ref, vc_ref, _rows(p_start, r)
            k2 = jnp.concatenate([src_k[prev, :], kc_ref[cur, :]], axis=0).astype(BF16)
            v2 = jnp.concatenate([src_v[prev, :], vc_ref[cur, :]], axis=0).astype(BF16)
            zero = jnp.zeros_like(qt)
            q2 = jnp.concatenate([jnp.where(first_half, qt, zero), jnp.where(first_half, zero, qt)], axis=0)
            o, lse = _band_pair(q2, k2, v2, first_bias_scr[bi] if first else bias_ref[bi], None)
            o_scr[bi, cur, :] = jnp.where(first_half, o[:BLOCK], o[BLOCK:])
            l_scr[bi, cur, :] = jnp.where(first_half, lse[:BLOCK], lse[BLOCK:])

        n_blocks = SPAN // step
        if r == 1:
            block(0, 0, True)
            lax.fori_loop(1, n_blocks, lambda n, _: block(0, n, False), None, unroll=True)
        else:
            def per_class(c, _, block=block, n_blocks=n_blocks):
                block(c, 0, True)
                for n in range(1, n_blocks):
                    block(c, n, False)
            lax.fori_loop(0, r, per_class, None, unroll=True)

    chunk = 2 * BLOCK

    def combine(j, _):
        rows = pl.ds(pl.multiple_of(j * chunk, chunk), chunk)
        ls = [l_scr[bi, rows, :] for bi in range(len(DILATIONS))]
        m = functools.reduce(jnp.maximum, ls)
        ws = [jnp.exp(l - m) for l in ls]
        num = sum(w * o_scr[bi, rows, :] for bi, w in enumerate(ws))
        o_ref[rows, :] = (num / sum(ws)).astype(o_ref.dtype)

    lax.fori_loop(0, SPAN // chunk, combine, None)


def _dilated_attention(z, bias, n_seq, seq_len):
    spans = seq_len // SPAN
    qb, kb, vb = COL_QB // LANES, COL_KB // LANES, COL_VB // LANES
    blk = (SPAN, LANES)
    cur = lambda col: (lambda b, s, i: (b * spans + s, col + i))
    prev = lambda col: (lambda b, s, i: (b * spans + jnp.maximum(s - 1, 0), col + i))
    return pl.pallas_call(
        _dilated_kernel,
        out_shape=jax.ShapeDtypeStruct((n_seq * seq_len, B_W), BF16),
        grid=(n_seq, spans, B_W // LANES),
        in_specs=[pl.BlockSpec(blk, cur(qb)),
                  pl.BlockSpec(blk, cur(kb)), pl.BlockSpec(blk, prev(kb)),
                  pl.BlockSpec(blk, cur(vb)), pl.BlockSpec(blk, prev(vb)),
                  pl.BlockSpec((None, len(DILATIONS), 2 * BLOCK, 2 * BLOCK), lambda b, s, i: (i, 0, 0, 0))],
        out_specs=pl.BlockSpec(blk, lambda b, s, i: (b * spans + s, i)),
        scratch_shapes=[pltpu.VMEM((len(DILATIONS), SPAN, LANES), F32),
                        pltpu.VMEM((len(DILATIONS), SPAN, LANES), F32),
                        pltpu.VMEM((len(DILATIONS), 2 * BLOCK, 2 * BLOCK), F32)],
        compiler_params=_params("parallel", "parallel", "parallel"),
    )(z, z, z, z, z, bias)


def _window_kernel(sink_ref, q_ref, kc_ref, kp_ref, vc_ref, vp_ref, bias_ref, o_ref):
    penalty = _prev_key_penalty(pl.program_id(1) == 0)
    first_half = _lane_half((BLOCK, LANES)) == 0
    first_half2 = _lane_half((2 * BLOCK, LANES)) == 0

    def block(n, first):
        start = n * BLOCK if isinstance(n, int) else pl.multiple_of(n * BLOCK, BLOCK)
        cur = pl.ds(start, BLOCK)
        if first:
            src_k, src_v, prev = kp_ref, vp_ref, pl.ds(SPAN - BLOCK, BLOCK)
        else:
            src_k, src_v, prev = kc_ref, vc_ref, pl.ds(pl.multiple_of(start - BLOCK, BLOCK), BLOCK)
        k2 = jnp.concatenate([src_k[prev, :], kc_ref[cur, :]], axis=0).astype(BF16)
        vf = jnp.concatenate([src_v[prev, :], vc_ref[cur, :]], axis=0)
        v_swap = pltpu.roll(vf, HEAD_DIM, 1)
        for i in range(A_KV_HEADS):
            qf = q_ref[cur, i * LANES:(i + 1) * LANES] * BAND_Q_SCALE
            q_swap = pltpu.roll(qf, HEAD_DIM, 1)
            q2 = jnp.concatenate([jnp.where(first_half == (i == 0), qf if a == i else q_swap, 0.0)
                                  for a in range(2)], axis=0).astype(BF16)
            v2 = jnp.where(first_half2 == (i == 0), vf, v_swap).astype(BF16)
            o, lse = _band_pair(q2, k2, v2, bias_ref[i], penalty if first else None)
            outs = [o[a * BLOCK:(a + 1) * BLOCK] * jax.nn.sigmoid(lse[a * BLOCK:(a + 1) * BLOCK] - sink_ref[2 * i + a])
                    for a in range(2)]
            o_ref[cur, i * LANES:(i + 1) * LANES] = jnp.where(first_half, outs[0], outs[1]).astype(o_ref.dtype)

    block(0, True)
    lax.fori_loop(1, SPAN // BLOCK, lambda n, _: block(n, False), None, unroll=5)


def _window_attention(z, bias, sinks, n_seq, seq_len):
    spans = seq_len // SPAN
    ka, va = COL_KA // LANES, COL_VA // LANES
    blk = (SPAN, LANES)
    cur = lambda col: (lambda b, s: (b * spans + s, col))
    prev = lambda col: (lambda b, s: (b * spans + jnp.maximum(s - 1, 0), col))
    return pl.pallas_call(
        _window_kernel,
        out_shape=jax.ShapeDtypeStruct((n_seq * seq_len, A_Q_W), BF16),
        grid=(n_seq, spans),
        in_specs=[pl.BlockSpec(memory_space=pltpu.SMEM),
                  pl.BlockSpec((SPAN, A_Q_W), lambda b, s: (b * spans + s, COL_QA // A_Q_W)),
                  pl.BlockSpec(blk, cur(ka)), pl.BlockSpec(blk, prev(ka)),
                  pl.BlockSpec(blk, cur(va)), pl.BlockSpec(blk, prev(va)),
                  pl.BlockSpec((A_KV_HEADS, 2 * BLOCK, 2 * BLOCK), lambda b, s: (0, 0, 0))],
        out_specs=pl.BlockSpec((SPAN, A_Q_W), lambda b, s: (b * spans + s, 0)),
        compiler_params=_params("parallel", "parallel"),
    )(sinks, z, z, z, z, z, bias)


def _shift_rows(u, filler, k):
    rolled = pltpu.roll(u, k, 0)
    row = lax.broadcasted_iota(jnp.int32, u.shape, 0)
    n_fill = filler.shape[0]
    for j in range(k):
        rolled = jnp.where(row == j, filler[n_fill - k + j:n_fill - k + j + 1, :], rolled)
    return rolled


def _gated_conv(xc, bg, cg, filler, cw):
    u = cg * xc
    conv = cw[0:1, :] * _shift_rows(u, filler, 2) + cw[1:2, :] * _shift_rows(u, filler, 1) + cw[2:3, :] * u
    return bg * conv, u


def _pad_rows(x, rows):
    return jnp.concatenate([x, jnp.zeros((rows - x.shape[0], x.shape[1]), x.dtype)], axis=0)


def _heads_by_dim(ref, j):
    _, h, d, length = ref.shape
    return ref[j].reshape(h * d, length).astype(BF16)


def _sample_kernel(z_ref, cak_ref, cav_ref, cbk_ref, cbv_ref, st_ref, cw_ref, sink_ref, bias_a_ref,
                   bias_b_ref, mix_ref, conv_ref, *, n_new):
    for j in range(z_ref.shape[0] // n_new):
        _sample_sequence(j, z_ref, cak_ref, cav_ref, cbk_ref, cbv_ref, st_ref, cw_ref, sink_ref, bias_a_ref,
                         bias_b_ref, mix_ref, conv_ref, n_new)


def _sample_sequence(j, z_ref, cak_ref, cav_ref, cbk_ref, cbv_ref, st_ref, cw_ref, sink_ref, bias_a_ref,
                     bias_b_ref, mix_ref, conv_ref, n_new):
    tokens = slice(j * n_new, (j + 1) * n_new)
    z = z_ref[tokens, :]
    la = cak_ref.shape[3]
    lb = cbk_ref.shape[3]

    ka_new = _pad_rows(z[:, COL_KA:COL_KA + A_KV_W], LANES).astype(BF16)
    va_new = _pad_rows(z[:, COL_VA:COL_VA + A_KV_W], LANES).astype(BF16)
    half = _lane_half((n_new, LANES))
    pieces = []
    for i in range(A_KV_HEADS):
        qf = z[:, COL_QA + i * LANES:COL_QA + (i + 1) * LANES] * ATTN_SCALE
        for a in range(2):
            pieces.append(jnp.where(half == i, qf if a == i else pltpu.roll(qf, HEAD_DIM, 1), 0.0))
    qa = jnp.concatenate(pieces, axis=0).astype(BF16)
    lc = jnp.dot(qa, _heads_by_dim(cak_ref, j), preferred_element_type=F32) + bias_a_ref[:, :la]
    ln = _nt_dot(qa, ka_new) + bias_a_ref[:, la:]
    m = jnp.maximum(jnp.max(lc, axis=-1, keepdims=True), jnp.max(ln, axis=-1, keepdims=True))
    pc = jnp.exp(lc - m)
    pn = jnp.exp(ln - m)
    s = jnp.sum(pc, axis=-1, keepdims=True) + jnp.sum(pn, axis=-1, keepdims=True)
    oa = (_nt_dot(pc.astype(BF16), _heads_by_dim(cav_ref, j))
          + jnp.dot(pn.astype(BF16), va_new, preferred_element_type=F32))
    oa = oa / s * jax.nn.sigmoid(m + jnp.log(s) - sink_ref[...])
    oa_blocks = []
    for i in range(A_KV_HEADS):
        per_half = []
        for a in range(2):
            rows = oa[(2 * i + a) * n_new:(2 * i + a + 1) * n_new, :]
            per_half.append(rows if a == i else pltpu.roll(rows, HEAD_DIM, 1))
        oa_blocks.append(jnp.where(half == 0, per_half[0], per_half[1]))

    qf = z[:, COL_QB:COL_QB + B_W] * ATTN_SCALE
    head_of_lane = _lane_half((n_new, B_W))
    qb = jnp.concatenate([jnp.where(head_of_lane == h, qf, 0.0) for h in range(B_HEADS)], axis=0).astype(BF16)
    kb_new = _pad_rows(z[:, COL_KB:COL_KB + B_W], LANES).astype(BF16)
    vb_new = _pad_rows(z[:, COL_VB:COL_VB + B_W], LANES).astype(BF16)
    lg_c = jnp.dot(qb, _heads_by_dim(cbk_ref, j), preferred_element_type=F32)
    lg_n = _nt_dot(qb, kb_new)
    parts = []
    for bi, (w, r) in enumerate(B_BRANCHES):
        lo = lb - min(lb, -(-w // LANES) * LANES)
        lc = lg_c[:, lo:] + bias_b_ref[bi, :, lo:lb]
        ln = lg_n + bias_b_ref[bi, :, lb:]
        m = jnp.maximum(jnp.max(lc, axis=-1, keepdims=True), jnp.max(ln, axis=-1, keepdims=True))
        pc = jnp.exp(lc - m)
        pn = jnp.exp(ln - m)
        s = jnp.sum(pc, axis=-1, keepdims=True) + jnp.sum(pn, axis=-1, keepdims=True)
        parts.append((lo, pc, pn, s, m + jnp.log(s)))
    m_all = functools.reduce(jnp.maximum, [p[4] for p in parts])
    ws = [jnp.exp(p[4] - m_all) for p in parts]
    den = sum(ws)
    p_new = None
    los = sorted({p[0] for p in parts} | {lb})
    segs = [None] * (len(los) - 1)
    for (lo, pc, pn, s, _), w in zip(parts, ws):
        coef = w / (den * s)
        p_new = coef * pn if p_new is None else p_new + coef * pn
        for si in range(len(segs)):
            a0, a1 = los[si], los[si + 1]
            if a0 >= lo:
                piece = coef * pc[:, a0 - lo:a1 - lo]
                segs[si] = piece if segs[si] is None else segs[si] + piece
    p_cache = jnp.concatenate(segs, axis=1) if len(segs) > 1 else segs[0]
    ob = (_nt_dot(p_cache.astype(BF16), _heads_by_dim(cbv_ref, j))
          + jnp.dot(p_new.astype(BF16), vb_new, preferred_element_type=F32))
    ob_rows = sum(jnp.where(head_of_lane == h, ob[h * n_new:(h + 1) * n_new, :], 0.0) for h in range(B_HEADS))

    cz = z[:, COL_CONV:COL_CONV + 3 * C_WIDTH]
    oc, u = _gated_conv(cz[:, :C_WIDTH], cz[:, C_WIDTH:2 * C_WIDTH], cz[:, 2 * C_WIDTH:], st_ref[j], cw_ref[...])
    conv_ref[j] = u[n_new - (CONV_WIDTH - 1):, :]
    mix_ref[tokens, :] = jnp.concatenate(oa_blocks + [ob_rows, oc], axis=1)


def _sample_mixer(z, row0, cak, cav, cbk, cbv, layer, state, cw, sink_rows, bias_a, bias_b, n_new):
    n_seq = cak.shape[1]
    g = SEQS_PER_STEP
    rows = g * n_new
    blk0 = row0 // rows
    cache = lambda a: pl.BlockSpec((None, g) + a.shape[2:], lambda b: (layer, b, 0, 0, 0))
    per_seq = lambda shape: pl.BlockSpec((g,) + shape, lambda b: (b, 0, 0))
    const = lambda a: pl.BlockSpec(a.shape, lambda b: (0,) * a.ndim)
    return pl.pallas_call(
        functools.partial(_sample_kernel, n_new=n_new),
        out_shape=(jax.ShapeDtypeStruct((n_seq * n_new, D_MODEL), F32),
                   jax.ShapeDtypeStruct((n_seq, CONV_WIDTH - 1, C_WIDTH), F32)),
        grid=(n_seq // g,),
        in_specs=[pl.BlockSpec((rows, IN_WIDTH), lambda b: (blk0 + b, 0)),
                  cache(cak), cache(cav), cache(cbk), cache(cbv),
                  per_seq((CONV_WIDTH - 1, C_WIDTH)), const(cw), const(sink_rows), const(bias_a), const(bias_b)],
        out_specs=(pl.BlockSpec((rows, D_MODEL), lambda b: (b, 0)), per_seq((CONV_WIDTH - 1, C_WIDTH))),
        compiler_params=_params("parallel"),
    )(z, cak, cav, cbk, cbv, state, cw, sink_rows, bias_a, bias_b)


def _first_index(vals, best):
    idx = jnp.full(best.shape, len(vals) - 1, jnp.int32)
    for j in range(len(vals) - 2, -1, -1):
        idx = jnp.where(vals[j] == best, j, idx)
    return idx


def _route(lt):
    g = [lt[k:k + 1, :] for k in range(N_GROUPS)]
    g_max = functools.reduce(jnp.maximum, g)
    g_idx = _first_index(g, g_max)
    g_w = 1.0 / sum(jnp.exp(v - g_max) for v in g)
    e = []
    for j in range(EXPERTS_PER_GROUP):
        v = lt[N_GROUPS + j:N_GROUPS + j + 1, :]
        for gi in range(1, N_GROUPS):
            row = N_GROUPS + gi * EXPERTS_PER_GROUP + j
            v = jnp.where(g_idx == gi, lt[row:row + 1, :], v)
        e.append(v)
    e1 = functools.reduce(jnp.maximum, e)
    i1 = _first_index(e, e1)
    rest = [jnp.where(i1 == j, -jnp.inf, e[j]) for j in range(EXPERTS_PER_GROUP)]
    e2 = functools.reduce(jnp.maximum, rest)
    i2 = _first_index(rest, e2)
    t = jnp.exp(e2 - e1)
    w1 = g_w / (1.0 + t)
    w2 = g_w * t / (1.0 + t)
    swap = i2 < i1
    lo = jnp.where(swap, i2, i1)
    hi = jnp.where(swap, i1, i2)
    pair = jnp.where(lo == 0, hi - 1, jnp.where(lo == 1, hi + 1, len(PAIRS) - 1))
    bucket = g_idx * len(PAIRS) + pair
    return bucket, jnp.where(swap, w2, w1), jnp.where(swap, w1, w2)


def _outproj_kernel(yp_ref, ys_ref, oa_ref, ob_ref, zc_ref, zh_ref, ms_ref, wout_ref, cw_ref, gn_ref, wr_ref,
                    br_ref, y1_ref, info_ref, cnt_ref, ut_ref, y1_scr, carry_scr, *, prompt_tiles, tiles_per_seq):
    i = pl.program_id(0)
    tile = yp_ref.shape[0]

    @pl.when(i == 0)
    def _():
        carry_scr[...] = jnp.zeros_like(carry_scr)

    @pl.when(i < prompt_tiles)
    def _():
        zc = zc_ref[...]
        zh = zh_ref[...]
        halo = zh[:, 2 * C_WIDTH:] * zh[:, :C_WIDTH]
        halo = jnp.where(i % tiles_per_seq == 0, 0.0, halo)
        oc, u = _gated_conv(zc[:, :C_WIDTH], zc[:, C_WIDTH:2 * C_WIDTH], zc[:, 2 * C_WIDTH:], halo, cw_ref[...])
        ut_ref[...] = u[tile - 8:, :]
        y1_scr[...] = (
            yp_ref[...]
            + jnp.dot(oa_ref[...], wout_ref[0:A_Q_W, :], preferred_element_type=F32)
            + jnp.dot(ob_ref[...], wout_ref[A_Q_W:A_Q_W + B_W, :], preferred_element_type=F32)
            + jnp.dot(oc.astype(BF16), wout_ref[A_Q_W + B_W:, :], preferred_element_type=F32))

    @pl.when(i >= prompt_tiles)
    def _():
        ut_ref[...] = jnp.zeros_like(ut_ref)
        y1_scr[...] = ys_ref[...] + jnp.dot(ms_ref[...].astype(BF16), wout_ref[...], preferred_element_type=F32)

    y1 = y1_scr[...]
    y1_ref[:, :D_MODEL] = y1
    xn = _rms(y1, gn_ref[...])

    x_hi = xn.astype(BF16)
    x_lo = (xn - x_hi.astype(F32)).astype(BF16)
    wr = wr_ref[...]
    w_hi = wr.astype(BF16)
    w_lo = (wr - w_hi.astype(F32)).astype(BF16)
    lt_hi = _nt_dot(jnp.concatenate([w_hi, w_lo], axis=0), x_hi)
    lt = lt_hi[:ROUTE_ROWS] + lt_hi[ROUTE_ROWS:] + _nt_dot(w_hi, x_lo) + br_ref[...]
    bucket, w_lo, w_hi = _route(lt)
    onehot = (lax.broadcasted_iota(jnp.int32, (ROUTE_ROWS, tile), 0) == bucket).astype(F32)
    upper = (lax.broadcasted_iota(jnp.int32, (tile, tile), 0)
             <= lax.broadcasted_iota(jnp.int32, (tile, tile), 1)).astype(BF16)
    running = jnp.dot(onehot.astype(BF16), upper, preferred_element_type=F32)
    carry = carry_scr[...]
    rank = jnp.sum(onehot * (running - 1.0 + carry), axis=0, keepdims=True)
    carry = carry + jnp.sum(onehot, axis=1, keepdims=True)
    carry_scr[...] = carry
    cnt_ref[...] = jnp.broadcast_to(carry, cnt_ref.shape)
    info_ref[...] = jnp.concatenate([bucket.astype(F32), rank, jnp.zeros((SUBLANES - 2, tile), F32)], axis=0)
    y1_ref[:, D_MODEL:] = jnp.concatenate([w_lo, w_hi, jnp.zeros((LANES - 2, tile), F32)], axis=0).T


def _outproj_route(yp, ys, sample_row0, oa, ob, z, mix_s, w_out, cw, gain, w_route, b_route, n_prompt, seq_len):
    n = n_prompt + mix_s.shape[0]
    tiles = n // ROW_TILE
    p_tiles = n_prompt // ROW_TILE
    halo_blocks = ROW_TILE // 8
    conv_w = 3 * C_WIDTH
    pidx = lambda i: jnp.minimum(i, p_tiles - 1)
    sidx = lambda i: jnp.maximum(i - p_tiles, 0)
    const = lambda a: pl.BlockSpec(a.shape, lambda i: (0,) * a.ndim)
    gain = gain.reshape(1, D_MODEL)
    return pl.pallas_call(
        functools.partial(_outproj_kernel, prompt_tiles=p_tiles, tiles_per_seq=seq_len // ROW_TILE),
        out_shape=(jax.ShapeDtypeStruct((n, ROW_EXT), F32),
                   jax.ShapeDtypeStruct((8, n), F32),
                   jax.ShapeDtypeStruct((ROUTE_ROWS, LANES), F32),
                   jax.ShapeDtypeStruct((tiles * 8, C_WIDTH), F32)),
        grid=(tiles,),
        in_specs=[pl.BlockSpec((ROW_TILE, D_MODEL), lambda i: (pidx(i), 0)),
                  pl.BlockSpec((ROW_TILE, D_MODEL), lambda i: (sample_row0 // ROW_TILE + sidx(i), 0)),
                  pl.BlockSpec((ROW_TILE, A_Q_W), lambda i: (pidx(i), 0)),
                  pl.BlockSpec((ROW_TILE, B_W), lambda i: (pidx(i), 0)),
                  pl.BlockSpec((ROW_TILE, conv_w), lambda i: (pidx(i), 0)),
                  pl.BlockSpec((8, conv_w), lambda i: (jnp.maximum(pidx(i) * halo_blocks - 1, 0), 0)),
                  pl.BlockSpec((ROW_TILE, D_MODEL), lambda i: (sidx(i), 0)),
                  const(w_out), const(cw), const(gain), const(w_route), const(b_route)],
        out_specs=(pl.BlockSpec((ROW_TILE, ROW_EXT), lambda i: (i, 0)),
                   pl.BlockSpec((8, ROW_TILE), lambda i: (0, i)),
                   pl.BlockSpec((ROUTE_ROWS, LANES), lambda i: (0, 0)),
                   pl.BlockSpec((8, C_WIDTH), lambda i: (i, 0))),
        scratch_shapes=[pltpu.VMEM((ROW_TILE, D_MODEL), F32), pltpu.VMEM((ROUTE_ROWS, 1), F32)],
        compiler_params=_params("arbitrary"),
    )(yp, ys, oa, ob, z, z, mix_s, w_out, cw, gain, w_route, b_route)


def _moe_kernel(e_lo_ref, e_hi_ref, used_ref, x_ref, g_ref, wg1, wu1, wd1, wg2, wu2, wd2, *rest):
    o_ref = rest[-1]
    t = pl.program_id(0)

    @pl.when(used_ref[t] > 0)
    def _():
        y1 = x_ref[:, :D_MODEL]
        w = x_ref[:, D_MODEL:]
        x = _rms(y1, g_ref[...]).astype(BF16)

        def expert(wg, wu, wd, scale):
            g = jnp.dot(x, wg[...], preferred_element_type=F32)
            u = jnp.dot(x, wu[...], preferred_element_type=F32)
            h = g * jax.nn.sigmoid(g) * u * scale
            return jnp.dot(h.astype(BF16), wd[...], preferred_element_type=F32)

        y2 = y1 + expert(wg1, wu1, wd1, w[:, 0:1]) + expert(wg2, wu2, wd2, w[:, 1:2])
        o_ref[...] = _rms(y2, rest[0][...]) if len(rest) > 1 else y2

    @pl.when(used_ref[t] == 0)
    def _():
        o_ref[...] = jnp.zeros_like(o_ref)


def _experts(xs, gain, e_lo, e_hi, used, w_gate, w_up, w_down, layer, final_gain=None):
    n_tiles = xs.shape[0] // MOE_TILE
    up_spec = lambda sel: pl.BlockSpec((None, None, D_MODEL, EXPERT_FF),
                                       lambda t, lo, hi, u: (layer, (lo, hi)[sel][t], 0, 0))
    down_spec = lambda sel: pl.BlockSpec((None, None, EXPERT_FF, D_MODEL),
                                         lambda t, lo, hi, u: (layer, (lo, hi)[sel][t], 0, 0))
    gain_spec = pl.BlockSpec((1, D_MODEL), lambda t, lo, hi, u: (0, 0))
    extra = [] if final_gain is None else [final_gain.reshape(1, D_MODEL)]
    grid_spec = pltpu.PrefetchScalarGridSpec(
        num_scalar_prefetch=3,
        grid=(n_tiles,),
        in_specs=[pl.BlockSpec((MOE_TILE, ROW_EXT), lambda t, lo, hi, u: (t, 0)), gain_spec,
                  up_spec(0), up_spec(0), down_spec(0), up_spec(1), up_spec(1), down_spec(1)]
                 + [gain_spec] * len(extra),
        out_specs=pl.BlockSpec((MOE_TILE, D_MODEL), lambda t, lo, hi, u: (t, 0)))
    return pl.pallas_call(
        _moe_kernel,
        out_shape=jax.ShapeDtypeStruct((xs.shape[0], D_MODEL), F32),
        grid_spec=grid_spec,
        compiler_params=_params("arbitrary"),
    )(e_lo, e_hi, used, xs, gain.reshape(1, D_MODEL), w_gate, w_up, w_down, w_gate, w_up, w_down, *extra)


def _dispatch_plan(info, counts, n):
    n_tiles = -(-n // MOE_TILE) + N_ROUTE_BUCKETS
    bucket = info[0].astype(jnp.int32)
    rank = info[1].astype(jnp.int32)
    counts = counts[:N_ROUTE_BUCKETS, 0].astype(jnp.int32)
    tiles_per_bucket = (counts + MOE_TILE - 1) // MOE_TILE
    tile_end = jnp.cumsum(tiles_per_bucket)
    row_start = (tile_end - tiles_per_bucket) * MOE_TILE
    dest = row_start[bucket] + rank
    src = (jnp.arange(n_tiles * MOE_TILE, dtype=jnp.int32) % n).at[dest].set(jnp.arange(n, dtype=jnp.int32))
    tile_ids = jnp.arange(n_tiles, dtype=jnp.int32)
    tile_bucket = jnp.minimum(jnp.searchsorted(tile_end, tile_ids, side="right").astype(jnp.int32),
                              N_ROUTE_BUCKETS - 1)
    used = (tile_ids < tile_end[-1]).astype(jnp.int32)
    pair = tile_bucket % len(PAIRS)
    base = (tile_bucket // len(PAIRS)) * EXPERTS_PER_GROUP
    pairs = jnp.asarray(PAIRS, jnp.int32)
    return dest, src, base + pairs[pair, 0], base + pairs[pair, 1], used


def _permute_in_columns(w):
    attn = A_Q_W + 2 * A_KV_W + 3 * B_W
    return jnp.concatenate([w[:, attn:], w[:, A_Q_W:A_Q_W + A_KV_W], w[:, :A_Q_W], w[:, A_Q_W + A_KV_W:attn]],
                           axis=1)


def kernel(x_prompt, x_sample, cache_a_k, cache_a_v, cache_b_k, cache_b_v, state_conv, rel_bias_table,
           w_in, w_out, conv_w, attn_sinks, norm_mix, norm_ffn, w_group, b_group, w_router, b_router,
           w_gate, w_up, w_down, norm_final):
    n_seq, seq_len, _ = x_prompt.shape
    dec_seq, n_new, _ = x_sample.shape
    depth = w_in.shape[0]
    n_prompt = n_seq * seq_len
    n_sample = dec_seq * n_new
    n = n_prompt + n_sample
    la, lb = cache_a_k.shape[2], cache_b_k.shape[2]
    assert seq_len % SPAN == 0 and n_prompt % IN_TILE == 0 and n_sample % IN_TILE == 0 and IN_TILE % ROW_TILE == 0
    assert seq_len >= SPAN and la % LANES == 0 and lb % LANES == 0 and n_new == 8

    table_a, table_b = rel_bias_table[:, :A_Q_HEADS], rel_bias_table[:, A_Q_HEADS:]
    bias_a = _band_bias(table_a, 1)
    bias_b = jnp.stack([_band_bias(table_b, r) for r in DILATIONS], axis=1)
    sbias_a = _sample_bias(table_a, n_new, la, np.arange(la + LANES), A_WINDOW, 1)
    sbias_a = sbias_a.reshape(A_Q_HEADS * n_new, la + LANES)
    sbias_b = jnp.stack([_sample_bias(table_b, n_new, lb, np.arange(lb + LANES), w, r)
                         .reshape(B_HEADS * n_new, lb + LANES) for w, r in B_BRANCHES])
    cak, cav, cbk, cbv = (c.transpose(0, 1, 3, 4, 2) for c in (cache_a_k, cache_a_v, cache_b_k, cache_b_v))

    w_in_b = jnp.stack([_permute_in_columns(w_in[l]) for l in range(depth)]).astype(BF16)
    w_out_b = w_out.astype(BF16)
    expert_w = tuple(w.astype(BF16) for w in (w_gate, w_up, w_down))
    pad = ROUTE_ROWS - N_GROUPS - N_EXPERTS
    w_route = jnp.pad(jnp.concatenate([w_group, w_router], axis=2).transpose(0, 2, 1), ((0, 0), (0, pad), (0, 0)))
    b_route = jnp.pad(jnp.concatenate([b_group, b_router], axis=1), ((0, 0), (0, pad)))[..., None]
    sink_rows = jnp.repeat(attn_sinks, n_new, axis=1)[..., None]

    yp, ys, s_row0 = x_prompt.reshape(n_prompt, D_MODEL), x_sample.reshape(n_sample, D_MODEL), 0
    states = []
    for l in range(depth):
        z = _inproj(yp, ys, s_row0, n_prompt, n_sample, norm_mix[l], w_in_b[l])
        oa = _window_attention(z, bias_a, attn_sinks[l], n_seq, seq_len)
        ob = _dilated_attention(z, bias_b, n_seq, seq_len)
        mix_s, conv_s = _sample_mixer(z, n_prompt, cak, cav, cbk, cbv, l, state_conv[l], conv_w[l], sink_rows[l],
                                      sbias_a, sbias_b, n_new)
        y1, info, counts, u_tail = _outproj_route(
            yp, ys, s_row0, oa, ob, z, mix_s, w_out_b[l], conv_w[l], norm_ffn[l], w_route[l], b_route[l], n_prompt, seq_len)
        lap, lbp = min(A_WINDOW, seq_len), min(SPAN, seq_len)

        def prompt_tail(length, col, width, heads):
            rows = [z[(b + 1) * seq_len - length:(b + 1) * seq_len, col:col + width] for b in range(n_seq)]
            return jnp.stack(rows).reshape(n_seq, length, heads, HEAD_DIM)

        def sample_rows(col, width, heads):
            return z[n_prompt:, col:col + width].reshape(dec_seq, n_new, heads, HEAD_DIM)

        last_tile = [((b + 1) * seq_len // ROW_TILE - 1) * SUBLANES for b in range(n_seq)]
        conv_p = jnp.stack([u_tail[t + SUBLANES - (CONV_WIDTH - 1):t + SUBLANES, :] for t in last_tile])
        layer_states = (
            prompt_tail(lap, COL_KA, A_KV_W, A_KV_HEADS), prompt_tail(lap, COL_VA, A_KV_W, A_KV_HEADS),
            prompt_tail(lbp, COL_KB, B_W, B_HEADS), prompt_tail(lbp, COL_VB, B_W, B_HEADS), conv_p,
            sample_rows(COL_KA, A_KV_W, A_KV_HEADS), sample_rows(COL_VA, A_KV_W, A_KV_HEADS),
            sample_rows(COL_KB, B_W, B_HEADS), sample_rows(COL_VB, B_W, B_HEADS), conv_s)

        dest, src, e_lo, e_hi, used = _dispatch_plan(info, counts, n)
        xs = y1[src]
        if l == depth - 1:
            xs, layer_states = lax.optimization_barrier((xs, layer_states))
        y_sorted = _experts(xs, norm_ffn[l], e_lo, e_hi, used, *expert_w, l,
                            final_gain=norm_final if l == depth - 1 else None)
        if l < depth - 1:
            y_next, layer_states = lax.optimization_barrier((y_sorted[dest], layer_states))
            yp, ys, s_row0 = y_next, y_next, n_prompt
        states.append(layer_states)

    y_prompt = y_sorted[dest[:n_prompt]].reshape(n_seq, seq_len, D_MODEL)
    y_sample = y_sorted[dest[n_prompt:]].reshape(dec_seq, n_new, D_MODEL)
    st = [jnp.stack([s[k] for s in states]) for k in range(10)]
    return (y_prompt, y_sample, st[0], st[1], st[2], st[3], st[4], st[5], st[6], st[7], st[8], st[9])
```

```python
import functools
import math

import numpy as np
import jax
import jax.numpy as jnp
from jax import lax
from jax.experimental import pallas as pl
from jax.experimental.pallas import tpu as pltpu

F32 = jnp.float32
BF16 = jnp.bfloat16

D_MODEL = 1024
HEAD_DIM = 64
ATTN_SCALE = HEAD_DIM ** -0.5
LOG2E = math.log2(math.e)
LN2 = math.log(2.0)
BAND_Q_SCALE = ATTN_SCALE * LOG2E
BLOCK = 128
LANES = 128
SUBLANES = 8
A_Q_HEADS = 4
A_KV_HEADS = 2
A_WINDOW = 128
B_HEADS = 6
B_BRANCHES = ((128, 1), (512, 4), (2048, 16))
DILATIONS = tuple(r for _, r in B_BRANCHES)
SPAN = BLOCK * max(DILATIONS)
C_WIDTH = 6 * HEAD_DIM
CONV_WIDTH = 3
A_Q_W = A_Q_HEADS * HEAD_DIM
A_KV_W = A_KV_HEADS * HEAD_DIM
B_W = B_HEADS * HEAD_DIM
IN_WIDTH = A_Q_W + 2 * A_KV_W + 3 * B_W + 3 * C_WIDTH
N_BUCKETS = 32
MAX_DISTANCE = 2048
N_GROUPS = 4
EXPERTS_PER_GROUP = 4
N_EXPERTS = N_GROUPS * EXPERTS_PER_GROUP
EXPERT_FF = 512
RMS_EPS = 1e-6
NEG_INF = -1e30

COL_CONV = 0
COL_KA = 3 * C_WIDTH
COL_QA = COL_KA + A_KV_W
COL_VA = COL_QA + A_Q_W
COL_QB = COL_VA + A_KV_W
COL_KB = COL_QB + B_W
COL_VB = COL_KB + B_W

ROW_TILE = 512
IN_TILE = 1024
MOE_TILE = 256
SEQS_PER_STEP = 2
PAIRS = ((1, 0), (2, 0), (2, 1), (3, 1), (3, 0), (3, 2))
N_ROUTE_BUCKETS = N_GROUPS * len(PAIRS)
ROUTE_ROWS = 32
ROW_EXT = D_MODEL + LANES
VMEM_LIMIT = 56 * 1024 * 1024


def _params(*sem):
    return pltpu.CompilerParams(dimension_semantics=sem, vmem_limit_bytes=VMEM_LIMIT)


def _rel_bucket(dist):
    d = np.maximum(dist, 0)
    max_exact = N_BUCKETS // 2
    df = np.maximum(d, max_exact).astype(np.float32)
    large = max_exact + (np.log(df / max_exact) / math.log(MAX_DISTANCE / max_exact)
                         * (N_BUCKETS - max_exact)).astype(np.int32)
    large = np.minimum(large, N_BUCKETS - 1)
    return np.where(d < max_exact, d, large)


def _masked_bias(table, dist, valid):
    bucket = _rel_bucket(dist).reshape(-1, 1)
    onehot = (jnp.asarray(bucket, jnp.int32) == jnp.arange(N_BUCKETS, dtype=jnp.int32)[None, :]).astype(F32)
    b = jnp.dot(onehot, table.astype(F32), precision=lax.Precision.HIGHEST)
    b = jnp.where(jnp.asarray(valid.reshape(-1, 1)), b, NEG_INF)
    return jnp.moveaxis(b.reshape(dist.shape + (table.shape[1],)), -1, 0)


def _band_bias(table, scale):
    i = np.arange(BLOCK)[:, None]
    j = np.arange(2 * BLOCK)[None, :]
    dist = i + BLOCK - j
    b = _masked_bias(table, dist * scale, (dist >= 0) & (dist <= BLOCK)) * LOG2E
    return b.reshape(table.shape[1] // 2, 2 * BLOCK, 2 * BLOCK)


def _sample_bias(table, n_new, first_new, positions, window, dilation):
    dist = first_new + np.arange(n_new)[:, None] - np.asarray(positions)[None, :]
    valid = (dist >= 0) & (dist <= window) & (dist % dilation == 0)
    return _masked_bias(table, dist, valid)


def _rms(x, gain):
    return x * lax.rsqrt(jnp.mean(x * x, axis=-1, keepdims=True) + RMS_EPS) * gain


def _inproj_kernel(xp_ref, xs_ref, g_ref, w_ref, z_ref, kbt_ref, vbt_ref, *, prompt_tiles, tiles_per_seq, tail_tiles):
    i = pl.program_id(0)

    def project(x_ref):
        h = _rms(x_ref[...], g_ref[...])
        z_ref[...] = jnp.dot(h.astype(BF16), w_ref[...], preferred_element_type=F32)

    pl.when(i < prompt_tiles)(lambda: project(xp_ref))
    pl.when(i >= prompt_tiles)(lambda: project(xs_ref))

    @pl.when((i < prompt_tiles) & (i % tiles_per_seq >= tiles_per_seq - tail_tiles))
    def _():
        kbt_ref[...] = z_ref[:, COL_KB:COL_KB + B_W].T
        vbt_ref[...] = z_ref[:, COL_VB:COL_VB + B_W].T


def _inproj(xp, xs, sample_row0, n_prompt, n_sample, gain, w, seq_len, tail_len):
    p_tiles, s_tile0 = n_prompt // IN_TILE, sample_row0 // IN_TILE
    tps, tail = seq_len // IN_TILE, tail_len // IN_TILE
    n_seq = n_prompt // seq_len

    def tail_block(i):
        j = jnp.minimum(i, p_tiles - 1)
        return j // tps, 0, jnp.maximum(j % tps - (tps - tail), 0)

    tail_spec = pl.BlockSpec((None, B_W, IN_TILE), tail_block)
    tail_shape = jax.ShapeDtypeStruct((n_seq, B_W, tail_len), F32)
    return pl.pallas_call(
        functools.partial(_inproj_kernel, prompt_tiles=p_tiles, tiles_per_seq=tps, tail_tiles=tail),
        out_shape=(jax.ShapeDtypeStruct((n_prompt + n_sample, IN_WIDTH), F32), tail_shape, tail_shape),
        grid=((n_prompt + n_sample) // IN_TILE,),
        in_specs=[pl.BlockSpec((IN_TILE, D_MODEL), lambda i: (jnp.minimum(i, p_tiles - 1), 0)),
                  pl.BlockSpec((IN_TILE, D_MODEL), lambda i: (s_tile0 + jnp.maximum(i - p_tiles, 0), 0)),
                  pl.BlockSpec((1, D_MODEL), lambda i: (0, 0)),
                  pl.BlockSpec((D_MODEL, IN_WIDTH), lambda i: (0, 0))],
        out_specs=(pl.BlockSpec((IN_TILE, IN_WIDTH), lambda i: (i, 0)), tail_spec, tail_spec),
        compiler_params=_params("arbitrary"),
    )(xp, xs, gain.reshape(1, D_MODEL), w)


def _lane_half(shape):
    return lax.broadcasted_iota(jnp.int32, shape, len(shape) - 1) // HEAD_DIM


def _nt_dot(a, b):
    return lax.dot_general(a, b, (((1,), (1,)), ((), ())), preferred_element_type=F32)

def _band_pair(q2, k2, v2, bias2, penalty):
    lg = _nt_dot(q2, k2) + bias2
    if penalty is not None:
        lg = lg + penalty
    m = jnp.max(lg, axis=-1, keepdims=True)
    p = jnp.exp2(lg - m)
    s = jnp.sum(p, axis=-1, keepdims=True)
    acc = jnp.dot(p.astype(BF16), v2, preferred_element_type=F32)
    return acc / s, (m + jnp.log2(s)) * LN2


def _prev_key_penalty(first_span):
    col = lax.broadcasted_iota(jnp.int32, (1, 2 * BLOCK), 1)
    return jnp.where((col < BLOCK) & first_span, NEG_INF, 0.0).astype(F32)


def _rows(start, dilation):
    if dilation == 1:
        return pl.ds(start, BLOCK)
    return pl.ds(start, BLOCK, stride=dilation)


def _dilated_kernel(q_ref, kc_ref, kp_ref, vc_ref, vp_ref, bias_ref, o_ref, o_scr, l_scr, first_bias_scr):
    first_half = _lane_half((BLOCK, LANES)) == 0
    penalty = _prev_key_penalty(pl.program_id(1) == 0)
    for bi in range(len(DILATIONS)):
        first_bias_scr[bi] = bias_ref[bi] + penalty

    for bi, r in enumerate(DILATIONS):
        step = r * BLOCK

        def block(c, n, first, bi=bi, r=r, step=step):
            q_start = c + n * step
            if not isinstance(q_start, int):
                q_start = pl.multiple_of(q_start, BLOCK) if r == 1 else q_start
            cur = _rows(q_start, r)
            qt = (q_ref[cur, :] * BAND_Q_SCALE).astype(BF16)
            if first:
                src_k, src_v, prev = kp_ref, vp_ref, _rows(SPAN - step + c, r)
            else:
                p_start = q_start - step
                if r == 1:
                    p_start = pl.multiple_of(p_start, BLOCK)
                src_k, src_v, prev = kc_ref, vc_ref, _rows(p_start, r)
            k2 = jnp.concatenate([src_k[prev, :], kc_ref[cur, :]], axis=0).astype(BF16)
            v2 = jnp.concatenate([src_v[prev, :], vc_ref[cur, :]], axis=0).astype(BF16)
            zero = jnp.zeros_like(qt)
            q2 = jnp.concatenate([jnp.where(first_half, qt, zero), jnp.where(first_half, zero, qt)], axis=0)
            o, lse = _band_pair(q2, k2, v2, first_bias_scr[bi] if first else bias_ref[bi], None)
            o_scr[bi, cur, :] = jnp.where(first_half, o[:BLOCK], o[BLOCK:])
            l_scr[bi, cur, :] = jnp.where(first_half, lse[:BLOCK], lse[BLOCK:])

        n_blocks = SPAN // step
        if r == 1:
            block(0, 0, True)
            lax.fori_loop(1, n_blocks, lambda n, _: block(0, n, False), None, unroll=True)
        else:
            def per_class(c, _, block=block, n_blocks=n_blocks):
                block(c, 0, True)
                for n in range(1, n_blocks):
                    block(c, n, False)
            lax.fori_loop(0, r, per_class, None, unroll=True)

    chunk = 2 * BLOCK

    def combine(j, _):
        rows = pl.ds(pl.multiple_of(j * chunk, chunk), chunk)
        ls = [l_scr[bi, rows, :] for bi in range(len(DILATIONS))]
        m = functools.reduce(jnp.maximum, ls)
        ws = [jnp.exp(l - m) for l in ls]
        num = sum(w * o_scr[bi, rows, :] for bi, w in enumerate(ws))
        o_ref[rows, :] = (num / sum(ws)).astype(o_ref.dtype)

    lax.fori_loop(0, SPAN // chunk, combine, None)


def _dilated_attention(z, bias, n_seq, seq_len):
    spans = seq_len // SPAN
    qb, kb, vb = COL_QB // LANES, COL_KB // LANES, COL_VB // LANES
    blk = (SPAN, LANES)
    cur = lambda col: (lambda b, s, i: (b * spans + s, col + i))
    prev = lambda col: (lambda b, s, i: (b * spans + jnp.maximum(s - 1, 0), col + i))
    return pl.pallas_call(
        _dilated_kernel,
        out_shape=jax.ShapeDtypeStruct((n_seq * seq_len, B_W), BF16),
        grid=(n_seq, spans, B_W // LANES),
        in_specs=[pl.BlockSpec(blk, cur(qb)),
                  pl.BlockSpec(blk, cur(kb)), pl.BlockSpec(blk, prev(kb)),
                  pl.BlockSpec(blk, cur(vb)), pl.BlockSpec(blk, prev(vb)),
                  pl.BlockSpec((None, len(DILATIONS), 2 * BLOCK, 2 * BLOCK), lambda b, s, i: (i, 0, 0, 0))],
        out_specs=pl.BlockSpec(blk, lambda b, s, i: (b * spans + s, i)),
        scratch_shapes=[pltpu.VMEM((len(DILATIONS), SPAN, LANES), F32),
                        pltpu.VMEM((len(DILATIONS), SPAN, LANES), F32),
                        pltpu.VMEM((len(DILATIONS), 2 * BLOCK, 2 * BLOCK), F32)],
        compiler_params=_params("parallel", "parallel", "parallel"),
    )(z, z, z, z, z, bias)


def _window_kernel(sink_ref, q_ref, kc_ref, kp_ref, vc_ref, vp_ref, bias_ref, o_ref):
    penalty = _prev_key_penalty(pl.program_id(1) == 0)
    first_half = _lane_half((BLOCK, LANES)) == 0
    first_half2 = _lane_half((2 * BLOCK, LANES)) == 0

    def block(n, first):
        start = n * BLOCK if isinstance(n, int) else pl.multiple_of(n * BLOCK, BLOCK)
        cur = pl.ds(start, BLOCK)
        if first:
            src_k, src_v, prev = kp_ref, vp_ref, pl.ds(SPAN - BLOCK, BLOCK)
        else:
            src_k, src_v, prev = kc_ref, vc_ref, pl.ds(pl.multiple_of(start - BLOCK, BLOCK), BLOCK)
        k2 = jnp.concatenate([src_k[prev, :], kc_ref[cur, :]], axis=0).astype(BF16)
        vf = jnp.concatenate([src_v[prev, :], vc_ref[cur, :]], axis=0)
        v_swap = pltpu.roll(vf, HEAD_DIM, 1)
        for i in range(A_KV_HEADS):
            qf = q_ref[cur, i * LANES:(i + 1) * LANES] * BAND_Q_SCALE
            q_swap = pltpu.roll(qf, HEAD_DIM, 1)
            q2 = jnp.concatenate([jnp.where(first_half == (i == 0), qf if a == i else q_swap, 0.0)
                                  for a in range(2)], axis=0).astype(BF16)
            v2 = jnp.where(first_half2 == (i == 0), vf, v_swap).astype(BF16)
            o, lse = _band_pair(q2, k2, v2, bias_ref[i], penalty if first else None)
            outs = [o[a * BLOCK:(a + 1) * BLOCK] * jax.nn.sigmoid(lse[a * BLOCK:(a + 1) * BLOCK] - sink_ref[2 * i + a])
                    for a in range(2)]
            o_ref[cur, i * LANES:(i + 1) * LANES] = jnp.where(first_half, outs[0], outs[1]).astype(o_ref.dtype)

    block(0, True)
    lax.fori_loop(1, SPAN // BLOCK, lambda n, _: block(n, False), None, unroll=5)


def _window_attention(z, bias, sinks, n_seq, seq_len):
    spans = seq_len // SPAN
    ka, va = COL_KA // LANES, COL_VA // LANES
    blk = (SPAN, LANES)
    cur = lambda col: (lambda b, s: (b * spans + s, col))
    prev = lambda col: (lambda b, s: (b * spans + jnp.maximum(s - 1, 0), col))
    return pl.pallas_call(
        _window_kernel,
        out_shape=jax.ShapeDtypeStruct((n_seq * seq_len, A_Q_W), BF16),
        grid=(n_seq, spans),
        in_specs=[pl.BlockSpec(memory_space=pltpu.SMEM),
                  pl.BlockSpec((SPAN, A_Q_W), lambda b, s: (b * spans + s, COL_QA // A_Q_W)),
                  pl.BlockSpec(blk, cur(ka)), pl.BlockSpec(blk, prev(ka)),
                  pl.BlockSpec(blk, cur(va)), pl.BlockSpec(blk, prev(va)),
                  pl.BlockSpec((A_KV_HEADS, 2 * BLOCK, 2 * BLOCK), lambda b, s: (0, 0, 0))],
        out_specs=pl.BlockSpec((SPAN, A_Q_W), lambda b, s: (b * spans + s, 0)),
        compiler_params=_params("parallel", "parallel"),
    )(sinks, z, z, z, z, z, bias)


def _shift_rows(u, filler, k):
    rolled = pltpu.roll(u, k, 0)
    row = lax.broadcasted_iota(jnp.int32, u.shape, 0)
    n_fill = filler.shape[0]
    for j in range(k):
        rolled = jnp.where(row == j, filler[n_fill - k + j:n_fill - k + j + 1, :], rolled)
    return rolled


def _gated_conv(xc, bg, cg, filler, cw):
    u = cg * xc
    conv = cw[0:1, :] * _shift_rows(u, filler, 2) + cw[1:2, :] * _shift_rows(u, filler, 1) + cw[2:3, :] * u
    return bg * conv, u


def _pad_rows(x, rows):
    return jnp.concatenate([x, jnp.zeros((rows - x.shape[0], x.shape[1]), x.dtype)], axis=0)


def _heads_by_dim(ref, j):
    _, h, d, length = ref.shape
    return ref[j].reshape(h * d, length).astype(BF16)


def _sample_kernel(z_ref, cak_ref, cav_ref, cbk_ref, cbv_ref, st_ref, cw_ref, sink_ref, bias_a_ref,
                   bias_b_ref, mix_ref, conv_ref, *, n_new):
    for j in range(z_ref.shape[0] // n_new):
        _sample_sequence(j, z_ref, cak_ref, cav_ref, cbk_ref, cbv_ref, st_ref, cw_ref, sink_ref, bias_a_ref,
                         bias_b_ref, mix_ref, conv_ref, n_new)


def _sample_sequence(j, z_ref, cak_ref, cav_ref, cbk_ref, cbv_ref, st_ref, cw_ref, sink_ref, bias_a_ref,
                     bias_b_ref, mix_ref, conv_ref, n_new):
    tokens = slice(j * n_new, (j + 1) * n_new)
    z = z_ref[tokens, :]
    la = cak_ref.shape[3]
    lb = cbk_ref.shape[3]

    ka_new = _pad_rows(z[:, COL_KA:COL_KA + A_KV_W], LANES).astype(BF16)
    va_new = _pad_rows(z[:, COL_VA:COL_VA + A_KV_W], LANES).astype(BF16)
    half = _lane_half((n_new, LANES))
    pieces = []
    for i in range(A_KV_HEADS):
        qf = z[:, COL_QA + i * LANES:COL_QA + (i + 1) * LANES] * ATTN_SCALE
        for a in range(2):
            pieces.append(jnp.where(half == i, qf if a == i else pltpu.roll(qf, HEAD_DIM, 1), 0.0))
    qa = jnp.concatenate(pieces, axis=0).astype(BF16)
    lc = jnp.dot(qa, _heads_by_dim(cak_ref, j), preferred_element_type=F32) + bias_a_ref[:, :la]
    ln = _nt_dot(qa, ka_new) + bias_a_ref[:, la:]
    m = jnp.maximum(jnp.max(lc, axis=-1, keepdims=True), jnp.max(ln, axis=-1, keepdims=True))
    pc = jnp.exp(lc - m)
    pn = jnp.exp(ln - m)
    s = jnp.sum(pc, axis=-1, keepdims=True) + jnp.sum(pn, axis=-1, keepdims=True)
    oa = (_nt_dot(pc.astype(BF16), _heads_by_dim(cav_ref, j))
          + jnp.dot(pn.astype(BF16), va_new, preferred_element_type=F32))
    oa = oa / s * jax.nn.sigmoid(m + jnp.log(s) - sink_ref[...])
    oa_blocks = []
    for i in range(A_KV_HEADS):
        per_half = []
        for a in range(2):
            rows = oa[(2 * i + a) * n_new:(2 * i + a + 1) * n_new, :]
            per_half.append(rows if a == i else pltpu.roll(rows, HEAD_DIM, 1))
        oa_blocks.append(jnp.where(half == 0, per_half[0], per_half[1]))

    qf = z[:, COL_QB:COL_QB + B_W] * ATTN_SCALE
    head_of_lane = _lane_half((n_new, B_W))
    qb = jnp.concatenate([jnp.where(head_of_lane == h, qf, 0.0) for h in range(B_HEADS)], axis=0).astype(BF16)
    kb_new = _pad_rows(z[:, COL_KB:COL_KB + B_W], LANES).astype(BF16)
    vb_new = _pad_rows(z[:, COL_VB:COL_VB + B_W], LANES).astype(BF16)
    lg_c = jnp.dot(qb, _heads_by_dim(cbk_ref, j), preferred_element_type=F32)
    lg_n = _nt_dot(qb, kb_new)
    parts = []
    for bi, (w, r) in enumerate(B_BRANCHES):
        lo = lb - min(lb, -(-w // LANES) * LANES)
        lc = lg_c[:, lo:] + bias_b_ref[bi, :, lo:lb]
        ln = lg_n + bias_b_ref[bi, :, lb:]
        m = jnp.maximum(jnp.max(lc, axis=-1, keepdims=True), jnp.max(ln, axis=-1, keepdims=True))
        pc = jnp.exp(lc - m)
        pn = jnp.exp(ln - m)
        s = jnp.sum(pc, axis=-1, keepdims=True) + jnp.sum(pn, axis=-1, keepdims=True)
        parts.append((lo, pc, pn, s, m + jnp.log(s)))
    m_all = functools.reduce(jnp.maximum, [p[4] for p in parts])
    ws = [jnp.exp(p[4] - m_all) for p in parts]
    den = sum(ws)
    p_new = None
    los = sorted({p[0] for p in parts} | {lb})
    segs = [None] * (len(los) - 1)
    for (lo, pc, pn, s, _), w in zip(parts, ws):
        coef = w / (den * s)
        p_new = coef * pn if p_new is None else p_new + coef * pn
        for si in range(len(segs)):
            a0, a1 = los[si], los[si + 1]
            if a0 >= lo:
                piece = coef * pc[:, a0 - lo:a1 - lo]
                segs[si] = piece if segs[si] is None else segs[si] + piece
    p_cache = jnp.concatenate(segs, axis=1) if len(segs) > 1 else segs[0]
    ob = (_nt_dot(p_cache.astype(BF16), _heads_by_dim(cbv_ref, j))
          + jnp.dot(p_new.astype(BF16), vb_new, preferred_element_type=F32))
    ob_rows = sum(jnp.where(head_of_lane == h, ob[h * n_new:(h + 1) * n_new, :], 0.0) for h in range(B_HEADS))

    cz = z[:, COL_CONV:COL_CONV + 3 * C_WIDTH]
    oc, u = _gated_conv(cz[:, :C_WIDTH], cz[:, C_WIDTH:2 * C_WIDTH], cz[:, 2 * C_WIDTH:], st_ref[j], cw_ref[...])
    conv_ref[j] = u[n_new - (CONV_WIDTH - 1):, :]
    mix_ref[tokens, :] = jnp.concatenate(oa_blocks + [ob_rows, oc], axis=1)


def _sample_mixer(z, row0, cak, cav, cbk, cbv, layer, state, cw, sink_rows, bias_a, bias_b, n_new):
    n_seq = cak.shape[1]
    g = SEQS_PER_STEP
    rows = g * n_new
    blk0 = row0 // rows
    cache = lambda a: pl.BlockSpec((None, g) + a.shape[2:], lambda b: (layer, b, 0, 0, 0))
    per_seq = lambda shape: pl.BlockSpec((g,) + shape, lambda b: (b, 0, 0))
    const = lambda a: pl.BlockSpec(a.shape, lambda b: (0,) * a.ndim)
    return pl.pallas_call(
        functools.partial(_sample_kernel, n_new=n_new),
        out_shape=(jax.ShapeDtypeStruct((n_seq * n_new, D_MODEL), F32),
                   jax.ShapeDtypeStruct((n_seq, CONV_WIDTH - 1, C_WIDTH), F32)),
        grid=(n_seq // g,),
        in_specs=[pl.BlockSpec((rows, IN_WIDTH), lambda b: (blk0 + b, 0)),
                  cache(cak), cache(cav), cache(cbk), cache(cbv),
                  per_seq((CONV_WIDTH - 1, C_WIDTH)), const(cw), const(sink_rows), const(bias_a), const(bias_b)],
        out_specs=(pl.BlockSpec((rows, D_MODEL), lambda b: (b, 0)), per_seq((CONV_WIDTH - 1, C_WIDTH))),
        compiler_params=_params("parallel"),
    )(z, cak, cav, cbk, cbv, state, cw, sink_rows, bias_a, bias_b)


def _first_index(vals, best):
    idx = jnp.full(best.shape, len(vals) - 1, jnp.int32)
    for j in range(len(vals) - 2, -1, -1):
        idx = jnp.where(vals[j] == best, j, idx)
    return idx


def _route(lt):
    g = [lt[k:k + 1, :] for k in range(N_GROUPS)]
    g_max = functools.reduce(jnp.maximum, g)
    g_idx = _first_index(g, g_max)
    g_w = 1.0 / sum(jnp.exp(v - g_max) for v in g)
    e = []
    for j in range(EXPERTS_PER_GROUP):
        v = lt[N_GROUPS + j:N_GROUPS + j + 1, :]
        for gi in range(1, N_GROUPS):
            row = N_GROUPS + gi * EXPERTS_PER_GROUP + j
            v = jnp.where(g_idx == gi, lt[row:row + 1, :], v)
        e.append(v)
    e1 = functools.reduce(jnp.maximum, e)
    i1 = _first_index(e, e1)
    rest = [jnp.where(i1 == j, -jnp.inf, e[j]) for j in range(EXPERTS_PER_GROUP)]
    e2 = functools.reduce(jnp.maximum, rest)
    i2 = _first_index(rest, e2)
    t = jnp.exp(e2 - e1)
    w1 = g_w / (1.0 + t)
    w2 = g_w * t / (1.0 + t)
    swap = i2 < i1
    lo = jnp.where(swap, i2, i1)
    hi = jnp.where(swap, i1, i2)
    pair = jnp.zeros_like(lo)
    for p, (slot0, slot1) in enumerate(PAIRS):
        pair = jnp.where((hi == slot0) & (lo == slot1), p, pair)
    bucket = g_idx * len(PAIRS) + pair
    return bucket, jnp.where(swap, w1, w2), jnp.where(swap, w2, w1)


def _outproj_kernel(yp_ref, ys_ref, oa_ref, ob_ref, zc_ref, zh_ref, ms_ref, wout_ref, cw_ref, gn_ref, wr_ref,
                    br_ref, y1_ref, info_ref, cnt_ref, ut_ref, y1_scr, carry_scr, *, prompt_tiles, tiles_per_seq):
    i = pl.program_id(0)
    tile = yp_ref.shape[0]

    @pl.when(i == 0)
    def _():
        carry_scr[...] = jnp.zeros_like(carry_scr)

    @pl.when(i < prompt_tiles)
    def _():
        zc = zc_ref[...]
        zh = zh_ref[...]
        halo = zh[:, 2 * C_WIDTH:] * zh[:, :C_WIDTH]
        halo = jnp.where(i % tiles_per_seq == 0, 0.0, halo)
        oc, u = _gated_conv(zc[:, :C_WIDTH], zc[:, C_WIDTH:2 * C_WIDTH], zc[:, 2 * C_WIDTH:], halo, cw_ref[...])
        ut_ref[...] = u[tile - 8:, :]
        y1_scr[...] = (
            yp_ref[...]
            + jnp.dot(oa_ref[...], wout_ref[0:A_Q_W, :], preferred_element_type=F32)
            + jnp.dot(ob_ref[...], wout_ref[A_Q_W:A_Q_W + B_W, :], preferred_element_type=F32)
            + jnp.dot(oc.astype(BF16), wout_ref[A_Q_W + B_W:, :], preferred_element_type=F32))

    @pl.when(i >= prompt_tiles)
    def _():
        ut_ref[...] = jnp.zeros_like(ut_ref)
        y1_scr[...] = ys_ref[...] + jnp.dot(ms_ref[...].astype(BF16), wout_ref[...], preferred_element_type=F32)

    y1 = y1_scr[...]
    y1_ref[:, :D_MODEL] = y1
    xn = _rms(y1, gn_ref[...])

    x_hi = xn.astype(BF16)
    x_lo = (xn - x_hi.astype(F32)).astype(BF16)
    wr = wr_ref[...]
    w_hi = wr.astype(BF16)
    w_lo = (wr - w_hi.astype(F32)).astype(BF16)
    lt_hi = _nt_dot(jnp.concatenate([w_hi, w_lo], axis=0), x_hi)
    lt = lt_hi[:ROUTE_ROWS] + lt_hi[ROUTE_ROWS:] + _nt_dot(w_hi, x_lo) + br_ref[...]
    bucket, w_slot0, w_slot1 = _route(lt)
    onehot = (lax.broadcasted_iota(jnp.int32, (ROUTE_ROWS, tile), 0) == bucket).astype(F32)
    upper = (lax.broadcasted_iota(jnp.int32, (tile, tile), 0)
             <= lax.broadcasted_iota(jnp.int32, (tile, tile), 1)).astype(BF16)
    running = jnp.dot(onehot.astype(BF16), upper, preferred_element_type=F32)
    carry = carry_scr[...]
    rank = jnp.sum(onehot * (running - 1.0 + carry), axis=0, keepdims=True)
    carry = carry + jnp.sum(onehot, axis=1, keepdims=True)
    carry_scr[...] = carry
    cnt_ref[...] = jnp.broadcast_to(carry, cnt_ref.shape)
    info_ref[...] = jnp.concatenate([bucket.astype(F32), rank, jnp.zeros((SUBLANES - 2, tile), F32)], axis=0)
    y1_ref[:, D_MODEL:] = jnp.concatenate([w_slot0, w_slot1, jnp.zeros((LANES - 2, tile), F32)], axis=0).T


def _outproj_route(yp, ys, sample_row0, oa, ob, z, mix_s, w_out, cw, gain, w_route, b_route, n_prompt, seq_len):
    n = n_prompt + mix_s.shape[0]
    tiles = n // ROW_TILE
    p_tiles = n_prompt // ROW_TILE
    halo_blocks = ROW_TILE // 8
    conv_w = 3 * C_WIDTH
    pidx = lambda i: jnp.minimum(i, p_tiles - 1)
    sidx = lambda i: jnp.maximum(i - p_tiles, 0)
    const = lambda a: pl.BlockSpec(a.shape, lambda i: (0,) * a.ndim)
    gain = gain.reshape(1, D_MODEL)
    return pl.pallas_call(
        functools.partial(_outproj_kernel, prompt_tiles=p_tiles, tiles_per_seq=seq_len // ROW_TILE),
        out_shape=(jax.ShapeDtypeStruct((n, ROW_EXT), F32),
                   jax.ShapeDtypeStruct((8, n), F32),
                   jax.ShapeDtypeStruct((ROUTE_ROWS, LANES), F32),
                   jax.ShapeDtypeStruct((tiles * 8, C_WIDTH), F32)),
        grid=(tiles,),
        in_specs=[pl.BlockSpec((ROW_TILE, D_MODEL), lambda i: (pidx(i), 0)),
                  pl.BlockSpec((ROW_TILE, D_MODEL), lambda i: (sample_row0 // ROW_TILE + sidx(i), 0)),
                  pl.BlockSpec((ROW_TILE, A_Q_W), lambda i: (pidx(i), 0)),
                  pl.BlockSpec((ROW_TILE, B_W), lambda i: (pidx(i), 0)),
                  pl.BlockSpec((ROW_TILE, conv_w), lambda i: (pidx(i), 0)),
                  pl.BlockSpec((8, conv_w), lambda i: (jnp.maximum(pidx(i) * halo_blocks - 1, 0), 0)),
                  pl.BlockSpec((ROW_TILE, D_MODEL), lambda i: (sidx(i), 0)),
                  const(w_out), const(cw), const(gain), const(w_route), const(b_route)],
        out_specs=(pl.BlockSpec((ROW_TILE, ROW_EXT), lambda i: (i, 0)),
                   pl.BlockSpec((8, ROW_TILE), lambda i: (0, i)),
                   pl.BlockSpec((ROUTE_ROWS, LANES), lambda i: (0, 0)),
                   pl.BlockSpec((8, C_WIDTH), lambda i: (i, 0))),
        scratch_shapes=[pltpu.VMEM((ROW_TILE, D_MODEL), F32), pltpu.VMEM((ROUTE_ROWS, 1), F32)],
        compiler_params=_params("arbitrary"),
    )(yp, ys, oa, ob, z, z, mix_s, w_out, cw, gain, w_route, b_route)


def _moe_kernel(e_lo_ref, e_hi_ref, used_ref, x_ref, g_ref, wg1, wu1, wd1, wg2, wu2, wd2, *rest):
    o_ref = rest[-1]
    t = pl.program_id(0)

    @pl.when(used_ref[t] > 0)
    def _():
        y1 = x_ref[:, :D_MODEL]
        w = x_ref[:, D_MODEL:]
        x = _rms(y1, g_ref[...]).astype(BF16)

        def expert(wg, wu, wd, scale):
            g = jnp.dot(x, wg[...], preferred_element_type=F32)
            u = jnp.dot(x, wu[...], preferred_element_type=F32)
            h = g * jax.nn.sigmoid(g) * u * scale
            return jnp.dot(h.astype(BF16), wd[...], preferred_element_type=F32)

        y2 = y1 + expert(wg1, wu1, wd1, w[:, 0:1]) + expert(wg2, wu2, wd2, w[:, 1:2])
        o_ref[...] = _rms(y2, rest[0][...]) if len(rest) > 1 else y2

    @pl.when(used_ref[t] == 0)
    def _():
        o_ref[...] = jnp.zeros_like(o_ref)


def _experts(xs, gain, e_lo, e_hi, used, w_gate, w_up, w_down, layer, final_gain=None):
    n_tiles = xs.shape[0] // MOE_TILE
    up_spec = lambda sel: pl.BlockSpec((None, None, D_MODEL, EXPERT_FF),
                                       lambda t, lo, hi, u: (layer, (lo, hi)[sel][t], 0, 0))
    down_spec = lambda sel: pl.BlockSpec((None, None, EXPERT_FF, D_MODEL),
                                         lambda t, lo, hi, u: (layer, (lo, hi)[sel][t], 0, 0))
    gain_spec = pl.BlockSpec((1, D_MODEL), lambda t, lo, hi, u: (0, 0))
    extra = [] if final_gain is None else [final_gain.reshape(1, D_MODEL)]
    grid_spec = pltpu.PrefetchScalarGridSpec(
        num_scalar_prefetch=3,
        grid=(n_tiles,),
        in_specs=[pl.BlockSpec((MOE_TILE, ROW_EXT), lambda t, lo, hi, u: (t, 0)), gain_spec,
                  up_spec(0), up_spec(0), down_spec(0), up_spec(1), up_spec(1), down_spec(1)]
                 + [gain_spec] * len(extra),
        out_specs=pl.BlockSpec((MOE_TILE, D_MODEL), lambda t, lo, hi, u: (t, 0)))
    return pl.pallas_call(
        _moe_kernel,
        out_shape=jax.ShapeDtypeStruct((xs.shape[0], D_MODEL), F32),
        grid_spec=grid_spec,
        compiler_params=_params("arbitrary"),
    )(e_lo, e_hi, used, xs, gain.reshape(1, D_MODEL), w_gate, w_up, w_down, w_gate, w_up, w_down, *extra)


def _dispatch_plan(info, counts, n):
    n_tiles = -(-n // MOE_TILE) + N_ROUTE_BUCKETS
    bucket = info[0].astype(jnp.int32)
    rank = info[1].astype(jnp.int32)
    counts = counts[:N_ROUTE_BUCKETS, 0].astype(jnp.int32)
    tiles_per_bucket = (counts + MOE_TILE - 1) // MOE_TILE
    tile_end = jnp.cumsum(tiles_per_bucket)
    row_start = (tile_end - tiles_per_bucket) * MOE_TILE
    dest = row_start[bucket] + rank
    src = (jnp.arange(n_tiles * MOE_TILE, dtype=jnp.int32) % n).at[dest].set(jnp.arange(n, dtype=jnp.int32))
    tile_ids = jnp.arange(n_tiles, dtype=jnp.int32)
    tile_bucket = jnp.minimum(jnp.searchsorted(tile_end, tile_ids, side="right").astype(jnp.int32),
                              N_ROUTE_BUCKETS - 1)
    used = (tile_ids < tile_end[-1]).astype(jnp.int32)
    pair = tile_bucket % len(PAIRS)
    base = (tile_bucket // len(PAIRS)) * EXPERTS_PER_GROUP
    pairs = jnp.asarray(PAIRS, jnp.int32)
    return dest, src, base + pairs[pair, 0], base + pairs[pair, 1], used


def _permute_in_columns(w):
    attn = A_Q_W + 2 * A_KV_W + 3 * B_W
    return jnp.concatenate([w[:, attn:], w[:, A_Q_W:A_Q_W + A_KV_W], w[:, :A_Q_W], w[:, A_Q_W + A_KV_W:attn]],
                           axis=1)


def kernel(x_prompt, x_sample, cache_a_k, cache_a_v, cache_b_k, cache_b_v, state_conv, rel_bias_table,
           w_in, w_out, conv_w, attn_sinks, norm_mix, norm_ffn, w_group, b_group, w_router, b_router,
           w_gate, w_up, w_down, norm_final):
    n_seq, seq_len, _ = x_prompt.shape
    dec_seq, n_new, _ = x_sample.shape
    depth = w_in.shape[0]
    n_prompt = n_seq * seq_len
    n_sample = dec_seq * n_new
    n = n_prompt + n_sample
    la, lb = cache_a_k.shape[2], cache_b_k.shape[2]
    assert seq_len % SPAN == 0 and n_prompt % IN_TILE == 0 and n_sample % IN_TILE == 0 and IN_TILE % ROW_TILE == 0
    assert seq_len >= SPAN and la % LANES == 0 and lb % LANES == 0 and n_new == 8

    table_a, table_b = rel_bias_table[:, :A_Q_HEADS], rel_bias_table[:, A_Q_HEADS:]
    bias_a = _band_bias(table_a, 1)
    bias_b = jnp.stack([_band_bias(table_b, r) for r in DILATIONS], axis=1)
    sbias_a = _sample_bias(table_a, n_new, la, np.arange(la + LANES), A_WINDOW, 1)
    sbias_a = sbias_a.reshape(A_Q_HEADS * n_new, la + LANES)
    sbias_b = jnp.stack([_sample_bias(table_b, n_new, lb, np.arange(lb + LANES), w, r)
                         .reshape(B_HEADS * n_new, lb + LANES) for w, r in B_BRANCHES])
    cak, cav, cbk, cbv = (c.transpose(0, 1, 3, 4, 2) for c in (cache_a_k, cache_a_v, cache_b_k, cache_b_v))

    w_in_b = jnp.stack([_permute_in_columns(w_in[l]) for l in range(depth)]).astype(BF16)
    w_out_b = w_out.astype(BF16)
    expert_w = tuple(w.astype(BF16) for w in (w_gate, w_up, w_down))
    pad = ROUTE_ROWS - N_GROUPS - N_EXPERTS
    w_route = jnp.pad(jnp.concatenate([w_group, w_router], axis=2).transpose(0, 2, 1), ((0, 0), (0, pad), (0, 0)))
    b_route = jnp.pad(jnp.concatenate([b_group, b_router], axis=1), ((0, 0), (0, pad)))[..., None]
    sink_rows = jnp.repeat(attn_sinks, n_new, axis=1)[..., None]

    yp, ys, s_row0 = x_prompt.reshape(n_prompt, D_MODEL), x_sample.reshape(n_sample, D_MODEL), 0
    states = []
    for l in range(depth):
        lap, lbp = min(A_WINDOW, seq_len), min(SPAN, seq_len)
        z, kbt, vbt = _inproj(yp, ys, s_row0, n_prompt, n_sample, norm_mix[l], w_in_b[l], seq_len, lbp)
        oa = _window_attention(z, bias_a, attn_sinks[l], n_seq, seq_len)
        ob = _dilated_attention(z, bias_b, n_seq, seq_len)
        mix_s, conv_s = _sample_mixer(z, n_prompt, cak, cav, cbk, cbv, l, state_conv[l], conv_w[l], sink_rows[l],
                                      sbias_a, sbias_b, n_new)
        y1, info, counts, u_tail = _outproj_route(
            yp, ys, s_row0, oa, ob, z, mix_s, w_out_b[l], conv_w[l], norm_ffn[l], w_route[l], b_route[l], n_prompt, seq_len)
        def prompt_tail(length, col, width, heads):
            rows = [z[(b + 1) * seq_len - length:(b + 1) * seq_len, col:col + width] for b in range(n_seq)]
            return jnp.stack(rows).reshape(n_seq, length, heads, HEAD_DIM)

        def sample_rows(col, width, heads):
            return z[n_prompt:, col:col + width].reshape(dec_seq, n_new, heads, HEAD_DIM)

        last_tile = [((b + 1) * seq_len // ROW_TILE - 1) * SUBLANES for b in range(n_seq)]
        conv_p = jnp.stack([u_tail[t + SUBLANES - (CONV_WIDTH - 1):t + SUBLANES, :] for t in last_tile])
        layer_states = (
            prompt_tail(lap, COL_KA, A_KV_W, A_KV_HEADS), prompt_tail(lap, COL_VA, A_KV_W, A_KV_HEADS),
            kbt.reshape(n_seq, B_HEADS, HEAD_DIM, lbp).transpose(0, 3, 1, 2),
            vbt.reshape(n_seq, B_HEADS, HEAD_DIM, lbp).transpose(0, 3, 1, 2), conv_p,
            sample_rows(COL_KA, A_KV_W, A_KV_HEADS), sample_rows(COL_VA, A_KV_W, A_KV_HEADS),
            sample_rows(COL_KB, B_W, B_HEADS), sample_rows(COL_VB, B_W, B_HEADS), conv_s)

        dest, src, e_lo, e_hi, used = _dispatch_plan(info, counts, n)
        xs = y1[src]
        if l == depth - 1:
            xs, layer_states = lax.optimization_barrier((xs, layer_states))
        y_sorted = _experts(xs, norm_ffn[l], e_lo, e_hi, used, *expert_w, l,
                            final_gain=norm_final if l == depth - 1 else None)
        if l < depth - 1:
            y_next, layer_states = lax.optimization_barrier((y_sorted[dest], layer_states))
            yp, ys, s_row0 = y_next, y_next, n_prompt
        states.append(layer_states)

    y_prompt = y_sorted[dest[:n_prompt]].reshape(n_seq, seq_len, D_MODEL)
    y_sample = y_sorted[dest[n_prompt:]].reshape(dec_seq, n_new, D_MODEL)
    st = [jnp.stack([s[k] for s in states]) for k in range(10)]
    return (y_prompt, y_sample, st[0], st[1], st[2], st[3], st[4], st[5], st[6], st[7], st[8], st[9])
```

```python
import functools
import math

import numpy as np
import jax
import jax.numpy as jnp
from jax import lax
from jax.experimental import pallas as pl
from jax.experimental.pallas import tpu as pltpu

F32 = jnp.float32
BF16 = jnp.bfloat16

D_MODEL = 1024
HEAD_DIM = 64
ATTN_SCALE = HEAD_DIM ** -0.5
LOG2E = math.log2(math.e)
LN2 = math.log(2.0)
BAND_Q_SCALE = ATTN_SCALE * LOG2E
BLOCK = 128
LANES = 128
SUBLANES = 8
A_Q_HEADS = 4
A_KV_HEADS = 2
A_WINDOW = 128
B_HEADS = 6
B_BRANCHES = ((128, 1), (512, 4), (2048, 16))
DILATIONS = tuple(r for _, r in B_BRANCHES)
SPAN = BLOCK * max(DILATIONS)
C_WIDTH = 6 * HEAD_DIM
CONV_WIDTH = 3
A_Q_W = A_Q_HEADS * HEAD_DIM
A_KV_W = A_KV_HEADS * HEAD_DIM
B_W = B_HEADS * HEAD_DIM
IN_WIDTH = A_Q_W + 2 * A_KV_W + 3 * B_W + 3 * C_WIDTH
N_BUCKETS = 32
MAX_DISTANCE = 2048
N_GROUPS = 4
EXPERTS_PER_GROUP = 4
N_EXPERTS = N_GROUPS * EXPERTS_PER_GROUP
EXPERT_FF = 512
RMS_EPS = 1e-6
NEG_INF = -1e30

COL_CONV = 0
COL_KA = 3 * C_WIDTH
COL_QA = COL_KA + A_KV_W
COL_VA = COL_QA + A_Q_W
COL_QB = COL_VA + A_KV_W
COL_KB = COL_QB + B_W
COL_VB = COL_KB + B_W

ROW_TILE = 512
IN_TILE = 1024
MOE_TILE = 256
TILES_PER_STEP = 2
SEQS_PER_STEP = 2
PAIRS = ((1, 0), (2, 0), (2, 1), (3, 1), (3, 0), (3, 2))
N_ROUTE_BUCKETS = N_GROUPS * len(PAIRS)
ROUTE_ROWS = 32
ROW_EXT = D_MODEL + LANES
VMEM_LIMIT = 56 * 1024 * 1024


def _params(*sem):
    return pltpu.CompilerParams(dimension_semantics=sem, vmem_limit_bytes=VMEM_LIMIT)


def _rel_bucket(dist):
    d = np.maximum(dist, 0)
    max_exact = N_BUCKETS // 2
    df = np.maximum(d, max_exact).astype(np.float32)
    large = max_exact + (np.log(df / max_exact) / math.log(MAX_DISTANCE / max_exact)
                         * (N_BUCKETS - max_exact)).astype(np.int32)
    large = np.minimum(large, N_BUCKETS - 1)
    return np.where(d < max_exact, d, large)


def _masked_bias(table, dist, valid):
    bucket = _rel_bucket(dist).reshape(-1, 1)
    onehot = (jnp.asarray(bucket, jnp.int32) == jnp.arange(N_BUCKETS, dtype=jnp.int32)[None, :]).astype(F32)
    b = jnp.dot(onehot, table.astype(F32), precision=lax.Precision.HIGHEST)
    b = jnp.where(jnp.asarray(valid.reshape(-1, 1)), b, NEG_INF)
    return jnp.moveaxis(b.reshape(dist.shape + (table.shape[1],)), -1, 0)


def _band_bias(table, scale):
    i = np.arange(BLOCK)[:, None]
    j = np.arange(2 * BLOCK)[None, :]
    dist = i + BLOCK - j
    b = _masked_bias(table, dist * scale, (dist >= 0) & (dist <= BLOCK)) * LOG2E
    return b.reshape(table.shape[1] // 2, 2 * BLOCK, 2 * BLOCK)


def _sample_bias(table, n_new, first_new, positions, window, dilation):
    dist = first_new + np.arange(n_new)[:, None] - np.asarray(positions)[None, :]
    valid = (dist >= 0) & (dist <= window) & (dist % dilation == 0)
    return _masked_bias(table, dist, valid)


def _rms(x, gain):
    return x * lax.rsqrt(jnp.mean(x * x, axis=-1, keepdims=True) + RMS_EPS) * gain


def _inproj_kernel(xp_ref, xs_ref, g_ref, w_ref, z_ref, kbt_ref, vbt_ref, *, prompt_tiles, tiles_per_seq, tail_tiles):
    i = pl.program_id(0)

    def project(x_ref):
        h = _rms(x_ref[...], g_ref[...])
        z_ref[...] = jnp.dot(h.astype(BF16), w_ref[...], preferred_element_type=F32)

    pl.when(i < prompt_tiles)(lambda: project(xp_ref))
    pl.when(i >= prompt_tiles)(lambda: project(xs_ref))

    @pl.when((i < prompt_tiles) & (i % tiles_per_seq >= tiles_per_seq - tail_tiles))
    def _():
        kbt_ref[...] = z_ref[:, COL_KB:COL_KB + B_W].T
        vbt_ref[...] = z_ref[:, COL_VB:COL_VB + B_W].T


def _inproj(xp, xs, sample_row0, n_prompt, n_sample, gain, w, seq_len, tail_len):
    p_tiles, s_tile0 = n_prompt // IN_TILE, sample_row0 // IN_TILE
    tps, tail = seq_len // IN_TILE, tail_len // IN_TILE
    n_seq = n_prompt // seq_len

    def tail_block(i):
        j = jnp.minimum(i, p_tiles - 1)
        return j // tps, 0, jnp.maximum(j % tps - (tps - tail), 0)

    tail_spec = pl.BlockSpec((None, B_W, IN_TILE), tail_block)
    tail_shape = jax.ShapeDtypeStruct((n_seq, B_W, tail_len), F32)
    return pl.pallas_call(
        functools.partial(_inproj_kernel, prompt_tiles=p_tiles, tiles_per_seq=tps, tail_tiles=tail),
        out_shape=(jax.ShapeDtypeStruct((n_prompt + n_sample, IN_WIDTH), F32), tail_shape, tail_shape),
        grid=((n_prompt + n_sample) // IN_TILE,),
        in_specs=[pl.BlockSpec((IN_TILE, D_MODEL), lambda i: (jnp.minimum(i, p_tiles - 1), 0)),
                  pl.BlockSpec((IN_TILE, D_MODEL), lambda i: (s_tile0 + jnp.maximum(i - p_tiles, 0), 0)),
                  pl.BlockSpec((1, D_MODEL), lambda i: (0, 0)),
                  pl.BlockSpec((D_MODEL, IN_WIDTH), lambda i: (0, 0))],
        out_specs=(pl.BlockSpec((IN_TILE, IN_WIDTH), lambda i: (i, 0)), tail_spec, tail_spec),
        compiler_params=_params("arbitrary"),
    )(xp, xs, gain.reshape(1, D_MODEL), w)


def _lane_half(shape):
    return lax.broadcasted_iota(jnp.int32, shape, len(shape) - 1) // HEAD_DIM


def _nt_dot(a, b):
    return lax.dot_general(a, b, (((1,), (1,)), ((), ())), preferred_element_type=F32)

def _band_pair(q2, k2, v2, bias2, penalty):
    lg = _nt_dot(q2, k2) + bias2
    if penalty is not None:
        lg = lg + penalty
    m = jnp.max(lg, axis=-1, keepdims=True)
    p = jnp.exp2(lg - m)
    s = jnp.sum(p, axis=-1, keepdims=True)
    acc = jnp.dot(p.astype(BF16), v2, preferred_element_type=F32)
    return acc / s, (m + jnp.log2(s)) * LN2


def _prev_key_penalty(first_span):
    col = lax.broadcasted_iota(jnp.int32, (1, 2 * BLOCK), 1)
    return jnp.where((col < BLOCK) & first_span, NEG_INF, 0.0).astype(F32)


def _rows(start, dilation):
    if dilation == 1:
        return pl.ds(start, BLOCK)
    return pl.ds(start, BLOCK, stride=dilation)


def _dilated_kernel(q_ref, kc_ref, kp_ref, vc_ref, vp_ref, bias_ref, o_ref, o_scr, l_scr, first_bias_scr):
    first_half = _lane_half((BLOCK, LANES)) == 0
    penalty = _prev_key_penalty(pl.program_id(1) == 0)
    for bi in range(len(DILATIONS)):
        first_bias_scr[bi] = bias_ref[bi] + penalty

    for bi, r in enumerate(DILATIONS):
        step = r * BLOCK

        def block(c, n, first, bi=bi, r=r, step=step):
            q_start = c + n * step
            if not isinstance(q_start, int):
                q_start = pl.multiple_of(q_start, BLOCK) if r == 1 else q_start
            cur = _rows(q_start, r)
            qt = (q_ref[cur, :] * BAND_Q_SCALE).astype(BF16)
            if first:
                src_k, src_v, prev = kp_ref, vp_ref, _rows(SPAN - step + c, r)
            else:
                p_start = q_start - step
                if r == 1:
                    p_start = pl.multiple_of(p_start, BLOCK)
                src_k, src_v, prev = kc_ref, vc_ref, _rows(p_start, r)
            k2 = jnp.concatenate([src_k[prev, :], kc_ref[cur, :]], axis=0).astype(BF16)
            v2 = jnp.concatenate([src_v[prev, :], vc_ref[cur, :]], axis=0).astype(BF16)
            zero = jnp.zeros_like(qt)
            q2 = jnp.concatenate([jnp.where(first_half, qt, zero), jnp.where(first_half, zero, qt)], axis=0)
            o, lse = _band_pair(q2, k2, v2, first_bias_scr[bi] if first else bias_ref[bi], None)
            o_scr[bi, cur, :] = jnp.where(first_half, o[:BLOCK], o[BLOCK:])
            l_scr[bi, cur, :] = jnp.where(first_half, lse[:BLOCK], lse[BLOCK:])

        n_blocks = SPAN // step
        if r == 1:
            block(0, 0, True)
            lax.fori_loop(1, n_blocks, lambda n, _: block(0, n, False), None, unroll=True)
        else:
            def per_class(c, _, block=block, n_blocks=n_blocks):
                block(c, 0, True)
                for n in range(1, n_blocks):
                    block(c, n, False)
            lax.fori_loop(0, r, per_class, None, unroll=True)

    chunk = 2 * BLOCK

    def combine(j, _):
        rows = pl.ds(pl.multiple_of(j * chunk, chunk), chunk)
        ls = [l_scr[bi, rows, :] for bi in range(len(DILATIONS))]
        m = functools.reduce(jnp.maximum, ls)
        ws = [jnp.exp(l - m) for l in ls]
        num = sum(w * o_scr[bi, rows, :] for bi, w in enumerate(ws))
        o_ref[rows, :] = (num / sum(ws)).astype(o_ref.dtype)

    lax.fori_loop(0, SPAN // chunk, combine, None)


def _dilated_attention(z, bias, n_seq, seq_len):
    spans = seq_len // SPAN
    qb, kb, vb = COL_QB // LANES, COL_KB // LANES, COL_VB // LANES
    blk = (SPAN, LANES)
    cur = lambda col: (lambda b, s, i: (b * spans + s, col + i))
    prev = lambda col: (lambda b, s, i: (b * spans + jnp.maximum(s - 1, 0), col + i))
    return pl.pallas_call(
        _dilated_kernel,
        out_shape=jax.ShapeDtypeStruct((n_seq * seq_len, B_W), BF16),
        grid=(n_seq, spans, B_W // LANES),
        in_specs=[pl.BlockSpec(blk, cur(qb)),
                  pl.BlockSpec(blk, cur(kb)), pl.BlockSpec(blk, prev(kb)),
                  pl.BlockSpec(blk, cur(vb)), pl.BlockSpec(blk, prev(vb)),
                  pl.BlockSpec((None, len(DILATIONS), 2 * BLOCK, 2 * BLOCK), lambda b, s, i: (i, 0, 0, 0))],
        out_specs=pl.BlockSpec(blk, lambda b, s, i: (b * spans + s, i)),
        scratch_shapes=[pltpu.VMEM((len(DILATIONS), SPAN, LANES), F32),
                        pltpu.VMEM((len(DILATIONS), SPAN, LANES), F32),
                        pltpu.VMEM((len(DILATIONS), 2 * BLOCK, 2 * BLOCK), F32)],
        compiler_params=_params("parallel", "parallel", "parallel"),
    )(z, z, z, z, z, bias)


def _window_kernel(sink_ref, q_ref, kc_ref, kp_ref, vc_ref, vp_ref, bias_ref, o_ref):
    penalty = _prev_key_penalty(pl.program_id(1) == 0)
    first_half = _lane_half((BLOCK, LANES)) == 0
    first_half2 = _lane_half((2 * BLOCK, LANES)) == 0

    def block(n, first):
        start = n * BLOCK if isinstance(n, int) else pl.multiple_of(n * BLOCK, BLOCK)
        cur = pl.ds(start, BLOCK)
        if first:
            src_k, src_v, prev = kp_ref, vp_ref, pl.ds(SPAN - BLOCK, BLOCK)
        else:
            src_k, src_v, prev = kc_ref, vc_ref, pl.ds(pl.multiple_of(start - BLOCK, BLOCK), BLOCK)
        k2 = jnp.concatenate([src_k[prev, :], kc_ref[cur, :]], axis=0).astype(BF16)
        vf = jnp.concatenate([src_v[prev, :], vc_ref[cur, :]], axis=0)
        v_swap = pltpu.roll(vf, HEAD_DIM, 1)
        for i in range(A_KV_HEADS):
            qf = q_ref[cur, i * LANES:(i + 1) * LANES] * BAND_Q_SCALE
            q_swap = pltpu.roll(qf, HEAD_DIM, 1)
            q2 = jnp.concatenate([jnp.where(first_half == (i == 0), qf if a == i else q_swap, 0.0)
                                  for a in range(2)], axis=0).astype(BF16)
            v2 = jnp.where(first_half2 == (i == 0), vf, v_swap).astype(BF16)
            o, lse = _band_pair(q2, k2, v2, bias_ref[i], penalty if first else None)
            outs = [o[a * BLOCK:(a + 1) * BLOCK] * jax.nn.sigmoid(lse[a * BLOCK:(a + 1) * BLOCK] - sink_ref[2 * i + a])
                    for a in range(2)]
            o_ref[cur, i * LANES:(i + 1) * LANES] = jnp.where(first_half, outs[0], outs[1]).astype(o_ref.dtype)

    block(0, True)
    lax.fori_loop(1, SPAN // BLOCK, lambda n, _: block(n, False), None, unroll=5)


def _window_attention(z, bias, sinks, n_seq, seq_len):
    spans = seq_len // SPAN
    ka, va = COL_KA // LANES, COL_VA // LANES
    blk = (SPAN, LANES)
    cur = lambda col: (lambda b, s: (b * spans + s, col))
    prev = lambda col: (lambda b, s: (b * spans + jnp.maximum(s - 1, 0), col))
    return pl.pallas_call(
        _window_kernel,
        out_shape=jax.ShapeDtypeStruct((n_seq * seq_len, A_Q_W), BF16),
        grid=(n_seq, spans),
        in_specs=[pl.BlockSpec(memory_space=pltpu.SMEM),
                  pl.BlockSpec((SPAN, A_Q_W), lambda b, s: (b * spans + s, COL_QA // A_Q_W)),
                  pl.BlockSpec(blk, cur(ka)), pl.BlockSpec(blk, prev(ka)),
                  pl.BlockSpec(blk, cur(va)), pl.BlockSpec(blk, prev(va)),
                  pl.BlockSpec((A_KV_HEADS, 2 * BLOCK, 2 * BLOCK), lambda b, s: (0, 0, 0))],
        out_specs=pl.BlockSpec((SPAN, A_Q_W), lambda b, s: (b * spans + s, 0)),
        compiler_params=_params("parallel", "parallel"),
    )(sinks, z, z, z, z, z, bias)


def _shift_rows(u, filler, k):
    rolled = pltpu.roll(u, k, 0)
    row = lax.broadcasted_iota(jnp.int32, u.shape, 0)
    n_fill = filler.shape[0]
    for j in range(k):
        rolled = jnp.where(row == j, filler[n_fill - k + j:n_fill - k + j + 1, :], rolled)
    return rolled


def _gated_conv(xc, bg, cg, filler, cw):
    u = cg * xc
    conv = cw[0:1, :] * _shift_rows(u, filler, 2) + cw[1:2, :] * _shift_rows(u, filler, 1) + cw[2:3, :] * u
    return bg * conv, u


def _pad_rows(x, rows):
    return jnp.concatenate([x, jnp.zeros((rows - x.shape[0], x.shape[1]), x.dtype)], axis=0)


def _heads_by_dim(ref, j):
    _, h, d, length = ref.shape
    return ref[j].reshape(h * d, length).astype(BF16)


def _sample_kernel(z_ref, cak_ref, cav_ref, cbk_ref, cbv_ref, st_ref, cw_ref, sink_ref, bias_a_ref,
                   bias_b_ref, mix_ref, conv_ref, *, n_new):
    for j in range(z_ref.shape[0] // n_new):
        _sample_sequence(j, z_ref, cak_ref, cav_ref, cbk_ref, cbv_ref, st_ref, cw_ref, sink_ref, bias_a_ref,
                         bias_b_ref, mix_ref, conv_ref, n_new)


def _sample_sequence(j, z_ref, cak_ref, cav_ref, cbk_ref, cbv_ref, st_ref, cw_ref, sink_ref, bias_a_ref,
                     bias_b_ref, mix_ref, conv_ref, n_new):
    tokens = slice(j * n_new, (j + 1) * n_new)
    z = z_ref[tokens, :]
    la = cak_ref.shape[3]
    lb = cbk_ref.shape[3]

    ka_new = _pad_rows(z[:, COL_KA:COL_KA + A_KV_W], LANES).astype(BF16)
    va_new = _pad_rows(z[:, COL_VA:COL_VA + A_KV_W], LANES).astype(BF16)
    half = _lane_half((n_new, LANES))
    pieces = []
    for i in range(A_KV_HEADS):
        qf = z[:, COL_QA + i * LANES:COL_QA + (i + 1) * LANES] * ATTN_SCALE
        for a in range(2):
            pieces.append(jnp.where(half == i, qf if a == i else pltpu.roll(qf, HEAD_DIM, 1), 0.0))
    qa = jnp.concatenate(pieces, axis=0).astype(BF16)
    lc = jnp.dot(qa, _heads_by_dim(cak_ref, j), preferred_element_type=F32) + bias_a_ref[:, :la]
    ln = _nt_dot(qa, ka_new) + bias_a_ref[:, la:]
    m = jnp.maximum(jnp.max(lc, axis=-1, keepdims=True), jnp.max(ln, axis=-1, keepdims=True))
    pc = jnp.exp(lc - m)
    pn = jnp.exp(ln - m)
    s = jnp.sum(pc, axis=-1, keepdims=True) + jnp.sum(pn, axis=-1, keepdims=True)
    oa = (_nt_dot(pc.astype(BF16), _heads_by_dim(cav_ref, j))
          + jnp.dot(pn.astype(BF16), va_new, preferred_element_type=F32))
    oa = oa / s * jax.nn.sigmoid(m + jnp.log(s) - sink_ref[...])
    oa_blocks = []
    for i in range(A_KV_HEADS):
        per_half = []
        for a in range(2):
            rows = oa[(2 * i + a) * n_new:(2 * i + a + 1) * n_new, :]
            per_half.append(rows if a == i else pltpu.roll(rows, HEAD_DIM, 1))
        oa_blocks.append(jnp.where(half == 0, per_half[0], per_half[1]))

    qf = z[:, COL_QB:COL_QB + B_W] * ATTN_SCALE
    head_of_lane = _lane_half((n_new, B_W))
    qb = jnp.concatenate([jnp.where(head_of_lane == h, qf, 0.0) for h in range(B_HEADS)], axis=0).astype(BF16)
    kb_new = _pad_rows(z[:, COL_KB:COL_KB + B_W], LANES).astype(BF16)
    vb_new = _pad_rows(z[:, COL_VB:COL_VB + B_W], LANES).astype(BF16)
    lg_c = jnp.dot(qb, _heads_by_dim(cbk_ref, j), preferred_element_type=F32)
    lg_n = _nt_dot(qb, kb_new)
    parts = []
    for bi, (w, r) in enumerate(B_BRANCHES):
        lo = lb - min(lb, -(-w // LANES) * LANES)
        lc = lg_c[:, lo:] + bias_b_ref[bi, :, lo:lb]
        ln = lg_n + bias_b_ref[bi, :, lb:]
        m = jnp.maximum(jnp.max(lc, axis=-1, keepdims=True), jnp.max(ln, axis=-1, keepdims=True))
        pc = jnp.exp(lc - m)
        pn = jnp.exp(ln - m)
        s = jnp.sum(pc, axis=-1, keepdims=True) + jnp.sum(pn, axis=-1, keepdims=True)
        parts.append((lo, pc, pn, s, m + jnp.log(s)))
    m_all = functools.reduce(jnp.maximum, [p[4] for p in parts])
    ws = [jnp.exp(p[4] - m_all) for p in parts]
    den = sum(ws)
    p_new = None
    los = sorted({p[0] for p in parts} | {lb})
    segs = [None] * (len(los) - 1)
    for (lo, pc, pn, s, _), w in zip(parts, ws):
        coef = w / (den * s)
        p_new = coef * pn if p_new is None else p_new + coef * pn
        for si in range(len(segs)):
            a0, a1 = los[si], los[si + 1]
            if a0 >= lo:
                piece = coef * pc[:, a0 - lo:a1 - lo]
                segs[si] = piece if segs[si] is None else segs[si] + piece
    p_cache = jnp.concatenate(segs, axis=1) if len(segs) > 1 else segs[0]
    ob = (_nt_dot(p_cache.astype(BF16), _heads_by_dim(cbv_ref, j))
          + jnp.dot(p_new.astype(BF16), vb_new, preferred_element_type=F32))
    ob_rows = sum(jnp.where(head_of_lane == h, ob[h * n_new:(h + 1) * n_new, :], 0.0) for h in range(B_HEADS))

    cz = z[:, COL_CONV:COL_CONV + 3 * C_WIDTH]
    oc, u = _gated_conv(cz[:, :C_WIDTH], cz[:, C_WIDTH:2 * C_WIDTH], cz[:, 2 * C_WIDTH:], st_ref[j], cw_ref[...])
    conv_ref[j] = u[n_new - (CONV_WIDTH - 1):, :]
    mix_ref[tokens, :] = jnp.concatenate(oa_blocks + [ob_rows, oc], axis=1)


def _sample_mixer(z, row0, cak, cav, cbk, cbv, layer, state, cw, sink_rows, bias_a, bias_b, n_new):
    n_seq = cak.shape[1]
    g = SEQS_PER_STEP
    rows = g * n_new
    blk0 = row0 // rows
    cache = lambda a: pl.BlockSpec((None, g) + a.shape[2:], lambda b: (layer, b, 0, 0, 0))
    per_seq = lambda shape: pl.BlockSpec((g,) + shape, lambda b: (b, 0, 0))
    const = lambda a: pl.BlockSpec(a.shape, lambda b: (0,) * a.ndim)
    return pl.pallas_call(
        functools.partial(_sample_kernel, n_new=n_new),
        out_shape=(jax.ShapeDtypeStruct((n_seq * n_new, D_MODEL), F32),
                   jax.ShapeDtypeStruct((n_seq, CONV_WIDTH - 1, C_WIDTH), F32)),
        grid=(n_seq // g,),
        in_specs=[pl.BlockSpec((rows, IN_WIDTH), lambda b: (blk0 + b, 0)),
                  cache(cak), cache(cav), cache(cbk), cache(cbv),
                  per_seq((CONV_WIDTH - 1, C_WIDTH)), const(cw), const(sink_rows), const(bias_a), const(bias_b)],
        out_specs=(pl.BlockSpec((rows, D_MODEL), lambda b: (b, 0)), per_seq((CONV_WIDTH - 1, C_WIDTH))),
        compiler_params=_params("parallel"),
    )(z, cak, cav, cbk, cbv, state, cw, sink_rows, bias_a, bias_b)


def _first_index(vals, best):
    idx = jnp.full(best.shape, len(vals) - 1, jnp.int32)
    for j in range(len(vals) - 2, -1, -1):
        idx = jnp.where(vals[j] == best, j, idx)
    return idx


def _route(lt):
    g = [lt[k:k + 1, :] for k in range(N_GROUPS)]
    g_max = functools.reduce(jnp.maximum, g)
    g_idx = _first_index(g, g_max)
    g_w = 1.0 / sum(jnp.exp(v - g_max) for v in g)
    e = []
    for j in range(EXPERTS_PER_GROUP):
        v = lt[N_GROUPS + j:N_GROUPS + j + 1, :]
        for gi in range(1, N_GROUPS):
            row = N_GROUPS + gi * EXPERTS_PER_GROUP + j
            v = jnp.where(g_idx == gi, lt[row:row + 1, :], v)
        e.append(v)
    e1 = functools.reduce(jnp.maximum, e)
    i1 = _first_index(e, e1)
    rest = [jnp.where(i1 == j, -jnp.inf, e[j]) for j in range(EXPERTS_PER_GROUP)]
    e2 = functools.reduce(jnp.maximum, rest)
    i2 = _first_index(rest, e2)
    t = jnp.exp(e2 - e1)
    w1 = g_w / (1.0 + t)
    w2 = g_w * t / (1.0 + t)
    swap = i2 < i1
    lo = jnp.where(swap, i2, i1)
    hi = jnp.where(swap, i1, i2)
    pair = jnp.zeros_like(lo)
    for p, (slot0, slot1) in enumerate(PAIRS):
        pair = jnp.where((hi == slot0) & (lo == slot1), p, pair)
    bucket = g_idx * len(PAIRS) + pair
    return bucket, jnp.where(swap, w1, w2), jnp.where(swap, w2, w1)


def _outproj_kernel(yp_ref, ys_ref, oa_ref, ob_ref, zc_ref, zh_ref, ms_ref, wout_ref, cw_ref, gn_ref, wr_ref,
                    br_ref, y1_ref, info_ref, cnt_ref, ut_ref, y1_scr, carry_scr, *, prompt_tiles, tiles_per_seq):
    i = pl.program_id(0)
    tile = yp_ref.shape[0]

    @pl.when(i == 0)
    def _():
        carry_scr[...] = jnp.zeros_like(carry_scr)

    @pl.when(i < prompt_tiles)
    def _():
        zc = zc_ref[...]
        zh = zh_ref[...]
        halo = zh[:, 2 * C_WIDTH:] * zh[:, :C_WIDTH]
        halo = jnp.where(i % tiles_per_seq == 0, 0.0, halo)
        oc, u = _gated_conv(zc[:, :C_WIDTH], zc[:, C_WIDTH:2 * C_WIDTH], zc[:, 2 * C_WIDTH:], halo, cw_ref[...])
        ut_ref[...] = u[tile - 8:, :]
        y1_scr[...] = (
            yp_ref[...]
            + jnp.dot(oa_ref[...], wout_ref[0:A_Q_W, :], preferred_element_type=F32)
            + jnp.dot(ob_ref[...], wout_ref[A_Q_W:A_Q_W + B_W, :], preferred_element_type=F32)
            + jnp.dot(oc.astype(BF16), wout_ref[A_Q_W + B_W:, :], preferred_element_type=F32))

    @pl.when(i >= prompt_tiles)
    def _():
        ut_ref[...] = jnp.zeros_like(ut_ref)
        y1_scr[...] = ys_ref[...] + jnp.dot(ms_ref[...].astype(BF16), wout_ref[...], preferred_element_type=F32)

    y1 = y1_scr[...]
    y1_ref[:, :D_MODEL] = y1
    xn = _rms(y1, gn_ref[...])

    x_hi = xn.astype(BF16)
    x_lo = (xn - x_hi.astype(F32)).astype(BF16)
    wr = wr_ref[...]
    w_hi = wr.astype(BF16)
    w_lo = (wr - w_hi.astype(F32)).astype(BF16)
    lt_hi = _nt_dot(jnp.concatenate([w_hi, w_lo], axis=0), x_hi)
    lt = lt_hi[:ROUTE_ROWS] + lt_hi[ROUTE_ROWS:] + _nt_dot(w_hi, x_lo) + br_ref[...]
    bucket, w_slot0, w_slot1 = _route(lt)
    onehot = (lax.broadcasted_iota(jnp.int32, (ROUTE_ROWS, tile), 0) == bucket).astype(F32)
    upper = (lax.broadcasted_iota(jnp.int32, (tile, tile), 0)
             <= lax.broadcasted_iota(jnp.int32, (tile, tile), 1)).astype(BF16)
    running = jnp.dot(onehot.astype(BF16), upper, preferred_element_type=F32)
    carry = carry_scr[...]
    rank = jnp.sum(onehot * (running - 1.0 + carry), axis=0, keepdims=True)
    carry = carry + jnp.sum(onehot, axis=1, keepdims=True)
    carry_scr[...] = carry
    cnt_ref[...] = jnp.broadcast_to(carry, cnt_ref.shape)
    info_ref[...] = jnp.concatenate([bucket.astype(F32), rank, jnp.zeros((SUBLANES - 2, tile), F32)], axis=0)
    y1_ref[:, D_MODEL:] = jnp.concatenate([w_slot0, w_slot1, jnp.zeros((LANES - 2, tile), F32)], axis=0).T


def _outproj_route(yp, ys, sample_row0, oa, ob, z, mix_s, w_out, cw, gain, w_route, b_route, n_prompt, seq_len):
    n = n_prompt + mix_s.shape[0]
    tiles = n // ROW_TILE
    p_tiles = n_prompt // ROW_TILE
    halo_blocks = ROW_TILE // 8
    conv_w = 3 * C_WIDTH
    pidx = lambda i: jnp.minimum(i, p_tiles - 1)
    sidx = lambda i: jnp.maximum(i - p_tiles, 0)
    const = lambda a: pl.BlockSpec(a.shape, lambda i: (0,) * a.ndim)
    gain = gain.reshape(1, D_MODEL)
    return pl.pallas_call(
        functools.partial(_outproj_kernel, prompt_tiles=p_tiles, tiles_per_seq=seq_len // ROW_TILE),
        out_shape=(jax.ShapeDtypeStruct((n, ROW_EXT), F32),
                   jax.ShapeDtypeStruct((8, n), F32),
                   jax.ShapeDtypeStruct((ROUTE_ROWS, LANES), F32),
                   jax.ShapeDtypeStruct((tiles * 8, C_WIDTH), F32)),
        grid=(tiles,),
        in_specs=[pl.BlockSpec((ROW_TILE, D_MODEL), lambda i: (pidx(i), 0)),
                  pl.BlockSpec((ROW_TILE, D_MODEL), lambda i: (sample_row0 // ROW_TILE + sidx(i), 0)),
                  pl.BlockSpec((ROW_TILE, A_Q_W), lambda i: (pidx(i), 0)),
                  pl.BlockSpec((ROW_TILE, B_W), lambda i: (pidx(i), 0)),
                  pl.BlockSpec((ROW_TILE, conv_w), lambda i: (pidx(i), 0)),
                  pl.BlockSpec((8, conv_w), lambda i: (jnp.maximum(pidx(i) * halo_blocks - 1, 0), 0)),
                  pl.BlockSpec((ROW_TILE, D_MODEL), lambda i: (sidx(i), 0)),
                  const(w_out), const(cw), const(gain), const(w_route), const(b_route)],
        out_specs=(pl.BlockSpec((ROW_TILE, ROW_EXT), lambda i: (i, 0)),
                   pl.BlockSpec((8, ROW_TILE), lambda i: (0, i)),
                   pl.BlockSpec((ROUTE_ROWS, LANES), lambda i: (0, 0)),
                   pl.BlockSpec((8, C_WIDTH), lambda i: (i, 0))),
        scratch_shapes=[pltpu.VMEM((ROW_TILE, D_MODEL), F32), pltpu.VMEM((ROUTE_ROWS, 1), F32)],
        compiler_params=_params("arbitrary"),
    )(yp, ys, oa, ob, z, z, mix_s, w_out, cw, gain, w_route, b_route)


def _moe_kernel(e0_ref, e1_ref, used_ref, x_ref, g_ref, *rest, has_final_gain):
    o_ref = rest[-1]
    weights = rest[:6 * TILES_PER_STEP]
    t = pl.program_id(0)

    @pl.when(used_ref[t * TILES_PER_STEP] > 0)
    def _():
        for k in range(TILES_PER_STEP):
            rows = slice(k * MOE_TILE, (k + 1) * MOE_TILE)
            wg1, wu1, wd1, wg2, wu2, wd2 = weights[6 * k:6 * k + 6]
            y1 = x_ref[rows, :D_MODEL]
            w = x_ref[rows, D_MODEL:]
            x = _rms(y1, g_ref[...]).astype(BF16)

            def expert(wg, wu, wd, scale, x=x):
                g = jnp.dot(x, wg[...], preferred_element_type=F32)
                u = jnp.dot(x, wu[...], preferred_element_type=F32)
                h = g * jax.nn.sigmoid(g) * u * scale
                return jnp.dot(h.astype(BF16), wd[...], preferred_element_type=F32)

            y2 = y1 + expert(wg1, wu1, wd1, w[:, 0:1]) + expert(wg2, wu2, wd2, w[:, 1:2])
            o_ref[rows, :] = _rms(y2, rest[-2][...]) if has_final_gain else y2

    @pl.when(used_ref[t * TILES_PER_STEP] == 0)
    def _():
        o_ref[...] = jnp.zeros_like(o_ref)


def _experts(xs, gain, e_slot0, e_slot1, used, w_gate, w_up, w_down, layer, final_gain=None):
    step_rows = TILES_PER_STEP * MOE_TILE
    n_steps = xs.shape[0] // step_rows

    def weight_specs(k):
        pick = lambda sel: (lambda t, e0, e1, u: (layer, (e0, e1)[sel][t * TILES_PER_STEP + k], 0, 0))
        up = lambda sel: pl.BlockSpec((None, None, D_MODEL, EXPERT_FF), pick(sel))
        down = lambda sel: pl.BlockSpec((None, None, EXPERT_FF, D_MODEL), pick(sel))
        return [up(0), up(0), down(0), up(1), up(1), down(1)]

    gain_spec = pl.BlockSpec((1, D_MODEL), lambda t, e0, e1, u: (0, 0))
    extra = [] if final_gain is None else [final_gain.reshape(1, D_MODEL)]
    grid_spec = pltpu.PrefetchScalarGridSpec(
        num_scalar_prefetch=3,
        grid=(n_steps,),
        in_specs=[pl.BlockSpec((step_rows, ROW_EXT), lambda t, e0, e1, u: (t, 0)), gain_spec]
                 + [spec for k in range(TILES_PER_STEP) for spec in weight_specs(k)] + [gain_spec] * len(extra),
        out_specs=pl.BlockSpec((step_rows, D_MODEL), lambda t, e0, e1, u: (t, 0)))
    return pl.pallas_call(
        functools.partial(_moe_kernel, has_final_gain=final_gain is not None),
        out_shape=jax.ShapeDtypeStruct((xs.shape[0], D_MODEL), F32),
        grid_spec=grid_spec,
        compiler_params=_params("arbitrary"),
    )(e_slot0, e_slot1, used, xs, gain.reshape(1, D_MODEL), *([w_gate, w_up, w_down] * (2 * TILES_PER_STEP)), *extra)


def _dispatch_plan(info, counts, n):
    n_tiles = -(-n // MOE_TILE) + N_ROUTE_BUCKETS
    n_tiles = -(-n_tiles // TILES_PER_STEP) * TILES_PER_STEP
    bucket = info[0].astype(jnp.int32)
    rank = info[1].astype(jnp.int32)
    counts = counts[:N_ROUTE_BUCKETS, 0].astype(jnp.int32)
    tiles_per_bucket = (counts + MOE_TILE - 1) // MOE_TILE
    tile_end = jnp.cumsum(tiles_per_bucket)
    row_start = (tile_end - tiles_per_bucket) * MOE_TILE
    dest = row_start[bucket] + rank
    src = (jnp.arange(n_tiles * MOE_TILE, dtype=jnp.int32) % n).at[dest].set(jnp.arange(n, dtype=jnp.int32))
    tile_ids = jnp.arange(n_tiles, dtype=jnp.int32)
    tile_bucket = jnp.minimum(jnp.searchsorted(tile_end, tile_ids, side="right").astype(jnp.int32),
                              N_ROUTE_BUCKETS - 1)
    used = (tile_ids < tile_end[-1]).astype(jnp.int32)
    pair = tile_bucket % len(PAIRS)
    base = (tile_bucket // len(PAIRS)) * EXPERTS_PER_GROUP
    pairs = jnp.asarray(PAIRS, jnp.int32)
    return dest, src, base + pairs[pair, 0], base + pairs[pair, 1], used


def _permute_in_columns(w):
    attn = A_Q_W + 2 * A_KV_W + 3 * B_W
    return jnp.concatenate([w[:, attn:], w[:, A_Q_W:A_Q_W + A_KV_W], w[:, :A_Q_W], w[:, A_Q_W + A_KV_W:attn]],
                           axis=1)


def kernel(x_prompt, x_sample, cache_a_k, cache_a_v, cache_b_k, cache_b_v, state_conv, rel_bias_table,
           w_in, w_out, conv_w, attn_sinks, norm_mix, norm_ffn, w_group, b_group, w_router, b_router,
           w_gate, w_up, w_down, norm_final):
    n_seq, seq_len, _ = x_prompt.shape
    dec_seq, n_new, _ = x_sample.shape
    depth = w_in.shape[0]
    n_prompt = n_seq * seq_len
    n_sample = dec_seq * n_new
    n = n_prompt + n_sample
    la, lb = cache_a_k.shape[2], cache_b_k.shape[2]
    assert seq_len % SPAN == 0 and n_prompt % IN_TILE == 0 and n_sample % IN_TILE == 0 and IN_TILE % ROW_TILE == 0
    assert seq_len >= SPAN and la % LANES == 0 and lb % LANES == 0 and n_new == 8

    table_a, table_b = rel_bias_table[:, :A_Q_HEADS], rel_bias_table[:, A_Q_HEADS:]
    bias_a = _band_bias(table_a, 1)
    bias_b = jnp.stack([_band_bias(table_b, r) for r in DILATIONS], axis=1)
    sbias_a = _sample_bias(table_a, n_new, la, np.arange(la + LANES), A_WINDOW, 1)
    sbias_a = sbias_a.reshape(A_Q_HEADS * n_new, la + LANES)
    sbias_b = jnp.stack([_sample_bias(table_b, n_new, lb, np.arange(lb + LANES), w, r)
                         .reshape(B_HEADS * n_new, lb + LANES) for w, r in B_BRANCHES])
    cak, cav, cbk, cbv = (c.transpose(0, 1, 3, 4, 2) for c in (cache_a_k, cache_a_v, cache_b_k, cache_b_v))

    w_in_b = jnp.stack([_permute_in_columns(w_in[l]) for l in range(depth)]).astype(BF16)
    w_out_b = w_out.astype(BF16)
    expert_w = tuple(w.astype(BF16) for w in (w_gate, w_up, w_down))
    pad = ROUTE_ROWS - N_GROUPS - N_EXPERTS
    w_route = jnp.pad(jnp.concatenate([w_group, w_router], axis=2).transpose(0, 2, 1), ((0, 0), (0, pad), (0, 0)))
    b_route = jnp.pad(jnp.concatenate([b_group, b_router], axis=1), ((0, 0), (0, pad)))[..., None]
    sink_rows = jnp.repeat(attn_sinks, n_new, axis=1)[..., None]

    yp, ys, s_row0 = x_prompt.reshape(n_prompt, D_MODEL), x_sample.reshape(n_sample, D_MODEL), 0
    states = []
    for l in range(depth):
        lap, lbp = min(A_WINDOW, seq_len), min(SPAN, seq_len)
        z, kbt, vbt = _inproj(yp, ys, s_row0, n_prompt, n_sample, norm_mix[l], w_in_b[l], seq_len, lbp)
        oa = _window_attention(z, bias_a, attn_sinks[l], n_seq, seq_len)
        ob = _dilated_attention(z, bias_b, n_seq, seq_len)
        mix_s, conv_s = _sample_mixer(z, n_prompt, cak, cav, cbk, cbv, l, state_conv[l], conv_w[l], sink_rows[l],
                                      sbias_a, sbias_b, n_new)
        y1, info, counts, u_tail = _outproj_route(
            yp, ys, s_row0, oa, ob, z, mix_s, w_out_b[l], conv_w[l], norm_ffn[l], w_route[l], b_route[l], n_prompt, seq_len)
        def prompt_tail(length, col, width, heads):
            rows = [z[(b + 1) * seq_len - length:(b + 1) * seq_len, col:col + width] for b in range(n_seq)]
            return jnp.stack(rows).reshape(n_seq, length, heads, HEAD_DIM)

        def sample_rows(col, width, heads):
            return z[n_prompt:, col:col + width].reshape(dec_seq, n_new, heads, HEAD_DIM)

        last_tile = [((b + 1) * seq_len // ROW_TILE - 1) * SUBLANES for b in range(n_seq)]
        conv_p = jnp.stack([u_tail[t + SUBLANES - (CONV_WIDTH - 1):t + SUBLANES, :] for t in last_tile])
        layer_states = (
            prompt_tail(lap, COL_KA, A_KV_W, A_KV_HEADS), prompt_tail(lap, COL_VA, A_KV_W, A_KV_HEADS),
            kbt.reshape(n_seq, B_HEADS, HEAD_DIM, lbp).transpose(0, 3, 1, 2),
            vbt.reshape(n_seq, B_HEADS, HEAD_DIM, lbp).transpose(0, 3, 1, 2), conv_p,
            sample_rows(COL_KA, A_KV_W, A_KV_HEADS), sample_rows(COL_VA, A_KV_W, A_KV_HEADS),
            sample_rows(COL_KB, B_W, B_HEADS), sample_rows(COL_VB, B_W, B_HEADS), conv_s)

        dest, src, e_lo, e_hi, used = _dispatch_plan(info, counts, n)
        xs = y1[src]
        if l == depth - 1:
            xs, layer_states = lax.optimization_barrier((xs, layer_states))
        y_sorted = _experts(xs, norm_ffn[l], e_lo, e_hi, used, *expert_w, l,
                            final_gain=norm_final if l == depth - 1 else None)
        if l < depth - 1:
            y_next, layer_states = lax.optimization_barrier((y_sorted[dest], layer_states))
            yp, ys, s_row0 = y_next, y_next, n_prompt
        states.append(layer_states)

    y_prompt = y_sorted[dest[:n_prompt]].reshape(n_seq, seq_len, D_MODEL)
    y_sample = y_sorted[dest[n_prompt:]].reshape(dec_seq, n_new, D_MODEL)
    st = [jnp.stack([s[k] for s in states]) for k in range(10)]
    return (y_prompt, y_sample, st[0], st[1], st[2], st[3], st[4], st[5], st[6], st[7], st[8], st[9])
```

```python
import functools
import math

import numpy as np
import jax
import jax.numpy as jnp
from jax import lax
from jax.experimental import pallas as pl
from jax.experimental.pallas import tpu as pltpu

F32 = jnp.float32
BF16 = jnp.bfloat16

D_MODEL = 1024
HEAD_DIM = 64
ATTN_SCALE = HEAD_DIM ** -0.5
LOG2E = math.log2(math.e)
LN2 = math.log(2.0)
BAND_Q_SCALE = ATTN_SCALE * LOG2E
BLOCK = 128
LANES = 128
SUBLANES = 8
A_Q_HEADS = 4
A_KV_HEADS = 2
A_WINDOW = 128
B_HEADS = 6
B_BRANCHES = ((128, 1), (512, 4), (2048, 16))
DILATIONS = tuple(r for _, r in B_BRANCHES)
SPAN = BLOCK * max(DILATIONS)
C_WIDTH = 6 * HEAD_DIM
CONV_WIDTH = 3
A_Q_W = A_Q_HEADS * HEAD_DIM
A_KV_W = A_KV_HEADS * HEAD_DIM
B_W = B_HEADS * HEAD_DIM
IN_WIDTH = A_Q_W + 2 * A_KV_W + 3 * B_W + 3 * C_WIDTH
N_BUCKETS = 32
MAX_DISTANCE = 2048
N_GROUPS = 4
EXPERTS_PER_GROUP = 4
N_EXPERTS = N_GROUPS * EXPERTS_PER_GROUP
EXPERT_FF = 512
RMS_EPS = 1e-6
NEG_INF = -1e30

COL_CONV = 0
COL_KA = 3 * C_WIDTH
COL_QA = COL_KA + A_KV_W
COL_VA = COL_QA + A_Q_W
COL_QB = COL_VA + A_KV_W
COL_KB = COL_QB + B_W
COL_VB = COL_KB + B_W
SPLIT = COL_QB

ROW_TILE = 512
IN_TILE = 1024
MOE_TILE = 256
TILES_PER_STEP = 2
SEQS_PER_STEP = 2
PAIRS = ((1, 0), (2, 0), (2, 1), (3, 1), (3, 0), (3, 2))
N_ROUTE_BUCKETS = N_GROUPS * len(PAIRS)
ROUTE_ROWS = 32
ROW_EXT = D_MODEL + LANES
VMEM_LIMIT = 56 * 1024 * 1024


def _params(*sem):
    return pltpu.CompilerParams(dimension_semantics=sem, vmem_limit_bytes=VMEM_LIMIT)


def _rel_bucket(dist):
    d = np.maximum(dist, 0)
    max_exact = N_BUCKETS // 2
    df = np.maximum(d, max_exact).astype(np.float32)
    large = max_exact + (np.log(df / max_exact) / math.log(MAX_DISTANCE / max_exact)
                         * (N_BUCKETS - max_exact)).astype(np.int32)
    large = np.minimum(large, N_BUCKETS - 1)
    return np.where(d < max_exact, d, large)


def _masked_bias(table, dist, valid):
    bucket = _rel_bucket(dist).reshape(-1, 1)
    onehot = (jnp.asarray(bucket, jnp.int32) == jnp.arange(N_BUCKETS, dtype=jnp.int32)[None, :]).astype(F32)
    b = jnp.dot(onehot, table.astype(F32), precision=lax.Precision.HIGHEST)
    b = jnp.where(jnp.asarray(valid.reshape(-1, 1)), b, NEG_INF)
    return jnp.moveaxis(b.reshape(dist.shape + (table.shape[1],)), -1, 0)


def _band_bias(table, scale):
    i = np.arange(BLOCK)[:, None]
    j = np.arange(2 * BLOCK)[None, :]
    dist = i + BLOCK - j
    b = _masked_bias(table, dist * scale, (dist >= 0) & (dist <= BLOCK)) * LOG2E
    return b.reshape(table.shape[1] // 2, 2 * BLOCK, 2 * BLOCK)


def _sample_bias(table, n_new, first_new, positions, window, dilation):
    dist = first_new + np.arange(n_new)[:, None] - np.asarray(positions)[None, :]
    valid = (dist >= 0) & (dist <= window) & (dist % dilation == 0)
    return _masked_bias(table, dist, valid)


def _rms(x, gain):
    return x * lax.rsqrt(jnp.mean(x * x, axis=-1, keepdims=True) + RMS_EPS) * gain


def _inproj_kernel(xp_ref, xs_ref, g_ref, w_ref, zh_ref, zf_ref, kbt_ref, vbt_ref, *, prompt_tiles, tiles_per_seq,
                   tail_tiles):
    i = pl.program_id(0)

    def project(x_ref):
        h = _rms(x_ref[...], g_ref[...])
        z = jnp.dot(h.astype(BF16), w_ref[...], preferred_element_type=F32)
        zh_ref[...] = z[:, :SPLIT].astype(BF16)
        zf_ref[...] = z[:, SPLIT:]

    pl.when(i < prompt_tiles)(lambda: project(xp_ref))
    pl.when(i >= prompt_tiles)(lambda: project(xs_ref))

    @pl.when((i < prompt_tiles) & (i % tiles_per_seq >= tiles_per_seq - tail_tiles))
    def _():
        kbt_ref[...] = zf_ref[:, COL_KB - SPLIT:COL_KB - SPLIT + B_W].T
        vbt_ref[...] = zf_ref[:, COL_VB - SPLIT:COL_VB - SPLIT + B_W].T


def _inproj(xp, xs, sample_row0, n_prompt, n_sample, gain, w, seq_len, tail_len):
    p_tiles, s_tile0 = n_prompt // IN_TILE, sample_row0 // IN_TILE
    tps, tail = seq_len // IN_TILE, tail_len // IN_TILE
    n_seq = n_prompt // seq_len

    def tail_block(i):
        j = jnp.minimum(i, p_tiles - 1)
        return j // tps, 0, jnp.maximum(j % tps - (tps - tail), 0)

    tail_spec = pl.BlockSpec((None, B_W, IN_TILE), tail_block)
    tail_shape = jax.ShapeDtypeStruct((n_seq, B_W, tail_len), F32)
    return pl.pallas_call(
        functools.partial(_inproj_kernel, prompt_tiles=p_tiles, tiles_per_seq=tps, tail_tiles=tail),
        out_shape=(jax.ShapeDtypeStruct((n_prompt + n_sample, SPLIT), BF16),
                   jax.ShapeDtypeStruct((n_prompt + n_sample, IN_WIDTH - SPLIT), F32), tail_shape, tail_shape),
        grid=((n_prompt + n_sample) // IN_TILE,),
        in_specs=[pl.BlockSpec((IN_TILE, D_MODEL), lambda i: (jnp.minimum(i, p_tiles - 1), 0)),
                  pl.BlockSpec((IN_TILE, D_MODEL), lambda i: (s_tile0 + jnp.maximum(i - p_tiles, 0), 0)),
                  pl.BlockSpec((1, D_MODEL), lambda i: (0, 0)),
                  pl.BlockSpec((D_MODEL, IN_WIDTH), lambda i: (0, 0))],
        out_specs=(pl.BlockSpec((IN_TILE, SPLIT), lambda i: (i, 0)),
                   pl.BlockSpec((IN_TILE, IN_WIDTH - SPLIT), lambda i: (i, 0)), tail_spec, tail_spec),
        compiler_params=_params("arbitrary"),
    )(xp, xs, gain.reshape(1, D_MODEL), w)


def _lane_half(shape):
    return lax.broadcasted_iota(jnp.int32, shape, len(shape) - 1) // HEAD_DIM


def _nt_dot(a, b):
    return lax.dot_general(a, b, (((1,), (1,)), ((), ())), preferred_element_type=F32)

def _band_pair(q2, k2, v2, bias2, penalty):
    lg = _nt_dot(q2, k2) + bias2
    if penalty is not None:
        lg = lg + penalty
    m = jnp.max(lg, axis=-1, keepdims=True)
    p = jnp.exp2(lg - m)
    s = jnp.sum(p, axis=-1, keepdims=True)
    acc = jnp.dot(p.astype(BF16), v2, preferred_element_type=F32)
    return acc / s, (m + jnp.log2(s)) * LN2


def _prev_key_penalty(first_span):
    col = lax.broadcasted_iota(jnp.int32, (1, 2 * BLOCK), 1)
    return jnp.where((col < BLOCK) & first_span, NEG_INF, 0.0).astype(F32)


def _rows(start, dilation):
    if dilation == 1:
        return pl.ds(start, BLOCK)
    return pl.ds(start, BLOCK, stride=dilation)


def _dilated_kernel(q_ref, kc_ref, kp_ref, vc_ref, vp_ref, bias_ref, o_ref, o_scr, l_scr, first_bias_scr):
    first_half = _lane_half((BLOCK, LANES)) == 0
    penalty = _prev_key_penalty(pl.program_id(1) == 0)
    for bi in range(len(DILATIONS)):
        first_bias_scr[bi] = bias_ref[bi] + penalty

    for bi, r in enumerate(DILATIONS):
        step = r * BLOCK

        def block(c, n, first, bi=bi, r=r, step=step):
            q_start = c + n * step
            if not isinstance(q_start, int):
                q_start = pl.multiple_of(q_start, BLOCK) if r == 1 else q_start
            cur = _rows(q_start, r)
            qt = (q_ref[cur, :] * BAND_Q_SCALE).astype(BF16)
            if first:
                src_k, src_v, prev = kp_ref, vp_ref, _rows(SPAN - step + c, r)
            else:
                p_start = q_start - step
                if r == 1:
                    p_start = pl.multiple_of(p_start, BLOCK)
                src_k, src_v, prev = kc_ref, vc_ref, _rows(p_start, r)
            k2 = jnp.concatenate([src_k[prev, :], kc_ref[cur, :]], axis=0).astype(BF16)
            v2 = jnp.concatenate([src_v[prev, :], vc_ref[cur, :]], axis=0).astype(BF16)
            zero = jnp.zeros_like(qt)
            q2 = jnp.concatenate([jnp.where(first_half, qt, zero), jnp.where(first_half, zero, qt)], axis=0)
            o, lse = _band_pair(q2, k2, v2, first_bias_scr[bi] if first else bias_ref[bi], None)
            o_scr[bi, cur, :] = jnp.where(first_half, o[:BLOCK], o[BLOCK:])
            l_scr[bi, cur, :] = jnp.where(first_half, lse[:BLOCK], lse[BLOCK:])

        n_blocks = SPAN // step
        if r == 1:
            block(0, 0, True)
            lax.fori_loop(1, n_blocks, lambda n, _: block(0, n, False), None, unroll=True)
        else:
            def per_class(c, _, block=block, n_blocks=n_blocks):
                block(c, 0, True)
                for n in range(1, n_blocks):
                    block(c, n, False)
            lax.fori_loop(0, r, per_class, None, unroll=True)

    chunk = 2 * BLOCK

    def combine(j, _):
        rows = pl.ds(pl.multiple_of(j * chunk, chunk), chunk)
        ls = [l_scr[bi, rows, :] for bi in range(len(DILATIONS))]
        m = functools.reduce(jnp.maximum, ls)
        ws = [jnp.exp(l - m) for l in ls]
        num = sum(w * o_scr[bi, rows, :] for bi, w in enumerate(ws))
        o_ref[rows, :] = (num / sum(ws)).astype(o_ref.dtype)

    lax.fori_loop(0, SPAN // chunk, combine, None)


def _dilated_attention(z, bias, n_seq, seq_len):
    spans = seq_len // SPAN
    qb, kb, vb = ((c - SPLIT) // LANES for c in (COL_QB, COL_KB, COL_VB))
    blk = (SPAN, LANES)
    cur = lambda col: (lambda b, s, i: (b * spans + s, col + i))
    prev = lambda col: (lambda b, s, i: (b * spans + jnp.maximum(s - 1, 0), col + i))
    return pl.pallas_call(
        _dilated_kernel,
        out_shape=jax.ShapeDtypeStruct((n_seq * seq_len, B_W), BF16),
        grid=(n_seq, spans, B_W // LANES),
        in_specs=[pl.BlockSpec(blk, cur(qb)),
                  pl.BlockSpec(blk, cur(kb)), pl.BlockSpec(blk, prev(kb)),
                  pl.BlockSpec(blk, cur(vb)), pl.BlockSpec(blk, prev(vb)),
                  pl.BlockSpec((None, len(DILATIONS), 2 * BLOCK, 2 * BLOCK), lambda b, s, i: (i, 0, 0, 0))],
        out_specs=pl.BlockSpec(blk, lambda b, s, i: (b * spans + s, i)),
        scratch_shapes=[pltpu.VMEM((len(DILATIONS), SPAN, LANES), F32),
                        pltpu.VMEM((len(DILATIONS), SPAN, LANES), F32),
                        pltpu.VMEM((len(DILATIONS), 2 * BLOCK, 2 * BLOCK), F32)],
        compiler_params=_params("parallel", "parallel", "parallel"),
    )(z, z, z, z, z, bias)


def _window_kernel(sink_ref, q_ref, kc_ref, kp_ref, vc_ref, vp_ref, bias_ref, o_ref):
    penalty = _prev_key_penalty(pl.program_id(1) == 0)
    first_half = _lane_half((BLOCK, LANES)) == 0
    first_half2 = _lane_half((2 * BLOCK, LANES)) == 0

    def block(n, first):
        start = n * BLOCK if isinstance(n, int) else pl.multiple_of(n * BLOCK, BLOCK)
        cur = pl.ds(start, BLOCK)
        if first:
            src_k, src_v, prev = kp_ref, vp_ref, pl.ds(SPAN - BLOCK, BLOCK)
        else:
            src_k, src_v, prev = kc_ref, vc_ref, pl.ds(pl.multiple_of(start - BLOCK, BLOCK), BLOCK)
        k2 = jnp.concatenate([src_k[prev, :], kc_ref[cur, :]], axis=0)
        vf = jnp.concatenate([src_v[prev, :], vc_ref[cur, :]], axis=0).astype(F32)
        v_swap = pltpu.roll(vf, HEAD_DIM, 1)
        for i in range(A_KV_HEADS):
            qf = q_ref[cur, i * LANES:(i + 1) * LANES].astype(F32) * BAND_Q_SCALE
            q_swap = pltpu.roll(qf, HEAD_DIM, 1)
            q2 = jnp.concatenate([jnp.where(first_half == (i == 0), qf if a == i else q_swap, 0.0)
                                  for a in range(2)], axis=0).astype(BF16)
            v2 = jnp.where(first_half2 == (i == 0), vf, v_swap).astype(BF16)
            o, lse = _band_pair(q2, k2, v2, bias_ref[i], penalty if first else None)
            outs = [o[a * BLOCK:(a + 1) * BLOCK] * jax.nn.sigmoid(lse[a * BLOCK:(a + 1) * BLOCK] - sink_ref[2 * i + a])
                    for a in range(2)]
            o_ref[cur, i * LANES:(i + 1) * LANES] = jnp.where(first_half, outs[0], outs[1]).astype(o_ref.dtype)

    block(0, True)
    lax.fori_loop(1, SPAN // BLOCK, lambda n, _: block(n, False), None, unroll=5)


def _window_attention(z, bias, sinks, n_seq, seq_len):
    spans = seq_len // SPAN
    ka, va = COL_KA // LANES, COL_VA // LANES
    blk = (SPAN, LANES)
    cur = lambda col: (lambda b, s: (b * spans + s, col))
    prev = lambda col: (lambda b, s: (b * spans + jnp.maximum(s - 1, 0), col))
    return pl.pallas_call(
        _window_kernel,
        out_shape=jax.ShapeDtypeStruct((n_seq * seq_len, A_Q_W), BF16),
        grid=(n_seq, spans),
        in_specs=[pl.BlockSpec(memory_space=pltpu.SMEM),
                  pl.BlockSpec((SPAN, A_Q_W), lambda b, s: (b * spans + s, COL_QA // A_Q_W)),
                  pl.BlockSpec(blk, cur(ka)), pl.BlockSpec(blk, prev(ka)),
                  pl.BlockSpec(blk, cur(va)), pl.BlockSpec(blk, prev(va)),
                  pl.BlockSpec((A_KV_HEADS, 2 * BLOCK, 2 * BLOCK), lambda b, s: (0, 0, 0))],
        out_specs=pl.BlockSpec((SPAN, A_Q_W), lambda b, s: (b * spans + s, 0)),
        compiler_params=_params("parallel", "parallel"),
    )(sinks, z, z, z, z, z, bias)


def _shift_rows(u, filler, k):
    rolled = pltpu.roll(u, k, 0)
    row = lax.broadcasted_iota(jnp.int32, u.shape, 0)
    n_fill = filler.shape[0]
    for j in range(k):
        rolled = jnp.where(row == j, filler[n_fill - k + j:n_fill - k + j + 1, :], rolled)
    return rolled


def _gated_conv(xc, bg, cg, filler, cw):
    u = cg * xc
    conv = cw[0:1, :] * _shift_rows(u, filler, 2) + cw[1:2, :] * _shift_rows(u, filler, 1) + cw[2:3, :] * u
    return bg * conv, u


def _pad_rows(x, rows):
    return jnp.concatenate([x, jnp.zeros((rows - x.shape[0], x.shape[1]), x.dtype)], axis=0)


def _heads_by_dim(ref, j):
    _, h, d, length = ref.shape
    return ref[j].reshape(h * d, length).astype(BF16)


def _sample_kernel(zh_ref, zf_ref, cak_ref, cav_ref, cbk_ref, cbv_ref, st_ref, cw_ref, sink_ref, bias_a_ref,
                   bias_b_ref, mix_ref, conv_ref, *, n_new):
    z_all = jnp.concatenate([zh_ref[...].astype(F32), zf_ref[...]], axis=1)
    for j in range(z_all.shape[0] // n_new):
        _sample_sequence(j, z_all, cak_ref, cav_ref, cbk_ref, cbv_ref, st_ref, cw_ref, sink_ref, bias_a_ref,
                         bias_b_ref, mix_ref, conv_ref, n_new)


def _sample_sequence(j, z_all, cak_ref, cav_ref, cbk_ref, cbv_ref, st_ref, cw_ref, sink_ref, bias_a_ref,
                     bias_b_ref, mix_ref, conv_ref, n_new):
    tokens = slice(j * n_new, (j + 1) * n_new)
    z = z_all[tokens, :]
    la = cak_ref.shape[3]
    lb = cbk_ref.shape[3]

    ka_new = _pad_rows(z[:, COL_KA:COL_KA + A_KV_W], LANES).astype(BF16)
    va_new = _pad_rows(z[:, COL_VA:COL_VA + A_KV_W], LANES).astype(BF16)
    half = _lane_half((n_new, LANES))
    pieces = []
    for i in range(A_KV_HEADS):
        qf = z[:, COL_QA + i * LANES:COL_QA + (i + 1) * LANES] * ATTN_SCALE
        for a in range(2):
            pieces.append(jnp.where(half == i, qf if a == i else pltpu.roll(qf, HEAD_DIM, 1), 0.0))
    qa = jnp.concatenate(pieces, axis=0).astype(BF16)
    lc = jnp.dot(qa, _heads_by_dim(cak_ref, j), preferred_element_type=F32) + bias_a_ref[:, :la]
    ln = _nt_dot(qa, ka_new) + bias_a_ref[:, la:]
    m = jnp.maximum(jnp.max(lc, axis=-1, keepdims=True), jnp.max(ln, axis=-1, keepdims=True))
    pc = jnp.exp(lc - m)
    pn = jnp.exp(ln - m)
    s = jnp.sum(pc, axis=-1, keepdims=True) + jnp.sum(pn, axis=-1, keepdims=True)
    oa = (_nt_dot(pc.astype(BF16), _heads_by_dim(cav_ref, j))
          + jnp.dot(pn.astype(BF16), va_new, preferred_element_type=F32))
    oa = oa / s * jax.nn.sigmoid(m + jnp.log(s) - sink_ref[...])
    oa_blocks = []
    for i in range(A_KV_HEADS):
        per_half = []
        for a in range(2):
            rows = oa[(2 * i + a) * n_new:(2 * i + a + 1) * n_new, :]
            per_half.append(rows if a == i else pltpu.roll(rows, HEAD_DIM, 1))
        oa_blocks.append(jnp.where(half == 0, per_half[0], per_half[1]))

    qf = z[:, COL_QB:COL_QB + B_W] * ATTN_SCALE
    head_of_lane = _lane_half((n_new, B_W))
    qb = jnp.concatenate([jnp.where(head_of_lane == h, qf, 0.0) for h in range(B_HEADS)], axis=0).astype(BF16)
    kb_new = _pad_rows(z[:, COL_KB:COL_KB + B_W], LANES).astype(BF16)
    vb_new = _pad_rows(z[:, COL_VB:COL_VB + B_W], LANES).astype(BF16)
    lg_c = jnp.dot(qb, _heads_by_dim(cbk_ref, j), preferred_element_type=F32)
    lg_n = _nt_dot(qb, kb_new)
    parts = []
    for bi, (w, r) in enumerate(B_BRANCHES):
        lo = lb - min(lb, -(-w // LANES) * LANES)
        lc = lg_c[:, lo:] + bias_b_ref[bi, :, lo:lb]
        ln = lg_n + bias_b_ref[bi, :, lb:]
        m = jnp.maximum(jnp.max(lc, axis=-1, keepdims=True), jnp.max(ln, axis=-1, keepdims=True))
        pc = jnp.exp(lc - m)
        pn = jnp.exp(ln - m)
        s = jnp.sum(pc, axis=-1, keepdims=True) + jnp.sum(pn, axis=-1, keepdims=True)
        parts.append((lo, pc, pn, s, m + jnp.log(s)))
    m_all = functools.reduce(jnp.maximum, [p[4] for p in parts])
    ws = [jnp.exp(p[4] - m_all) for p in parts]
    den = sum(ws)
    p_new = None
    los = sorted({p[0] for p in parts} | {lb})
    segs = [None] * (len(los) - 1)
    for (lo, pc, pn, s, _), w in zip(parts, ws):
        coef = w / (den * s)
        p_new = coef * pn if p_new is None else p_new + coef * pn
        for si in range(len(segs)):
            a0, a1 = los[si], los[si + 1]
            if a0 >= lo:
                piece = coef * pc[:, a0 - lo:a1 - lo]
                segs[si] = piece if segs[si] is None else segs[si] + piece
    p_cache = jnp.concatenate(segs, axis=1) if len(segs) > 1 else segs[0]
    ob = (_nt_dot(p_cache.astype(BF16), _heads_by_dim(cbv_ref, j))
          + jnp.dot(p_new.astype(BF16), vb_new, preferred_element_type=F32))
    ob_rows = sum(jnp.where(head_of_lane == h, ob[h * n_new:(h + 1) * n_new, :], 0.0) for h in range(B_HEADS))

    cz = z[:, COL_CONV:COL_CONV + 3 * C_WIDTH]
    oc, u = _gated_conv(cz[:, :C_WIDTH], cz[:, C_WIDTH:2 * C_WIDTH], cz[:, 2 * C_WIDTH:], st_ref[j], cw_ref[...])
    conv_ref[j] = u[n_new - (CONV_WIDTH - 1):, :]
    mix_ref[tokens, :] = jnp.concatenate(oa_blocks + [ob_rows, oc], axis=1)


def _sample_mixer(zh, zf, row0, cak, cav, cbk, cbv, layer, state, cw, sink_rows, bias_a, bias_b, n_new):
    n_seq = cak.shape[1]
    g = SEQS_PER_STEP
    rows = g * n_new
    blk0 = row0 // rows
    cache = lambda a: pl.BlockSpec((None, g) + a.shape[2:], lambda b: (layer, b, 0, 0, 0))
    per_seq = lambda shape: pl.BlockSpec((g,) + shape, lambda b: (b, 0, 0))
    const = lambda a: pl.BlockSpec(a.shape, lambda b: (0,) * a.ndim)
    return pl.pallas_call(
        functools.partial(_sample_kernel, n_new=n_new),
        out_shape=(jax.ShapeDtypeStruct((n_seq * n_new, D_MODEL), F32),
                   jax.ShapeDtypeStruct((n_seq, CONV_WIDTH - 1, C_WIDTH), F32)),
        grid=(n_seq // g,),
        in_specs=[pl.BlockSpec((rows, SPLIT), lambda b: (blk0 + b, 0)),
                  pl.BlockSpec((rows, IN_WIDTH - SPLIT), lambda b: (blk0 + b, 0)),
                  cache(cak), cache(cav), cache(cbk), cache(cbv),
                  per_seq((CONV_WIDTH - 1, C_WIDTH)), const(cw), const(sink_rows), const(bias_a), const(bias_b)],
        out_specs=(pl.BlockSpec((rows, D_MODEL), lambda b: (b, 0)), per_seq((CONV_WIDTH - 1, C_WIDTH))),
        compiler_params=_params("parallel"),
    )(zh, zf, cak, cav, cbk, cbv, state, cw, sink_rows, bias_a, bias_b)


def _first_index(vals, best):
    idx = jnp.full(best.shape, len(vals) - 1, jnp.int32)
    for j in range(len(vals) - 2, -1, -1):
        idx = jnp.where(vals[j] == best, j, idx)
    return idx


def _route(lt):
    g = [lt[k:k + 1, :] for k in range(N_GROUPS)]
    g_max = functools.reduce(jnp.maximum, g)
    g_idx = _first_index(g, g_max)
    g_w = 1.0 / sum(jnp.exp(v - g_max) for v in g)
    e = []
    for j in range(EXPERTS_PER_GROUP):
        v = lt[N_GROUPS + j:N_GROUPS + j + 1, :]
        for gi in range(1, N_GROUPS):
            row = N_GROUPS + gi * EXPERTS_PER_GROUP + j
            v = jnp.where(g_idx == gi, lt[row:row + 1, :], v)
        e.append(v)
    e1 = functools.reduce(jnp.maximum, e)
    i1 = _first_index(e, e1)
    rest = [jnp.where(i1 == j, -jnp.inf, e[j]) for j in range(EXPERTS_PER_GROUP)]
    e2 = functools.reduce(jnp.maximum, rest)
    i2 = _first_index(rest, e2)
    t = jnp.exp(e2 - e1)
    w1 = g_w / (1.0 + t)
    w2 = g_w * t / (1.0 + t)
    swap = i2 < i1
    lo = jnp.where(swap, i2, i1)
    hi = jnp.where(swap, i1, i2)
    pair = jnp.zeros_like(lo)
    for p, (slot0, slot1) in enumerate(PAIRS):
        pair = jnp.where((hi == slot0) & (lo == slot1), p, pair)
    bucket = g_idx * len(PAIRS) + pair
    return bucket, jnp.where(swap, w1, w2), jnp.where(swap, w2, w1)


def _outproj_kernel(yp_ref, ys_ref, oa_ref, ob_ref, zc_ref, zh_ref, ms_ref, wout_ref, cw_ref, gn_ref, wr_ref,
                    br_ref, y1_ref, info_ref, cnt_ref, ut_ref, y1_scr, carry_scr, *, prompt_tiles, tiles_per_seq):
    i = pl.program_id(0)
    tile = yp_ref.shape[0]

    @pl.when(i == 0)
    def _():
        carry_scr[...] = jnp.zeros_like(carry_scr)

    @pl.when(i < prompt_tiles)
    def _():
        zc = zc_ref[...].astype(F32)
        zh = zh_ref[...].astype(F32)
        halo = zh[:, 2 * C_WIDTH:] * zh[:, :C_WIDTH]
        halo = jnp.where(i % tiles_per_seq == 0, 0.0, halo)
        oc, u = _gated_conv(zc[:, :C_WIDTH], zc[:, C_WIDTH:2 * C_WIDTH], zc[:, 2 * C_WIDTH:], halo, cw_ref[...])
        ut_ref[...] = u[tile - 8:, :]
        y1_scr[...] = (
            yp_ref[...]
            + jnp.dot(oa_ref[...], wout_ref[0:A_Q_W, :], preferred_element_type=F32)
            + jnp.dot(ob_ref[...], wout_ref[A_Q_W:A_Q_W + B_W, :], preferred_element_type=F32)
            + jnp.dot(oc.astype(BF16), wout_ref[A_Q_W + B_W:, :], preferred_element_type=F32))

    @pl.when(i >= prompt_tiles)
    def _():
        ut_ref[...] = jnp.zeros_like(ut_ref)
        y1_scr[...] = ys_ref[...] + jnp.dot(ms_ref[...].astype(BF16), wout_ref[...], preferred_element_type=F32)

    y1 = y1_scr[...]
    y1_ref[:, :D_MODEL] = y1
    xn = _rms(y1, gn_ref[...])

    x_hi = xn.astype(BF16)
    x_lo = (xn - x_hi.astype(F32)).astype(BF16)
    wr = wr_ref[...]
    w_hi = wr.astype(BF16)
    w_lo = (wr - w_hi.astype(F32)).astype(BF16)
    lt_hi = _nt_dot(jnp.concatenate([w_hi, w_lo], axis=0), x_hi)
    lt = lt_hi[:ROUTE_ROWS] + lt_hi[ROUTE_ROWS:] + _nt_dot(w_hi, x_lo) + br_ref[...]
    bucket, w_slot0, w_slot1 = _route(lt)
    onehot = (lax.broadcasted_iota(jnp.int32, (ROUTE_ROWS, tile), 0) == bucket).astype(F32)
    upper = (lax.broadcasted_iota(jnp.int32, (tile, tile), 0)
             <= lax.broadcasted_iota(jnp.int32, (tile, tile), 1)).astype(BF16)
    running = jnp.dot(onehot.astype(BF16), upper, preferred_element_type=F32)
    carry = carry_scr[...]
    rank = jnp.sum(onehot * (running - 1.0 + carry), axis=0, keepdims=True)
    carry = carry + jnp.sum(onehot, axis=1, keepdims=True)
    carry_scr[...] = carry
    cnt_ref[...] = jnp.broadcast_to(carry, cnt_ref.shape)
    info_ref[...] = jnp.concatenate([bucket.astype(F32), rank, jnp.zeros((SUBLANES - 2, tile), F32)], axis=0)
    y1_ref[:, D_MODEL:] = jnp.concatenate([w_slot0, w_slot1, jnp.zeros((LANES - 2, tile), F32)], axis=0).T


def _outproj_route(yp, ys, sample_row0, oa, ob, z, mix_s, w_out, cw, gain, w_route, b_route, n_prompt, seq_len):
    n = n_prompt + mix_s.shape[0]
    tiles = n // ROW_TILE
    p_tiles = n_prompt // ROW_TILE
    halo_rows = 16
    halo_blocks = ROW_TILE // halo_rows
    conv_w = 3 * C_WIDTH
    pidx = lambda i: jnp.minimum(i, p_tiles - 1)
    sidx = lambda i: jnp.maximum(i - p_tiles, 0)
    const = lambda a: pl.BlockSpec(a.shape, lambda i: (0,) * a.ndim)
    gain = gain.reshape(1, D_MODEL)
    return pl.pallas_call(
        functools.partial(_outproj_kernel, prompt_tiles=p_tiles, tiles_per_seq=seq_len // ROW_TILE),
        out_shape=(jax.ShapeDtypeStruct((n, ROW_EXT), F32),
                   jax.ShapeDtypeStruct((8, n), F32),
                   jax.ShapeDtypeStruct((ROUTE_ROWS, LANES), F32),
                   jax.ShapeDtypeStruct((tiles * 8, C_WIDTH), F32)),
        grid=(tiles,),
        in_specs=[pl.BlockSpec((ROW_TILE, D_MODEL), lambda i: (pidx(i), 0)),
                  pl.BlockSpec((ROW_TILE, D_MODEL), lambda i: (sample_row0 // ROW_TILE + sidx(i), 0)),
                  pl.BlockSpec((ROW_TILE, A_Q_W), lambda i: (pidx(i), 0)),
                  pl.BlockSpec((ROW_TILE, B_W), lambda i: (pidx(i), 0)),
                  pl.BlockSpec((ROW_TILE, conv_w), lambda i: (pidx(i), 0)),
                  pl.BlockSpec((halo_rows, conv_w), lambda i: (jnp.maximum(pidx(i) * halo_blocks - 1, 0), 0)),
                  pl.BlockSpec((ROW_TILE, D_MODEL), lambda i: (sidx(i), 0)),
                  const(w_out), const(cw), const(gain), const(w_route), const(b_route)],
        out_specs=(pl.BlockSpec((ROW_TILE, ROW_EXT), lambda i: (i, 0)),
                   pl.BlockSpec((8, ROW_TILE), lambda i: (0, i)),
                   pl.BlockSpec((ROUTE_ROWS, LANES), lambda i: (0, 0)),
                   pl.BlockSpec((8, C_WIDTH), lambda i: (i, 0))),
        scratch_shapes=[pltpu.VMEM((ROW_TILE, D_MODEL), F32), pltpu.VMEM((ROUTE_ROWS, 1), F32)],
        compiler_params=_params("arbitrary"),
    )(yp, ys, oa, ob, z, z, mix_s, w_out, cw, gain, w_route, b_route)


def _moe_kernel(e0_ref, e1_ref, used_ref, x_ref, g_ref, *rest, has_final_gain):
    o_ref = rest[-1]
    weights = rest[:6 * TILES_PER_STEP]
    t = pl.program_id(0)

    @pl.when(used_ref[t * TILES_PER_STEP] > 0)
    def _():
        for k in range(TILES_PER_STEP):
            rows = slice(k * MOE_TILE, (k + 1) * MOE_TILE)
            wg1, wu1, wd1, wg2, wu2, wd2 = weights[6 * k:6 * k + 6]
            y1 = x_ref[rows, :D_MODEL]
            w = x_ref[rows, D_MODEL:]
            x = _rms(y1, g_ref[...]).astype(BF16)

            def expert(wg, wu, wd, scale, x=x):
                g = jnp.dot(x, wg[...], preferred_element_type=F32)
                u = jnp.dot(x, wu[...], preferred_element_type=F32)
                h = g * jax.nn.sigmoid(g) * u * scale
                return jnp.dot(h.astype(BF16), wd[...], preferred_element_type=F32)

            y2 = y1 + expert(wg1, wu1, wd1, w[:, 0:1]) + expert(wg2, wu2, wd2, w[:, 1:2])
            o_ref[rows, :] = _rms(y2, rest[-2][...]) if has_final_gain else y2

    @pl.when(used_ref[t * TILES_PER_STEP] == 0)
    def _():
        o_ref[...] = jnp.zeros_like(o_ref)


def _experts(xs, gain, e_slot0, e_slot1, used, w_gate, w_up, w_down, layer, final_gain=None):
    step_rows = TILES_PER_STEP * MOE_TILE
    n_steps = xs.shape[0] // step_rows

    def weight_specs(k):
        pick = lambda sel: (lambda t, e0, e1, u: (layer, (e0, e1)[sel][t * TILES_PER_STEP + k], 0, 0))
        up = lambda sel: pl.BlockSpec((None, None, D_MODEL, EXPERT_FF), pick(sel))
        down = lambda sel: pl.BlockSpec((None, None, EXPERT_FF, D_MODEL), pick(sel))
        return [up(0), up(0), down(0), up(1), up(1), down(1)]

    gain_spec = pl.BlockSpec((1, D_MODEL), lambda t, e0, e1, u: (0, 0))
    extra = [] if final_gain is None else [final_gain.reshape(1, D_MODEL)]
    grid_spec = pltpu.PrefetchScalarGridSpec(
        num_scalar_prefetch=3,
        grid=(n_steps,),
        in_specs=[pl.BlockSpec((step_rows, ROW_EXT), lambda t, e0, e1, u: (t, 0)), gain_spec]
                 + [spec for k in range(TILES_PER_STEP) for spec in weight_specs(k)] + [gain_spec] * len(extra),
        out_specs=pl.BlockSpec((step_rows, D_MODEL), lambda t, e0, e1, u: (t, 0)))
    return pl.pallas_call(
        functools.partial(_moe_kernel, has_final_gain=final_gain is not None),
        out_shape=jax.ShapeDtypeStruct((xs.shape[0], D_MODEL), F32),
        grid_spec=grid_spec,
        compiler_params=_params("arbitrary"),
    )(e_slot0, e_slot1, used, xs, gain.reshape(1, D_MODEL), *([w_gate, w_up, w_down] * (2 * TILES_PER_STEP)), *extra)


def _dispatch_plan(info, counts, n):
    n_tiles = -(-n // MOE_TILE) + N_ROUTE_BUCKETS
    n_tiles = -(-n_tiles // TILES_PER_STEP) * TILES_PER_STEP
    bucket = info[0].astype(jnp.int32)
    rank = info[1].astype(jnp.int32)
    counts = counts[:N_ROUTE_BUCKETS, 0].astype(jnp.int32)
    tiles_per_bucket = (counts + MOE_TILE - 1) // MOE_TILE
    tile_end = jnp.cumsum(tiles_per_bucket)
    row_start = (tile_end - tiles_per_bucket) * MOE_TILE
    dest = row_start[bucket] + rank
    src = (jnp.arange(n_tiles * MOE_TILE, dtype=jnp.int32) % n).at[dest].set(jnp.arange(n, dtype=jnp.int32))
    tile_ids = jnp.arange(n_tiles, dtype=jnp.int32)
    tile_bucket = jnp.minimum(jnp.searchsorted(tile_end, tile_ids, side="right").astype(jnp.int32),
                              N_ROUTE_BUCKETS - 1)
    used = (tile_ids < tile_end[-1]).astype(jnp.int32)
    pair = tile_bucket % len(PAIRS)
    base = (tile_bucket // len(PAIRS)) * EXPERTS_PER_GROUP
    pairs = jnp.asarray(PAIRS, jnp.int32)
    return dest, src, base + pairs[pair, 0], base + pairs[pair, 1], used


def _permute_in_columns(w):
    attn = A_Q_W + 2 * A_KV_W + 3 * B_W
    return jnp.concatenate([w[:, attn:], w[:, A_Q_W:A_Q_W + A_KV_W], w[:, :A_Q_W], w[:, A_Q_W + A_KV_W:attn]],
                           axis=1)


def kernel(x_prompt, x_sample, cache_a_k, cache_a_v, cache_b_k, cache_b_v, state_conv, rel_bias_table,
           w_in, w_out, conv_w, attn_sinks, norm_mix, norm_ffn, w_group, b_group, w_router, b_router,
           w_gate, w_up, w_down, norm_final):
    n_seq, seq_len, _ = x_prompt.shape
    dec_seq, n_new, _ = x_sample.shape
    depth = w_in.shape[0]
    n_prompt = n_seq * seq_len
    n_sample = dec_seq * n_new
    n = n_prompt + n_sample
    la, lb = cache_a_k.shape[2], cache_b_k.shape[2]
    assert seq_len % SPAN == 0 and n_prompt % IN_TILE == 0 and n_sample % IN_TILE == 0 and IN_TILE % ROW_TILE == 0
    assert seq_len >= SPAN and la % LANES == 0 and lb % LANES == 0 and n_new == 8

    table_a, table_b = rel_bias_table[:, :A_Q_HEADS], rel_bias_table[:, A_Q_HEADS:]
    bias_a = _band_bias(table_a, 1)
    bias_b = jnp.stack([_band_bias(table_b, r) for r in DILATIONS], axis=1)
    sbias_a = _sample_bias(table_a, n_new, la, np.arange(la + LANES), A_WINDOW, 1)
    sbias_a = sbias_a.reshape(A_Q_HEADS * n_new, la + LANES)
    sbias_b = jnp.stack([_sample_bias(table_b, n_new, lb, np.arange(lb + LANES), w, r)
                         .reshape(B_HEADS * n_new, lb + LANES) for w, r in B_BRANCHES])
    cak, cav, cbk, cbv = (c.transpose(0, 1, 3, 4, 2) for c in (cache_a_k, cache_a_v, cache_b_k, cache_b_v))

    w_in_b = jnp.stack([_permute_in_columns(w_in[l]) for l in range(depth)]).astype(BF16)
    w_out_b = w_out.astype(BF16)
    expert_w = tuple(w.astype(BF16) for w in (w_gate, w_up, w_down))
    pad = ROUTE_ROWS - N_GROUPS - N_EXPERTS
    w_route = jnp.pad(jnp.concatenate([w_group, w_router], axis=2).transpose(0, 2, 1), ((0, 0), (0, pad), (0, 0)))
    b_route = jnp.pad(jnp.concatenate([b_group, b_router], axis=1), ((0, 0), (0, pad)))[..., None]
    sink_rows = jnp.repeat(attn_sinks, n_new, axis=1)[..., None]

    yp, ys, s_row0 = x_prompt.reshape(n_prompt, D_MODEL), x_sample.reshape(n_sample, D_MODEL), 0
    states = []
    for l in range(depth):
        lap, lbp = min(A_WINDOW, seq_len), min(SPAN, seq_len)
        zh, zf, kbt, vbt = _inproj(yp, ys, s_row0, n_prompt, n_sample, norm_mix[l], w_in_b[l], seq_len, lbp)
        oa = _window_attention(zh, bias_a, attn_sinks[l], n_seq, seq_len)
        ob = _dilated_attention(zf, bias_b, n_seq, seq_len)
        mix_s, conv_s = _sample_mixer(zh, zf, n_prompt, cak, cav, cbk, cbv, l, state_conv[l], conv_w[l], sink_rows[l],
                                      sbias_a, sbias_b, n_new)
        y1, info, counts, u_tail = _outproj_route(
            yp, ys, s_row0, oa, ob, zh, mix_s, w_out_b[l], conv_w[l], norm_ffn[l], w_route[l], b_route[l], n_prompt, seq_len)
        def columns(col):
            return (zh, col) if col < SPLIT else (zf, col - SPLIT)

        def prompt_tail(length, col, width, heads):
            z, c = columns(col)
            rows = [z[(b + 1) * seq_len - length:(b + 1) * seq_len, c:c + width] for b in range(n_seq)]
            return jnp.stack(rows).astype(F32).reshape(n_seq, length, heads, HEAD_DIM)

        def sample_rows(col, width, heads):
            z, c = columns(col)
            return z[n_prompt:, c:c + width].astype(F32).reshape(dec_seq, n_new, heads, HEAD_DIM)

        last_tile = [((b + 1) * seq_len // ROW_TILE - 1) * SUBLANES for b in range(n_seq)]
        conv_p = jnp.stack([u_tail[t + SUBLANES - (CONV_WIDTH - 1):t + SUBLANES, :] for t in last_tile])
        layer_states = (
            prompt_tail(lap, COL_KA, A_KV_W, A_KV_HEADS), prompt_tail(lap, COL_VA, A_KV_W, A_KV_HEADS),
            kbt.reshape(n_seq, B_HEADS, HEAD_DIM, lbp).transpose(0, 3, 1, 2),
            vbt.reshape(n_seq, B_HEADS, HEAD_DIM, lbp).transpose(0, 3, 1, 2), conv_p,
            sample_rows(COL_KA, A_KV_W, A_KV_HEADS), sample_rows(COL_VA, A_KV_W, A_KV_HEADS),
            sample_rows(COL_KB, B_W, B_HEADS), sample_rows(COL_VB, B_W, B_HEADS), conv_s)

        dest, src, e_lo, e_hi, used = _dispatch_plan(info, counts, n)
        xs = y1[src]
        if l == depth - 1:
            xs, layer_states = lax.optimization_barrier((xs, layer_states))
        y_sorted = _experts(xs, norm_ffn[l], e_lo, e_hi, used, *expert_w, l,
                            final_gain=norm_final if l == depth - 1 else None)
        if l < depth - 1:
            y_next, layer_states = lax.optimization_barrier((y_sorted[dest], layer_states))
            yp, ys, s_row0 = y_next, y_next, n_prompt
        states.append(layer_states)

    y_prompt = y_sorted[dest[:n_prompt]].reshape(n_seq, seq_len, D_MODEL)
    y_sample = y_sorted[dest[n_prompt:]].reshape(dec_seq, n_new, D_MODEL)
    st = [jnp.stack([s[k] for s in states]) for k in range(10)]
    return (y_prompt, y_sample, st[0], st[1], st[2], st[3], st[4], st[5], st[6], st[7], st[8], st[9])
```

```python
import functools
import math

import numpy as np
import jax
import jax.numpy as jnp
from jax import lax
from jax.experimental import pallas as pl
from jax.experimental.pallas import tpu as pltpu

F32 = jnp.float32
BF16 = jnp.bfloat16

D_MODEL = 1024
HEAD_DIM = 64
ATTN_SCALE = HEAD_DIM ** -0.5
LOG2E = math.log2(math.e)
LN2 = math.log(2.0)
BAND_Q_SCALE = ATTN_SCALE * LOG2E
BLOCK = 128
LANES = 128
SUBLANES = 8
A_Q_HEADS = 4
A_KV_HEADS = 2
A_WINDOW = 128
B_HEADS = 6
B_BRANCHES = ((128, 1), (512, 4), (2048, 16))
DILATIONS = tuple(r for _, r in B_BRANCHES)
SPAN = BLOCK * max(DILATIONS)
C_WIDTH = 6 * HEAD_DIM
CONV_WIDTH = 3
A_Q_W = A_Q_HEADS * HEAD_DIM
A_KV_W = A_KV_HEADS * HEAD_DIM
B_W = B_HEADS * HEAD_DIM
IN_WIDTH = A_Q_W + 2 * A_KV_W + 3 * B_W + 3 * C_WIDTH
N_BUCKETS = 32
MAX_DISTANCE = 2048
N_GROUPS = 4
EXPERTS_PER_GROUP = 4
N_EXPERTS = N_GROUPS * EXPERTS_PER_GROUP
EXPERT_FF = 512
RMS_EPS = 1e-6
NEG_INF = -1e30

COL_CONV = 0
COL_KA = 3 * C_WIDTH
COL_QA = COL_KA + A_KV_W
COL_VA = COL_QA + A_Q_W
COL_QB = COL_VA + A_KV_W
COL_KB = COL_QB + B_W
COL_VB = COL_KB + B_W
SPLIT = COL_QB

ROW_TILE = 512
IN_TILE = 1024
MOE_TILE = 256
TILES_PER_STEP = 2
SEQS_PER_STEP = 4
DECODE_VMEM_LIMIT = 60 * 1024 * 1024
PAIRS = ((1, 0), (2, 0), (2, 1), (3, 1), (3, 0), (3, 2))
N_ROUTE_BUCKETS = N_GROUPS * len(PAIRS)
ROUTE_ROWS = 32
ROW_EXT = D_MODEL + LANES
VMEM_LIMIT = 56 * 1024 * 1024


def _params(*sem, vmem_limit=VMEM_LIMIT):
    return pltpu.CompilerParams(dimension_semantics=sem, vmem_limit_bytes=vmem_limit)


def _rel_bucket(dist):
    d = np.maximum(dist, 0)
    max_exact = N_BUCKETS // 2
    df = np.maximum(d, max_exact).astype(np.float32)
    large = max_exact + (np.log(df / max_exact) / math.log(MAX_DISTANCE / max_exact)
                         * (N_BUCKETS - max_exact)).astype(np.int32)
    large = np.minimum(large, N_BUCKETS - 1)
    return np.where(d < max_exact, d, large)


def _masked_bias(table, dist, valid):
    bucket = _rel_bucket(dist).reshape(-1, 1)
    onehot = (jnp.asarray(bucket, jnp.int32) == jnp.arange(N_BUCKETS, dtype=jnp.int32)[None, :]).astype(F32)
    b = jnp.dot(onehot, table.astype(F32), precision=lax.Precision.HIGHEST)
    b = jnp.where(jnp.asarray(valid.reshape(-1, 1)), b, NEG_INF)
    return jnp.moveaxis(b.reshape(dist.shape + (table.shape[1],)), -1, 0)


def _band_bias(table, scale):
    i = np.arange(BLOCK)[:, None]
    j = np.arange(2 * BLOCK)[None, :]
    dist = i + BLOCK - j
    b = _masked_bias(table, dist * scale, (dist >= 0) & (dist <= BLOCK)) * LOG2E
    return b.reshape(table.shape[1] // 2, 2 * BLOCK, 2 * BLOCK)


def _sample_bias(table, n_new, first_new, positions, window, dilation):
    dist = first_new + np.arange(n_new)[:, None] - np.asarray(positions)[None, :]
    valid = (dist >= 0) & (dist <= window) & (dist % dilation == 0)
    return _masked_bias(table, dist, valid)


def _rms(x, gain):
    return x * lax.rsqrt(jnp.mean(x * x, axis=-1, keepdims=True) + RMS_EPS) * gain


def _inproj_kernel(xp_ref, xs_ref, g_ref, w_ref, zh_ref, zf_ref, kbt_ref, vbt_ref, *, prompt_tiles, tiles_per_seq,
                   tail_tiles):
    i = pl.program_id(0)

    def project(x_ref):
        h = _rms(x_ref[...], g_ref[...])
        z = jnp.dot(h.astype(BF16), w_ref[...], preferred_element_type=F32)
        zh_ref[...] = z[:, :SPLIT].astype(BF16)
        zf_ref[...] = z[:, SPLIT:]

    pl.when(i < prompt_tiles)(lambda: project(xp_ref))
    pl.when(i >= prompt_tiles)(lambda: project(xs_ref))

    @pl.when((i < prompt_tiles) & (i % tiles_per_seq >= tiles_per_seq - tail_tiles))
    def _():
        kbt_ref[...] = zf_ref[:, COL_KB - SPLIT:COL_KB - SPLIT + B_W].T
        vbt_ref[...] = zf_ref[:, COL_VB - SPLIT:COL_VB - SPLIT + B_W].T


def _inproj(xp, xs, sample_row0, n_prompt, n_sample, gain, w, seq_len, tail_len):
    p_tiles, s_tile0 = n_prompt // IN_TILE, sample_row0 // IN_TILE
    tps, tail = seq_len // IN_TILE, tail_len // IN_TILE
    n_seq = n_prompt // seq_len

    def tail_block(i):
        j = jnp.minimum(i, p_tiles - 1)
        return j // tps, 0, jnp.maximum(j % tps - (tps - tail), 0)

    tail_spec = pl.BlockSpec((None, B_W, IN_TILE), tail_block)
    tail_shape = jax.ShapeDtypeStruct((n_seq, B_W, tail_len), F32)
    return pl.pallas_call(
        functools.partial(_inproj_kernel, prompt_tiles=p_tiles, tiles_per_seq=tps, tail_tiles=tail),
        out_shape=(jax.ShapeDtypeStruct((n_prompt + n_sample, SPLIT), BF16),
                   jax.ShapeDtypeStruct((n_prompt + n_sample, IN_WIDTH - SPLIT), F32), tail_shape, tail_shape),
        grid=((n_prompt + n_sample) // IN_TILE,),
        in_specs=[pl.BlockSpec((IN_TILE, D_MODEL), lambda i: (jnp.minimum(i, p_tiles - 1), 0)),
                  pl.BlockSpec((IN_TILE, D_MODEL), lambda i: (s_tile0 + jnp.maximum(i - p_tiles, 0), 0)),
                  pl.BlockSpec((1, D_MODEL), lambda i: (0, 0)),
                  pl.BlockSpec((D_MODEL, IN_WIDTH), lambda i: (0, 0))],
        out_specs=(pl.BlockSpec((IN_TILE, SPLIT), lambda i: (i, 0)),
                   pl.BlockSpec((IN_TILE, IN_WIDTH - SPLIT), lambda i: (i, 0)), tail_spec, tail_spec),
        compiler_params=_params("arbitrary"),
    )(xp, xs, gain.reshape(1, D_MODEL), w)


def _lane_half(shape):
    return lax.broadcasted_iota(jnp.int32, shape, len(shape) - 1) // HEAD_DIM


def _nt_dot(a, b):
    return lax.dot_general(a, b, (((1,), (1,)), ((), ())), preferred_element_type=F32)

def _band_pair(q2, k2, v2, bias2, penalty):
    lg = _nt_dot(q2, k2) + bias2
    if penalty is not None:
        lg = lg + penalty
    m = jnp.max(lg, axis=-1, keepdims=True)
    p = jnp.exp2(lg - m)
    s = jnp.sum(p, axis=-1, keepdims=True)
    acc = jnp.dot(p.astype(BF16), v2, preferred_element_type=F32)
    return acc / s, (m + jnp.log2(s)) * LN2


def _prev_key_penalty(first_span):
    col = lax.broadcasted_iota(jnp.int32, (1, 2 * BLOCK), 1)
    return jnp.where((col < BLOCK) & first_span, NEG_INF, 0.0).astype(F32)


def _rows(start, dilation):
    if dilation == 1:
        return pl.ds(start, BLOCK)
    return pl.ds(start, BLOCK, stride=dilation)


def _dilated_kernel(q_ref, kc_ref, kp_ref, vc_ref, vp_ref, bias_ref, o_ref, o_scr, l_scr, first_bias_scr):
    first_half = _lane_half((BLOCK, LANES)) == 0
    penalty = _prev_key_penalty(pl.program_id(1) == 0)
    for bi in range(len(DILATIONS)):
        first_bias_scr[bi] = bias_ref[bi] + penalty

    for bi, r in enumerate(DILATIONS):
        step = r * BLOCK

        def block(c, n, first, bi=bi, r=r, step=step):
            q_start = c + n * step
            if not isinstance(q_start, int):
                q_start = pl.multiple_of(q_start, BLOCK) if r == 1 else q_start
            cur = _rows(q_start, r)
            qt = (q_ref[cur, :] * BAND_Q_SCALE).astype(BF16)
            if first:
                src_k, src_v, prev = kp_ref, vp_ref, _rows(SPAN - step + c, r)
            else:
                p_start = q_start - step
                if r == 1:
                    p_start = pl.multiple_of(p_start, BLOCK)
                src_k, src_v, prev = kc_ref, vc_ref, _rows(p_start, r)
            k2 = jnp.concatenate([src_k[prev, :], kc_ref[cur, :]], axis=0).astype(BF16)
            v2 = jnp.concatenate([src_v[prev, :], vc_ref[cur, :]], axis=0).astype(BF16)
            zero = jnp.zeros_like(qt)
            q2 = jnp.concatenate([jnp.where(first_half, qt, zero), jnp.where(first_half, zero, qt)], axis=0)
            o, lse = _band_pair(q2, k2, v2, first_bias_scr[bi] if first else bias_ref[bi], None)
            o_scr[bi, cur, :] = jnp.where(first_half, o[:BLOCK], o[BLOCK:])
            l_scr[bi, cur, :] = jnp.where(first_half, lse[:BLOCK], lse[BLOCK:])

        n_blocks = SPAN // step
        if r == 1:
            block(0, 0, True)
            lax.fori_loop(1, n_blocks, lambda n, _: block(0, n, False), None, unroll=True)
        else:
            def per_class(c, _, block=block, n_blocks=n_blocks):
                block(c, 0, True)
                for n in range(1, n_blocks):
                    block(c, n, False)
            lax.fori_loop(0, r, per_class, None, unroll=True)

    chunk = 2 * BLOCK

    def combine(j, _):
        rows = pl.ds(pl.multiple_of(j * chunk, chunk), chunk)
        ls = [l_scr[bi, rows, :] for bi in range(len(DILATIONS))]
        m = functools.reduce(jnp.maximum, ls)
        ws = [jnp.exp(l - m) for l in ls]
        num = sum(w * o_scr[bi, rows, :] for bi, w in enumerate(ws))
        o_ref[rows, :] = (num / sum(ws)).astype(o_ref.dtype)

    lax.fori_loop(0, SPAN // chunk, combine, None)


def _dilated_attention(z, bias, n_seq, seq_len):
    spans = seq_len // SPAN
    qb, kb, vb = ((c - SPLIT) // LANES for c in (COL_QB, COL_KB, COL_VB))
    blk = (SPAN, LANES)
    cur = lambda col: (lambda b, s, i: (b * spans + s, col + i))
    prev = lambda col: (lambda b, s, i: (b * spans + jnp.maximum(s - 1, 0), col + i))
    return pl.pallas_call(
        _dilated_kernel,
        out_shape=jax.ShapeDtypeStruct((n_seq * seq_len, B_W), BF16),
        grid=(n_seq, spans, B_W // LANES),
        in_specs=[pl.BlockSpec(blk, cur(qb)),
                  pl.BlockSpec(blk, cur(kb)), pl.BlockSpec(blk, prev(kb)),
                  pl.BlockSpec(blk, cur(vb)), pl.BlockSpec(blk, prev(vb)),
                  pl.BlockSpec((None, len(DILATIONS), 2 * BLOCK, 2 * BLOCK), lambda b, s, i: (i, 0, 0, 0))],
        out_specs=pl.BlockSpec(blk, lambda b, s, i: (b * spans + s, i)),
        scratch_shapes=[pltpu.VMEM((len(DILATIONS), SPAN, LANES), F32),
                        pltpu.VMEM((len(DILATIONS), SPAN, LANES), F32),
                        pltpu.VMEM((len(DILATIONS), 2 * BLOCK, 2 * BLOCK), F32)],
        compiler_params=_params("parallel", "parallel", "parallel"),
    )(z, z, z, z, z, bias)


def _window_kernel(sink_ref, q_ref, kc_ref, kp_ref, vc_ref, vp_ref, bias_ref, o_ref):
    penalty = _prev_key_penalty(pl.program_id(1) == 0)
    first_half = _lane_half((BLOCK, LANES)) == 0
    first_half2 = _lane_half((2 * BLOCK, LANES)) == 0

    def block(n, first):
        start = n * BLOCK if isinstance(n, int) else pl.multiple_of(n * BLOCK, BLOCK)
        cur = pl.ds(start, BLOCK)
        if first:
            src_k, src_v, prev = kp_ref, vp_ref, pl.ds(SPAN - BLOCK, BLOCK)
        else:
            src_k, src_v, prev = kc_ref, vc_ref, pl.ds(pl.multiple_of(start - BLOCK, BLOCK), BLOCK)
        k2 = jnp.concatenate([src_k[prev, :], kc_ref[cur, :]], axis=0)
        vf = jnp.concatenate([src_v[prev, :], vc_ref[cur, :]], axis=0).astype(F32)
        v_swap = pltpu.roll(vf, HEAD_DIM, 1)
        for i in range(A_KV_HEADS):
            qf = q_ref[cur, i * LANES:(i + 1) * LANES].astype(F32) * BAND_Q_SCALE
            q_swap = pltpu.roll(qf, HEAD_DIM, 1)
            q2 = jnp.concatenate([jnp.where(first_half == (i == 0), qf if a == i else q_swap, 0.0)
                                  for a in range(2)], axis=0).astype(BF16)
            v2 = jnp.where(first_half2 == (i == 0), vf, v_swap).astype(BF16)
            o, lse = _band_pair(q2, k2, v2, bias_ref[i], penalty if first else None)
            outs = [o[a * BLOCK:(a + 1) * BLOCK] * jax.nn.sigmoid(lse[a * BLOCK:(a + 1) * BLOCK] - sink_ref[2 * i + a])
                    for a in range(2)]
            o_ref[cur, i * LANES:(i + 1) * LANES] = jnp.where(first_half, outs[0], outs[1]).astype(o_ref.dtype)

    block(0, True)
    lax.fori_loop(1, SPAN // BLOCK, lambda n, _: block(n, False), None, unroll=5)


def _window_attention(z, bias, sinks, n_seq, seq_len):
    spans = seq_len // SPAN
    ka, va = COL_KA // LANES, COL_VA // LANES
    blk = (SPAN, LANES)
    cur = lambda col: (lambda b, s: (b * spans + s, col))
    prev = lambda col: (lambda b, s: (b * spans + jnp.maximum(s - 1, 0), col))
    return pl.pallas_call(
        _window_kernel,
        out_shape=jax.ShapeDtypeStruct((n_seq * seq_len, A_Q_W), BF16),
        grid=(n_seq, spans),
        in_specs=[pl.BlockSpec(memory_space=pltpu.SMEM),
                  pl.BlockSpec((SPAN, A_Q_W), lambda b, s: (b * spans + s, COL_QA // A_Q_W)),
                  pl.BlockSpec(blk, cur(ka)), pl.BlockSpec(blk, prev(ka)),
                  pl.BlockSpec(blk, cur(va)), pl.BlockSpec(blk, prev(va)),
                  pl.BlockSpec((A_KV_HEADS, 2 * BLOCK, 2 * BLOCK), lambda b, s: (0, 0, 0))],
        out_specs=pl.BlockSpec((SPAN, A_Q_W), lambda b, s: (b * spans + s, 0)),
        compiler_params=_params("parallel", "parallel"),
    )(sinks, z, z, z, z, z, bias)


def _shift_rows(u, filler, k):
    rolled = pltpu.roll(u, k, 0)
    row = lax.broadcasted_iota(jnp.int32, u.shape, 0)
    n_fill = filler.shape[0]
    for j in range(k):
        rolled = jnp.where(row == j, filler[n_fill - k + j:n_fill - k + j + 1, :], rolled)
    return rolled


def _gated_conv(xc, bg, cg, filler, cw):
    u = cg * xc
    conv = cw[0:1, :] * _shift_rows(u, filler, 2) + cw[1:2, :] * _shift_rows(u, filler, 1) + cw[2:3, :] * u
    return bg * conv, u


def _pad_rows(x, rows):
    return jnp.concatenate([x, jnp.zeros((rows - x.shape[0], x.shape[1]), x.dtype)], axis=0)


def _heads_by_dim(ref, j):
    _, h, d, length = ref.shape
    return ref[j].reshape(h * d, length).astype(BF16)


def _sample_kernel(zh_ref, zf_ref, cak_ref, cav_ref, cbk_ref, cbv_ref, st_ref, cw_ref, sink_ref, bias_a_ref,
                   bias_b_ref, mix_ref, conv_ref, *, n_new):
    z_all = jnp.concatenate([zh_ref[...].astype(F32), zf_ref[...]], axis=1)
    for j in range(z_all.shape[0] // n_new):
        _sample_sequence(j, z_all, cak_ref, cav_ref, cbk_ref, cbv_ref, st_ref, cw_ref, sink_ref, bias_a_ref,
                         bias_b_ref, mix_ref, conv_ref, n_new)


def _sample_sequence(j, z_all, cak_ref, cav_ref, cbk_ref, cbv_ref, st_ref, cw_ref, sink_ref, bias_a_ref,
                     bias_b_ref, mix_ref, conv_ref, n_new):
    tokens = slice(j * n_new, (j + 1) * n_new)
    z = z_all[tokens, :]
    la = cak_ref.shape[3]
    lb = cbk_ref.shape[3]

    ka_new = _pad_rows(z[:, COL_KA:COL_KA + A_KV_W], LANES).astype(BF16)
    va_new = _pad_rows(z[:, COL_VA:COL_VA + A_KV_W], LANES).astype(BF16)
    half = _lane_half((n_new, LANES))
    pieces = []
    for i in range(A_KV_HEADS):
        qf = z[:, COL_QA + i * LANES:COL_QA + (i + 1) * LANES] * ATTN_SCALE
        for a in range(2):
            pieces.append(jnp.where(half == i, qf if a == i else pltpu.roll(qf, HEAD_DIM, 1), 0.0))
    qa = jnp.concatenate(pieces, axis=0).astype(BF16)
    lc = jnp.dot(qa, _heads_by_dim(cak_ref, j), preferred_element_type=F32) + bias_a_ref[:, :la]
    ln = _nt_dot(qa, ka_new) + bias_a_ref[:, la:]
    m = jnp.maximum(jnp.max(lc, axis=-1, keepdims=True), jnp.max(ln, axis=-1, keepdims=True))
    pc = jnp.exp(lc - m)
    pn = jnp.exp(ln - m)
    s = jnp.sum(pc, axis=-1, keepdims=True) + jnp.sum(pn, axis=-1, keepdims=True)
    oa = (_nt_dot(pc.astype(BF16), _heads_by_dim(cav_ref, j))
          + jnp.dot(pn.astype(BF16), va_new, preferred_element_type=F32))
    oa = oa / s * jax.nn.sigmoid(m + jnp.log(s) - sink_ref[...])
    oa_blocks = []
    for i in range(A_KV_HEADS):
        per_half = []
        for a in range(2):
            rows = oa[(2 * i + a) * n_new:(2 * i + a + 1) * n_new, :]
            per_half.append(rows if a == i else pltpu.roll(rows, HEAD_DIM, 1))
        oa_blocks.append(jnp.where(half == 0, per_half[0], per_half[1]))

    qf = z[:, COL_QB:COL_QB + B_W] * ATTN_SCALE
    head_of_lane = _lane_half((n_new, B_W))
    qb = jnp.concatenate([jnp.where(head_of_lane == h, qf, 0.0) for h in range(B_HEADS)], axis=0).astype(BF16)
    kb_new = _pad_rows(z[:, COL_KB:COL_KB + B_W], LANES).astype(BF16)
    vb_new = _pad_rows(z[:, COL_VB:COL_VB + B_W], LANES).astype(BF16)
    lg_c = jnp.dot(qb, _heads_by_dim(cbk_ref, j), preferred_element_type=F32)
    lg_n = _nt_dot(qb, kb_new)
    parts = []
    for bi, (w, r) in enumerate(B_BRANCHES):
        lo = lb - min(lb, -(-w // LANES) * LANES)
        lc = lg_c[:, lo:] + bias_b_ref[bi, :, lo:lb]
        ln = lg_n + bias_b_ref[bi, :, lb:]
        m = jnp.maximum(jnp.max(lc, axis=-1, keepdims=True), jnp.max(ln, axis=-1, keepdims=True))
        pc = jnp.exp(lc - m)
        pn = jnp.exp(ln - m)
        s = jnp.sum(pc, axis=-1, keepdims=True) + jnp.sum(pn, axis=-1, keepdims=True)
        parts.append((lo, pc, pn, s, m + jnp.log(s)))
    m_all = functools.reduce(jnp.maximum, [p[4] for p in parts])
    ws = [jnp.exp(p[4] - m_all) for p in parts]
    den = sum(ws)
    p_new = None
    los = sorted({p[0] for p in parts} | {lb})
    segs = [None] * (len(los) - 1)
    for (lo, pc, pn, s, _), w in zip(parts, ws):
        coef = w / (den * s)
        p_new = coef * pn if p_new is None else p_new + coef * pn
        for si in range(len(segs)):
            a0, a1 = los[si], los[si + 1]
            if a0 >= lo:
                piece = coef * pc[:, a0 - lo:a1 - lo]
                segs[si] = piece if segs[si] is None else segs[si] + piece
    p_cache = jnp.concatenate(segs, axis=1) if len(segs) > 1 else segs[0]
    ob = (_nt_dot(p_cache.astype(BF16), _heads_by_dim(cbv_ref, j))
          + jnp.dot(p_new.astype(BF16), vb_new, preferred_element_type=F32))
    ob_rows = sum(jnp.where(head_of_lane == h, ob[h * n_new:(h + 1) * n_new, :], 0.0) for h in range(B_HEADS))

    cz = z[:, COL_CONV:COL_CONV + 3 * C_WIDTH]
    oc, u = _gated_conv(cz[:, :C_WIDTH], cz[:, C_WIDTH:2 * C_WIDTH], cz[:, 2 * C_WIDTH:], st_ref[j], cw_ref[...])
    conv_ref[j] = u[n_new - (CONV_WIDTH - 1):, :]
    mix_ref[tokens, :] = jnp.concatenate(oa_blocks + [ob_rows, oc], axis=1)


def _sample_mixer(zh, zf, row0, cak, cav, cbk, cbv, layer, state, cw, sink_rows, bias_a, bias_b, n_new):
    n_seq = cak.shape[1]
    g = SEQS_PER_STEP
    rows = g * n_new
    blk0 = row0 // rows
    cache = lambda a: pl.BlockSpec((None, g) + a.shape[2:], lambda b: (layer, b, 0, 0, 0))
    per_seq = lambda shape: pl.BlockSpec((g,) + shape, lambda b: (b, 0, 0))
    const = lambda a: pl.BlockSpec(a.shape, lambda b: (0,) * a.ndim)
    return pl.pallas_call(
        functools.partial(_sample_kernel, n_new=n_new),
        out_shape=(jax.ShapeDtypeStruct((n_seq * n_new, D_MODEL), F32),
                   jax.ShapeDtypeStruct((n_seq, CONV_WIDTH - 1, C_WIDTH), F32)),
        grid=(n_seq // g,),
        in_specs=[pl.BlockSpec((rows, SPLIT), lambda b: (blk0 + b, 0)),
                  pl.BlockSpec((rows, IN_WIDTH - SPLIT), lambda b: (blk0 + b, 0)),
                  cache(cak), cache(cav), cache(cbk), cache(cbv),
                  per_seq((CONV_WIDTH - 1, C_WIDTH)), const(cw), const(sink_rows), const(bias_a), const(bias_b)],
        out_specs=(pl.BlockSpec((rows, D_MODEL), lambda b: (b, 0)), per_seq((CONV_WIDTH - 1, C_WIDTH))),
        compiler_params=_params("parallel", vmem_limit=DECODE_VMEM_LIMIT),
    )(zh, zf, cak, cav, cbk, cbv, state, cw, sink_rows, bias_a, bias_b)


def _first_index(vals, best):
    idx = jnp.full(best.shape, len(vals) - 1, jnp.int32)
    for j in range(len(vals) - 2, -1, -1):
        idx = jnp.where(vals[j] == best, j, idx)
    return idx


def _route(lt):
    g = [lt[k:k + 1, :] for k in range(N_GROUPS)]
    g_max = functools.reduce(jnp.maximum, g)
    g_idx = _first_index(g, g_max)
    g_w = 1.0 / sum(jnp.exp(v - g_max) for v in g)
    e = []
    for j in range(EXPERTS_PER_GROUP):
        v = lt[N_GROUPS + j:N_GROUPS + j + 1, :]
        for gi in range(1, N_GROUPS):
            row = N_GROUPS + gi * EXPERTS_PER_GROUP + j
            v = jnp.where(g_idx == gi, lt[row:row + 1, :], v)
        e.append(v)
    e1 = functools.reduce(jnp.maximum, e)
    i1 = _first_index(e, e1)
    rest = [jnp.where(i1 == j, -jnp.inf, e[j]) for j in range(EXPERTS_PER_GROUP)]
    e2 = functools.reduce(jnp.maximum, rest)
    i2 = _first_index(rest, e2)
    t = jnp.exp(e2 - e1)
    w1 = g_w / (1.0 + t)
    w2 = g_w * t / (1.0 + t)
    swap = i2 < i1
    lo = jnp.where(swap, i2, i1)
    hi = jnp.where(swap, i1, i2)
    pair = jnp.zeros_like(lo)
    for p, (slot0, slot1) in enumerate(PAIRS):
        pair = jnp.where((hi == slot0) & (lo == slot1), p, pair)
    bucket = g_idx * len(PAIRS) + pair
    return bucket, jnp.where(swap, w1, w2), jnp.where(swap, w2, w1)


def _outproj_kernel(yp_ref, ys_ref, oa_ref, ob_ref, zc_ref, zh_ref, ms_ref, wout_ref, cw_ref, gn_ref, wr_ref,
                    br_ref, y1_ref, info_ref, cnt_ref, ut_ref, y1_scr, carry_scr, *, prompt_tiles, tiles_per_seq):
    i = pl.program_id(0)
    tile = yp_ref.shape[0]

    @pl.when(i == 0)
    def _():
        carry_scr[...] = jnp.zeros_like(carry_scr)

    @pl.when(i < prompt_tiles)
    def _():
        zc = zc_ref[...].astype(F32)
        zh = zh_ref[...].astype(F32)
        halo = zh[:, 2 * C_WIDTH:] * zh[:, :C_WIDTH]
        halo = jnp.where(i % tiles_per_seq == 0, 0.0, halo)
        oc, u = _gated_conv(zc[:, :C_WIDTH], zc[:, C_WIDTH:2 * C_WIDTH], zc[:, 2 * C_WIDTH:], halo, cw_ref[...])
        ut_ref[...] = u[tile - 8:, :]
        y1_scr[...] = (
            yp_ref[...]
            + jnp.dot(oa_ref[...], wout_ref[0:A_Q_W, :], preferred_element_type=F32)
            + jnp.dot(ob_ref[...], wout_ref[A_Q_W:A_Q_W + B_W, :], preferred_element_type=F32)
            + jnp.dot(oc.astype(BF16), wout_ref[A_Q_W + B_W:, :], preferred_element_type=F32))

    @pl.when(i >= prompt_tiles)
    def _():
        ut_ref[...] = jnp.zeros_like(ut_ref)
        y1_scr[...] = ys_ref[...] + jnp.dot(ms_ref[...].astype(BF16), wout_ref[...], preferred_element_type=F32)

    y1 = y1_scr[...]
    y1_ref[:, :D_MODEL] = y1
    xn = _rms(y1, gn_ref[...])

    x_hi = xn.astype(BF16)
    x_lo = (xn - x_hi.astype(F32)).astype(BF16)
    wr = wr_ref[...]
    w_hi = wr.astype(BF16)
    w_lo = (wr - w_hi.astype(F32)).astype(BF16)
    lt_hi = _nt_dot(jnp.concatenate([w_hi, w_lo], axis=0), x_hi)
    lt = lt_hi[:ROUTE_ROWS] + lt_hi[ROUTE_ROWS:] + _nt_dot(w_hi, x_lo) + br_ref[...]
    bucket, w_slot0, w_slot1 = _route(lt)
    onehot = (lax.broadcasted_iota(jnp.int32, (ROUTE_ROWS, tile), 0) == bucket).astype(F32)
    upper = (lax.broadcasted_iota(jnp.int32, (tile, tile), 0)
             <= lax.broadcasted_iota(jnp.int32, (tile, tile), 1)).astype(BF16)
    running = jnp.dot(onehot.astype(BF16), upper, preferred_element_type=F32)
    carry = carry_scr[...]
    rank = jnp.sum(onehot * (running - 1.0 + carry), axis=0, keepdims=True)
    carry = carry + jnp.sum(onehot, axis=1, keepdims=True)
    carry_scr[...] = carry
    cnt_ref[...] = jnp.broadcast_to(carry, cnt_ref.shape)
    info_ref[...] = jnp.concatenate([bucket.astype(F32), rank, jnp.zeros((SUBLANES - 2, tile), F32)], axis=0)
    y1_ref[:, D_MODEL:] = jnp.concatenate([w_slot0, w_slot1, jnp.zeros((LANES - 2, tile), F32)], axis=0).T


def _outproj_route(yp, ys, sample_row0, oa, ob, z, mix_s, w_out, cw, gain, w_route, b_route, n_prompt, seq_len):
    n = n_prompt + mix_s.shape[0]
    tiles = n // ROW_TILE
    p_tiles = n_prompt // ROW_TILE
    halo_rows = 16
    halo_blocks = ROW_TILE // halo_rows
    conv_w = 3 * C_WIDTH
    pidx = lambda i: jnp.minimum(i, p_tiles - 1)
    sidx = lambda i: jnp.maximum(i - p_tiles, 0)
    const = lambda a: pl.BlockSpec(a.shape, lambda i: (0,) * a.ndim)
    gain = gain.reshape(1, D_MODEL)
    return pl.pallas_call(
        functools.partial(_outproj_kernel, prompt_tiles=p_tiles, tiles_per_seq=seq_len // ROW_TILE),
        out_shape=(jax.ShapeDtypeStruct((n, ROW_EXT), F32),
                   jax.ShapeDtypeStruct((8, n), F32),
                   jax.ShapeDtypeStruct((ROUTE_ROWS, LANES), F32),
                   jax.ShapeDtypeStruct((tiles * 8, C_WIDTH), F32)),
        grid=(tiles,),
        in_specs=[pl.BlockSpec((ROW_TILE, D_MODEL), lambda i: (pidx(i), 0)),
                  pl.BlockSpec((ROW_TILE, D_MODEL), lambda i: (sample_row0 // ROW_TILE + sidx(i), 0)),
                  pl.BlockSpec((ROW_TILE, A_Q_W), lambda i: (pidx(i), 0)),
                  pl.BlockSpec((ROW_TILE, B_W), lambda i: (pidx(i), 0)),
                  pl.BlockSpec((ROW_TILE, conv_w), lambda i: (pidx(i), 0)),
                  pl.BlockSpec((halo_rows, conv_w), lambda i: (jnp.maximum(pidx(i) * halo_blocks - 1, 0), 0)),
                  pl.BlockSpec((ROW_TILE, D_MODEL), lambda i: (sidx(i), 0)),
                  const(w_out), const(cw), const(gain), const(w_route), const(b_route)],
        out_specs=(pl.BlockSpec((ROW_TILE, ROW_EXT), lambda i: (i, 0)),
                   pl.BlockSpec((8, ROW_TILE), lambda i: (0, i)),
                   pl.BlockSpec((ROUTE_ROWS, LANES), lambda i: (0, 0)),
                   pl.BlockSpec((8, C_WIDTH), lambda i: (i, 0))),
        scratch_shapes=[pltpu.VMEM((ROW_TILE, D_MODEL), F32), pltpu.VMEM((ROUTE_ROWS, 1), F32)],
        compiler_params=_params("arbitrary"),
    )(yp, ys, oa, ob, z, z, mix_s, w_out, cw, gain, w_route, b_route)


def _moe_kernel(e0_ref, e1_ref, used_ref, x_ref, g_ref, *rest, has_final_gain):
    o_ref = rest[-1]
    weights = rest[:6 * TILES_PER_STEP]
    t = pl.program_id(0)

    @pl.when(used_ref[t * TILES_PER_STEP] > 0)
    def _():
        for k in range(TILES_PER_STEP):
            rows = slice(k * MOE_TILE, (k + 1) * MOE_TILE)
            wg1, wu1, wd1, wg2, wu2, wd2 = weights[6 * k:6 * k + 6]
            y1 = x_ref[rows, :D_MODEL]
            w = x_ref[rows, D_MODEL:]
            x = _rms(y1, g_ref[...]).astype(BF16)

            def expert(wg, wu, wd, scale, x=x):
                g = jnp.dot(x, wg[...], preferred_element_type=F32)
                u = jnp.dot(x, wu[...], preferred_element_type=F32)
                h = g * jax.nn.sigmoid(g) * u * scale
                return jnp.dot(h.astype(BF16), wd[...], preferred_element_type=F32)

            y2 = y1 + expert(wg1, wu1, wd1, w[:, 0:1]) + expert(wg2, wu2, wd2, w[:, 1:2])
            o_ref[rows, :] = _rms(y2, rest[-2][...]) if has_final_gain else y2

    @pl.when(used_ref[t * TILES_PER_STEP] == 0)
    def _():
        o_ref[...] = jnp.zeros_like(o_ref)


def _experts(xs, gain, e_slot0, e_slot1, used, w_gate, w_up, w_down, layer, final_gain=None):
    step_rows = TILES_PER_STEP * MOE_TILE
    n_steps = xs.shape[0] // step_rows

    def weight_specs(k):
        pick = lambda sel: (lambda t, e0, e1, u: (layer, (e0, e1)[sel][t * TILES_PER_STEP + k], 0, 0))
        up = lambda sel: pl.BlockSpec((None, None, D_MODEL, EXPERT_FF), pick(sel))
        down = lambda sel: pl.BlockSpec((None, None, EXPERT_FF, D_MODEL), pick(sel))
        return [up(0), up(0), down(0), up(1), up(1), down(1)]

    gain_spec = pl.BlockSpec((1, D_MODEL), lambda t, e0, e1, u: (0, 0))
    extra = [] if final_gain is None else [final_gain.reshape(1, D_MODEL)]
    grid_spec = pltpu.PrefetchScalarGridSpec(
        num_scalar_prefetch=3,
        grid=(n_steps,),
        in_specs=[pl.BlockSpec((step_rows, ROW_EXT), lambda t, e0, e1, u: (t, 0)), gain_spec]
                 + [spec for k in range(TILES_PER_STEP) for spec in weight_specs(k)] + [gain_spec] * len(extra),
        out_specs=pl.BlockSpec((step_rows, D_MODEL), lambda t, e0, e1, u: (t, 0)))
    return pl.pallas_call(
        functools.partial(_moe_kernel, has_final_gain=final_gain is not None),
        out_shape=jax.ShapeDtypeStruct((xs.shape[0], D_MODEL), F32),
        grid_spec=grid_spec,
        compiler_params=_params("arbitrary"),
    )(e_slot0, e_slot1, used, xs, gain.reshape(1, D_MODEL), *([w_gate, w_up, w_down] * (2 * TILES_PER_STEP)), *extra)


def _dispatch_plan(info, counts, n):
    n_tiles = -(-n // MOE_TILE) + N_ROUTE_BUCKETS
    n_tiles = -(-n_tiles // TILES_PER_STEP) * TILES_PER_STEP
    bucket = info[0].astype(jnp.int32)
    rank = info[1].astype(jnp.int32)
    counts = counts[:N_ROUTE_BUCKETS, 0].astype(jnp.int32)
    tiles_per_bucket = (counts + MOE_TILE - 1) // MOE_TILE
    tile_end = jnp.cumsum(tiles_per_bucket)
    row_start = (tile_end - tiles_per_bucket) * MOE_TILE
    dest = row_start[bucket] + rank
    src = (jnp.arange(n_tiles * MOE_TILE, dtype=jnp.int32) % n).at[dest].set(jnp.arange(n, dtype=jnp.int32))
    tile_ids = jnp.arange(n_tiles, dtype=jnp.int32)
    tile_bucket = jnp.minimum(jnp.searchsorted(tile_end, tile_ids, side="right").astype(jnp.int32),
                              N_ROUTE_BUCKETS - 1)
    used = (tile_ids < tile_end[-1]).astype(jnp.int32)
    pair = tile_bucket % len(PAIRS)
    base = (tile_bucket // len(PAIRS)) * EXPERTS_PER_GROUP
    pairs = jnp.asarray(PAIRS, jnp.int32)
    return dest, src, base + pairs[pair, 0], base + pairs[pair, 1], used


def _permute_in_columns(w):
    attn = A_Q_W + 2 * A_KV_W + 3 * B_W
    return jnp.concatenate([w[:, attn:], w[:, A_Q_W:A_Q_W + A_KV_W], w[:, :A_Q_W], w[:, A_Q_W + A_KV_W:attn]],
                           axis=1)


def kernel(x_prompt, x_sample, cache_a_k, cache_a_v, cache_b_k, cache_b_v, state_conv, rel_bias_table,
           w_in, w_out, conv_w, attn_sinks, norm_mix, norm_ffn, w_group, b_group, w_router, b_router,
           w_gate, w_up, w_down, norm_final):
    n_seq, seq_len, _ = x_prompt.shape
    dec_seq, n_new, _ = x_sample.shape
    depth = w_in.shape[0]
    n_prompt = n_seq * seq_len
    n_sample = dec_seq * n_new
    n = n_prompt + n_sample
    la, lb = cache_a_k.shape[2], cache_b_k.shape[2]
    assert seq_len % SPAN == 0 and n_prompt % IN_TILE == 0 and n_sample % IN_TILE == 0 and IN_TILE % ROW_TILE == 0
    assert seq_len >= SPAN and la % LANES == 0 and lb % LANES == 0 and n_new == 8
    assert dec_seq % SEQS_PER_STEP == 0 and n_prompt % (SEQS_PER_STEP * n_new) == 0

    table_a, table_b = rel_bias_table[:, :A_Q_HEADS], rel_bias_table[:, A_Q_HEADS:]
    bias_a = _band_bias(table_a, 1)
    bias_b = jnp.stack([_band_bias(table_b, r) for r in DILATIONS], axis=1)
    sbias_a = _sample_bias(table_a, n_new, la, np.arange(la + LANES), A_WINDOW, 1)
    sbias_a = sbias_a.reshape(A_Q_HEADS * n_new, la + LANES)
    sbias_b = jnp.stack([_sample_bias(table_b, n_new, lb, np.arange(lb + LANES), w, r)
                         .reshape(B_HEADS * n_new, lb + LANES) for w, r in B_BRANCHES])
    cak, cav, cbk, cbv = (c.transpose(0, 1, 3, 4, 2) for c in (cache_a_k, cache_a_v, cache_b_k, cache_b_v))

    w_in_b = jnp.stack([_permute_in_columns(w_in[l]) for l in range(depth)]).astype(BF16)
    w_out_b = w_out.astype(BF16)
    expert_w = tuple(w.astype(BF16) for w in (w_gate, w_up, w_down))
    pad = ROUTE_ROWS - N_GROUPS - N_EXPERTS
    w_route = jnp.pad(jnp.concatenate([w_group, w_router], axis=2).transpose(0, 2, 1), ((0, 0), (0, pad), (0, 0)))
    b_route = jnp.pad(jnp.concatenate([b_group, b_router], axis=1), ((0, 0), (0, pad)))[..., None]
    sink_rows = jnp.repeat(attn_sinks, n_new, axis=1)[..., None]

    yp, ys, s_row0 = x_prompt.reshape(n_prompt, D_MODEL), x_sample.reshape(n_sample, D_MODEL), 0
    states = []
    for l in range(depth):
        lap, lbp = min(A_WINDOW, seq_len), min(SPAN, seq_len)
        zh, zf, kbt, vbt = _inproj(yp, ys, s_row0, n_prompt, n_sample, norm_mix[l], w_in_b[l], seq_len, lbp)
        oa = _window_attention(zh, bias_a, attn_sinks[l], n_seq, seq_len)
        ob = _dilated_attention(zf, bias_b, n_seq, seq_len)
        mix_s, conv_s = _sample_mixer(zh, zf, n_prompt, cak, cav, cbk, cbv, l, state_conv[l], conv_w[l], sink_rows[l],
                                      sbias_a, sbias_b, n_new)
        y1, info, counts, u_tail = _outproj_route(
            yp, ys, s_row0, oa, ob, zh, mix_s, w_out_b[l], conv_w[l], norm_ffn[l], w_route[l], b_route[l], n_prompt, seq_len)
        def columns(col):
            return (zh, col) if col < SPLIT else (zf, col - SPLIT)

        def prompt_tail(length, col, width, heads):
            z, c = columns(col)
            rows = [z[(b + 1) * seq_len - length:(b + 1) * seq_len, c:c + width] for b in range(n_seq)]
            return jnp.stack(rows).astype(F32).reshape(n_seq, length, heads, HEAD_DIM)

        def sample_rows(col, width, heads):
            z, c = columns(col)
            return z[n_prompt:, c:c + width].astype(F32).reshape(dec_seq, n_new, heads, HEAD_DIM)

        last_tile = [((b + 1) * seq_len // ROW_TILE - 1) * SUBLANES for b in range(n_seq)]
        conv_p = jnp.stack([u_tail[t + SUBLANES - (CONV_WIDTH - 1):t + SUBLANES, :] for t in last_tile])
        layer_states = (
            prompt_tail(lap, COL_KA, A_KV_W, A_KV_HEADS), prompt_tail(lap, COL_VA, A_KV_W, A_KV_HEADS),
            kbt.reshape(n_seq, B_HEADS, HEAD_DIM, lbp).transpose(0, 3, 1, 2),
            vbt.reshape(n_seq, B_HEADS, HEAD_DIM, lbp).transpose(0, 3, 1, 2), conv_p,
            sample_rows(COL_KA, A_KV_W, A_KV_HEADS), sample_rows(COL_VA, A_KV_W, A_KV_HEADS),
            sample_rows(COL_KB, B_W, B_HEADS), sample_rows(COL_VB, B_W, B_HEADS), conv_s)

        dest, src, e_lo, e_hi, used = _dispatch_plan(info, counts, n)
        xs = y1[src]
        if l == depth - 1:
            xs, layer_states = lax.optimization_barrier((xs, layer_states))
        y_sorted = _experts(xs, norm_ffn[l], e_lo, e_hi, used, *expert_w, l,
                            final_gain=norm_final if l == depth - 1 else None)
        if l < depth - 1:
            y_next, layer_states = lax.optimization_barrier((y_sorted[dest], layer_states))
            yp, ys, s_row0 = y_next, y_next, n_prompt
        states.append(layer_states)

    y_prompt = y_sorted[dest[:n_prompt]].reshape(n_seq, seq_len, D_MODEL)
    y_sample = y_sorted[dest[n_prompt:]].reshape(dec_seq, n_new, D_MODEL)
    st = [jnp.stack([s[k] for s in states]) for k in range(10)]
    return (y_prompt, y_sample, st[0], st[1], st[2], st[3], st[4], st[5], st[6], st[7], st[8], st[9])
```

```python
import functools
import math

import numpy as np
import jax
import jax.numpy as jnp
from jax import lax
from jax.experimental import pallas as pl
from jax.experimental.pallas import tpu as pltpu

F32 = jnp.float32
BF16 = jnp.bfloat16

D_MODEL = 1024
HEAD_DIM = 64
ATTN_SCALE = HEAD_DIM ** -0.5
LOG2E = math.log2(math.e)
BAND_Q_SCALE = ATTN_SCALE * LOG2E
BLOCK = 128
LANES = 128
SUBLANES = 8
A_Q_HEADS = 4
A_KV_HEADS = 2
A_WINDOW = 128
B_HEADS = 6
B_BRANCHES = ((128, 1), (512, 4), (2048, 16))
DILATIONS = tuple(r for _, r in B_BRANCHES)
SPAN = BLOCK * max(DILATIONS)
C_WIDTH = 6 * HEAD_DIM
CONV_WIDTH = 3
A_Q_W = A_Q_HEADS * HEAD_DIM
A_KV_W = A_KV_HEADS * HEAD_DIM
B_W = B_HEADS * HEAD_DIM
IN_WIDTH = A_Q_W + 2 * A_KV_W + 3 * B_W + 3 * C_WIDTH
N_BUCKETS = 32
MAX_DISTANCE = 2048
N_GROUPS = 4
EXPERTS_PER_GROUP = 4
N_EXPERTS = N_GROUPS * EXPERTS_PER_GROUP
EXPERT_FF = 512
RMS_EPS = 1e-6
NEG_INF = -1e30

COL_CONV = 0
COL_KA = 3 * C_WIDTH
COL_QA = COL_KA + A_KV_W
COL_VA = COL_QA + A_Q_W
COL_QB = COL_VA + A_KV_W
COL_KB = COL_QB + B_W
COL_VB = COL_KB + B_W
SPLIT = COL_QB

ROW_TILE = 512
IN_TILE = 1024
MOE_TILE = 256
TILES_PER_STEP = 3
SEQS_PER_STEP = 4
DECODE_VMEM_LIMIT = 60 * 1024 * 1024
PAIRS = ((1, 0), (2, 0), (2, 1), (3, 1), (3, 0), (3, 2))
N_ROUTE_BUCKETS = N_GROUPS * len(PAIRS)
ROUTE_ROWS = 32
ROW_EXT = D_MODEL + LANES
VMEM_LIMIT = 56 * 1024 * 1024


def _params(*sem, vmem_limit=VMEM_LIMIT):
    return pltpu.CompilerParams(dimension_semantics=sem, vmem_limit_bytes=vmem_limit)


def _rel_bucket(dist):
    d = np.maximum(dist, 0)
    max_exact = N_BUCKETS // 2
    df = np.maximum(d, max_exact).astype(np.float32)
    large = max_exact + (np.log(df / max_exact) / math.log(MAX_DISTANCE / max_exact)
                         * (N_BUCKETS - max_exact)).astype(np.int32)
    large = np.minimum(large, N_BUCKETS - 1)
    return np.where(d < max_exact, d, large)


def _masked_bias(table, dist, valid):
    bucket = _rel_bucket(dist).reshape(-1, 1)
    onehot = (jnp.asarray(bucket, jnp.int32) == jnp.arange(N_BUCKETS, dtype=jnp.int32)[None, :]).astype(F32)
    b = jnp.dot(onehot, table.astype(F32), precision=lax.Precision.HIGHEST)
    b = jnp.where(jnp.asarray(valid.reshape(-1, 1)), b, NEG_INF)
    return jnp.moveaxis(b.reshape(dist.shape + (table.shape[1],)), -1, 0)


def _band_bias(table, scale):
    i = np.arange(BLOCK)[:, None]
    j = np.arange(2 * BLOCK)[None, :]
    dist = i + BLOCK - j
    b = _masked_bias(table, dist * scale, (dist >= 0) & (dist <= BLOCK)) * LOG2E
    return b.reshape(table.shape[1] // 2, 2 * BLOCK, 2 * BLOCK)


def _sample_bias(table, n_new, first_new, positions, window, dilation):
    dist = first_new + np.arange(n_new)[:, None] - np.asarray(positions)[None, :]
    valid = (dist >= 0) & (dist <= window) & (dist % dilation == 0)
    return _masked_bias(table, dist, valid)


def _rms(x, gain):
    return x * lax.rsqrt(jnp.mean(x * x, axis=-1, keepdims=True) + RMS_EPS) * gain


def _inproj_kernel(xp_ref, xs_ref, g_ref, w_ref, zh_ref, zf_ref, kbt_ref, vbt_ref, *, prompt_tiles, tiles_per_seq,
                   tail_tiles):
    i = pl.program_id(0)

    def project(x_ref):
        h = _rms(x_ref[...], g_ref[...])
        z = jnp.dot(h.astype(BF16), w_ref[...], preferred_element_type=F32)
        zh_ref[...] = z[:, :SPLIT].astype(BF16)
        zf_ref[...] = z[:, SPLIT:]

    pl.when(i < prompt_tiles)(lambda: project(xp_ref))
    pl.when(i >= prompt_tiles)(lambda: project(xs_ref))

    @pl.when((i < prompt_tiles) & (i % tiles_per_seq >= tiles_per_seq - tail_tiles))
    def _():
        kbt_ref[...] = zf_ref[:, COL_KB - SPLIT:COL_KB - SPLIT + B_W].T
        vbt_ref[...] = zf_ref[:, COL_VB - SPLIT:COL_VB - SPLIT + B_W].T


def _inproj(xp, xs, sample_row0, n_prompt, n_sample, gain, w, seq_len, tail_len):
    p_tiles, s_tile0 = n_prompt // IN_TILE, sample_row0 // IN_TILE
    tps, tail = seq_len // IN_TILE, tail_len // IN_TILE
    n_seq = n_prompt // seq_len

    def tail_block(i):
        j = jnp.minimum(i, p_tiles - 1)
        return j // tps, 0, jnp.maximum(j % tps - (tps - tail), 0)

    tail_spec = pl.BlockSpec((None, B_W, IN_TILE), tail_block)
    tail_shape = jax.ShapeDtypeStruct((n_seq, B_W, tail_len), F32)
    return pl.pallas_call(
        functools.partial(_inproj_kernel, prompt_tiles=p_tiles, tiles_per_seq=tps, tail_tiles=tail),
        out_shape=(jax.ShapeDtypeStruct((n_prompt + n_sample, SPLIT), BF16),
                   jax.ShapeDtypeStruct((n_prompt + n_sample, IN_WIDTH - SPLIT), F32), tail_shape, tail_shape),
        grid=((n_prompt + n_sample) // IN_TILE,),
        in_specs=[pl.BlockSpec((IN_TILE, D_MODEL), lambda i: (jnp.minimum(i, p_tiles - 1), 0)),
                  pl.BlockSpec((IN_TILE, D_MODEL), lambda i: (s_tile0 + jnp.maximum(i - p_tiles, 0), 0)),
                  pl.BlockSpec((1, D_MODEL), lambda i: (0, 0)),
                  pl.BlockSpec((D_MODEL, IN_WIDTH), lambda i: (0, 0))],
        out_specs=(pl.BlockSpec((IN_TILE, SPLIT), lambda i: (i, 0)),
                   pl.BlockSpec((IN_TILE, IN_WIDTH - SPLIT), lambda i: (i, 0)), tail_spec, tail_spec),
        compiler_params=_params("arbitrary"),
    )(xp, xs, gain.reshape(1, D_MODEL), w)


def _lane_half(shape):
    return lax.broadcasted_iota(jnp.int32, shape, len(shape) - 1) // HEAD_DIM


def _nt_dot(a, b):
    return lax.dot_general(a, b, (((1,), (1,)), ((), ())), preferred_element_type=F32)

def _band_pair(q2, k2, v2, bias2, penalty):
    lg = _nt_dot(q2, k2) + bias2
    if penalty is not None:
        lg = lg + penalty
    m = jnp.max(lg, axis=-1, keepdims=True)
    p = jnp.exp2(lg - m)
    s = jnp.sum(p, axis=-1, keepdims=True)
    acc = jnp.dot(p.astype(BF16), v2, preferred_element_type=F32)
    return acc / s, m + jnp.log2(s)


def _prev_key_penalty(first_span):
    col = lax.broadcasted_iota(jnp.int32, (1, 2 * BLOCK), 1)
    return jnp.where((col < BLOCK) & first_span, NEG_INF, 0.0).astype(F32)


def _rows(start, dilation):
    if dilation == 1:
        return pl.ds(start, BLOCK)
    return pl.ds(start, BLOCK, stride=dilation)


def _dilated_kernel(q_ref, kc_ref, kp_ref, vc_ref, vp_ref, bias_ref, o_ref, o_scr, l_scr, first_bias_scr):
    first_half = _lane_half((BLOCK, LANES)) == 0
    penalty = _prev_key_penalty(pl.program_id(1) == 0)
    for bi in range(len(DILATIONS)):
        first_bias_scr[bi] = bias_ref[bi] + penalty

    for bi, r in enumerate(DILATIONS):
        step = r * BLOCK

        def block(c, n, first, bi=bi, r=r, step=step):
            q_start = c + n * step
            if not isinstance(q_start, int):
                q_start = pl.multiple_of(q_start, BLOCK) if r == 1 else q_start
            cur = _rows(q_start, r)
            qt = (q_ref[cur, :] * BAND_Q_SCALE).astype(BF16)
            if first:
                src_k, src_v, prev = kp_ref, vp_ref, _rows(SPAN - step + c, r)
            else:
                p_start = q_start - step
                if r == 1:
                    p_start = pl.multiple_of(p_start, BLOCK)
                src_k, src_v, prev = kc_ref, vc_ref, _rows(p_start, r)
            k2 = jnp.concatenate([src_k[prev, :], kc_ref[cur, :]], axis=0).astype(BF16)
            v2 = jnp.concatenate([src_v[prev, :], vc_ref[cur, :]], axis=0).astype(BF16)
            zero = jnp.zeros_like(qt)
            q2 = jnp.concatenate([jnp.where(first_half, qt, zero), jnp.where(first_half, zero, qt)], axis=0)
            o, lse = _band_pair(q2, k2, v2, first_bias_scr[bi] if first else bias_ref[bi], None)
            o_scr[bi, cur, :] = jnp.where(first_half, o[:BLOCK], o[BLOCK:])
            l_scr[bi, cur, :] = jnp.where(first_half, lse[:BLOCK], lse[BLOCK:])

        n_blocks = SPAN // step
        if r == 1:
            block(0, 0, True)
            lax.fori_loop(1, n_blocks, lambda n, _: block(0, n, False), None, unroll=True)
        else:
            def per_class(c, _, block=block, n_blocks=n_blocks):
                block(c, 0, True)
                for n in range(1, n_blocks):
                    block(c, n, False)
            lax.fori_loop(0, r, per_class, None, unroll=True)

    chunk = 2 * BLOCK

    def combine(j, _):
        rows = pl.ds(pl.multiple_of(j * chunk, chunk), chunk)
        ls = [l_scr[bi, rows, :] for bi in range(len(DILATIONS))]
        m = functools.reduce(jnp.maximum, ls)
        ws = [jnp.exp2(l - m) for l in ls]
        num = sum(w * o_scr[bi, rows, :] for bi, w in enumerate(ws))
        o_ref[rows, :] = (num / sum(ws)).astype(o_ref.dtype)

    lax.fori_loop(0, SPAN // chunk, combine, None)


def _dilated_attention(z, bias, n_seq, seq_len):
    spans = seq_len // SPAN
    qb, kb, vb = ((c - SPLIT) // LANES for c in (COL_QB, COL_KB, COL_VB))
    blk = (SPAN, LANES)
    cur = lambda col: (lambda b, s, i: (b * spans + s, col + i))
    prev = lambda col: (lambda b, s, i: (b * spans + jnp.maximum(s - 1, 0), col + i))
    return pl.pallas_call(
        _dilated_kernel,
        out_shape=jax.ShapeDtypeStruct((n_seq * seq_len, B_W), BF16),
        grid=(n_seq, spans, B_W // LANES),
        in_specs=[pl.BlockSpec(blk, cur(qb)),
                  pl.BlockSpec(blk, cur(kb)), pl.BlockSpec(blk, prev(kb)),
                  pl.BlockSpec(blk, cur(vb)), pl.BlockSpec(blk, prev(vb)),
                  pl.BlockSpec((None, len(DILATIONS), 2 * BLOCK, 2 * BLOCK), lambda b, s, i: (i, 0, 0, 0))],
        out_specs=pl.BlockSpec(blk, lambda b, s, i: (b * spans + s, i)),
        scratch_shapes=[pltpu.VMEM((len(DILATIONS), SPAN, LANES), F32),
                        pltpu.VMEM((len(DILATIONS), SPAN, LANES), F32),
                        pltpu.VMEM((len(DILATIONS), 2 * BLOCK, 2 * BLOCK), F32)],
        compiler_params=_params("parallel", "parallel", "parallel"),
    )(z, z, z, z, z, bias)


def _window_kernel(sink_ref, q_ref, kc_ref, kp_ref, vc_ref, vp_ref, bias_ref, o_ref):
    penalty = _prev_key_penalty(pl.program_id(1) == 0)
    first_half = _lane_half((BLOCK, LANES)) == 0
    first_half2 = _lane_half((2 * BLOCK, LANES)) == 0

    def block(n, first):
        start = n * BLOCK if isinstance(n, int) else pl.multiple_of(n * BLOCK, BLOCK)
        cur = pl.ds(start, BLOCK)
        if first:
            src_k, src_v, prev = kp_ref, vp_ref, pl.ds(SPAN - BLOCK, BLOCK)
        else:
            src_k, src_v, prev = kc_ref, vc_ref, pl.ds(pl.multiple_of(start - BLOCK, BLOCK), BLOCK)
        k2 = jnp.concatenate([src_k[prev, :], kc_ref[cur, :]], axis=0)
        vf = jnp.concatenate([src_v[prev, :], vc_ref[cur, :]], axis=0).astype(F32)
        v_swap = pltpu.roll(vf, HEAD_DIM, 1)
        for i in range(A_KV_HEADS):
            qf = q_ref[cur, i * LANES:(i + 1) * LANES].astype(F32) * BAND_Q_SCALE
            q_swap = pltpu.roll(qf, HEAD_DIM, 1)
            q2 = jnp.concatenate([jnp.where(first_half == (i == 0), qf if a == i else q_swap, 0.0)
                                  for a in range(2)], axis=0).astype(BF16)
            v2 = jnp.where(first_half2 == (i == 0), vf, v_swap).astype(BF16)
            o, lse = _band_pair(q2, k2, v2, bias_ref[i], penalty if first else None)
            outs = [o[a * BLOCK:(a + 1) * BLOCK]
                    / (1.0 + jnp.exp2(sink_ref[2 * i + a] * LOG2E - lse[a * BLOCK:(a + 1) * BLOCK]))
                    for a in range(2)]
            o_ref[cur, i * LANES:(i + 1) * LANES] = jnp.where(first_half, outs[0], outs[1]).astype(o_ref.dtype)

    block(0, True)
    lax.fori_loop(1, SPAN // BLOCK, lambda n, _: block(n, False), None, unroll=5)


def _window_attention(z, bias, sinks, n_seq, seq_len):
    spans = seq_len // SPAN
    ka, va = COL_KA // LANES, COL_VA // LANES
    blk = (SPAN, LANES)
    cur = lambda col: (lambda b, s: (b * spans + s, col))
    prev = lambda col: (lambda b, s: (b * spans + jnp.maximum(s - 1, 0), col))
    return pl.pallas_call(
        _window_kernel,
        out_shape=jax.ShapeDtypeStruct((n_seq * seq_len, A_Q_W), BF16),
        grid=(n_seq, spans),
        in_specs=[pl.BlockSpec(memory_space=pltpu.SMEM),
                  pl.BlockSpec((SPAN, A_Q_W), lambda b, s: (b * spans + s, COL_QA // A_Q_W)),
                  pl.BlockSpec(blk, cur(ka)), pl.BlockSpec(blk, prev(ka)),
                  pl.BlockSpec(blk, cur(va)), pl.BlockSpec(blk, prev(va)),
                  pl.BlockSpec((A_KV_HEADS, 2 * BLOCK, 2 * BLOCK), lambda b, s: (0, 0, 0))],
        out_specs=pl.BlockSpec((SPAN, A_Q_W), lambda b, s: (b * spans + s, 0)),
        compiler_params=_params("parallel", "parallel"),
    )(sinks, z, z, z, z, z, bias)


def _shift_rows(u, filler, k):
    rolled = pltpu.roll(u, k, 0)
    row = lax.broadcasted_iota(jnp.int32, u.shape, 0)
    n_fill = filler.shape[0]
    for j in range(k):
        rolled = jnp.where(row == j, filler[n_fill - k + j:n_fill - k + j + 1, :], rolled)
    return rolled


def _gated_conv(xc, bg, cg, filler, cw):
    u = cg * xc
    conv = cw[0:1, :] * _shift_rows(u, filler, 2) + cw[1:2, :] * _shift_rows(u, filler, 1) + cw[2:3, :] * u
    return bg * conv, u


def _pad_rows(x, rows):
    return jnp.concatenate([x, jnp.zeros((rows - x.shape[0], x.shape[1]), x.dtype)], axis=0)


def _heads_by_dim(ref, j):
    _, h, d, length = ref.shape
    return ref[j].reshape(h * d, length).astype(BF16)


def _sample_kernel(zh_ref, zf_ref, cak_ref, cav_ref, cbk_ref, cbv_ref, st_ref, cw_ref, sink_ref, bias_a_ref,
                   bias_b_ref, mix_ref, conv_ref, *, n_new):
    z_all = jnp.concatenate([zh_ref[...].astype(F32), zf_ref[...]], axis=1)
    for j in range(z_all.shape[0] // n_new):
        _sample_sequence(j, z_all, cak_ref, cav_ref, cbk_ref, cbv_ref, st_ref, cw_ref, sink_ref, bias_a_ref,
                         bias_b_ref, mix_ref, conv_ref, n_new)


def _sample_sequence(j, z_all, cak_ref, cav_ref, cbk_ref, cbv_ref, st_ref, cw_ref, sink_ref, bias_a_ref,
                     bias_b_ref, mix_ref, conv_ref, n_new):
    tokens = slice(j * n_new, (j + 1) * n_new)
    z = z_all[tokens, :]
    la = cak_ref.shape[3]
    lb = cbk_ref.shape[3]

    ka_new = _pad_rows(z[:, COL_KA:COL_KA + A_KV_W], LANES).astype(BF16)
    va_new = _pad_rows(z[:, COL_VA:COL_VA + A_KV_W], LANES).astype(BF16)
    half = _lane_half((n_new, LANES))
    pieces = []
    for i in range(A_KV_HEADS):
        qf = z[:, COL_QA + i * LANES:COL_QA + (i + 1) * LANES] * ATTN_SCALE
        for a in range(2):
            pieces.append(jnp.where(half == i, qf if a == i else pltpu.roll(qf, HEAD_DIM, 1), 0.0))
    qa = jnp.concatenate(pieces, axis=0).astype(BF16)
    lc = jnp.dot(qa, _heads_by_dim(cak_ref, j), preferred_element_type=F32) + bias_a_ref[:, :la]
    ln = _nt_dot(qa, ka_new) + bias_a_ref[:, la:]
    m = jnp.maximum(jnp.max(lc, axis=-1, keepdims=True), jnp.max(ln, axis=-1, keepdims=True))
    pc = jnp.exp(lc - m)
    pn = jnp.exp(ln - m)
    s = jnp.sum(pc, axis=-1, keepdims=True) + jnp.sum(pn, axis=-1, keepdims=True)
    oa = (_nt_dot(pc.astype(BF16), _heads_by_dim(cav_ref, j))
          + jnp.dot(pn.astype(BF16), va_new, preferred_element_type=F32))
    oa = oa / s * jax.nn.sigmoid(m + jnp.log(s) - sink_ref[...])
    oa_blocks = []
    for i in range(A_KV_HEADS):
        per_half = []
        for a in range(2):
            rows = oa[(2 * i + a) * n_new:(2 * i + a + 1) * n_new, :]
            per_half.append(rows if a == i else pltpu.roll(rows, HEAD_DIM, 1))
        oa_blocks.append(jnp.where(half == 0, per_half[0], per_half[1]))

    qf = z[:, COL_QB:COL_QB + B_W] * ATTN_SCALE
    head_of_lane = _lane_half((n_new, B_W))
    qb = jnp.concatenate([jnp.where(head_of_lane == h, qf, 0.0) for h in range(B_HEADS)], axis=0).astype(BF16)
    kb_new = _pad_rows(z[:, COL_KB:COL_KB + B_W], LANES).astype(BF16)
    vb_new = _pad_rows(z[:, COL_VB:COL_VB + B_W], LANES).astype(BF16)
    lg_c = jnp.dot(qb, _heads_by_dim(cbk_ref, j), preferred_element_type=F32)
    lg_n = _nt_dot(qb, kb_new)
    parts = []
    for bi, (w, r) in enumerate(B_BRANCHES):
        lo = lb - min(lb, -(-w // LANES) * LANES)
        lc = lg_c[:, lo:] + bias_b_ref[bi, :, lo:lb]
        ln = lg_n + bias_b_ref[bi, :, lb:]
        m = jnp.maximum(jnp.max(lc, axis=-1, keepdims=True), jnp.max(ln, axis=-1, keepdims=True))
        pc = jnp.exp(lc - m)
        pn = jnp.exp(ln - m)
        s = jnp.sum(pc, axis=-1, keepdims=True) + jnp.sum(pn, axis=-1, keepdims=True)
        parts.append((lo, pc, pn, s, m + jnp.log(s)))
    m_all = functools.reduce(jnp.maximum, [p[4] for p in parts])
    ws = [jnp.exp(p[4] - m_all) for p in parts]
    den = sum(ws)
    p_new = None
    los = sorted({p[0] for p in parts} | {lb})
    segs = [None] * (len(los) - 1)
    for (lo, pc, pn, s, _), w in zip(parts, ws):
        coef = w / (den * s)
        p_new = coef * pn if p_new is None else p_new + coef * pn
        for si in range(len(segs)):
            a0, a1 = los[si], los[si + 1]
            if a0 >= lo:
                piece = coef * pc[:, a0 - lo:a1 - lo]
                segs[si] = piece if segs[si] is None else segs[si] + piece
    p_cache = jnp.concatenate(segs, axis=1) if len(segs) > 1 else segs[0]
    ob = (_nt_dot(p_cache.astype(BF16), _heads_by_dim(cbv_ref, j))
          + jnp.dot(p_new.astype(BF16), vb_new, preferred_element_type=F32))
    ob_rows = sum(jnp.where(head_of_lane == h, ob[h * n_new:(h + 1) * n_new, :], 0.0) for h in range(B_HEADS))

    cz = z[:, COL_CONV:COL_CONV + 3 * C_WIDTH]
    oc, u = _gated_conv(cz[:, :C_WIDTH], cz[:, C_WIDTH:2 * C_WIDTH], cz[:, 2 * C_WIDTH:], st_ref[j], cw_ref[...])
    conv_ref[j] = u[n_new - (CONV_WIDTH - 1):, :]
    mix_ref[tokens, :] = jnp.concatenate(oa_blocks + [ob_rows, oc], axis=1)


def _sample_mixer(zh, zf, row0, cak, cav, cbk, cbv, layer, state, cw, sink_rows, bias_a, bias_b, n_new):
    n_seq = cak.shape[1]
    g = SEQS_PER_STEP
    rows = g * n_new
    blk0 = row0 // rows
    cache = lambda a: pl.BlockSpec((None, g) + a.shape[2:], lambda b: (layer, b, 0, 0, 0))
    per_seq = lambda shape: pl.BlockSpec((g,) + shape, lambda b: (b, 0, 0))
    const = lambda a: pl.BlockSpec(a.shape, lambda b: (0,) * a.ndim)
    return pl.pallas_call(
        functools.partial(_sample_kernel, n_new=n_new),
        out_shape=(jax.ShapeDtypeStruct((n_seq * n_new, D_MODEL), F32),
                   jax.ShapeDtypeStruct((n_seq, CONV_WIDTH - 1, C_WIDTH), F32)),
        grid=(n_seq // g,),
        in_specs=[pl.BlockSpec((rows, SPLIT), lambda b: (blk0 + b, 0)),
                  pl.BlockSpec((rows, IN_WIDTH - SPLIT), lambda b: (blk0 + b, 0)),
                  cache(cak), cache(cav), cache(cbk), cache(cbv),
                  per_seq((CONV_WIDTH - 1, C_WIDTH)), const(cw), const(sink_rows), const(bias_a), const(bias_b)],
        out_specs=(pl.BlockSpec((rows, D_MODEL), lambda b: (b, 0)), per_seq((CONV_WIDTH - 1, C_WIDTH))),
        compiler_params=_params("parallel", vmem_limit=DECODE_VMEM_LIMIT),
    )(zh, zf, cak, cav, cbk, cbv, state, cw, sink_rows, bias_a, bias_b)


def _first_index(vals, best):
    idx = jnp.full(best.shape, len(vals) - 1, jnp.int32)
    for j in range(len(vals) - 2, -1, -1):
        idx = jnp.where(vals[j] == best, j, idx)
    return idx


def _route(lt):
    g = [lt[k:k + 1, :] for k in range(N_GROUPS)]
    g_max = functools.reduce(jnp.maximum, g)
    g_idx = _first_index(g, g_max)
    g_w = 1.0 / sum(jnp.exp(v - g_max) for v in g)
    e = []
    for j in range(EXPERTS_PER_GROUP):
        v = lt[N_GROUPS + j:N_GROUPS + j + 1, :]
        for gi in range(1, N_GROUPS):
            row = N_GROUPS + gi * EXPERTS_PER_GROUP + j
            v = jnp.where(g_idx == gi, lt[row:row + 1, :], v)
        e.append(v)
    e1 = functools.reduce(jnp.maximum, e)
    i1 = _first_index(e, e1)
    rest = [jnp.where(i1 == j, -jnp.inf, e[j]) for j in range(EXPERTS_PER_GROUP)]
    e2 = functools.reduce(jnp.maximum, rest)
    i2 = _first_index(rest, e2)
    t = jnp.exp(e2 - e1)
    w1 = g_w / (1.0 + t)
    w2 = g_w * t / (1.0 + t)
    swap = i2 < i1
    lo = jnp.where(swap, i2, i1)
    hi = jnp.where(swap, i1, i2)
    pair = jnp.zeros_like(lo)
    for p, (slot0, slot1) in enumerate(PAIRS):
        pair = jnp.where((hi == slot0) & (lo == slot1), p, pair)
    bucket = g_idx * len(PAIRS) + pair
    return bucket, jnp.where(swap, w1, w2), jnp.where(swap, w2, w1)


def _outproj_kernel(yp_ref, ys_ref, oa_ref, ob_ref, zc_ref, zh_ref, ms_ref, wout_ref, cw_ref, gn_ref, wr_ref,
                    br_ref, y1_ref, info_ref, cnt_ref, ut_ref, y1_scr, carry_scr, *, prompt_tiles, tiles_per_seq):
    i = pl.program_id(0)
    tile = yp_ref.shape[0]

    @pl.when(i == 0)
    def _():
        carry_scr[...] = jnp.zeros_like(carry_scr)

    @pl.when(i < prompt_tiles)
    def _():
        zc = zc_ref[...].astype(F32)
        zh = zh_ref[...].astype(F32)
        halo = zh[:, 2 * C_WIDTH:] * zh[:, :C_WIDTH]
        halo = jnp.where(i % tiles_per_seq == 0, 0.0, halo)
        oc, u = _gated_conv(zc[:, :C_WIDTH], zc[:, C_WIDTH:2 * C_WIDTH], zc[:, 2 * C_WIDTH:], halo, cw_ref[...])
        ut_ref[...] = u[tile - 8:, :]
        y1_scr[...] = (
            yp_ref[...]
            + jnp.dot(oa_ref[...], wout_ref[0:A_Q_W, :], preferred_element_type=F32)
            + jnp.dot(ob_ref[...], wout_ref[A_Q_W:A_Q_W + B_W, :], preferred_element_type=F32)
            + jnp.dot(oc.astype(BF16), wout_ref[A_Q_W + B_W:, :], preferred_element_type=F32))

    @pl.when(i >= prompt_tiles)
    def _():
        ut_ref[...] = jnp.zeros_like(ut_ref)
        y1_scr[...] = ys_ref[...] + jnp.dot(ms_ref[...].astype(BF16), wout_ref[...], preferred_element_type=F32)

    y1 = y1_scr[...]
    y1_ref[:, :D_MODEL] = y1
    xn = _rms(y1, gn_ref[...])

    x_hi = xn.astype(BF16)
    x_lo = (xn - x_hi.astype(F32)).astype(BF16)
    wr = wr_ref[...]
    w_hi = wr.astype(BF16)
    w_lo = (wr - w_hi.astype(F32)).astype(BF16)
    lt_hi = _nt_dot(jnp.concatenate([w_hi, w_lo], axis=0), x_hi)
    lt = lt_hi[:ROUTE_ROWS] + lt_hi[ROUTE_ROWS:] + _nt_dot(w_hi, x_lo) + br_ref[...]
    bucket, w_slot0, w_slot1 = _route(lt)
    onehot = (lax.broadcasted_iota(jnp.int32, (ROUTE_ROWS, tile), 0) == bucket).astype(F32)
    upper = (lax.broadcasted_iota(jnp.int32, (tile, tile), 0)
             <= lax.broadcasted_iota(jnp.int32, (tile, tile), 1)).astype(BF16)
    running = jnp.dot(onehot.astype(BF16), upper, preferred_element_type=F32)
    carry = carry_scr[...]
    rank = jnp.sum(onehot * (running - 1.0 + carry), axis=0, keepdims=True)
    carry = carry + jnp.sum(onehot, axis=1, keepdims=True)
    carry_scr[...] = carry
    cnt_ref[...] = jnp.broadcast_to(carry, cnt_ref.shape)
    info_ref[...] = jnp.concatenate([bucket.astype(F32), rank, jnp.zeros((SUBLANES - 2, tile), F32)], axis=0)
    y1_ref[:, D_MODEL:] = jnp.concatenate([w_slot0, w_slot1, jnp.zeros((LANES - 2, tile), F32)], axis=0).T


def _outproj_route(yp, ys, sample_row0, oa, ob, z, mix_s, w_out, cw, gain, w_route, b_route, n_prompt, seq_len):
    n = n_prompt + mix_s.shape[0]
    tiles = n // ROW_TILE
    p_tiles = n_prompt // ROW_TILE
    halo_rows = 16
    halo_blocks = ROW_TILE // halo_rows
    conv_w = 3 * C_WIDTH
    pidx = lambda i: jnp.minimum(i, p_tiles - 1)
    sidx = lambda i: jnp.maximum(i - p_tiles, 0)
    const = lambda a: pl.BlockSpec(a.shape, lambda i: (0,) * a.ndim)
    gain = gain.reshape(1, D_MODEL)
    return pl.pallas_call(
        functools.partial(_outproj_kernel, prompt_tiles=p_tiles, tiles_per_seq=seq_len // ROW_TILE),
        out_shape=(jax.ShapeDtypeStruct((n, ROW_EXT), F32),
                   jax.ShapeDtypeStruct((8, n), F32),
                   jax.ShapeDtypeStruct((ROUTE_ROWS, LANES), F32),
                   jax.ShapeDtypeStruct((tiles * 8, C_WIDTH), F32)),
        grid=(tiles,),
        in_specs=[pl.BlockSpec((ROW_TILE, D_MODEL), lambda i: (pidx(i), 0)),
                  pl.BlockSpec((ROW_TILE, D_MODEL), lambda i: (sample_row0 // ROW_TILE + sidx(i), 0)),
                  pl.BlockSpec((ROW_TILE, A_Q_W), lambda i: (pidx(i), 0)),
                  pl.BlockSpec((ROW_TILE, B_W), lambda i: (pidx(i), 0)),
                  pl.BlockSpec((ROW_TILE, conv_w), lambda i: (pidx(i), 0)),
                  pl.BlockSpec((halo_rows, conv_w), lambda i: (jnp.maximum(pidx(i) * halo_blocks - 1, 0), 0)),
                  pl.BlockSpec((ROW_TILE, D_MODEL), lambda i: (sidx(i), 0)),
                  const(w_out), const(cw), const(gain), const(w_route), const(b_route)],
        out_specs=(pl.BlockSpec((ROW_TILE, ROW_EXT), lambda i: (i, 0)),
                   pl.BlockSpec((8, ROW_TILE), lambda i: (0, i)),
                   pl.BlockSpec((ROUTE_ROWS, LANES), lambda i: (0, 0)),
                   pl.BlockSpec((8, C_WIDTH), lambda i: (i, 0))),
        scratch_shapes=[pltpu.VMEM((ROW_TILE, D_MODEL), F32), pltpu.VMEM((ROUTE_ROWS, 1), F32)],
        compiler_params=_params("arbitrary"),
    )(yp, ys, oa, ob, z, z, mix_s, w_out, cw, gain, w_route, b_route)


def _moe_kernel(e0_ref, e1_ref, used_ref, x_ref, g_ref, *rest, has_final_gain):
    o_ref = rest[-1]
    weights = rest[:6 * TILES_PER_STEP]
    t = pl.program_id(0)

    @pl.when(used_ref[t * TILES_PER_STEP] > 0)
    def _():
        for k in range(TILES_PER_STEP):
            rows = slice(k * MOE_TILE, (k + 1) * MOE_TILE)
            wg1, wu1, wd1, wg2, wu2, wd2 = weights[6 * k:6 * k + 6]
            y1 = x_ref[rows, :D_MODEL]
            w = x_ref[rows, D_MODEL:]
            x = _rms(y1, g_ref[...]).astype(BF16)

            def expert(wg, wu, wd, scale, x=x):
                g = jnp.dot(x, wg[...], preferred_element_type=F32)
                u = jnp.dot(x, wu[...], preferred_element_type=F32)
                h = g * jax.nn.sigmoid(g) * u * scale
                return jnp.dot(h.astype(BF16), wd[...], preferred_element_type=F32)

            y2 = y1 + expert(wg1, wu1, wd1, w[:, 0:1]) + expert(wg2, wu2, wd2, w[:, 1:2])
            o_ref[rows, :] = _rms(y2, rest[-2][...]) if has_final_gain else y2

    @pl.when(used_ref[t * TILES_PER_STEP] == 0)
    def _():
        o_ref[...] = jnp.zeros_like(o_ref)


def _experts(xs, gain, e_slot0, e_slot1, used, w_gate, w_up, w_down, layer, final_gain=None):
    step_rows = TILES_PER_STEP * MOE_TILE
    n_steps = xs.shape[0] // step_rows

    def weight_specs(k):
        pick = lambda sel: (lambda t, e0, e1, u: (layer, (e0, e1)[sel][t * TILES_PER_STEP + k], 0, 0))
        up = lambda sel: pl.BlockSpec((None, None, D_MODEL, EXPERT_FF), pick(sel))
        down = lambda sel: pl.BlockSpec((None, None, EXPERT_FF, D_MODEL), pick(sel))
        return [up(0), up(0), down(0), up(1), up(1), down(1)]

    gain_spec = pl.BlockSpec((1, D_MODEL), lambda t, e0, e1, u: (0, 0))
    extra = [] if final_gain is None else [final_gain.reshape(1, D_MODEL)]
    grid_spec = pltpu.PrefetchScalarGridSpec(
        num_scalar_prefetch=3,
        grid=(n_steps,),
        in_specs=[pl.BlockSpec((step_rows, ROW_EXT), lambda t, e0, e1, u: (t, 0)), gain_spec]
                 + [spec for k in range(TILES_PER_STEP) for spec in weight_specs(k)] + [gain_spec] * len(extra),
        out_specs=pl.BlockSpec((step_rows, D_MODEL), lambda t, e0, e1, u: (t, 0)))
    return pl.pallas_call(
        functools.partial(_moe_kernel, has_final_gain=final_gain is not None),
        out_shape=jax.ShapeDtypeStruct((xs.shape[0], D_MODEL), F32),
        grid_spec=grid_spec,
        compiler_params=_params("arbitrary"),
    )(e_slot0, e_slot1, used, xs, gain.reshape(1, D_MODEL), *([w_gate, w_up, w_down] * (2 * TILES_PER_STEP)), *extra)


def _dispatch_plan(info, counts, n):
    n_tiles = -(-n // MOE_TILE) + N_ROUTE_BUCKETS
    n_tiles = -(-n_tiles // TILES_PER_STEP) * TILES_PER_STEP
    bucket = info[0].astype(jnp.int32)
    rank = info[1].astype(jnp.int32)
    counts = counts[:N_ROUTE_BUCKETS, 0].astype(jnp.int32)
    tiles_per_bucket = (counts + MOE_TILE - 1) // MOE_TILE
    tile_end = jnp.cumsum(tiles_per_bucket)
    row_start = (tile_end - tiles_per_bucket) * MOE_TILE
    dest = row_start[bucket] + rank
    src = (jnp.arange(n_tiles * MOE_TILE, dtype=jnp.int32) % n).at[dest].set(jnp.arange(n, dtype=jnp.int32))
    tile_ids = jnp.arange(n_tiles, dtype=jnp.int32)
    tile_bucket = jnp.sum((tile_ids[:, None] >= tile_end[None, :]).astype(jnp.int32), axis=1)
    tile_bucket = jnp.minimum(tile_bucket, N_ROUTE_BUCKETS - 1)
    used = (tile_ids < tile_end[-1]).astype(jnp.int32)
    pair = tile_bucket % len(PAIRS)
    base = (tile_bucket // len(PAIRS)) * EXPERTS_PER_GROUP
    pairs = jnp.asarray(PAIRS, jnp.int32)
    return dest, src, base + pairs[pair, 0], base + pairs[pair, 1], used


def _permute_in_columns(w):
    attn = A_Q_W + 2 * A_KV_W + 3 * B_W
    return jnp.concatenate([w[:, attn:], w[:, A_Q_W:A_Q_W + A_KV_W], w[:, :A_Q_W], w[:, A_Q_W + A_KV_W:attn]],
                           axis=1)


def kernel(x_prompt, x_sample, cache_a_k, cache_a_v, cache_b_k, cache_b_v, state_conv, rel_bias_table,
           w_in, w_out, conv_w, attn_sinks, norm_mix, norm_ffn, w_group, b_group, w_router, b_router,
           w_gate, w_up, w_down, norm_final):
    n_seq, seq_len, _ = x_prompt.shape
    dec_seq, n_new, _ = x_sample.shape
    depth = w_in.shape[0]
    n_prompt = n_seq * seq_len
    n_sample = dec_seq * n_new
    n = n_prompt + n_sample
    la, lb = cache_a_k.shape[2], cache_b_k.shape[2]
    assert seq_len % SPAN == 0 and n_prompt % IN_TILE == 0 and n_sample % IN_TILE == 0 and IN_TILE % ROW_TILE == 0
    assert seq_len >= SPAN and la % LANES == 0 and lb % LANES == 0 and n_new == 8
    assert dec_seq % SEQS_PER_STEP == 0 and n_prompt % (SEQS_PER_STEP * n_new) == 0

    table_a, table_b = rel_bias_table[:, :A_Q_HEADS], rel_bias_table[:, A_Q_HEADS:]
    bias_a = _band_bias(table_a, 1)
    bias_b = jnp.stack([_band_bias(table_b, r) for r in DILATIONS], axis=1)
    sbias_a = _sample_bias(table_a, n_new, la, np.arange(la + LANES), A_WINDOW, 1)
    sbias_a = sbias_a.reshape(A_Q_HEADS * n_new, la + LANES)
    sbias_b = jnp.stack([_sample_bias(table_b, n_new, lb, np.arange(lb + LANES), w, r)
                         .reshape(B_HEADS * n_new, lb + LANES) for w, r in B_BRANCHES])
    cak, cav, cbk, cbv = (c.transpose(0, 1, 3, 4, 2) for c in (cache_a_k, cache_a_v, cache_b_k, cache_b_v))

    w_in_b = jnp.stack([_permute_in_columns(w_in[l]) for l in range(depth)]).astype(BF16)
    w_out_b = w_out.astype(BF16)
    expert_w = tuple(w.astype(BF16) for w in (w_gate, w_up, w_down))
    pad = ROUTE_ROWS - N_GROUPS - N_EXPERTS
    w_route = jnp.pad(jnp.concatenate([w_group, w_router], axis=2).transpose(0, 2, 1), ((0, 0), (0, pad), (0, 0)))
    b_route = jnp.pad(jnp.concatenate([b_group, b_router], axis=1), ((0, 0), (0, pad)))[..., None]
    sink_rows = jnp.repeat(attn_sinks, n_new, axis=1)[..., None]

    yp, ys, s_row0 = x_prompt.reshape(n_prompt, D_MODEL), x_sample.reshape(n_sample, D_MODEL), 0
    states = []
    for l in range(depth):
        lap, lbp = min(A_WINDOW, seq_len), min(SPAN, seq_len)
        zh, zf, kbt, vbt = _inproj(yp, ys, s_row0, n_prompt, n_sample, norm_mix[l], w_in_b[l], seq_len, lbp)
        oa = _window_attention(zh, bias_a, attn_sinks[l], n_seq, seq_len)
        ob = _dilated_attention(zf, bias_b, n_seq, seq_len)
        mix_s, conv_s = _sample_mixer(zh, zf, n_prompt, cak, cav, cbk, cbv, l, state_conv[l], conv_w[l], sink_rows[l],
                                      sbias_a, sbias_b, n_new)
        y1, info, counts, u_tail = _outproj_route(
            yp, ys, s_row0, oa, ob, zh, mix_s, w_out_b[l], conv_w[l], norm_ffn[l], w_route[l], b_route[l], n_prompt, seq_len)
        def columns(col):
            return (zh, col) if col < SPLIT else (zf, col - SPLIT)

        def prompt_tail(length, col, width, heads):
            z, c = columns(col)
            rows = [z[(b + 1) * seq_len - length:(b + 1) * seq_len, c:c + width] for b in range(n_seq)]
            return jnp.stack(rows).astype(F32).reshape(n_seq, length, heads, HEAD_DIM)

        def sample_rows(col, width, heads):
            z, c = columns(col)
            return z[n_prompt:, c:c + width].astype(F32).reshape(dec_seq, n_new, heads, HEAD_DIM)

        last_tile = [((b + 1) * seq_len // ROW_TILE - 1) * SUBLANES for b in range(n_seq)]
        conv_p = jnp.stack([u_tail[t + SUBLANES - (CONV_WIDTH - 1):t + SUBLANES, :] for t in last_tile])
        layer_states = (
            prompt_tail(lap, COL_KA, A_KV_W, A_KV_HEADS), prompt_tail(lap, COL_VA, A_KV_W, A_KV_HEADS),
            kbt.reshape(n_seq, B_HEADS, HEAD_DIM, lbp).transpose(0, 3, 1, 2),
            vbt.reshape(n_seq, B_HEADS, HEAD_DIM, lbp).transpose(0, 3, 1, 2), conv_p,
            sample_rows(COL_KA, A_KV_W, A_KV_HEADS), sample_rows(COL_VA, A_KV_W, A_KV_HEADS),
            sample_rows(COL_KB, B_W, B_HEADS), sample_rows(COL_VB, B_W, B_HEADS), conv_s)

        dest, src, e_lo, e_hi, used = _dispatch_plan(info, counts, n)
        xs = y1[src]
        if l == depth - 1:
            xs, layer_states = lax.optimization_barrier((xs, layer_states))
        y_sorted = _experts(xs, norm_ffn[l], e_lo, e_hi, used, *expert_w, l,
                            final_gain=norm_final if l == depth - 1 else None)
        if l < depth - 1:
            y_next, layer_states = lax.optimization_barrier((y_sorted[dest], layer_states))
            yp, ys, s_row0 = y_next, y_next, n_prompt
        states.append(layer_states)

    y_prompt = y_sorted[dest[:n_prompt]].reshape(n_seq, seq_len, D_MODEL)
    y_sample = y_sorted[dest[n_prompt:]].reshape(dec_seq, n_new, D_MODEL)
    st = [jnp.stack([s[k] for s in states]) for k in range(10)]
    return (y_prompt, y_sample, st[0], st[1], st[2], st[3], st[4], st[5], st[6], st[7], st[8], st[9])
```

```python
import functools
import math

import numpy as np
import jax
import jax.numpy as jnp
from jax import lax
from jax.experimental import pallas as pl
from jax.experimental.pallas import tpu as pltpu

F32 = jnp.float32
BF16 = jnp.bfloat16

D_MODEL = 1024
HEAD_DIM = 64
ATTN_SCALE = HEAD_DIM ** -0.5
LOG2E = math.log2(math.e)
BAND_Q_SCALE = ATTN_SCALE * LOG2E
BLOCK = 128
LANES = 128
SUBLANES = 8
A_Q_HEADS = 4
A_KV_HEADS = 2
A_WINDOW = 128
B_HEADS = 6
B_BRANCHES = ((128, 1), (512, 4), (2048, 16))
DILATIONS = tuple(r for _, r in B_BRANCHES)
SPAN = BLOCK * max(DILATIONS)
C_WIDTH = 6 * HEAD_DIM
CONV_WIDTH = 3
A_Q_W = A_Q_HEADS * HEAD_DIM
A_KV_W = A_KV_HEADS * HEAD_DIM
B_W = B_HEADS * HEAD_DIM
IN_WIDTH = A_Q_W + 2 * A_KV_W + 3 * B_W + 3 * C_WIDTH
N_BUCKETS = 32
MAX_DISTANCE = 2048
N_GROUPS = 4
EXPERTS_PER_GROUP = 4
N_EXPERTS = N_GROUPS * EXPERTS_PER_GROUP
EXPERT_FF = 512
RMS_EPS = 1e-6
NEG_INF = -1e30

COL_CONV = 0
COL_KA = 3 * C_WIDTH
COL_QA = COL_KA + A_KV_W
COL_VA = COL_QA + A_Q_W
COL_QB = COL_VA + A_KV_W
COL_KB = COL_QB + B_W
COL_VB = COL_KB + B_W
SPLIT = COL_QB

ROW_TILE = 512
IN_TILE = 1024
MOE_TILE = 256
TILES_PER_STEP = 3
SEQS_PER_STEP = 4
DECODE_VMEM_LIMIT = 60 * 1024 * 1024
PAIRS = ((1, 0), (2, 0), (2, 1), (3, 1), (3, 0), (3, 2))
N_ROUTE_BUCKETS = N_GROUPS * len(PAIRS)
ROUTE_ROWS = 32
ROW_EXT = D_MODEL + LANES
VMEM_LIMIT = 56 * 1024 * 1024


def _params(*sem, vmem_limit=VMEM_LIMIT):
    return pltpu.CompilerParams(dimension_semantics=sem, vmem_limit_bytes=vmem_limit)


def _rel_bucket(dist):
    d = np.maximum(dist, 0)
    max_exact = N_BUCKETS // 2
    df = np.maximum(d, max_exact).astype(np.float32)
    large = max_exact + (np.log(df / max_exact) / math.log(MAX_DISTANCE / max_exact)
                         * (N_BUCKETS - max_exact)).astype(np.int32)
    large = np.minimum(large, N_BUCKETS - 1)
    return np.where(d < max_exact, d, large)


def _masked_bias(table, dist, valid):
    bucket = _rel_bucket(dist).reshape(-1, 1)
    onehot = (jnp.asarray(bucket, jnp.int32) == jnp.arange(N_BUCKETS, dtype=jnp.int32)[None, :]).astype(F32)
    b = jnp.dot(onehot, table.astype(F32), precision=lax.Precision.HIGHEST)
    b = jnp.where(jnp.asarray(valid.reshape(-1, 1)), b, NEG_INF)
    return jnp.moveaxis(b.reshape(dist.shape + (table.shape[1],)), -1, 0)


def _band_bias(table, scales):
    i = np.arange(BLOCK)[:, None]
    j = np.arange(2 * BLOCK)[None, :]
    dist = i + BLOCK - j
    dists = np.stack([dist * r for r in scales])
    valid = np.broadcast_to((dist >= 0) & (dist <= BLOCK), dists.shape)
    b = _masked_bias(table, dists, valid) * LOG2E
    pairs = table.shape[1] // 2
    b = b.reshape(pairs, 2, len(scales), BLOCK, 2 * BLOCK).transpose(0, 2, 1, 3, 4)
    return b.reshape(pairs, len(scales), 2 * BLOCK, 2 * BLOCK)


def _sample_bias(table, n_new, first_new, positions, branches):
    dist = first_new + np.arange(n_new)[:, None] - np.asarray(positions)[None, :]
    valid = np.stack([(dist >= 0) & (dist <= w) & (dist % r == 0) for w, r in branches])
    b = _masked_bias(table, np.broadcast_to(dist, valid.shape), valid)
    return b.transpose(1, 0, 2, 3).reshape(len(branches), table.shape[1] * n_new, dist.shape[1])


def _rms(x, gain):
    return x * lax.rsqrt(jnp.mean(x * x, axis=-1, keepdims=True) + RMS_EPS) * gain


def _inproj_kernel(xp_ref, xs_ref, g_ref, w_ref, zh_ref, zf_ref, kbt_ref, vbt_ref, *, prompt_tiles, tiles_per_seq,
                   tail_tiles):
    i = pl.program_id(0)

    def project(x_ref):
        h = _rms(x_ref[...], g_ref[...])
        z = jnp.dot(h.astype(BF16), w_ref[...], preferred_element_type=F32)
        zh_ref[...] = z[:, :SPLIT].astype(BF16)
        zf_ref[...] = z[:, SPLIT:]

    pl.when(i < prompt_tiles)(lambda: project(xp_ref))
    pl.when(i >= prompt_tiles)(lambda: project(xs_ref))

    @pl.when((i < prompt_tiles) & (i % tiles_per_seq >= tiles_per_seq - tail_tiles))
    def _():
        kbt_ref[...] = zf_ref[:, COL_KB - SPLIT:COL_KB - SPLIT + B_W].T
        vbt_ref[...] = zf_ref[:, COL_VB - SPLIT:COL_VB - SPLIT + B_W].T


def _inproj(xp, xs, sample_row0, n_prompt, n_sample, gain, w, seq_len, tail_len):
    p_tiles, s_tile0 = n_prompt // IN_TILE, sample_row0 // IN_TILE
    tps, tail = seq_len // IN_TILE, tail_len // IN_TILE
    n_seq = n_prompt // seq_len

    def tail_block(i):
        j = jnp.minimum(i, p_tiles - 1)
        return j // tps, 0, jnp.maximum(j % tps - (tps - tail), 0)

    tail_spec = pl.BlockSpec((None, B_W, IN_TILE), tail_block)
    tail_shape = jax.ShapeDtypeStruct((n_seq, B_W, tail_len), F32)
    return pl.pallas_call(
        functools.partial(_inproj_kernel, prompt_tiles=p_tiles, tiles_per_seq=tps, tail_tiles=tail),
        out_shape=(jax.ShapeDtypeStruct((n_prompt + n_sample, SPLIT), BF16),
                   jax.ShapeDtypeStruct((n_prompt + n_sample, IN_WIDTH - SPLIT), F32), tail_shape, tail_shape),
        grid=((n_prompt + n_sample) // IN_TILE,),
        in_specs=[pl.BlockSpec((IN_TILE, D_MODEL), lambda i: (jnp.minimum(i, p_tiles - 1), 0)),
                  pl.BlockSpec((IN_TILE, D_MODEL), lambda i: (s_tile0 + jnp.maximum(i - p_tiles, 0), 0)),
                  pl.BlockSpec((1, D_MODEL), lambda i: (0, 0)),
                  pl.BlockSpec((D_MODEL, IN_WIDTH), lambda i: (0, 0))],
        out_specs=(pl.BlockSpec((IN_TILE, SPLIT), lambda i: (i, 0)),
                   pl.BlockSpec((IN_TILE, IN_WIDTH - SPLIT), lambda i: (i, 0)), tail_spec, tail_spec),
        compiler_params=_params("arbitrary"),
    )(xp, xs, gain.reshape(1, D_MODEL), w)


def _lane_half(shape):
    return lax.broadcasted_iota(jnp.int32, shape, len(shape) - 1) // HEAD_DIM


def _nt_dot(a, b):
    return lax.dot_general(a, b, (((1,), (1,)), ((), ())), preferred_element_type=F32)

def _band_pair(q2, k2, v2, bias2, penalty):
    lg = _nt_dot(q2, k2) + bias2
    if penalty is not None:
        lg = lg + penalty
    m = jnp.max(lg, axis=-1, keepdims=True)
    p = jnp.exp2(lg - m)
    s = jnp.sum(p, axis=-1, keepdims=True)
    acc = jnp.dot(p.astype(BF16), v2, preferred_element_type=F32)
    return acc / s, m + jnp.log2(s)


def _prev_key_penalty(first_span):
    col = lax.broadcasted_iota(jnp.int32, (1, 2 * BLOCK), 1)
    return jnp.where((col < BLOCK) & first_span, NEG_INF, 0.0).astype(F32)


def _rows(start, dilation):
    if dilation == 1:
        return pl.ds(start, BLOCK)
    return pl.ds(start, BLOCK, stride=dilation)


def _dilated_kernel(q_ref, kc_ref, kp_ref, vc_ref, vp_ref, bias_ref, o_ref, o_scr, l_scr, first_bias_scr):
    first_half = _lane_half((BLOCK, LANES)) == 0
    penalty = _prev_key_penalty(pl.program_id(1) == 0)
    for bi in range(len(DILATIONS)):
        first_bias_scr[bi] = bias_ref[bi] + penalty

    for bi, r in enumerate(DILATIONS):
        step = r * BLOCK

        def block(c, n, first, bi=bi, r=r, step=step):
            q_start = c + n * step
            if not isinstance(q_start, int):
                q_start = pl.multiple_of(q_start, BLOCK) if r == 1 else q_start
            cur = _rows(q_start, r)
            qt = (q_ref[cur, :] * BAND_Q_SCALE).astype(BF16)
            if first:
                src_k, src_v, prev = kp_ref, vp_ref, _rows(SPAN - step + c, r)
            else:
                p_start = q_start - step
                if r == 1:
                    p_start = pl.multiple_of(p_start, BLOCK)
                src_k, src_v, prev = kc_ref, vc_ref, _rows(p_start, r)
            k2 = jnp.concatenate([src_k[prev, :], kc_ref[cur, :]], axis=0).astype(BF16)
            v2 = jnp.concatenate([src_v[prev, :], vc_ref[cur, :]], axis=0).astype(BF16)
            zero = jnp.zeros_like(qt)
            q2 = jnp.concatenate([jnp.where(first_half, qt, zero), jnp.where(first_half, zero, qt)], axis=0)
            o, lse = _band_pair(q2, k2, v2, first_bias_scr[bi] if first else bias_ref[bi], None)
            o_scr[bi, cur, :] = jnp.where(first_half, o[:BLOCK], o[BLOCK:])
            l_scr[bi, cur, :] = jnp.where(first_half, lse[:BLOCK], lse[BLOCK:])

        n_blocks = SPAN // step
        if r == 1:
            block(0, 0, True)
            lax.fori_loop(1, n_blocks, lambda n, _: block(0, n, False), None, unroll=True)
        else:
            def per_class(c, _, block=block, n_blocks=n_blocks):
                block(c, 0, True)
                for n in range(1, n_blocks):
                    block(c, n, False)
            lax.fori_loop(0, r, per_class, None, unroll=True)

    chunk = 2 * BLOCK

    def combine(j, _):
        rows = pl.ds(pl.multiple_of(j * chunk, chunk), chunk)
        ls = [l_scr[bi, rows, :] for bi in range(len(DILATIONS))]
        m = functools.reduce(jnp.maximum, ls)
        ws = [jnp.exp2(l - m) for l in ls]
        num = sum(w * o_scr[bi, rows, :] for bi, w in enumerate(ws))
        o_ref[rows, :] = (num / sum(ws)).astype(o_ref.dtype)

    lax.fori_loop(0, SPAN // chunk, combine, None)


def _dilated_attention(z, bias, n_seq, seq_len):
    spans = seq_len // SPAN
    qb, kb, vb = ((c - SPLIT) // LANES for c in (COL_QB, COL_KB, COL_VB))
    blk = (SPAN, LANES)
    cur = lambda col: (lambda b, s, i: (b * spans + s, col + i))
    prev = lambda col: (lambda b, s, i: (b * spans + jnp.maximum(s - 1, 0), col + i))
    return pl.pallas_call(
        _dilated_kernel,
        out_shape=jax.ShapeDtypeStruct((n_seq * seq_len, B_W), BF16),
        grid=(n_seq, spans, B_W // LANES),
        in_specs=[pl.BlockSpec(blk, cur(qb)),
                  pl.BlockSpec(blk, cur(kb)), pl.BlockSpec(blk, prev(kb)),
                  pl.BlockSpec(blk, cur(vb)), pl.BlockSpec(blk, prev(vb)),
                  pl.BlockSpec((None, len(DILATIONS), 2 * BLOCK, 2 * BLOCK), lambda b, s, i: (i, 0, 0, 0))],
        out_specs=pl.BlockSpec(blk, lambda b, s, i: (b * spans + s, i)),
        scratch_shapes=[pltpu.VMEM((len(DILATIONS), SPAN, LANES), F32),
                        pltpu.VMEM((len(DILATIONS), SPAN, LANES), F32),
                        pltpu.VMEM((len(DILATIONS), 2 * BLOCK, 2 * BLOCK), F32)],
        compiler_params=_params("parallel", "parallel", "parallel"),
    )(z, z, z, z, z, bias)


def _window_kernel(sink_ref, q_ref, kc_ref, kp_ref, vc_ref, vp_ref, bias_ref, o_ref):
    penalty = _prev_key_penalty(pl.program_id(1) == 0)
    first_half = _lane_half((BLOCK, LANES)) == 0
    first_half2 = _lane_half((2 * BLOCK, LANES)) == 0

    def block(n, first):
        start = n * BLOCK if isinstance(n, int) else pl.multiple_of(n * BLOCK, BLOCK)
        cur = pl.ds(start, BLOCK)
        if first:
            src_k, src_v, prev = kp_ref, vp_ref, pl.ds(SPAN - BLOCK, BLOCK)
        else:
            src_k, src_v, prev = kc_ref, vc_ref, pl.ds(pl.multiple_of(start - BLOCK, BLOCK), BLOCK)
        k2 = jnp.concatenate([src_k[prev, :], kc_ref[cur, :]], axis=0)
        vf = jnp.concatenate([src_v[prev, :], vc_ref[cur, :]], axis=0).astype(F32)
        v_swap = pltpu.roll(vf, HEAD_DIM, 1)
        for i in range(A_KV_HEADS):
            qf = q_ref[cur, i * LANES:(i + 1) * LANES].astype(F32) * BAND_Q_SCALE
            q_swap = pltpu.roll(qf, HEAD_DIM, 1)
            q2 = jnp.concatenate([jnp.where(first_half == (i == 0), qf if a == i else q_swap, 0.0)
                                  for a in range(2)], axis=0).astype(BF16)
            v2 = jnp.where(first_half2 == (i == 0), vf, v_swap).astype(BF16)
            o, lse = _band_pair(q2, k2, v2, bias_ref[i], penalty if first else None)
            outs = [o[a * BLOCK:(a + 1) * BLOCK]
                    / (1.0 + jnp.exp2(sink_ref[2 * i + a] * LOG2E - lse[a * BLOCK:(a + 1) * BLOCK]))
                    for a in range(2)]
            o_ref[cur, i * LANES:(i + 1) * LANES] = jnp.where(first_half, outs[0], outs[1]).astype(o_ref.dtype)

    block(0, True)
    lax.fori_loop(1, SPAN // BLOCK, lambda n, _: block(n, False), None, unroll=5)


def _window_attention(z, bias, sinks, n_seq, seq_len):
    spans = seq_len // SPAN
    ka, va = COL_KA // LANES, COL_VA // LANES
    blk = (SPAN, LANES)
    cur = lambda col: (lambda b, s: (b * spans + s, col))
    prev = lambda col: (lambda b, s: (b * spans + jnp.maximum(s - 1, 0), col))
    return pl.pallas_call(
        _window_kernel,
        out_shape=jax.ShapeDtypeStruct((n_seq * seq_len, A_Q_W), BF16),
        grid=(n_seq, spans),
        in_specs=[pl.BlockSpec(memory_space=pltpu.SMEM),
                  pl.BlockSpec((SPAN, A_Q_W), lambda b, s: (b * spans + s, COL_QA // A_Q_W)),
                  pl.BlockSpec(blk, cur(ka)), pl.BlockSpec(blk, prev(ka)),
                  pl.BlockSpec(blk, cur(va)), pl.BlockSpec(blk, prev(va)),
                  pl.BlockSpec((A_KV_HEADS, 2 * BLOCK, 2 * BLOCK), lambda b, s: (0, 0, 0))],
        out_specs=pl.BlockSpec((SPAN, A_Q_W), lambda b, s: (b * spans + s, 0)),
        compiler_params=_params("parallel", "parallel"),
    )(sinks, z, z, z, z, z, bias)


def _shift_rows(u, filler, k):
    rolled = pltpu.roll(u, k, 0)
    row = lax.broadcasted_iota(jnp.int32, u.shape, 0)
    n_fill = filler.shape[0]
    for j in range(k):
        rolled = jnp.where(row == j, filler[n_fill - k + j:n_fill - k + j + 1, :], rolled)
    return rolled


def _gated_conv(xc, bg, cg, filler, cw):
    u = cg * xc
    conv = cw[0:1, :] * _shift_rows(u, filler, 2) + cw[1:2, :] * _shift_rows(u, filler, 1) + cw[2:3, :] * u
    return bg * conv, u


def _pad_rows(x, rows):
    return jnp.concatenate([x, jnp.zeros((rows - x.shape[0], x.shape[1]), x.dtype)], axis=0)


def _heads_by_dim(ref, j):
    _, h, d, length = ref.shape
    return ref[j].reshape(h * d, length).astype(BF16)


def _sample_kernel(zh_ref, zf_ref, cak_ref, cav_ref, cbk_ref, cbv_ref, st_ref, cw_ref, sink_ref, bias_a_ref,
                   bias_b_ref, mix_ref, conv_ref, *, n_new):
    z_all = jnp.concatenate([zh_ref[...].astype(F32), zf_ref[...]], axis=1)
    for j in range(z_all.shape[0] // n_new):
        _sample_sequence(j, z_all, cak_ref, cav_ref, cbk_ref, cbv_ref, st_ref, cw_ref, sink_ref, bias_a_ref,
                         bias_b_ref, mix_ref, conv_ref, n_new)


def _sample_sequence(j, z_all, cak_ref, cav_ref, cbk_ref, cbv_ref, st_ref, cw_ref, sink_ref, bias_a_ref,
                     bias_b_ref, mix_ref, conv_ref, n_new):
    tokens = slice(j * n_new, (j + 1) * n_new)
    z = z_all[tokens, :]
    la = cak_ref.shape[3]
    lb = cbk_ref.shape[3]

    ka_new = _pad_rows(z[:, COL_KA:COL_KA + A_KV_W], LANES).astype(BF16)
    va_new = _pad_rows(z[:, COL_VA:COL_VA + A_KV_W], LANES).astype(BF16)
    half = _lane_half((n_new, LANES))
    pieces = []
    for i in range(A_KV_HEADS):
        qf = z[:, COL_QA + i * LANES:COL_QA + (i + 1) * LANES] * ATTN_SCALE
        for a in range(2):
            pieces.append(jnp.where(half == i, qf if a == i else pltpu.roll(qf, HEAD_DIM, 1), 0.0))
    qa = jnp.concatenate(pieces, axis=0).astype(BF16)
    lc = jnp.dot(qa, _heads_by_dim(cak_ref, j), preferred_element_type=F32) + bias_a_ref[:, :la]
    ln = _nt_dot(qa, ka_new) + bias_a_ref[:, la:]
    m = jnp.maximum(jnp.max(lc, axis=-1, keepdims=True), jnp.max(ln, axis=-1, keepdims=True))
    pc = jnp.exp(lc - m)
    pn = jnp.exp(ln - m)
    s = jnp.sum(pc, axis=-1, keepdims=True) + jnp.sum(pn, axis=-1, keepdims=True)
    oa = (_nt_dot(pc.astype(BF16), _heads_by_dim(cav_ref, j))
          + jnp.dot(pn.astype(BF16), va_new, preferred_element_type=F32))
    oa = oa / s * jax.nn.sigmoid(m + jnp.log(s) - sink_ref[...])
    oa_blocks = []
    for i in range(A_KV_HEADS):
        per_half = []
        for a in range(2):
            rows = oa[(2 * i + a) * n_new:(2 * i + a + 1) * n_new, :]
            per_half.append(rows if a == i else pltpu.roll(rows, HEAD_DIM, 1))
        oa_blocks.append(jnp.where(half == 0, per_half[0], per_half[1]))

    qf = z[:, COL_QB:COL_QB + B_W] * ATTN_SCALE
    head_of_lane = _lane_half((n_new, B_W))
    qb = jnp.concatenate([jnp.where(head_of_lane == h, qf, 0.0) for h in range(B_HEADS)], axis=0).astype(BF16)
    kb_new = _pad_rows(z[:, COL_KB:COL_KB + B_W], LANES).astype(BF16)
    vb_new = _pad_rows(z[:, COL_VB:COL_VB + B_W], LANES).astype(BF16)
    lg_c = jnp.dot(qb, _heads_by_dim(cbk_ref, j), preferred_element_type=F32)
    lg_n = _nt_dot(qb, kb_new)
    parts = []
    for bi, (w, r) in enumerate(B_BRANCHES):
        lo = lb - min(lb, -(-w // LANES) * LANES)
        lc = lg_c[:, lo:] + bias_b_ref[bi, :, lo:lb]
        ln = lg_n + bias_b_ref[bi, :, lb:]
        m = jnp.maximum(jnp.max(lc, axis=-1, keepdims=True), jnp.max(ln, axis=-1, keepdims=True))
        pc = jnp.exp(lc - m)
        pn = jnp.exp(ln - m)
        s = jnp.sum(pc, axis=-1, keepdims=True) + jnp.sum(pn, axis=-1, keepdims=True)
        parts.append((lo, pc, pn, s, m + jnp.log(s)))
    m_all = functools.reduce(jnp.maximum, [p[4] for p in parts])
    ws = [jnp.exp(p[4] - m_all) for p in parts]
    den = sum(ws)
    p_new = None
    los = sorted({p[0] for p in parts} | {lb})
    segs = [None] * (len(los) - 1)
    for (lo, pc, pn, s, _), w in zip(parts, ws):
        coef = w / (den * s)
        p_new = coef * pn if p_new is None else p_new + coef * pn
        for si in range(len(segs)):
            a0, a1 = los[si], los[si + 1]
            if a0 >= lo:
                piece = coef * pc[:, a0 - lo:a1 - lo]
                segs[si] = piece if segs[si] is None else segs[si] + piece
    p_cache = jnp.concatenate(segs, axis=1) if len(segs) > 1 else segs[0]
    ob = (_nt_dot(p_cache.astype(BF16), _heads_by_dim(cbv_ref, j))
          + jnp.dot(p_new.astype(BF16), vb_new, preferred_element_type=F32))
    ob_rows = sum(jnp.where(head_of_lane == h, ob[h * n_new:(h + 1) * n_new, :], 0.0) for h in range(B_HEADS))

    cz = z[:, COL_CONV:COL_CONV + 3 * C_WIDTH]
    oc, u = _gated_conv(cz[:, :C_WIDTH], cz[:, C_WIDTH:2 * C_WIDTH], cz[:, 2 * C_WIDTH:], st_ref[j], cw_ref[...])
    conv_ref[j] = u[n_new - (CONV_WIDTH - 1):, :]
    mix_ref[tokens, :] = jnp.concatenate(oa_blocks + [ob_rows, oc], axis=1)


def _sample_mixer(zh, zf, row0, cak, cav, cbk, cbv, layer, state, cw, sink_rows, bias_a, bias_b, n_new):
    n_seq = cak.shape[1]
    g = SEQS_PER_STEP
    rows = g * n_new
    blk0 = row0 // rows
    cache = lambda a: pl.BlockSpec((None, g) + a.shape[2:], lambda b: (layer, b, 0, 0, 0))
    per_seq = lambda shape: pl.BlockSpec((g,) + shape, lambda b: (b, 0, 0))
    const = lambda a: pl.BlockSpec(a.shape, lambda b: (0,) * a.ndim)
    return pl.pallas_call(
        functools.partial(_sample_kernel, n_new=n_new),
        out_shape=(jax.ShapeDtypeStruct((n_seq * n_new, D_MODEL), F32),
                   jax.ShapeDtypeStruct((n_seq, CONV_WIDTH - 1, C_WIDTH), F32)),
        grid=(n_seq // g,),
        in_specs=[pl.BlockSpec((rows, SPLIT), lambda b: (blk0 + b, 0)),
                  pl.BlockSpec((rows, IN_WIDTH - SPLIT), lambda b: (blk0 + b, 0)),
                  cache(cak), cache(cav), cache(cbk), cache(cbv),
                  per_seq((CONV_WIDTH - 1, C_WIDTH)), const(cw), const(sink_rows), const(bias_a), const(bias_b)],
        out_specs=(pl.BlockSpec((rows, D_MODEL), lambda b: (b, 0)), per_seq((CONV_WIDTH - 1, C_WIDTH))),
        compiler_params=_params("parallel", vmem_limit=DECODE_VMEM_LIMIT),
    )(zh, zf, cak, cav, cbk, cbv, state, cw, sink_rows, bias_a, bias_b)


def _first_index(vals, best):
    idx = jnp.full(best.shape, len(vals) - 1, jnp.int32)
    for j in range(len(vals) - 2, -1, -1):
        idx = jnp.where(vals[j] == best, j, idx)
    return idx


def _route(lt):
    g = [lt[k:k + 1, :] for k in range(N_GROUPS)]
    g_max = functools.reduce(jnp.maximum, g)
    g_idx = _first_index(g, g_max)
    g_w = 1.0 / sum(jnp.exp(v - g_max) for v in g)
    e = []
    for j in range(EXPERTS_PER_GROUP):
        v = lt[N_GROUPS + j:N_GROUPS + j + 1, :]
        for gi in range(1, N_GROUPS):
            row = N_GROUPS + gi * EXPERTS_PER_GROUP + j
            v = jnp.where(g_idx == gi, lt[row:row + 1, :], v)
        e.append(v)
    e1 = functools.reduce(jnp.maximum, e)
    i1 = _first_index(e, e1)
    rest = [jnp.where(i1 == j, -jnp.inf, e[j]) for j in range(EXPERTS_PER_GROUP)]
    e2 = functools.reduce(jnp.maximum, rest)
    i2 = _first_index(rest, e2)
    t = jnp.exp(e2 - e1)
    w1 = g_w / (1.0 + t)
    w2 = g_w * t / (1.0 + t)
    swap = i2 < i1
    lo = jnp.where(swap, i2, i1)
    hi = jnp.where(swap, i1, i2)
    pair = jnp.zeros_like(lo)
    for p, (slot0, slot1) in enumerate(PAIRS):
        pair = jnp.where((hi == slot0) & (lo == slot1), p, pair)
    bucket = g_idx * len(PAIRS) + pair
    return bucket, jnp.where(swap, w1, w2), jnp.where(swap, w2, w1)


def _outproj_kernel(yp_ref, ys_ref, oa_ref, ob_ref, zc_ref, zh_ref, ms_ref, wout_ref, cw_ref, gn_ref, wr_ref,
                    br_ref, y1_ref, info_ref, cnt_ref, ut_ref, y1_scr, carry_scr, *, prompt_tiles, tiles_per_seq):
    i = pl.program_id(0)
    tile = yp_ref.shape[0]

    @pl.when(i == 0)
    def _():
        carry_scr[...] = jnp.zeros_like(carry_scr)

    @pl.when(i < prompt_tiles)
    def _():
        zc = zc_ref[...].astype(F32)
        zh = zh_ref[...].astype(F32)
        halo = zh[:, 2 * C_WIDTH:] * zh[:, :C_WIDTH]
        halo = jnp.where(i % tiles_per_seq == 0, 0.0, halo)
        oc, u = _gated_conv(zc[:, :C_WIDTH], zc[:, C_WIDTH:2 * C_WIDTH], zc[:, 2 * C_WIDTH:], halo, cw_ref[...])
        ut_ref[...] = u[tile - 8:, :]
        y1_scr[...] = (
            yp_ref[...]
            + jnp.dot(oa_ref[...], wout_ref[0:A_Q_W, :], preferred_element_type=F32)
            + jnp.dot(ob_ref[...], wout_ref[A_Q_W:A_Q_W + B_W, :], preferred_element_type=F32)
            + jnp.dot(oc.astype(BF16), wout_ref[A_Q_W + B_W:, :], preferred_element_type=F32))

    @pl.when(i >= prompt_tiles)
    def _():
        ut_ref[...] = jnp.zeros_like(ut_ref)
        y1_scr[...] = ys_ref[...] + jnp.dot(ms_ref[...].astype(BF16), wout_ref[...], preferred_element_type=F32)

    y1 = y1_scr[...]
    y1_ref[:, :D_MODEL] = y1
    xn = _rms(y1, gn_ref[...])

    x_hi = xn.astype(BF16)
    x_lo = (xn - x_hi.astype(F32)).astype(BF16)
    wr = wr_ref[...]
    w_hi = wr.astype(BF16)
    w_lo = (wr - w_hi.astype(F32)).astype(BF16)
    lt_hi = _nt_dot(jnp.concatenate([w_hi, w_lo], axis=0), x_hi)
    lt = lt_hi[:ROUTE_ROWS] + lt_hi[ROUTE_ROWS:] + _nt_dot(w_hi, x_lo) + br_ref[...]
    bucket, w_slot0, w_slot1 = _route(lt)
    onehot = (lax.broadcasted_iota(jnp.int32, (ROUTE_ROWS, tile), 0) == bucket).astype(F32)
    upper = (lax.broadcasted_iota(jnp.int32, (tile, tile), 0)
             <= lax.broadcasted_iota(jnp.int32, (tile, tile), 1)).astype(BF16)
    running = jnp.dot(onehot.astype(BF16), upper, preferred_element_type=F32)
    carry = carry_scr[...]
    rank = jnp.sum(onehot * (running - 1.0 + carry), axis=0, keepdims=True)
    carry = carry + jnp.sum(onehot, axis=1, keepdims=True)
    carry_scr[...] = carry
    cnt_ref[...] = jnp.broadcast_to(carry, cnt_ref.shape)
    info_ref[...] = jnp.concatenate([bucket.astype(F32), rank, jnp.zeros((SUBLANES - 2, tile), F32)], axis=0)
    y1_ref[:, D_MODEL:] = jnp.concatenate([w_slot0, w_slot1, jnp.zeros((LANES - 2, tile), F32)], axis=0).T


def _outproj_route(yp, ys, sample_row0, oa, ob, z, mix_s, w_out, cw, gain, w_route, b_route, n_prompt, seq_len):
    n = n_prompt + mix_s.shape[0]
    tiles = n // ROW_TILE
    p_tiles = n_prompt // ROW_TILE
    halo_rows = 16
    halo_blocks = ROW_TILE // halo_rows
    conv_w = 3 * C_WIDTH
    pidx = lambda i: jnp.minimum(i, p_tiles - 1)
    sidx = lambda i: jnp.maximum(i - p_tiles, 0)
    const = lambda a: pl.BlockSpec(a.shape, lambda i: (0,) * a.ndim)
    gain = gain.reshape(1, D_MODEL)
    return pl.pallas_call(
        functools.partial(_outproj_kernel, prompt_tiles=p_tiles, tiles_per_seq=seq_len // ROW_TILE),
        out_shape=(jax.ShapeDtypeStruct((n, ROW_EXT), F32),
                   jax.ShapeDtypeStruct((8, n), F32),
                   jax.ShapeDtypeStruct((ROUTE_ROWS, LANES), F32),
                   jax.ShapeDtypeStruct((tiles * 8, C_WIDTH), F32)),
        grid=(tiles,),
        in_specs=[pl.BlockSpec((ROW_TILE, D_MODEL), lambda i: (pidx(i), 0)),
                  pl.BlockSpec((ROW_TILE, D_MODEL), lambda i: (sample_row0 // ROW_TILE + sidx(i), 0)),
                  pl.BlockSpec((ROW_TILE, A_Q_W), lambda i: (pidx(i), 0)),
                  pl.BlockSpec((ROW_TILE, B_W), lambda i: (pidx(i), 0)),
                  pl.BlockSpec((ROW_TILE, conv_w), lambda i: (pidx(i), 0)),
                  pl.BlockSpec((halo_rows, conv_w), lambda i: (jnp.maximum(pidx(i) * halo_blocks - 1, 0), 0)),
                  pl.BlockSpec((ROW_TILE, D_MODEL), lambda i: (sidx(i), 0)),
                  const(w_out), const(cw), const(gain), const(w_route), const(b_route)],
        out_specs=(pl.BlockSpec((ROW_TILE, ROW_EXT), lambda i: (i, 0)),
                   pl.BlockSpec((8, ROW_TILE), lambda i: (0, i)),
                   pl.BlockSpec((ROUTE_ROWS, LANES), lambda i: (0, 0)),
                   pl.BlockSpec((8, C_WIDTH), lambda i: (i, 0))),
        scratch_shapes=[pltpu.VMEM((ROW_TILE, D_MODEL), F32), pltpu.VMEM((ROUTE_ROWS, 1), F32)],
        compiler_params=_params("arbitrary"),
    )(yp, ys, oa, ob, z, z, mix_s, w_out, cw, gain, w_route, b_route)


def _moe_kernel(e0_ref, e1_ref, used_ref, x_ref, g_ref, *rest, has_final_gain):
    o_ref = rest[-1]
    weights = rest[:6 * TILES_PER_STEP]
    t = pl.program_id(0)

    @pl.when(used_ref[t * TILES_PER_STEP] > 0)
    def _():
        for k in range(TILES_PER_STEP):
            rows = slice(k * MOE_TILE, (k + 1) * MOE_TILE)
            wg1, wu1, wd1, wg2, wu2, wd2 = weights[6 * k:6 * k + 6]
            y1 = x_ref[rows, :D_MODEL]
            w = x_ref[rows, D_MODEL:]
            x = _rms(y1, g_ref[...]).astype(BF16)

            def expert(wg, wu, wd, scale, x=x):
                g = jnp.dot(x, wg[...], preferred_element_type=F32)
                u = jnp.dot(x, wu[...], preferred_element_type=F32)
                h = g * jax.nn.sigmoid(g) * u * scale
                return jnp.dot(h.astype(BF16), wd[...], preferred_element_type=F32)

            y2 = y1 + expert(wg1, wu1, wd1, w[:, 0:1]) + expert(wg2, wu2, wd2, w[:, 1:2])
            o_ref[rows, :] = _rms(y2, rest[-2][...]) if has_final_gain else y2

    @pl.when(used_ref[t * TILES_PER_STEP] == 0)
    def _():
        o_ref[...] = jnp.zeros_like(o_ref)


def _experts(xs, gain, e_slot0, e_slot1, used, w_gate, w_up, w_down, layer, final_gain=None):
    step_rows = TILES_PER_STEP * MOE_TILE
    n_steps = xs.shape[0] // step_rows

    def weight_specs(k):
        pick = lambda sel: (lambda t, e0, e1, u: (layer, (e0, e1)[sel][t * TILES_PER_STEP + k], 0, 0))
        up = lambda sel: pl.BlockSpec((None, None, D_MODEL, EXPERT_FF), pick(sel))
        down = lambda sel: pl.BlockSpec((None, None, EXPERT_FF, D_MODEL), pick(sel))
        return [up(0), up(0), down(0), up(1), up(1), down(1)]

    gain_spec = pl.BlockSpec((1, D_MODEL), lambda t, e0, e1, u: (0, 0))
    extra = [] if final_gain is None else [final_gain.reshape(1, D_MODEL)]
    grid_spec = pltpu.PrefetchScalarGridSpec(
        num_scalar_prefetch=3,
        grid=(n_steps,),
        in_specs=[pl.BlockSpec((step_rows, ROW_EXT), lambda t, e0, e1, u: (t, 0)), gain_spec]
                 + [spec for k in range(TILES_PER_STEP) for spec in weight_specs(k)] + [gain_spec] * len(extra),
        out_specs=pl.BlockSpec((step_rows, D_MODEL), lambda t, e0, e1, u: (t, 0)))
    return pl.pallas_call(
        functools.partial(_moe_kernel, has_final_gain=final_gain is not None),
        out_shape=jax.ShapeDtypeStruct((xs.shape[0], D_MODEL), F32),
        grid_spec=grid_spec,
        compiler_params=_params("arbitrary"),
    )(e_slot0, e_slot1, used, xs, gain.reshape(1, D_MODEL), *([w_gate, w_up, w_down] * (2 * TILES_PER_STEP)), *extra)


def _dispatch_plan(info, counts, n):
    n_tiles = -(-n // MOE_TILE) + N_ROUTE_BUCKETS
    n_tiles = -(-n_tiles // TILES_PER_STEP) * TILES_PER_STEP
    bucket = info[0].astype(jnp.int32)
    rank = info[1].astype(jnp.int32)
    counts = counts[:N_ROUTE_BUCKETS, 0].astype(jnp.int32)
    tiles_per_bucket = (counts + MOE_TILE - 1) // MOE_TILE
    tile_end = jnp.cumsum(tiles_per_bucket)
    row_start = (tile_end - tiles_per_bucket) * MOE_TILE
    dest = row_start[bucket] + rank
    src = (jnp.arange(n_tiles * MOE_TILE, dtype=jnp.int32) % n).at[dest].set(jnp.arange(n, dtype=jnp.int32))
    tile_ids = jnp.arange(n_tiles, dtype=jnp.int32)
    tile_bucket = jnp.sum((tile_ids[:, None] >= tile_end[None, :]).astype(jnp.int32), axis=1)
    tile_bucket = jnp.minimum(tile_bucket, N_ROUTE_BUCKETS - 1)
    used = (tile_ids < tile_end[-1]).astype(jnp.int32)
    pair = tile_bucket % len(PAIRS)
    base = (tile_bucket // len(PAIRS)) * EXPERTS_PER_GROUP
    pairs = jnp.asarray(PAIRS, jnp.int32)
    return dest, src, base + pairs[pair, 0], base + pairs[pair, 1], used


def _permute_in_columns(w):
    attn = A_Q_W + 2 * A_KV_W + 3 * B_W
    return jnp.concatenate([w[:, attn:], w[:, A_Q_W:A_Q_W + A_KV_W], w[:, :A_Q_W], w[:, A_Q_W + A_KV_W:attn]],
                           axis=1)


def kernel(x_prompt, x_sample, cache_a_k, cache_a_v, cache_b_k, cache_b_v, state_conv, rel_bias_table,
           w_in, w_out, conv_w, attn_sinks, norm_mix, norm_ffn, w_group, b_group, w_router, b_router,
           w_gate, w_up, w_down, norm_final):
    n_seq, seq_len, _ = x_prompt.shape
    dec_seq, n_new, _ = x_sample.shape
    depth = w_in.shape[0]
    n_prompt = n_seq * seq_len
    n_sample = dec_seq * n_new
    n = n_prompt + n_sample
    la, lb = cache_a_k.shape[2], cache_b_k.shape[2]
    assert seq_len % SPAN == 0 and n_prompt % IN_TILE == 0 and n_sample % IN_TILE == 0 and IN_TILE % ROW_TILE == 0
    assert seq_len >= SPAN and la % LANES == 0 and lb % LANES == 0 and n_new == 8
    assert dec_seq % SEQS_PER_STEP == 0 and n_prompt % (SEQS_PER_STEP * n_new) == 0

    table_a, table_b = rel_bias_table[:, :A_Q_HEADS], rel_bias_table[:, A_Q_HEADS:]
    bias_a = _band_bias(table_a, (1,))[:, 0]
    bias_b = _band_bias(table_b, DILATIONS)
    sbias_a = _sample_bias(table_a, n_new, la, np.arange(la + LANES), ((A_WINDOW, 1),))[0]
    sbias_b = _sample_bias(table_b, n_new, lb, np.arange(lb + LANES), B_BRANCHES)
    cak, cav, cbk, cbv = (c.transpose(0, 1, 3, 4, 2) for c in (cache_a_k, cache_a_v, cache_b_k, cache_b_v))

    w_in_b = jnp.stack([_permute_in_columns(w_in[l]) for l in range(depth)]).astype(BF16)
    w_out_b = w_out.astype(BF16)
    expert_w = tuple(w.astype(BF16) for w in (w_gate, w_up, w_down))
    pad = ROUTE_ROWS - N_GROUPS - N_EXPERTS
    w_route = jnp.pad(jnp.concatenate([w_group, w_router], axis=2).transpose(0, 2, 1), ((0, 0), (0, pad), (0, 0)))
    b_route = jnp.pad(jnp.concatenate([b_group, b_router], axis=1), ((0, 0), (0, pad)))[..., None]
    sink_rows = jnp.repeat(attn_sinks, n_new, axis=1)[..., None]

    yp, ys, s_row0 = x_prompt.reshape(n_prompt, D_MODEL), x_sample.reshape(n_sample, D_MODEL), 0
    states = []
    for l in range(depth):
        lap, lbp = min(A_WINDOW, seq_len), min(SPAN, seq_len)
        zh, zf, kbt, vbt = _inproj(yp, ys, s_row0, n_prompt, n_sample, norm_mix[l], w_in_b[l], seq_len, lbp)
        oa = _window_attention(zh, bias_a, attn_sinks[l], n_seq, seq_len)
        ob = _dilated_attention(zf, bias_b, n_seq, seq_len)
        mix_s, conv_s = _sample_mixer(zh, zf, n_prompt, cak, cav, cbk, cbv, l, state_conv[l], conv_w[l], sink_rows[l],
                                      sbias_a, sbias_b, n_new)
        y1, info, counts, u_tail = _outproj_route(
            yp, ys, s_row0, oa, ob, zh, mix_s, w_out_b[l], conv_w[l], norm_ffn[l], w_route[l], b_route[l], n_prompt, seq_len)
        def columns(col):
            return (zh, col) if col < SPLIT else (zf, col - SPLIT)

        def prompt_tail(length, col, width, heads):
            z, c = columns(col)
            rows = [z[(b + 1) * seq_len - length:(b + 1) * seq_len, c:c + width] for b in range(n_seq)]
            return jnp.stack(rows).astype(F32).reshape(n_seq, length, heads, HEAD_DIM)

        def sample_rows(col, width, heads):
            z, c = columns(col)
            return z[n_prompt:, c:c + width].astype(F32).reshape(dec_seq, n_new, heads, HEAD_DIM)

        last_tile = [((b + 1) * seq_len // ROW_TILE - 1) * SUBLANES for b in range(n_seq)]
        conv_p = jnp.stack([u_tail[t + SUBLANES - (CONV_WIDTH - 1):t + SUBLANES, :] for t in last_tile])
        layer_states = (
            prompt_tail(lap, COL_KA, A_KV_W, A_KV_HEADS), prompt_tail(lap, COL_VA, A_KV_W, A_KV_HEADS),
            kbt.reshape(n_seq, B_HEADS, HEAD_DIM, lbp).transpose(0, 3, 1, 2),
            vbt.reshape(n_seq, B_HEADS, HEAD_DIM, lbp).transpose(0, 3, 1, 2), conv_p,
            sample_rows(COL_KA, A_KV_W, A_KV_HEADS), sample_rows(COL_VA, A_KV_W, A_KV_HEADS),
            sample_rows(COL_KB, B_W, B_HEADS), sample_rows(COL_VB, B_W, B_HEADS), conv_s)

        dest, src, e_lo, e_hi, used = _dispatch_plan(info, counts, n)
        xs = y1[src]
        if l == depth - 1:
            xs, layer_states = lax.optimization_barrier((xs, layer_states))
        y_sorted = _experts(xs, norm_ffn[l], e_lo, e_hi, used, *expert_w, l,
                            final_gain=norm_final if l == depth - 1 else None)
        if l < depth - 1:
            y_next, layer_states = lax.optimization_barrier((y_sorted[dest], layer_states))
            yp, ys, s_row0 = y_next, y_next, n_prompt
        states.append(layer_states)

    y_prompt = y_sorted[dest[:n_prompt]].reshape(n_seq, seq_len, D_MODEL)
    y_sample = y_sorted[dest[n_prompt:]].reshape(dec_seq, n_new, D_MODEL)
    st = [jnp.stack([s[k] for s in states]) for k in range(10)]
    return (y_prompt, y_sample, st[0], st[1], st[2], st[3], st[4], st[5], st[6], st[7], st[8], st[9])
```

```python
import functools
import math

import numpy as np
import jax
import jax.numpy as jnp
from jax import lax
from jax.experimental import pallas as pl
from jax.experimental.pallas import tpu as pltpu

F32 = jnp.float32
BF16 = jnp.bfloat16

D_MODEL = 1024
HEAD_DIM = 64
ATTN_SCALE = HEAD_DIM ** -0.5
LOG2E = math.log2(math.e)
BAND_Q_SCALE = ATTN_SCALE * LOG2E
BLOCK = 128
LANES = 128
SUBLANES = 8
A_Q_HEADS = 4
A_KV_HEADS = 2
A_WINDOW = 128
B_HEADS = 6
B_BRANCHES = ((128, 1), (512, 4), (2048, 16))
DILATIONS = tuple(r for _, r in B_BRANCHES)
SPAN = BLOCK * max(DILATIONS)
C_WIDTH = 6 * HEAD_DIM
CONV_WIDTH = 3
A_Q_W = A_Q_HEADS * HEAD_DIM
A_KV_W = A_KV_HEADS * HEAD_DIM
B_W = B_HEADS * HEAD_DIM
IN_WIDTH = A_Q_W + 2 * A_KV_W + 3 * B_W + 3 * C_WIDTH
N_BUCKETS = 32
MAX_DISTANCE = 2048
N_GROUPS = 4
EXPERTS_PER_GROUP = 4
N_EXPERTS = N_GROUPS * EXPERTS_PER_GROUP
EXPERT_FF = 512
RMS_EPS = 1e-6
NEG_INF = -1e30

COL_CONV = 0
COL_KA = 3 * C_WIDTH
COL_QA = COL_KA + A_KV_W
COL_VA = COL_QA + A_Q_W
COL_QB = COL_VA + A_KV_W
COL_KB = COL_QB + B_W
COL_VB = COL_KB + B_W
SPLIT = COL_QB

ROW_TILE = 512
IN_TILE = 1024
MOE_TILE = 256
TILES_PER_STEP = 3
SEQS_PER_STEP = 4
DECODE_VMEM_LIMIT = 60 * 1024 * 1024
PAIRS = ((1, 0), (2, 0), (2, 1), (3, 1), (3, 0), (3, 2))
N_ROUTE_BUCKETS = N_GROUPS * len(PAIRS)
ROUTE_ROWS = 32
ROW_EXT = D_MODEL + LANES
VMEM_LIMIT = 56 * 1024 * 1024


def _params(*sem, vmem_limit=VMEM_LIMIT):
    return pltpu.CompilerParams(dimension_semantics=sem, vmem_limit_bytes=vmem_limit)


def _rel_bucket(dist):
    d = np.maximum(dist, 0)
    max_exact = N_BUCKETS // 2
    df = np.maximum(d, max_exact).astype(np.float32)
    large = max_exact + (np.log(df / max_exact) / math.log(MAX_DISTANCE / max_exact)
                         * (N_BUCKETS - max_exact)).astype(np.int32)
    large = np.minimum(large, N_BUCKETS - 1)
    return np.where(d < max_exact, d, large)


def _masked_bias(table, dist, valid):
    bucket = _rel_bucket(dist).reshape(-1, 1)
    onehot = (jnp.asarray(bucket, jnp.int32) == jnp.arange(N_BUCKETS, dtype=jnp.int32)[None, :]).astype(F32)
    b = jnp.dot(onehot, table.astype(F32), precision=lax.Precision.HIGHEST)
    b = jnp.where(jnp.asarray(valid.reshape(-1, 1)), b, NEG_INF)
    return jnp.moveaxis(b.reshape(dist.shape + (table.shape[1],)), -1, 0)


def _band_bias(table, scales):
    i = np.arange(BLOCK)[:, None]
    j = np.arange(2 * BLOCK)[None, :]
    dist = i + BLOCK - j
    dists = np.stack([dist * r for r in scales])
    valid = np.broadcast_to((dist >= 0) & (dist <= BLOCK), dists.shape)
    b = _masked_bias(table, dists, valid) * LOG2E
    pairs = table.shape[1] // 2
    b = b.reshape(pairs, 2, len(scales), BLOCK, 2 * BLOCK).transpose(0, 2, 1, 3, 4)
    return b.reshape(pairs, len(scales), 2 * BLOCK, 2 * BLOCK)


def _sample_bias(table, n_new, first_new, positions, branches):
    dist = first_new + np.arange(n_new)[:, None] - np.asarray(positions)[None, :]
    valid = np.stack([(dist >= 0) & (dist <= w) & (dist % r == 0) for w, r in branches])
    b = _masked_bias(table, np.broadcast_to(dist, valid.shape), valid)
    return b.transpose(1, 0, 2, 3).reshape(len(branches), table.shape[1] * n_new, dist.shape[1])


def _rms(x, gain):
    return x * lax.rsqrt(jnp.mean(x * x, axis=-1, keepdims=True) + RMS_EPS) * gain


def _inproj_kernel(xp_ref, xs_ref, g_ref, w_ref, zh_ref, zf_ref, kbt_ref, vbt_ref, *, prompt_tiles, tiles_per_seq,
                   tail_tiles):
    i = pl.program_id(0)

    def project(x_ref):
        h = _rms(x_ref[...], g_ref[...])
        z = jnp.dot(h.astype(BF16), w_ref[...], preferred_element_type=F32)
        zh_ref[...] = z[:, :SPLIT].astype(BF16)
        zf_ref[...] = z[:, SPLIT:]

    pl.when(i < prompt_tiles)(lambda: project(xp_ref))
    pl.when(i >= prompt_tiles)(lambda: project(xs_ref))

    @pl.when((i < prompt_tiles) & (i % tiles_per_seq >= tiles_per_seq - tail_tiles))
    def _():
        kbt_ref[...] = zf_ref[:, COL_KB - SPLIT:COL_KB - SPLIT + B_W].T
        vbt_ref[...] = zf_ref[:, COL_VB - SPLIT:COL_VB - SPLIT + B_W].T


def _inproj(xp, xs, sample_row0, n_prompt, n_sample, gain, w, seq_len, tail_len):
    p_tiles, s_tile0 = n_prompt // IN_TILE, sample_row0 // IN_TILE
    tps, tail = seq_len // IN_TILE, tail_len // IN_TILE
    n_seq = n_prompt // seq_len

    def tail_block(i):
        j = jnp.minimum(i, p_tiles - 1)
        return j // tps, 0, jnp.maximum(j % tps - (tps - tail), 0)

    tail_spec = pl.BlockSpec((None, B_W, IN_TILE), tail_block)
    tail_shape = jax.ShapeDtypeStruct((n_seq, B_W, tail_len), F32)
    return pl.pallas_call(
        functools.partial(_inproj_kernel, prompt_tiles=p_tiles, tiles_per_seq=tps, tail_tiles=tail),
        out_shape=(jax.ShapeDtypeStruct((n_prompt + n_sample, SPLIT), BF16),
                   jax.ShapeDtypeStruct((n_prompt + n_sample, IN_WIDTH - SPLIT), F32), tail_shape, tail_shape),
        grid=((n_prompt + n_sample) // IN_TILE,),
        in_specs=[pl.BlockSpec((IN_TILE, D_MODEL), lambda i: (jnp.minimum(i, p_tiles - 1), 0)),
                  pl.BlockSpec((IN_TILE, D_MODEL), lambda i: (s_tile0 + jnp.maximum(i - p_tiles, 0), 0)),
                  pl.BlockSpec((1, D_MODEL), lambda i: (0, 0)),
                  pl.BlockSpec((D_MODEL, IN_WIDTH), lambda i: (0, 0))],
        out_specs=(pl.BlockSpec((IN_TILE, SPLIT), lambda i: (i, 0)),
                   pl.BlockSpec((IN_TILE, IN_WIDTH - SPLIT), lambda i: (i, 0)), tail_spec, tail_spec),
        compiler_params=_params("arbitrary"),
    )(xp, xs, gain.reshape(1, D_MODEL), w)


def _lane_half(shape):
    return lax.broadcasted_iota(jnp.int32, shape, len(shape) - 1) // HEAD_DIM


def _nt_dot(a, b):
    return lax.dot_general(a, b, (((1,), (1,)), ((), ())), preferred_element_type=F32)

def _band_pair(q2, k2, v2, bias2, penalty, normalise=True):
    lg = _nt_dot(q2, k2) + bias2
    if penalty is not None:
        lg = lg + penalty
    m = jnp.max(lg, axis=-1, keepdims=True)
    p = jnp.exp2(lg - m)
    s = jnp.sum(p, axis=-1, keepdims=True)
    acc = jnp.dot(p.astype(BF16), v2, preferred_element_type=F32)
    if not normalise:
        return acc, m, s
    return acc / s, m + jnp.log2(s)


def _prev_key_penalty(first_span):
    col = lax.broadcasted_iota(jnp.int32, (1, 2 * BLOCK), 1)
    return jnp.where((col < BLOCK) & first_span, NEG_INF, 0.0).astype(F32)


def _rows(start, dilation):
    if dilation == 1:
        return pl.ds(start, BLOCK)
    return pl.ds(start, BLOCK, stride=dilation)


def _dilated_kernel(q_ref, kc_ref, kp_ref, vc_ref, vp_ref, bias_ref, o_ref, o_scr, l_scr, s_scr, first_bias_scr):
    first_half = _lane_half((BLOCK, LANES)) == 0
    penalty = _prev_key_penalty(pl.program_id(1) == 0)
    for bi in range(len(DILATIONS)):
        first_bias_scr[bi] = bias_ref[bi] + penalty

    for bi, r in enumerate(DILATIONS):
        step = r * BLOCK

        def block(c, n, first, bi=bi, r=r, step=step):
            q_start = c + n * step
            if not isinstance(q_start, int):
                q_start = pl.multiple_of(q_start, BLOCK) if r == 1 else q_start
            cur = _rows(q_start, r)
            qt = (q_ref[cur, :] * BAND_Q_SCALE).astype(BF16)
            if first:
                src_k, src_v, prev = kp_ref, vp_ref, _rows(SPAN - step + c, r)
            else:
                p_start = q_start - step
                if r == 1:
                    p_start = pl.multiple_of(p_start, BLOCK)
                src_k, src_v, prev = kc_ref, vc_ref, _rows(p_start, r)
            k2 = jnp.concatenate([src_k[prev, :], kc_ref[cur, :]], axis=0).astype(BF16)
            v2 = jnp.concatenate([src_v[prev, :], vc_ref[cur, :]], axis=0).astype(BF16)
            zero = jnp.zeros_like(qt)
            q2 = jnp.concatenate([jnp.where(first_half, qt, zero), jnp.where(first_half, zero, qt)], axis=0)
            acc, m, s = _band_pair(q2, k2, v2, first_bias_scr[bi] if first else bias_ref[bi], None, normalise=False)
            o_scr[bi, cur, :] = jnp.where(first_half, acc[:BLOCK], acc[BLOCK:])
            l_scr[bi, cur, :] = jnp.where(first_half, m[:BLOCK], m[BLOCK:])
            s_scr[bi, cur, :] = jnp.where(first_half, s[:BLOCK], s[BLOCK:])

        n_blocks = SPAN // step
        if r == 1:
            block(0, 0, True)
            lax.fori_loop(1, n_blocks, lambda n, _: block(0, n, False), None, unroll=True)
        else:
            def per_class(c, _, block=block, n_blocks=n_blocks):
                block(c, 0, True)
                for n in range(1, n_blocks):
                    block(c, n, False)
            lax.fori_loop(0, r, per_class, None, unroll=True)

    chunk = 2 * BLOCK

    def combine(j, _):
        rows = pl.ds(pl.multiple_of(j * chunk, chunk), chunk)
        ms = [l_scr[bi, rows, :] for bi in range(len(DILATIONS))]
        m = functools.reduce(jnp.maximum, ms)
        ws = [jnp.exp2(mb - m) for mb in ms]
        num = sum(w * o_scr[bi, rows, :] for bi, w in enumerate(ws))
        den = sum(w * s_scr[bi, rows, :] for bi, w in enumerate(ws))
        o_ref[rows, :] = (num / den).astype(o_ref.dtype)

    lax.fori_loop(0, SPAN // chunk, combine, None)


def _dilated_attention(z, bias, n_seq, seq_len):
    spans = seq_len // SPAN
    qb, kb, vb = ((c - SPLIT) // LANES for c in (COL_QB, COL_KB, COL_VB))
    blk = (SPAN, LANES)
    cur = lambda col: (lambda b, s, i: (b * spans + s, col + i))
    prev = lambda col: (lambda b, s, i: (b * spans + jnp.maximum(s - 1, 0), col + i))
    return pl.pallas_call(
        _dilated_kernel,
        out_shape=jax.ShapeDtypeStruct((n_seq * seq_len, B_W), BF16),
        grid=(n_seq, spans, B_W // LANES),
        in_specs=[pl.BlockSpec(blk, cur(qb)),
                  pl.BlockSpec(blk, cur(kb)), pl.BlockSpec(blk, prev(kb)),
                  pl.BlockSpec(blk, cur(vb)), pl.BlockSpec(blk, prev(vb)),
                  pl.BlockSpec((None, len(DILATIONS), 2 * BLOCK, 2 * BLOCK), lambda b, s, i: (i, 0, 0, 0))],
        out_specs=pl.BlockSpec(blk, lambda b, s, i: (b * spans + s, i)),
        scratch_shapes=[pltpu.VMEM((len(DILATIONS), SPAN, LANES), F32),
                        pltpu.VMEM((len(DILATIONS), SPAN, LANES), F32),
                        pltpu.VMEM((len(DILATIONS), SPAN, LANES), F32),
                        pltpu.VMEM((len(DILATIONS), 2 * BLOCK, 2 * BLOCK), F32)],
        compiler_params=_params("parallel", "parallel", "parallel"),
    )(z, z, z, z, z, bias)


def _window_kernel(sink_ref, q_ref, kc_ref, kp_ref, vc_ref, vp_ref, bias_ref, o_ref):
    penalty = _prev_key_penalty(pl.program_id(1) == 0)
    first_half = _lane_half((BLOCK, LANES)) == 0
    first_half2 = _lane_half((2 * BLOCK, LANES)) == 0

    def block(n, first):
        start = n * BLOCK if isinstance(n, int) else pl.multiple_of(n * BLOCK, BLOCK)
        cur = pl.ds(start, BLOCK)
        if first:
            src_k, src_v, prev = kp_ref, vp_ref, pl.ds(SPAN - BLOCK, BLOCK)
        else:
            src_k, src_v, prev = kc_ref, vc_ref, pl.ds(pl.multiple_of(start - BLOCK, BLOCK), BLOCK)
        k2 = jnp.concatenate([src_k[prev, :], kc_ref[cur, :]], axis=0)
        vf = jnp.concatenate([src_v[prev, :], vc_ref[cur, :]], axis=0).astype(F32)
        v_swap = pltpu.roll(vf, HEAD_DIM, 1)
        for i in range(A_KV_HEADS):
            qf = q_ref[cur, i * LANES:(i + 1) * LANES].astype(F32) * BAND_Q_SCALE
            q_swap = pltpu.roll(qf, HEAD_DIM, 1)
            q2 = jnp.concatenate([jnp.where(first_half == (i == 0), qf if a == i else q_swap, 0.0)
                                  for a in range(2)], axis=0).astype(BF16)
            v2 = jnp.where(first_half2 == (i == 0), vf, v_swap).astype(BF16)
            o, lse = _band_pair(q2, k2, v2, bias_ref[i], penalty if first else None)
            outs = [o[a * BLOCK:(a + 1) * BLOCK]
                    / (1.0 + jnp.exp2(sink_ref[2 * i + a] * LOG2E - lse[a * BLOCK:(a + 1) * BLOCK]))
                    for a in range(2)]
            o_ref[cur, i * LANES:(i + 1) * LANES] = jnp.where(first_half, outs[0], outs[1]).astype(o_ref.dtype)

    block(0, True)
    lax.fori_loop(1, SPAN // BLOCK, lambda n, _: block(n, False), None, unroll=5)


def _window_attention(z, bias, sinks, n_seq, seq_len):
    spans = seq_len // SPAN
    ka, va = COL_KA // LANES, COL_VA // LANES
    blk = (SPAN, LANES)
    cur = lambda col: (lambda b, s: (b * spans + s, col))
    prev = lambda col: (lambda b, s: (b * spans + jnp.maximum(s - 1, 0), col))
    return pl.pallas_call(
        _window_kernel,
        out_shape=jax.ShapeDtypeStruct((n_seq * seq_len, A_Q_W), BF16),
        grid=(n_seq, spans),
        in_specs=[pl.BlockSpec(memory_space=pltpu.SMEM),
                  pl.BlockSpec((SPAN, A_Q_W), lambda b, s: (b * spans + s, COL_QA // A_Q_W)),
                  pl.BlockSpec(blk, cur(ka)), pl.BlockSpec(blk, prev(ka)),
                  pl.BlockSpec(blk, cur(va)), pl.BlockSpec(blk, prev(va)),
                  pl.BlockSpec((A_KV_HEADS, 2 * BLOCK, 2 * BLOCK), lambda b, s: (0, 0, 0))],
        out_specs=pl.BlockSpec((SPAN, A_Q_W), lambda b, s: (b * spans + s, 0)),
        compiler_params=_params("parallel", "parallel"),
    )(sinks, z, z, z, z, z, bias)


def _shift_rows(u, filler, k):
    rolled = pltpu.roll(u, k, 0)
    row = lax.broadcasted_iota(jnp.int32, u.shape, 0)
    n_fill = filler.shape[0]
    for j in range(k):
        rolled = jnp.where(row == j, filler[n_fill - k + j:n_fill - k + j + 1, :], rolled)
    return rolled


def _gated_conv(xc, bg, cg, filler, cw):
    u = cg * xc
    conv = cw[0:1, :] * _shift_rows(u, filler, 2) + cw[1:2, :] * _shift_rows(u, filler, 1) + cw[2:3, :] * u
    return bg * conv, u


def _pad_rows(x, rows):
    return jnp.concatenate([x, jnp.zeros((rows - x.shape[0], x.shape[1]), x.dtype)], axis=0)


def _heads_by_dim(ref, j):
    _, h, d, length = ref.shape
    return ref[j].reshape(h * d, length).astype(BF16)


def _sample_kernel(zh_ref, zf_ref, cak_ref, cav_ref, cbk_ref, cbv_ref, st_ref, cw_ref, sink_ref, bias_a_ref,
                   bias_b_ref, mix_ref, conv_ref, *, n_new):
    z_all = jnp.concatenate([zh_ref[...].astype(F32), zf_ref[...]], axis=1)
    for j in range(z_all.shape[0] // n_new):
        _sample_sequence(j, z_all, cak_ref, cav_ref, cbk_ref, cbv_ref, st_ref, cw_ref, sink_ref, bias_a_ref,
                         bias_b_ref, mix_ref, conv_ref, n_new)


def _sample_sequence(j, z_all, cak_ref, cav_ref, cbk_ref, cbv_ref, st_ref, cw_ref, sink_ref, bias_a_ref,
                     bias_b_ref, mix_ref, conv_ref, n_new):
    tokens = slice(j * n_new, (j + 1) * n_new)
    z = z_all[tokens, :]
    la = cak_ref.shape[3]
    lb = cbk_ref.shape[3]

    ka_new = _pad_rows(z[:, COL_KA:COL_KA + A_KV_W], LANES).astype(BF16)
    va_new = _pad_rows(z[:, COL_VA:COL_VA + A_KV_W], LANES).astype(BF16)
    half = _lane_half((n_new, LANES))
    pieces = []
    for i in range(A_KV_HEADS):
        qf = z[:, COL_QA + i * LANES:COL_QA + (i + 1) * LANES] * ATTN_SCALE
        for a in range(2):
            pieces.append(jnp.where(half == i, qf if a == i else pltpu.roll(qf, HEAD_DIM, 1), 0.0))
    qa = jnp.concatenate(pieces, axis=0).astype(BF16)
    lc = jnp.dot(qa, _heads_by_dim(cak_ref, j), preferred_element_type=F32) + bias_a_ref[:, :la]
    ln = _nt_dot(qa, ka_new) + bias_a_ref[:, la:]
    m = jnp.maximum(jnp.max(lc, axis=-1, keepdims=True), jnp.max(ln, axis=-1, keepdims=True))
    pc = jnp.exp(lc - m)
    pn = jnp.exp(ln - m)
    s = jnp.sum(pc, axis=-1, keepdims=True) + jnp.sum(pn, axis=-1, keepdims=True)
    oa = (_nt_dot(pc.astype(BF16), _heads_by_dim(cav_ref, j))
          + jnp.dot(pn.astype(BF16), va_new, preferred_element_type=F32))
    oa = oa / s * jax.nn.sigmoid(m + jnp.log(s) - sink_ref[...])
    oa_blocks = []
    for i in range(A_KV_HEADS):
        per_half = []
        for a in range(2):
            rows = oa[(2 * i + a) * n_new:(2 * i + a + 1) * n_new, :]
            per_half.append(rows if a == i else pltpu.roll(rows, HEAD_DIM, 1))
        oa_blocks.append(jnp.where(half == 0, per_half[0], per_half[1]))

    qf = z[:, COL_QB:COL_QB + B_W] * ATTN_SCALE
    head_of_lane = _lane_half((n_new, B_W))
    qb = jnp.concatenate([jnp.where(head_of_lane == h, qf, 0.0) for h in range(B_HEADS)], axis=0).astype(BF16)
    kb_new = _pad_rows(z[:, COL_KB:COL_KB + B_W], LANES).astype(BF16)
    vb_new = _pad_rows(z[:, COL_VB:COL_VB + B_W], LANES).astype(BF16)
    lg_c = jnp.dot(qb, _heads_by_dim(cbk_ref, j), preferred_element_type=F32)
    lg_n = _nt_dot(qb, kb_new)
    parts = []
    for bi, (w, r) in enumerate(B_BRANCHES):
        lo = lb - min(lb, -(-w // LANES) * LANES)
        lc = lg_c[:, lo:] + bias_b_ref[bi, :, lo:lb]
        ln = lg_n + bias_b_ref[bi, :, lb:]
        m = jnp.maximum(jnp.max(lc, axis=-1, keepdims=True), jnp.max(ln, axis=-1, keepdims=True))
        pc = jnp.exp(lc - m)
        pn = jnp.exp(ln - m)
        s = jnp.sum(pc, axis=-1, keepdims=True) + jnp.sum(pn, axis=-1, keepdims=True)
        parts.append((lo, pc, pn, s, m + jnp.log(s)))
    m_all = functools.reduce(jnp.maximum, [p[4] for p in parts])
    ws = [jnp.exp(p[4] - m_all) for p in parts]
    den = sum(ws)
    p_new = None
    los = sorted({p[0] for p in parts} | {lb})
    segs = [None] * (len(los) - 1)
    for (lo, pc, pn, s, _), w in zip(parts, ws):
        coef = w / (den * s)
        p_new = coef * pn if p_new is None else p_new + coef * pn
        for si in range(len(segs)):
            a0, a1 = los[si], los[si + 1]
            if a0 >= lo:
                piece = coef * pc[:, a0 - lo:a1 - lo]
                segs[si] = piece if segs[si] is None else segs[si] + piece
    p_cache = jnp.concatenate(segs, axis=1) if len(segs) > 1 else segs[0]
    ob = (_nt_dot(p_cache.astype(BF16), _heads_by_dim(cbv_ref, j))
          + jnp.dot(p_new.astype(BF16), vb_new, preferred_element_type=F32))
    ob_rows = sum(jnp.where(head_of_lane == h, ob[h * n_new:(h + 1) * n_new, :], 0.0) for h in range(B_HEADS))

    cz = z[:, COL_CONV:COL_CONV + 3 * C_WIDTH]
    oc, u = _gated_conv(cz[:, :C_WIDTH], cz[:, C_WIDTH:2 * C_WIDTH], cz[:, 2 * C_WIDTH:], st_ref[j], cw_ref[...])
    conv_ref[j] = u[n_new - (CONV_WIDTH - 1):, :]
    mix_ref[tokens, :] = jnp.concatenate(oa_blocks + [ob_rows, oc], axis=1)


def _sample_mixer(zh, zf, row0, cak, cav, cbk, cbv, layer, state, cw, sink_rows, bias_a, bias_b, n_new):
    n_seq = cak.shape[1]
    g = SEQS_PER_STEP
    rows = g * n_new
    blk0 = row0 // rows
    cache = lambda a: pl.BlockSpec((None, g) + a.shape[2:], lambda b: (layer, b, 0, 0, 0))
    per_seq = lambda shape: pl.BlockSpec((g,) + shape, lambda b: (b, 0, 0))
    const = lambda a: pl.BlockSpec(a.shape, lambda b: (0,) * a.ndim)
    return pl.pallas_call(
        functools.partial(_sample_kernel, n_new=n_new),
        out_shape=(jax.ShapeDtypeStruct((n_seq * n_new, D_MODEL), F32),
                   jax.ShapeDtypeStruct((n_seq, CONV_WIDTH - 1, C_WIDTH), F32)),
        grid=(n_seq // g,),
        in_specs=[pl.BlockSpec((rows, SPLIT), lambda b: (blk0 + b, 0)),
                  pl.BlockSpec((rows, IN_WIDTH - SPLIT), lambda b: (blk0 + b, 0)),
                  cache(cak), cache(cav), cache(cbk), cache(cbv),
                  per_seq((CONV_WIDTH - 1, C_WIDTH)), const(cw), const(sink_rows), const(bias_a), const(bias_b)],
        out_specs=(pl.BlockSpec((rows, D_MODEL), lambda b: (b, 0)), per_seq((CONV_WIDTH - 1, C_WIDTH))),
        compiler_params=_params("parallel", vmem_limit=DECODE_VMEM_LIMIT),
    )(zh, zf, cak, cav, cbk, cbv, state, cw, sink_rows, bias_a, bias_b)


def _first_index(vals, best):
    idx = jnp.full(best.shape, len(vals) - 1, jnp.int32)
    for j in range(len(vals) - 2, -1, -1):
        idx = jnp.where(vals[j] == best, j, idx)
    return idx


def _route(lt):
    g = [lt[k:k + 1, :] for k in range(N_GROUPS)]
    g_max = functools.reduce(jnp.maximum, g)
    g_idx = _first_index(g, g_max)
    g_w = 1.0 / sum(jnp.exp(v - g_max) for v in g)
    e = []
    for j in range(EXPERTS_PER_GROUP):
        v = lt[N_GROUPS + j:N_GROUPS + j + 1, :]
        for gi in range(1, N_GROUPS):
            row = N_GROUPS + gi * EXPERTS_PER_GROUP + j
            v = jnp.where(g_idx == gi, lt[row:row + 1, :], v)
        e.append(v)
    e1 = functools.reduce(jnp.maximum, e)
    i1 = _first_index(e, e1)
    rest = [jnp.where(i1 == j, -jnp.inf, e[j]) for j in range(EXPERTS_PER_GROUP)]
    e2 = functools.reduce(jnp.maximum, rest)
    i2 = _first_index(rest, e2)
    t = jnp.exp(e2 - e1)
    w1 = g_w / (1.0 + t)
    w2 = g_w * t / (1.0 + t)
    swap = i2 < i1
    lo = jnp.where(swap, i2, i1)
    hi = jnp.where(swap, i1, i2)
    pair = jnp.zeros_like(lo)
    for p, (slot0, slot1) in enumerate(PAIRS):
        pair = jnp.where((hi == slot0) & (lo == slot1), p, pair)
    bucket = g_idx * len(PAIRS) + pair
    return bucket, jnp.where(swap, w1, w2), jnp.where(swap, w2, w1)


def _outproj_kernel(yp_ref, ys_ref, oa_ref, ob_ref, zc_ref, zh_ref, ms_ref, wout_ref, cw_ref, gn_ref, wr_ref,
                    br_ref, y1_ref, info_ref, cnt_ref, ut_ref, y1_scr, carry_scr, *, prompt_tiles, tiles_per_seq):
    i = pl.program_id(0)
    tile = yp_ref.shape[0]

    @pl.when(i == 0)
    def _():
        carry_scr[...] = jnp.zeros_like(carry_scr)

    @pl.when(i < prompt_tiles)
    def _():
        zc = zc_ref[...].astype(F32)
        zh = zh_ref[...].astype(F32)
        halo = zh[:, 2 * C_WIDTH:] * zh[:, :C_WIDTH]
        halo = jnp.where(i % tiles_per_seq == 0, 0.0, halo)
        oc, u = _gated_conv(zc[:, :C_WIDTH], zc[:, C_WIDTH:2 * C_WIDTH], zc[:, 2 * C_WIDTH:], halo, cw_ref[...])
        ut_ref[...] = u[tile - 8:, :]
        y1_scr[...] = (
            yp_ref[...]
            + jnp.dot(oa_ref[...], wout_ref[0:A_Q_W, :], preferred_element_type=F32)
            + jnp.dot(ob_ref[...], wout_ref[A_Q_W:A_Q_W + B_W, :], preferred_element_type=F32)
            + jnp.dot(oc.astype(BF16), wout_ref[A_Q_W + B_W:, :], preferred_element_type=F32))

    @pl.when(i >= prompt_tiles)
    def _():
        ut_ref[...] = jnp.zeros_like(ut_ref)
        y1_scr[...] = ys_ref[...] + jnp.dot(ms_ref[...].astype(BF16), wout_ref[...], preferred_element_type=F32)

    y1 = y1_scr[...]
    y1_ref[:, :D_MODEL] = y1
    xn = _rms(y1, gn_ref[...])

    x_hi = xn.astype(BF16)
    x_lo = (xn - x_hi.astype(F32)).astype(BF16)
    wr = wr_ref[...]
    w_hi = wr.astype(BF16)
    w_lo = (wr - w_hi.astype(F32)).astype(BF16)
    lt_hi = _nt_dot(jnp.concatenate([w_hi, w_lo], axis=0), x_hi)
    lt = lt_hi[:ROUTE_ROWS] + lt_hi[ROUTE_ROWS:] + _nt_dot(w_hi, x_lo) + br_ref[...]
    bucket, w_slot0, w_slot1 = _route(lt)
    onehot = (lax.broadcasted_iota(jnp.int32, (ROUTE_ROWS, tile), 0) == bucket).astype(F32)
    upper = (lax.broadcasted_iota(jnp.int32, (tile, tile), 0)
             <= lax.broadcasted_iota(jnp.int32, (tile, tile), 1)).astype(BF16)
    running = jnp.dot(onehot.astype(BF16), upper, preferred_element_type=F32)
    carry = carry_scr[...]
    rank = jnp.sum(onehot * (running - 1.0 + carry), axis=0, keepdims=True)
    carry = carry + jnp.sum(onehot, axis=1, keepdims=True)
    carry_scr[...] = carry
    cnt_ref[...] = jnp.broadcast_to(carry, cnt_ref.shape)
    info_ref[...] = jnp.concatenate([bucket.astype(F32), rank, jnp.zeros((SUBLANES - 2, tile), F32)], axis=0)
    y1_ref[:, D_MODEL:] = jnp.concatenate([w_slot0, w_slot1, jnp.zeros((LANES - 2, tile), F32)], axis=0).T


def _outproj_route(yp, ys, sample_row0, oa, ob, z, mix_s, w_out, cw, gain, w_route, b_route, n_prompt, seq_len):
    n = n_prompt + mix_s.shape[0]
    tiles = n // ROW_TILE
    p_tiles = n_prompt // ROW_TILE
    halo_rows = 16
    halo_blocks = ROW_TILE // halo_rows
    conv_w = 3 * C_WIDTH
    pidx = lambda i: jnp.minimum(i, p_tiles - 1)
    sidx = lambda i: jnp.maximum(i - p_tiles, 0)
    const = lambda a: pl.BlockSpec(a.shape, lambda i: (0,) * a.ndim)
    gain = gain.reshape(1, D_MODEL)
    return pl.pallas_call(
        functools.partial(_outproj_kernel, prompt_tiles=p_tiles, tiles_per_seq=seq_len // ROW_TILE),
        out_shape=(jax.ShapeDtypeStruct((n, ROW_EXT), F32),
                   jax.ShapeDtypeStruct((8, n), F32),
                   jax.ShapeDtypeStruct((ROUTE_ROWS, LANES), F32),
                   jax.ShapeDtypeStruct((tiles * 8, C_WIDTH), F32)),
        grid=(tiles,),
        in_specs=[pl.BlockSpec((ROW_TILE, D_MODEL), lambda i: (pidx(i), 0)),
                  pl.BlockSpec((ROW_TILE, D_MODEL), lambda i: (sample_row0 // ROW_TILE + sidx(i), 0)),
                  pl.BlockSpec((ROW_TILE, A_Q_W), lambda i: (pidx(i), 0)),
                  pl.BlockSpec((ROW_TILE, B_W), lambda i: (pidx(i), 0)),
                  pl.BlockSpec((ROW_TILE, conv_w), lambda i: (pidx(i), 0)),
                  pl.BlockSpec((halo_rows, conv_w), lambda i: (jnp.maximum(pidx(i) * halo_blocks - 1, 0), 0)),
                  pl.BlockSpec((ROW_TILE, D_MODEL), lambda i: (sidx(i), 0)),
                  const(w_out), const(cw), const(gain), const(w_route), const(b_route)],
        out_specs=(pl.BlockSpec((ROW_TILE, ROW_EXT), lambda i: (i, 0)),
                   pl.BlockSpec((8, ROW_TILE), lambda i: (0, i)),
                   pl.BlockSpec((ROUTE_ROWS, LANES), lambda i: (0, 0)),
                   pl.BlockSpec((8, C_WIDTH), lambda i: (i, 0))),
        scratch_shapes=[pltpu.VMEM((ROW_TILE, D_MODEL), F32), pltpu.VMEM((ROUTE_ROWS, 1), F32)],
        compiler_params=_params("arbitrary"),
    )(yp, ys, oa, ob, z, z, mix_s, w_out, cw, gain, w_route, b_route)


def _moe_kernel(e0_ref, e1_ref, used_ref, x_ref, g_ref, *rest, has_final_gain):
    o_ref = rest[-1]
    weights = rest[:6 * TILES_PER_STEP]
    t = pl.program_id(0)

    @pl.when(used_ref[t * TILES_PER_STEP] > 0)
    def _():
        for k in range(TILES_PER_STEP):
            rows = slice(k * MOE_TILE, (k + 1) * MOE_TILE)
            wg1, wu1, wd1, wg2, wu2, wd2 = weights[6 * k:6 * k + 6]
            y1 = x_ref[rows, :D_MODEL]
            w = x_ref[rows, D_MODEL:]
            x = _rms(y1, g_ref[...]).astype(BF16)

            def expert(wg, wu, wd, scale, x=x):
                g = jnp.dot(x, wg[...], preferred_element_type=F32)
                u = jnp.dot(x, wu[...], preferred_element_type=F32)
                h = g * jax.nn.sigmoid(g) * u * scale
                return jnp.dot(h.astype(BF16), wd[...], preferred_element_type=F32)

            y2 = y1 + expert(wg1, wu1, wd1, w[:, 0:1]) + expert(wg2, wu2, wd2, w[:, 1:2])
            o_ref[rows, :] = _rms(y2, rest[-2][...]) if has_final_gain else y2

    @pl.when(used_ref[t * TILES_PER_STEP] == 0)
    def _():
        o_ref[...] = jnp.zeros_like(o_ref)


def _experts(xs, gain, e_slot0, e_slot1, used, w_gate, w_up, w_down, layer, final_gain=None):
    step_rows = TILES_PER_STEP * MOE_TILE
    n_steps = xs.shape[0] // step_rows

    def weight_specs(k):
        pick = lambda sel: (lambda t, e0, e1, u: (layer, (e0, e1)[sel][t * TILES_PER_STEP + k], 0, 0))
        up = lambda sel: pl.BlockSpec((None, None, D_MODEL, EXPERT_FF), pick(sel))
        down = lambda sel: pl.BlockSpec((None, None, EXPERT_FF, D_MODEL), pick(sel))
        return [up(0), up(0), down(0), up(1), up(1), down(1)]

    gain_spec = pl.BlockSpec((1, D_MODEL), lambda t, e0, e1, u: (0, 0))
    extra = [] if final_gain is None else [final_gain.reshape(1, D_MODEL)]
    grid_spec = pltpu.PrefetchScalarGridSpec(
        num_scalar_prefetch=3,
        grid=(n_steps,),
        in_specs=[pl.BlockSpec((step_rows, ROW_EXT), lambda t, e0, e1, u: (t, 0)), gain_spec]
                 + [spec for k in range(TILES_PER_STEP) for spec in weight_specs(k)] + [gain_spec] * len(extra),
        out_specs=pl.BlockSpec((step_rows, D_MODEL), lambda t, e0, e1, u: (t, 0)))
    return pl.pallas_call(
        functools.partial(_moe_kernel, has_final_gain=final_gain is not None),
        out_shape=jax.ShapeDtypeStruct((xs.shape[0], D_MODEL), F32),
        grid_spec=grid_spec,
        compiler_params=_params("arbitrary"),
    )(e_slot0, e_slot1, used, xs, gain.reshape(1, D_MODEL), *([w_gate, w_up, w_down] * (2 * TILES_PER_STEP)), *extra)


def _dispatch_plan(info, counts, n):
    n_tiles = -(-n // MOE_TILE) + N_ROUTE_BUCKETS
    n_tiles = -(-n_tiles // TILES_PER_STEP) * TILES_PER_STEP
    bucket = info[0].astype(jnp.int32)
    rank = info[1].astype(jnp.int32)
    counts = counts[:N_ROUTE_BUCKETS, 0].astype(jnp.int32)
    tiles_per_bucket = (counts + MOE_TILE - 1) // MOE_TILE
    tile_end = jnp.cumsum(tiles_per_bucket)
    row_start = (tile_end - tiles_per_bucket) * MOE_TILE
    dest = row_start[bucket] + rank
    src = (jnp.arange(n_tiles * MOE_TILE, dtype=jnp.int32) % n).at[dest].set(jnp.arange(n, dtype=jnp.int32))
    tile_ids = jnp.arange(n_tiles, dtype=jnp.int32)
    tile_bucket = jnp.sum((tile_ids[:, None] >= tile_end[None, :]).astype(jnp.int32), axis=1)
    tile_bucket = jnp.minimum(tile_bucket, N_ROUTE_BUCKETS - 1)
    used = (tile_ids < tile_end[-1]).astype(jnp.int32)
    pair = tile_bucket % len(PAIRS)
    base = (tile_bucket // len(PAIRS)) * EXPERTS_PER_GROUP
    pairs = jnp.asarray(PAIRS, jnp.int32)
    return dest, src, base + pairs[pair, 0], base + pairs[pair, 1], used


def _permute_in_columns(w):
    attn = A_Q_W + 2 * A_KV_W + 3 * B_W
    return jnp.concatenate([w[:, attn:], w[:, A_Q_W:A_Q_W + A_KV_W], w[:, :A_Q_W], w[:, A_Q_W + A_KV_W:attn]],
                           axis=1)


def kernel(x_prompt, x_sample, cache_a_k, cache_a_v, cache_b_k, cache_b_v, state_conv, rel_bias_table,
           w_in, w_out, conv_w, attn_sinks, norm_mix, norm_ffn, w_group, b_group, w_router, b_router,
           w_gate, w_up, w_down, norm_final):
    n_seq, seq_len, _ = x_prompt.shape
    dec_seq, n_new, _ = x_sample.shape
    depth = w_in.shape[0]
    n_prompt = n_seq * seq_len
    n_sample = dec_seq * n_new
    n = n_prompt + n_sample
    la, lb = cache_a_k.shape[2], cache_b_k.shape[2]
    assert seq_len % SPAN == 0 and n_prompt % IN_TILE == 0 and n_sample % IN_TILE == 0 and IN_TILE % ROW_TILE == 0
    assert seq_len >= SPAN and la % LANES == 0 and lb % LANES == 0 and n_new == 8
    assert dec_seq % SEQS_PER_STEP == 0 and n_prompt % (SEQS_PER_STEP * n_new) == 0

    table_a, table_b = rel_bias_table[:, :A_Q_HEADS], rel_bias_table[:, A_Q_HEADS:]
    bias_a = _band_bias(table_a, (1,))[:, 0]
    bias_b = _band_bias(table_b, DILATIONS)
    sbias_a = _sample_bias(table_a, n_new, la, np.arange(la + LANES), ((A_WINDOW, 1),))[0]
    sbias_b = _sample_bias(table_b, n_new, lb, np.arange(lb + LANES), B_BRANCHES)
    cak, cav, cbk, cbv = (c.transpose(0, 1, 3, 4, 2) for c in (cache_a_k, cache_a_v, cache_b_k, cache_b_v))

    w_in_b = jnp.stack([_permute_in_columns(w_in[l]) for l in range(depth)]).astype(BF16)
    w_out_b = w_out.astype(BF16)
    expert_w = tuple(w.astype(BF16) for w in (w_gate, w_up, w_down))
    pad = ROUTE_ROWS - N_GROUPS - N_EXPERTS
    w_route = jnp.pad(jnp.concatenate([w_group, w_router], axis=2).transpose(0, 2, 1), ((0, 0), (0, pad), (0, 0)))
    b_route = jnp.pad(jnp.concatenate([b_group, b_router], axis=1), ((0, 0), (0, pad)))[..., None]
    sink_rows = jnp.repeat(attn_sinks, n_new, axis=1)[..., None]

    yp, ys, s_row0 = x_prompt.reshape(n_prompt, D_MODEL), x_sample.reshape(n_sample, D_MODEL), 0
    states = []
    for l in range(depth):
        lap, lbp = min(A_WINDOW, seq_len), min(SPAN, seq_len)
        zh, zf, kbt, vbt = _inproj(yp, ys, s_row0, n_prompt, n_sample, norm_mix[l], w_in_b[l], seq_len, lbp)
        oa = _window_attention(zh, bias_a, attn_sinks[l], n_seq, seq_len)
        ob = _dilated_attention(zf, bias_b, n_seq, seq_len)
        mix_s, conv_s = _sample_mixer(zh, zf, n_prompt, cak, cav, cbk, cbv, l, state_conv[l], conv_w[l], sink_rows[l],
                                      sbias_a, sbias_b, n_new)
        y1, info, counts, u_tail = _outproj_route(
            yp, ys, s_row0, oa, ob, zh, mix_s, w_out_b[l], conv_w[l], norm_ffn[l], w_route[l], b_route[l], n_prompt, seq_len)
        def columns(col):
            return (zh, col) if col < SPLIT else (zf, col - SPLIT)

        def prompt_tail(length, col, width, heads):
            z, c = columns(col)
            rows = [z[(b + 1) * seq_len - length:(b + 1) * seq_len, c:c + width] for b in range(n_seq)]
            return jnp.stack(rows).astype(F32).reshape(n_seq, length, heads, HEAD_DIM)

        def sample_rows(col, width, heads):
            z, c = columns(col)
            return z[n_prompt:, c:c + width].astype(F32).reshape(dec_seq, n_new, heads, HEAD_DIM)

        last_tile = [((b + 1) * seq_len // ROW_TILE - 1) * SUBLANES for b in range(n_seq)]
        conv_p = jnp.stack([u_tail[t + SUBLANES - (CONV_WIDTH - 1):t + SUBLANES, :] for t in last_tile])
        layer_states = (
            prompt_tail(lap, COL_KA, A_KV_W, A_KV_HEADS), prompt_tail(lap, COL_VA, A_KV_W, A_KV_HEADS),
            kbt.reshape(n_seq, B_HEADS, HEAD_DIM, lbp).transpose(0, 3, 1, 2),
            vbt.reshape(n_seq, B_HEADS, HEAD_DIM, lbp).transpose(0, 3, 1, 2), conv_p,
            sample_rows(COL_KA, A_KV_W, A_KV_HEADS), sample_rows(COL_VA, A_KV_W, A_KV_HEADS),
            sample_rows(COL_KB, B_W, B_HEADS), sample_rows(COL_VB, B_W, B_HEADS), conv_s)

        dest, src, e_lo, e_hi, used = _dispatch_plan(info, counts, n)
        xs = y1[src]
        if l == depth - 1:
            xs, layer_states = lax.optimization_barrier((xs, layer_states))
        y_sorted = _experts(xs, norm_ffn[l], e_lo, e_hi, used, *expert_w, l,
                            final_gain=norm_final if l == depth - 1 else None)
        if l < depth - 1:
            y_next, layer_states = lax.optimization_barrier((y_sorted[dest], layer_states))
            yp, ys, s_row0 = y_next, y_next, n_prompt
        states.append(layer_states)

    y_prompt = y_sorted[dest[:n_prompt]].reshape(n_seq, seq_len, D_MODEL)
    y_sample = y_sorted[dest[n_prompt:]].reshape(dec_seq, n_new, D_MODEL)
    st = [jnp.stack([s[k] for s in states]) for k in range(10)]
    return (y_prompt, y_sample, st[0], st[1], st[2], st[3], st[4], st[5], st[6], st[7], st[8], st[9])
```
